```python
import math
import jax
import jax.numpy as jnp
from jax import lax
import numpy as np

D_MODEL = 1024
BATCH = 4
SEQ = 4096
DEPTH = 2
DEC_BATCH = 128
DEC_SEQ = 1
PAST_LEN = 2048
PAGE_SIZE = 128

N_REC_LAYERS = (DEPTH + 1) // 2
N_ATT_LAYERS = DEPTH // 2
NORM_EPS = 1e-6

SSD_WIDTH = D_MODEL
SSD_HEAD_DIM = 64
SSD_HEADS = SSD_WIDTH // SSD_HEAD_DIM
SSD_GROUPS = 4
SSD_STATE = 128
SSD_CONV = 4
SSD_CHUNK = 128
SSD_XBC = SSD_WIDTH + 2 * SSD_GROUPS * SSD_STATE

RG_WIDTH = D_MODEL
RG_BLOCKS = 16
RG_BLOCK_DIM = RG_WIDTH // RG_BLOCKS
RG_CONV = 4
RG_C = 8.0

REC_SPLITS = [SSD_WIDTH, SSD_WIDTH + SSD_XBC, SSD_WIDTH + SSD_XBC + SSD_HEADS,
              SSD_WIDTH + SSD_XBC + SSD_HEADS + RG_WIDTH]
REC_IN = REC_SPLITS[-1] + RG_WIDTH

NSA_HEADS = 16
NSA_KV_HEADS = 4
NSA_HEAD_DIM = 64
NSA_GROUP = NSA_HEADS // NSA_KV_HEADS
NSA_Q_WIDTH = NSA_HEADS * NSA_HEAD_DIM
NSA_KV_WIDTH = NSA_KV_HEADS * NSA_HEAD_DIM
NSA_IN = NSA_Q_WIDTH + 6 * NSA_KV_WIDTH + 3 * NSA_HEADS
CMP_LEN = 32
CMP_STRIDE = 16
CMP_R = CMP_LEN // CMP_STRIDE
CMP_HID = 128
SEL_BLOCK = 64
SEL_TOPN = 16
WINDOW = 512
NSA_Q_BLOCK = 64

N_EXPERTS = 32
TOP_K = 4
D_FF = D_MODEL
SWIGLU_LIMIT = 7.0
SWIGLU_ALPHA = 1.702
MOE_MAX_BLOCK = 128

kernel_name = 'hybrid_ssd_rglru_nsa_moe_decoder_step'


def rmsnorm(x, w):
    xf = x.astype(jnp.float32)
    y = xf * lax.rsqrt(jnp.mean(xf * xf, axis=-1, keepdims=True) + NORM_EPS)
    return (y * w.astype(jnp.float32)).astype(x.dtype)


def adaln(c, w, b):
    mod = jax.nn.silu(c) @ w + b
    return jnp.split(mod[:, None, :], 6, axis=-1)


def modulate(x, g, shift, scale):
    return rmsnorm(x, g) * (1 + scale) + shift


def masked_softmax(s, mask):
    s = jnp.where(mask, s.astype(jnp.float32), -jnp.inf)
    m = jnp.max(s, axis=-1, keepdims=True)
    e = jnp.exp(s - jnp.where(jnp.isfinite(m), m, 0.0))
    d = jnp.sum(e, axis=-1, keepdims=True)
    return e / jnp.where(d > 0, d, 1.0)


def causal_conv(x, prev, w, b):
    width = w.shape[0]
    L = x.shape[1]
    xp = jnp.concatenate([prev.astype(x.dtype), x], axis=1)
    y = b
    for k in range(width):
        y = y + xp[:, k:k + L] * w[k]
    return y, xp[:, xp.shape[1] - (width - 1):]


def ssd_scan(x, dt, A, Bm, Cm, h0):
    b, L, H, P = x.shape
    G, N = Bm.shape[2], Bm.shape[3]
    K = H // G
    q = math.gcd(L, SSD_CHUNK)
    nc = L // q
    a = (dt * A).reshape(b, nc, q, G, K)
    xdt = (x * dt[..., None]).reshape(b, nc, q, G, K, P)
    Bc = Bm.reshape(b, nc, q, G, N)
    Cc = Cm.reshape(b, nc, q, G, N)
    a_cs = jnp.cumsum(a, axis=2)
    causal = jnp.tril(jnp.ones((q, q), dtype=bool))[None, None, :, :, None, None]
    seg = a_cs[:, :, :, None] - a_cs[:, :, None, :]
    decay = jnp.exp(jnp.where(causal, seg, -jnp.inf))
    cb = jnp.einsum('bctgn,bcsgn->bctsg', Cc, Bc)
    y_diag = jnp.einsum('bctsgk,bcsgkp->bctgkp', cb[..., None] * decay, xdt)
    to_end = jnp.exp(a_cs[:, :, -1:] - a_cs)
    states = jnp.einsum('bcsgn,bcsgkp->bcgkpn', Bc, xdt * to_end[..., None])
    chunk_decay = jnp.exp(a_cs[:, :, -1])

    def step(h, inp):
        st, dec = inp
        return h * dec[..., None, None] + st, h

    h_last, h_prev = lax.scan(step, h0.reshape(b, G, K, P, N),
                              (jnp.moveaxis(states, 1, 0), jnp.moveaxis(chunk_decay, 1, 0)))
    h_prev = jnp.moveaxis(h_prev, 0, 1)
    y_off = jnp.einsum('bctgn,bcgkpn->bctgkp', Cc, h_prev) * jnp.exp(a_cs)[..., None]
    return (y_diag + y_off).reshape(b, L, H, P), h_last.reshape(b, H, P, N)


def rg_lru(x, h0, wa, ba, wi, bi, lam):
    b, L, _ = x.shape
    xb = x.reshape(b, L, RG_BLOCKS, RG_BLOCK_DIM)
    r = jax.nn.sigmoid(jnp.einsum('blnd,nde->blne', xb, wa).reshape(b, L, RG_WIDTH) + ba)
    i = jax.nn.sigmoid(jnp.einsum('blnd,nde->blne', xb, wi).reshape(b, L, RG_WIDTH) + bi)
    log_a = -RG_C * r * jax.nn.softplus(-lam.astype(jnp.float32))
    a = jnp.exp(log_a)
    u = jnp.sqrt(-jnp.expm1(2.0 * log_a)) * (i * x)
    u = u.at[:, 0].add(a[:, 0] * h0)

    def combine(e1, e2):
        return e1[0] * e2[0], e2[0] * e1[1] + e2[1]

    _, h = lax.associative_scan(combine, (a, u), axis=1)
    return h, h[:, -1]


def rec_mixer(h, conv_ssd0, ssm0, conv_rg0, rg0, w_in, conv_w, conv_b, dt_bias, a_log, d_skip,
              norm_w, rg_conv_w, rg_conv_b, wa, ba, wi, bi, lam, w_out):
    f32 = jnp.float32
    b, L, _ = h.shape
    z, xbc, dt, gate, xr = jnp.split(h @ w_in, REC_SPLITS, axis=-1)
    xbc, conv_ssd1 = causal_conv(xbc, conv_ssd0, conv_w, conv_b)
    xbc = jax.nn.silu(xbc)
    xs, Bm, Cm = jnp.split(xbc, [SSD_WIDTH, SSD_WIDTH + SSD_GROUPS * SSD_STATE], axis=-1)
    dt = jax.nn.softplus(dt.astype(f32) + dt_bias.astype(f32))
    A = -jnp.exp(a_log.astype(f32))
    xh = xs.astype(f32).reshape(b, L, SSD_HEADS, SSD_HEAD_DIM)
    y, ssm1 = ssd_scan(xh, dt, A,
                       Bm.astype(f32).reshape(b, L, SSD_GROUPS, SSD_STATE),
                       Cm.astype(f32).reshape(b, L, SSD_GROUPS, SSD_STATE),
                       ssm0.astype(f32))
    y = y + d_skip.astype(f32)[:, None] * xh
    y = y.reshape(b, L, SSD_WIDTH) * jax.nn.silu(z.astype(f32))
    y = rmsnorm(y.reshape(b, L, SSD_GROUPS, -1), norm_w.reshape(SSD_GROUPS, -1)).reshape(b, L, SSD_WIDTH)
    xr, conv_rg1 = causal_conv(xr, conv_rg0, rg_conv_w, rg_conv_b)
    r_out, rg1 = rg_lru(xr.astype(f32), rg0.astype(f32), wa, ba, wi, bi, lam)
    r_out = r_out * jax.nn.gelu(gate.astype(f32))
    out = jnp.concatenate([y, r_out], axis=-1).astype(h.dtype) @ w_out
    return out, conv_ssd1, ssm1.astype(h.dtype), conv_rg1, rg1.astype(h.dtype)


def nsa_project(h, w_in, q_norm, k_norm):
    b, L, _ = h.shape
    q, kv, g = jnp.split(h @ w_in, [NSA_Q_WIDTH, NSA_Q_WIDTH + 6 * NSA_KV_WIDTH], axis=-1)
    q = rmsnorm(q.reshape(b, L, NSA_KV_HEADS, NSA_GROUP, NSA_HEAD_DIM), q_norm) * (NSA_HEAD_DIM ** -0.5)
    kv = kv.reshape(b, L, 6, NSA_KV_HEADS, NSA_HEAD_DIM)
    k_slc = rmsnorm(kv[:, :, 2], k_norm[1])
    k_win = rmsnorm(kv[:, :, 4], k_norm[2])
    rows = jnp.stack([kv[:, :, 0], kv[:, :, 1], k_slc, kv[:, :, 3]], axis=2)
    win = jnp.stack([k_win, kv[:, :, 5]], axis=2)
    gates = jax.nn.sigmoid(g.astype(jnp.float32)).reshape(b, L, NSA_KV_HEADS, NSA_GROUP, 3)
    return q, rows, win, gates


def compress(r, w1, pe, w2):
    b, T, G, dh = r.shape
    n_chunk = T // CMP_STRIDE
    nc = n_chunk - CMP_R + 1
    ch = r[:, :n_chunk * CMP_STRIDE].reshape(b, n_chunk, CMP_STRIDE, G, dh)
    proj = jnp.einsum('bcsgd,rsdh->bcrgh', ch, w1.reshape(CMP_R, CMP_STRIDE, dh, CMP_HID))
    hid = jnp.einsum('ld,ldh->h', pe, w1)
    for rr in range(CMP_R):
        hid = hid + proj[:, rr:rr + nc, rr]
    return jax.nn.gelu(hid) @ w2


def nsa_context(rows, cmp_w1, cmp_pe, cmp_w2, k_norm_cmp):
    b, T = rows.shape[:2]
    kc = rmsnorm(compress(rows[:, :, 0], cmp_w1[0], cmp_pe[0], cmp_w2[0]), k_norm_cmp)
    vc = compress(rows[:, :, 1], cmp_w1[1], cmp_pe[1], cmp_w2[1])
    ns = -(-T // SEL_BLOCK)
    sel = jnp.pad(rows[:, :, 2:4], ((0, 0), (0, ns * SEL_BLOCK - T), (0, 0), (0, 0), (0, 0)))
    sel = sel.reshape(b, ns, SEL_BLOCK, 2, NSA_KV_HEADS, NSA_HEAD_DIM).transpose(3, 0, 4, 1, 2, 5)
    return kc, vc, sel[0], sel[1]


def overlap_matrix(nc, ns):
    i = np.arange(nc)[:, None]
    j = np.arange(ns)[None, :]
    ov = (i * CMP_STRIDE < (j + 1) * SEL_BLOCK) & (i * CMP_STRIDE + CMP_LEN > j * SEL_BLOCK)
    return ov.astype(np.float32)


def nsa_attend(q, gates, t_pos, kc, vc, ks, vs, kw, vw, w_pos):
    b, Q, G, K, dh = q.shape
    nc = kc.shape[1]
    ns = ks.shape[2]
    tq = t_pos[None, :, None, None, None]
    c_end = jnp.arange(nc) * CMP_STRIDE + CMP_LEN - 1
    p_c = masked_softmax(jnp.einsum('bqgkd,bngd->bqgkn', q, kc), c_end <= tq)
    o_c = jnp.einsum('bqgkn,bngd->bqgkd', p_c, vc)
    imp = jnp.einsum('bqgkn,ns->bqgs', p_c, jnp.asarray(overlap_matrix(nc, ns)))
    jj = jnp.arange(ns)[None, :]
    jt = (t_pos // SEL_BLOCK)[:, None]
    valid = jj <= jt
    forced = valid & ((jj == 0) | (jj == jt) | (jj == jt - 1))
    imp = jnp.where(forced[None, :, None], jnp.inf, jnp.where(valid[None, :, None], imp, -jnp.inf))
    _, idx = lax.top_k(imp, min(SEL_TOPN, ns))
    n = idx.shape[-1]
    bi = jnp.arange(b)[:, None, None, None]
    gi = jnp.arange(G)[None, None, :, None]
    k_sel = ks[bi, gi, idx].reshape(b, Q, G, n * SEL_BLOCK, dh)
    v_sel = vs[bi, gi, idx].reshape(b, Q, G, n * SEL_BLOCK, dh)
    kpos = (idx[..., None] * SEL_BLOCK + jnp.arange(SEL_BLOCK)).reshape(b, Q, G, 1, n * SEL_BLOCK)
    p_s = masked_softmax(jnp.einsum('bqgkd,bqgmd->bqgkm', q, k_sel), kpos <= tq)
    o_s = jnp.einsum('bqgkm,bqgmd->bqgkd', p_s, v_sel)
    m_w = (w_pos <= tq) & (w_pos > tq - WINDOW) & (w_pos >= 0)
    p_w = masked_softmax(jnp.einsum('bqgkd,bwgd->bqgkw', q, kw), m_w)
    o_w = jnp.einsum('bqgkw,bwgd->bqgkd', p_w, vw)
    o = gates[..., 0:1] * o_c + gates[..., 1:2] * o_s + gates[..., 2:3] * o_w
    return o.reshape(b, Q, G * K * dh).astype(q.dtype)


def nsa_prompt(h, w_in, q_norm, k_norm, cmp_w1, cmp_pe, cmp_w2, w_out):
    b, L, _ = h.shape
    q, rows, win, gates = nsa_project(h, w_in, q_norm, k_norm)
    kc, vc, ks, vs = nsa_context(rows, cmp_w1, cmp_pe, cmp_w2, k_norm[0])
    win_pad = jnp.pad(win, ((0, 0), (WINDOW, 0), (0, 0), (0, 0), (0, 0)))

    def block(i):
        s = i * NSA_Q_BLOCK
        qb = lax.dynamic_slice_in_dim(q, s, NSA_Q_BLOCK, axis=1)
        gb = lax.dynamic_slice_in_dim(gates, s, NSA_Q_BLOCK, axis=1)
        wb = lax.dynamic_slice_in_dim(win_pad, s, WINDOW + NSA_Q_BLOCK, axis=1)
        t_pos = s + jnp.arange(NSA_Q_BLOCK)
        w_pos = s - WINDOW + jnp.arange(WINDOW + NSA_Q_BLOCK)
        return nsa_attend(qb, gb, t_pos, kc, vc, ks, vs, wb[:, :, 0], wb[:, :, 1], w_pos)

    o = lax.map(block, jnp.arange(L // NSA_Q_BLOCK))
    o = jnp.moveaxis(o, 0, 1).reshape(b, L, NSA_Q_WIDTH)
    return o @ w_out, rows, win[:, L - min(WINDOW, L):]


def nsa_sample(h, cache, page_table, win_buf, w_in, q_norm, k_norm, cmp_w1, cmp_pe, cmp_w2, w_out):
    b, L, _ = h.shape
    q, rows, win, gates = nsa_project(h, w_in, q_norm, k_norm)
    past = page_table.shape[1] * cache.shape[1]
    past_rows = cache[page_table].reshape(b, past, 4, NSA_KV_HEADS, NSA_HEAD_DIM)
    kc, vc, ks, vs = nsa_context(jnp.concatenate([past_rows.astype(rows.dtype), rows], axis=1),
                                 cmp_w1, cmp_pe, cmp_w2, k_norm[0])
    wb_len = win_buf.shape[1]
    wk = jnp.concatenate([win_buf.astype(win.dtype), win], axis=1)
    t_pos = past + jnp.arange(L)
    w_pos = past - wb_len + jnp.arange(wb_len + L)
    o = nsa_attend(q, gates, t_pos, kc, vc, ks, vs, wk[:, :, 0], wk[:, :, 1], w_pos)
    return o @ w_out, rows, wk[:, L:]


def clamped_swiglu(u):
    g, lin = jnp.split(u, 2, axis=-1)
    g = jnp.minimum(g, SWIGLU_LIMIT)
    lin = jnp.clip(lin, -SWIGLU_LIMIT, SWIGLU_LIMIT)
    return g * jax.nn.sigmoid(SWIGLU_ALPHA * g) * (lin + 1)


def moe(h, router_w, router_b, w1, b1, w2, b2):
    shp = h.shape
    x = h.reshape(-1, shp[-1])
    T = x.shape[0]
    logits = x.astype(jnp.float32) @ router_w.astype(jnp.float32) + router_b.astype(jnp.float32)
    top_v, top_e = lax.top_k(logits, TOP_K)
    gate = jax.nn.softmax(top_v, axis=-1)
    TK = T * TOP_K
    blk = max(8, min(MOE_MAX_BLOCK, TK // N_EXPERTS))
    n_rows = (-(-TK // blk) + N_EXPERTS) * blk
    e_flat = top_e.reshape(-1)
    order = jnp.argsort(e_flat)
    e_sorted = e_flat[order]
    counts = jnp.bincount(e_flat, length=N_EXPERTS)
    padded = (counts + blk - 1) // blk * blk
    pad_end = jnp.cumsum(padded)
    pad_start = pad_end - padded
    start = jnp.cumsum(counts) - counts
    dest = pad_start[e_sorted] + jnp.arange(TK) - start[e_sorted]
    row_tok = jnp.full((n_rows,), T, jnp.int32).at[dest].set((order // TOP_K).astype(jnp.int32))
    row_gate = jnp.zeros((n_rows,), jnp.float32).at[dest].set(gate.reshape(-1)[order])
    n_blocks = n_rows // blk
    blk_e = jnp.minimum(jnp.searchsorted(pad_end, jnp.arange(n_blocks) * blk, side='right'), N_EXPERTS - 1)
    xpad = jnp.concatenate([x, jnp.zeros((1, x.shape[1]), x.dtype)], axis=0)
    xb = xpad[row_tok].reshape(n_blocks, blk, -1)

    def expert_block(args):
        xe, e = args
        return clamped_swiglu(xe @ w1[e] + b1[e]) @ w2[e] + b2[e]

    yb = lax.map(expert_block, (xb, blk_e)).reshape(n_rows, -1)
    y = jax.ops.segment_sum(yb * row_gate[:, None].astype(yb.dtype), row_tok, num_segments=T + 1)[:T]
    return y.reshape(shp).astype(h.dtype)


def setup_inputs(seed: int = 0) -> dict:
    key = jax.random.key(seed)
    keys = iter(jax.random.split(key, 64))
    f32 = jnp.float32

    def nrm(shape, scale):
        return jax.random.normal(next(keys), shape, f32) * scale

    def unif(shape, lo, hi):
        return jax.random.uniform(next(keys), shape, f32, lo, hi)

    n_pages = PAST_LEN // PAGE_SIZE
    n_phys = (5 * DEC_BATCH * n_pages + 3) // 4
    win_len = min(WINDOW, PAST_LEN)
    page_table = jax.random.permutation(next(keys), n_phys)[:DEC_BATCH * n_pages]
    page_table = page_table.reshape(DEC_BATCH, n_pages).astype(jnp.int32)
    dt0 = jnp.exp(unif((N_REC_LAYERS, SSD_HEADS), math.log(1e-3), math.log(1e-1)))
    a_pow = unif((N_REC_LAYERS, RG_WIDTH), 0.9, 0.999) ** (1.0 / RG_C)
    return {
        'x_prompt': nrm((BATCH, SEQ, D_MODEL), 1.0),
        'x_sample': nrm((DEC_BATCH, DEC_SEQ, D_MODEL), 1.0),
        'cache_nsa_kv': nrm((N_ATT_LAYERS, n_phys, PAGE_SIZE, 4, NSA_KV_HEADS, NSA_HEAD_DIM), 1.0),
        'state_nsa_win': nrm((N_ATT_LAYERS, DEC_BATCH, win_len, 2, NSA_KV_HEADS, NSA_HEAD_DIM), 1.0),
        'state_ssd_conv': nrm((N_REC_LAYERS, DEC_BATCH, SSD_CONV - 1, SSD_XBC), 1.0),
        'state_ssd': nrm((N_REC_LAYERS, DEC_BATCH, SSD_HEADS, SSD_HEAD_DIM, SSD_STATE), 0.1),
        'state_rg_conv': nrm((N_REC_LAYERS, DEC_BATCH, RG_CONV - 1, RG_WIDTH), 1.0),
        'state_rg': nrm((N_REC_LAYERS, DEC_BATCH, RG_WIDTH), 0.5),
        'page_table': page_table,
        'c_prompt': nrm((BATCH, D_MODEL), 1.0),
        'c_sample': nrm((DEC_BATCH, D_MODEL), 1.0),
        'ada_w': nrm((DEPTH, D_MODEL, 6 * D_MODEL), D_MODEL ** -0.5),
        'ada_b': nrm((DEPTH, 6 * D_MODEL), 0.02),
        'norm_mix': 1.0 + nrm((DEPTH, D_MODEL), 0.02),
        'norm_ffn': 1.0 + nrm((DEPTH, D_MODEL), 0.02),
        'rec_w_in': nrm((N_REC_LAYERS, D_MODEL, REC_IN), D_MODEL ** -0.5),
        'ssd_conv_w': nrm((N_REC_LAYERS, SSD_CONV, SSD_XBC), SSD_CONV ** -0.5),
        'ssd_conv_b': nrm((N_REC_LAYERS, SSD_XBC), 0.02),
        'ssd_dt_bias': dt0 + jnp.log(-jnp.expm1(-dt0)),
        'ssd_a_log': jnp.log(unif((N_REC_LAYERS, SSD_HEADS), 1.0, 16.0)),
        'ssd_d': 1.0 + nrm((N_REC_LAYERS, SSD_HEADS), 0.02),
        'ssd_norm_w': 1.0 + nrm((N_REC_LAYERS, SSD_WIDTH), 0.02),
        'rg_conv_w': nrm((N_REC_LAYERS, RG_CONV, RG_WIDTH), RG_CONV ** -0.5),
        'rg_conv_b': nrm((N_REC_LAYERS, RG_WIDTH), 0.02),
        'rg_wa': nrm((N_REC_LAYERS, RG_BLOCKS, RG_BLOCK_DIM, RG_BLOCK_DIM), RG_BLOCK_DIM ** -0.5),
        'rg_ba': nrm((N_REC_LAYERS, RG_WIDTH), 0.02),
        'rg_wi': nrm((N_REC_LAYERS, RG_BLOCKS, RG_BLOCK_DIM, RG_BLOCK_DIM), RG_BLOCK_DIM ** -0.5),
        'rg_bi': nrm((N_REC_LAYERS, RG_WIDTH), 0.02),
        'rg_lambda': jnp.log(a_pow) - jnp.log1p(-a_pow),
        'rec_w_out': nrm((N_REC_LAYERS, SSD_WIDTH + RG_WIDTH, D_MODEL), (SSD_WIDTH + RG_WIDTH) ** -0.5),
        'nsa_w_in': nrm((N_ATT_LAYERS, D_MODEL, NSA_IN), D_MODEL ** -0.5),
        'nsa_q_norm': 1.0 + nrm((N_ATT_LAYERS, NSA_HEAD_DIM), 0.02),
        'nsa_k_norm': 1.0 + nrm((N_ATT_LAYERS, 3, NSA_HEAD_DIM), 0.02),
        'cmp_w1': nrm((N_ATT_LAYERS, 2, CMP_LEN, NSA_HEAD_DIM, CMP_HID), (CMP_LEN * NSA_HEAD_DIM) ** -0.5),
        'cmp_pe': nrm((N_ATT_LAYERS, 2, CMP_LEN, NSA_HEAD_DIM), 0.1),
        'cmp_w2': nrm((N_ATT_LAYERS, 2, CMP_HID, NSA_HEAD_DIM), CMP_HID ** -0.5),
        'nsa_w_out': nrm((N_ATT_LAYERS, NSA_Q_WIDTH, D_MODEL), NSA_Q_WIDTH ** -0.5),
        'router_w': nrm((DEPTH, D_MODEL, N_EXPERTS), D_MODEL ** -0.5),
        'router_b': nrm((DEPTH, N_EXPERTS), 0.01),
        'moe_w1': nrm((DEPTH, N_EXPERTS, D_MODEL, 2 * D_FF), D_MODEL ** -0.5),
        'moe_b1': nrm((DEPTH, N_EXPERTS, 2 * D_FF), 0.01),
        'moe_w2': nrm((DEPTH, N_EXPERTS, D_FF, D_MODEL), D_FF ** -0.5),
        'moe_b2': nrm((DEPTH, N_EXPERTS, D_MODEL), 0.01),
    }


def reference(x_prompt, x_sample, cache_nsa_kv, state_nsa_win, state_ssd_conv, state_ssd, state_rg_conv,
              state_rg, page_table, c_prompt, c_sample, ada_w, ada_b, norm_mix, norm_ffn, rec_w_in,
              ssd_conv_w, ssd_conv_b, ssd_dt_bias, ssd_a_log, ssd_d, ssd_norm_w, rg_conv_w, rg_conv_b,
              rg_wa, rg_ba, rg_wi, rg_bi, rg_lambda, rec_w_out, nsa_w_in, nsa_q_norm, nsa_k_norm, cmp_w1,
              cmp_pe, cmp_w2, nsa_w_out, router_w, router_b, moe_w1, moe_b1, moe_w2, moe_b2):
    xp, xs = x_prompt, x_sample
    bp = xp.shape[0]
    rows_p, rows_s, win_p, win_s = [], [], [], []
    sconv_p, sconv_s, ssm_p, ssm_s = [], [], [], []
    rconv_p, rconv_s, rg_p, rg_s = [], [], [], []
    for i in range(DEPTH):
        j = i // 2
        mp = adaln(c_prompt, ada_w[i], ada_b[i])
        ms = adaln(c_sample, ada_w[i], ada_b[i])
        hp = modulate(xp, norm_mix[i], mp[0], mp[1])
        hs = modulate(xs, norm_mix[i], ms[0], ms[1])
        if i % 2 == 0:
            w = (rec_w_in[j], ssd_conv_w[j], ssd_conv_b[j], ssd_dt_bias[j], ssd_a_log[j], ssd_d[j],
                 ssd_norm_w[j], rg_conv_w[j], rg_conv_b[j], rg_wa[j], rg_ba[j], rg_wi[j], rg_bi[j],
                 rg_lambda[j], rec_w_out[j])
            op, a1, a2, a3, a4 = rec_mixer(
                hp, jnp.zeros((bp, SSD_CONV - 1, SSD_XBC), xp.dtype),
                jnp.zeros((bp, SSD_HEADS, SSD_HEAD_DIM, SSD_STATE), xp.dtype),
                jnp.zeros((bp, RG_CONV - 1, RG_WIDTH), xp.dtype),
                jnp.zeros((bp, RG_WIDTH), xp.dtype), *w)
            osm, b1, b2, b3, b4 = rec_mixer(hs, state_ssd_conv[j], state_ssd[j], state_rg_conv[j],
                                            state_rg[j], *w)
            sconv_p.append(a1); ssm_p.append(a2); rconv_p.append(a3); rg_p.append(a4)
            sconv_s.append(b1); ssm_s.append(b2); rconv_s.append(b3); rg_s.append(b4)
        else:
            w = (nsa_w_in[j], nsa_q_norm[j], nsa_k_norm[j], cmp_w1[j], cmp_pe[j], cmp_w2[j], nsa_w_out[j])
            op, rp, wp = nsa_prompt(hp, *w)
            osm, rs, ws = nsa_sample(hs, cache_nsa_kv[j], page_table, state_nsa_win[j], *w)
            rows_p.append(rp); win_p.append(wp); rows_s.append(rs); win_s.append(ws)
        xp = xp + mp[2] * op
        xs = xs + ms[2] * osm
        mw = (router_w[i], router_b[i], moe_w1[i], moe_b1[i], moe_w2[i], moe_b2[i])
        xp = xp + mp[5] * moe(modulate(xp, norm_ffn[i], mp[3], mp[4]), *mw)
        xs = xs + ms[5] * moe(modulate(xs, norm_ffn[i], ms[3], ms[4]), *mw)
    return (xp, xs, jnp.stack(rows_p), jnp.stack(rows_s), jnp.stack(win_p), jnp.stack(win_s),
            jnp.stack(sconv_p), jnp.stack(sconv_s), jnp.stack(ssm_p), jnp.stack(ssm_s),
            jnp.stack(rconv_p), jnp.stack(rconv_s), jnp.stack(rg_p), jnp.stack(rg_s))
```

```python
import functools
import math

import jax
import jax.numpy as jnp
import numpy as np
from jax import lax
from jax.experimental import pallas as pl
from jax.experimental.pallas import tpu as pltpu

F32 = jnp.float32
BF16 = jnp.bfloat16

D_MODEL = 1024
NORM_EPS = 1e-6

SSD_WIDTH = 1024
SSD_HEAD_DIM = 64
SSD_HEADS = 16
SSD_GROUPS = 4
SSD_STATE = 128
SSD_CONV = 4
SSD_CHUNK = 128
SSD_XBC = SSD_WIDTH + 2 * SSD_GROUPS * SSD_STATE

RG_WIDTH = 1024
RG_BLOCKS = 16
RG_BLOCK_DIM = 64
RG_C = 8.0

NSA_HEADS = 16
NSA_KV_HEADS = 4
NSA_HEAD_DIM = 64
NSA_GROUP = 4
NSA_Q_WIDTH = 1024
NSA_KV_WIDTH = 256
CMP_LEN = 32
CMP_STRIDE = 16
CMP_R = 2
CMP_HID = 128
SEL_BLOCK = 64
SEL_TOPN = 16
WINDOW = 512
NSA_Q_BLOCK = 64

N_EXPERTS = 32
TOP_K = 4
D_FF = 1024
SWIGLU_LIMIT = 7.0
SWIGLU_ALPHA = 1.702

V7X_LANES = 128
V7X_VMEM_LIMIT_BYTES = 56 * 1024 * 1024

MOE_BLOCK_ROWS = 256
ROW_TILE = 512


def _cparams(sem, vmem=None):
    return pltpu.CompilerParams(dimension_semantics=sem, vmem_limit_bytes=vmem)


def _adaln_body(c_ref, w_ref, b_ref, o_ref):
    c = c_ref[...]
    s = c * jax.nn.sigmoid(c)
    o_ref[...] = jnp.dot(s.astype(BF16), w_ref[...], preferred_element_type=F32) + b_ref[...]


def adaln_mod(c, w_bf, b):
    r, d = c.shape
    n = w_bf.shape[1]
    tn = 1536
    return pl.pallas_call(
        _adaln_body,
        out_shape=jax.ShapeDtypeStruct((r, n), F32),
        grid=(n // tn,),
        in_specs=[pl.BlockSpec((r, d), lambda j: (0, 0)),
                  pl.BlockSpec((d, tn), lambda j: (0, j)),
                  pl.BlockSpec((1, tn), lambda j: (0, j))],
        out_specs=pl.BlockSpec((r, tn), lambda j: (0, j)),
        compiler_params=_cparams(("arbitrary",)),
        name="adaln_mod",
    )(c, w_bf, b)


def _modulated(x, g, scale, shift):
    ms = jnp.mean(x * x, axis=-1, keepdims=True)
    y = x * lax.rsqrt(ms + NORM_EPS) * g
    return y * (1.0 + scale) + shift


def _mod_matmul_body(x_ref, g_ref, sc_ref, sh_ref, w_ref, o_ref, h_scr):
    @pl.when(pl.program_id(1) == 0)
    def _():
        h_scr[...] = _modulated(x_ref[...], g_ref[...], sc_ref[0], sh_ref[0]).astype(BF16)

    o_ref[...] = jnp.dot(h_scr[...], w_ref[...], preferred_element_type=F32)


def mod_matmul(x, g, scale, shift, w_bf, tm, tn):
    t, d = x.shape
    n = w_bf.shape[1]
    m, r, _ = scale.shape
    rows_per_mod = t // m
    mod_spec = pl.BlockSpec((1, r, d), lambda i, j: ((i * tm) // rows_per_mod, 0, 0))
    return pl.pallas_call(
        _mod_matmul_body,
        out_shape=jax.ShapeDtypeStruct((t, n), F32),
        grid=(t // tm, n // tn),
        in_specs=[pl.BlockSpec((tm, d), lambda i, j: (i, 0)),
                  pl.BlockSpec((1, d), lambda i, j: (0, 0)),
                  mod_spec, mod_spec,
                  pl.BlockSpec((d, tn), lambda i, j: (0, j))],
        out_specs=pl.BlockSpec((tm, tn), lambda i, j: (i, j)),
        scratch_shapes=[pltpu.VMEM((tm, d), BF16)],
        compiler_params=_cparams(("arbitrary", "arbitrary")),
        name="mod_matmul",
    )(x, g, scale, shift, w_bf)


def _mm_residual_body(n_a, *refs):
    a_refs = refs[:n_a]
    w_refs = refs[n_a:2 * n_a]
    x_ref, g_ref, o_ref = refs[2 * n_a:]
    acc = None
    for a_ref, w_ref in zip(a_refs, w_refs):
        p = jnp.dot(a_ref[...].astype(BF16), w_ref[...], preferred_element_type=F32)
        acc = p if acc is None else acc + p
    o_ref[...] = x_ref[...] + g_ref[0] * acc


def mm_residual(a_list, w_list, x, gate, tm):
    t, d = x.shape
    m, r, _ = gate.shape
    rows_per_mod = t // m
    in_specs = [pl.BlockSpec((tm, a.shape[1]), lambda i: (i, 0)) for a in a_list]
    in_specs += [pl.BlockSpec(w.shape, lambda i: (0, 0)) for w in w_list]
    in_specs += [pl.BlockSpec((tm, d), lambda i: (i, 0)),
                 pl.BlockSpec((1, r, d), lambda i: ((i * tm) // rows_per_mod, 0, 0))]
    return pl.pallas_call(
        functools.partial(_mm_residual_body, len(a_list)),
        out_shape=jax.ShapeDtypeStruct((t, d), F32),
        grid=(t // tm,),
        in_specs=in_specs,
        out_specs=pl.BlockSpec((tm, d), lambda i: (i, 0)),
        compiler_params=_cparams(("arbitrary",)),
        name="mm_residual",
    )(*a_list, *w_list, x, gate)


def _router_body(x_ref, g_ref, sc_ref, sh_ref, wh_ref, wl_ref, rb_ref, h_ref, e_ref, gt_ref):
    h = _modulated(x_ref[...], g_ref[...], sc_ref[0], sh_ref[0])
    h_ref[...] = h
    hh = h.astype(BF16)
    hl = (h - hh.astype(F32)).astype(BF16)
    wh = wh_ref[...]
    wl = wl_ref[...]
    logits = (jnp.dot(hh, wh, preferred_element_type=F32)
              + (jnp.dot(hh, wl, preferred_element_type=F32)
                 + jnp.dot(hl, wh, preferred_element_type=F32))) + rb_ref[...]
    lane = lax.broadcasted_iota(jnp.int32, logits.shape, 1)
    neg = jnp.float32(-jnp.inf)
    vals, idxs = [], []
    cur = logits
    for _ in range(TOP_K):
        m = jnp.max(cur, axis=-1, keepdims=True)
        idx = jnp.min(jnp.where(cur == m, lane, V7X_LANES), axis=-1, keepdims=True)
        vals.append(m)
        idxs.append(idx)
        cur = jnp.where(lane == idx, neg, cur)
    exps = [jnp.exp(v - vals[0]) for v in vals]
    den = exps[0] + exps[1] + exps[2] + exps[3]
    e_out = jnp.zeros(logits.shape, jnp.int32)
    g_out = jnp.zeros(logits.shape, F32)
    for k in range(TOP_K):
        e_out = jnp.where(lane == k, idxs[k], e_out)
        g_out = jnp.where(lane == k, exps[k] / den, g_out)
    e_ref[...] = e_out
    gt_ref[...] = g_out


def moe_router(x, g, scale, shift, wh, wl, rb, tm):
    t, d = x.shape
    m, r, _ = scale.shape
    rows_per_mod = t // m
    mod_spec = pl.BlockSpec((1, r, d), lambda i: ((i * tm) // rows_per_mod, 0, 0))
    return pl.pallas_call(
        _router_body,
        out_shape=(jax.ShapeDtypeStruct((t, d), F32),
                   jax.ShapeDtypeStruct((t, V7X_LANES), jnp.int32),
                   jax.ShapeDtypeStruct((t, V7X_LANES), F32)),
        grid=(t // tm,),
        in_specs=[pl.BlockSpec((tm, d), lambda i: (i, 0)),
                  pl.BlockSpec((1, d), lambda i: (0, 0)),
                  mod_spec, mod_spec,
                  pl.BlockSpec((d, V7X_LANES), lambda i: (0, 0)),
                  pl.BlockSpec((d, V7X_LANES), lambda i: (0, 0)),
                  pl.BlockSpec((1, V7X_LANES), lambda i: (0, 0))],
        out_specs=(pl.BlockSpec((tm, d), lambda i: (i, 0)),
                   pl.BlockSpec((tm, V7X_LANES), lambda i: (i, 0)),
                   pl.BlockSpec((tm, V7X_LANES), lambda i: (i, 0))),
        compiler_params=_cparams(("arbitrary",)),
        name="moe_router",
    )(x, g, scale, shift, wh, wl, rb)


def _ffn_body(blk_e_ref, nvalid_ref, nused_ref,
              idx_hbm, h_hbm, gate_ref, w1_ref, b1_ref, w2_ref, b2_ref,
              out_hbm,
              idx_smem, xbuf, ybuf, w1bf, w2bf, sem_idx, sem_g, sem_s):
    bm = MOE_BLOCK_ROWS
    i = pl.program_id(0)
    n_used = nused_ref[0]

    def idx_copy(blk, slot):
        return pltpu.make_async_copy(idx_hbm.at[blk], idx_smem.at[slot], sem_idx.at[slot])

    def gather_copy(tok, slot, r):
        return pltpu.make_async_copy(h_hbm.at[pl.ds(tok, 1)], xbuf.at[slot, pl.ds(r, 1)], sem_g.at[slot])

    def scatter_copy(dst, slot, r):
        return pltpu.make_async_copy(ybuf.at[slot, pl.ds(r, 1)], out_hbm.at[pl.ds(dst, 1)], sem_s.at[slot])

    def start_gather(islot, slot):
        def body(r, c):
            tok = lax.shift_right_logical(jnp.maximum(idx_smem[islot, r], 0), 2)
            gather_copy(tok, slot, r).start()
            return c
        lax.fori_loop(0, bm, body, 0)

    def wait_gather(slot):
        def body(r, c):
            gather_copy(0, slot, r).wait()
            return c
        lax.fori_loop(0, bm, body, 0)

    def start_scatter(islot, slot, n):
        def body(r, c):
            scatter_copy(idx_smem[islot, r], slot, r).start()
            return c
        lax.fori_loop(0, n, body, 0)

    def wait_scatter(slot, n):
        def body(r, c):
            scatter_copy(0, slot, r).wait()
            return c
        lax.fori_loop(0, n, body, 0)

    @pl.when(i < n_used)
    def _():
        slot = i % 2
        islot = i % 3

        @pl.when(i == 0)
        def _():
            idx_copy(0, 0).start()
            idx_copy(0, 0).wait()
            start_gather(0, 0)

            @pl.when(n_used > 1)
            def _():
                idx_copy(1, 1).start()

        @pl.when(i + 2 < n_used)
        def _():
            idx_copy(i + 2, (i + 2) % 3).start()

        @pl.when(i + 1 < n_used)
        def _():
            idx_copy(i + 1, (i + 1) % 3).wait()
            start_gather((i + 1) % 3, 1 - slot)

        @pl.when(jnp.logical_or(i == 0, blk_e_ref[i] != blk_e_ref[jnp.maximum(i - 1, 0)]))
        def _():
            w1bf[...] = w1_ref[0].astype(BF16)
            w2bf[...] = w2_ref[0].astype(BF16)

        wait_gather(slot)

        @pl.when(i >= 2)
        def _():
            wait_scatter(slot, nvalid_ref[jnp.maximum(i - 2, 0)])

        x = xbuf[slot].astype(BF16)
        u = jnp.dot(x, w1bf[...], preferred_element_type=F32) + b1_ref[0]
        gl = jnp.minimum(u[:, :D_FF], SWIGLU_LIMIT)
        lin = jnp.clip(u[:, D_FF:], -SWIGLU_LIMIT, SWIGLU_LIMIT)
        act = gl * jax.nn.sigmoid(SWIGLU_ALPHA * gl) * (lin + 1.0)
        y = jnp.dot(act.astype(BF16), w2bf[...], preferred_element_type=F32) + b2_ref[0]
        ybuf[slot] = y * gate_ref[...]
        start_scatter(islot, slot, nvalid_ref[i])

        @pl.when(i == n_used - 1)
        def _():
            @pl.when(i >= 1)
            def _():
                wait_scatter(1 - slot, nvalid_ref[jnp.maximum(i - 1, 0)])
            wait_scatter(slot, nvalid_ref[i])


def moe_ffn(h_all, blk_e, nvalid, n_used, idx, row_gate, w1, b1, w2, b2):
    t, d = h_all.shape
    bm = MOE_BLOCK_ROWS
    n_blocks = idx.shape[0]
    grid_spec = pltpu.PrefetchScalarGridSpec(
        num_scalar_prefetch=3,
        grid=(n_blocks,),
        in_specs=[pl.BlockSpec(memory_space=pl.ANY),
                  pl.BlockSpec(memory_space=pl.ANY),
                  pl.BlockSpec((bm, 1), lambda i, be, nv, nu: (i, 0)),
                  pl.BlockSpec((1, d, 2 * D_FF), lambda i, be, nv, nu: (be[i], 0, 0)),
                  pl.BlockSpec((1, 1, 2 * D_FF), lambda i, be, nv, nu: (be[i], 0, 0)),
                  pl.BlockSpec((1, D_FF, d), lambda i, be, nv, nu: (be[i], 0, 0)),
                  pl.BlockSpec((1, 1, d), lambda i, be, nv, nu: (be[i], 0, 0))],
        out_specs=pl.BlockSpec(memory_space=pl.ANY),
        scratch_shapes=[pltpu.SMEM((3, bm), jnp.int32),
                        pltpu.VMEM((2, bm, d), F32),
                        pltpu.VMEM((2, bm, d), F32),
                        pltpu.VMEM((d, 2 * D_FF), BF16),
                        pltpu.VMEM((D_FF, d), BF16),
                        pltpu.SemaphoreType.DMA((3,)),
                        pltpu.SemaphoreType.DMA((2,)),
                        pltpu.SemaphoreType.DMA((2,))],
    )
    return pl.pallas_call(
        _ffn_body,
        out_shape=jax.ShapeDtypeStruct((t * TOP_K, d), F32),
        grid_spec=grid_spec,
        compiler_params=_cparams(("arbitrary",), V7X_VMEM_LIMIT_BYTES),
        name="moe_ffn",
    )(blk_e, nvalid, n_used, idx, h_all, row_gate, w1, b1.reshape(N_EXPERTS, 1, -1), w2,
      b2.reshape(N_EXPERTS, 1, -1))


def _moe_combine_body(x_ref, g_ref, y_ref, o_ref):
    y = y_ref[...]
    d = D_MODEL
    acc = (y[:, 0:d] + y[:, d:2 * d]) + (y[:, 2 * d:3 * d] + y[:, 3 * d:4 * d])
    o_ref[...] = x_ref[...] + g_ref[0] * acc


def moe_combine(x, gate, y4, row_off, tm):
    t, d = x.shape
    m, r, _ = gate.shape
    rows_per_mod = t // m
    off = row_off // tm
    return pl.pallas_call(
        _moe_combine_body,
        out_shape=jax.ShapeDtypeStruct((t, d), F32),
        grid=(t // tm,),
        in_specs=[pl.BlockSpec((tm, d), lambda i: (i, 0)),
                  pl.BlockSpec((1, r, d), lambda i: ((i * tm) // rows_per_mod, 0, 0)),
                  pl.BlockSpec((tm, TOP_K * d), lambda i: (i + off, 0))],
        out_specs=pl.BlockSpec((tm, d), lambda i: (i, 0)),
        compiler_params=_cparams(("arbitrary",)),
        name="moe_combine",
    )(x, gate, y4)


def _moe_routing(top_e, gate):
    bm = MOE_BLOCK_ROWS
    t = top_e.shape[0]
    tk = t * TOP_K
    e_flat = top_e.reshape(-1)
    order = jnp.argsort(e_flat).astype(jnp.int32)
    e_sorted = e_flat[order]
    counts = jnp.bincount(e_flat, length=N_EXPERTS).astype(jnp.int32)
    padded = (counts + bm - 1) // bm * bm
    pad_end = jnp.cumsum(padded)
    pad_start = pad_end - padded
    start = jnp.cumsum(counts) - counts
    dest = pad_start[e_sorted] + jnp.arange(tk, dtype=jnp.int32) - start[e_sorted]
    n_blocks = -(-tk // bm) + N_EXPERTS
    n_rows = n_blocks * bm
    idx = jnp.full((n_rows,), -1, jnp.int32).at[dest].set(order)
    row_gate = jnp.zeros((n_rows,), F32).at[dest].set(gate.reshape(-1)[order])
    blk_start = jnp.arange(n_blocks, dtype=jnp.int32) * bm
    blk_e = jnp.minimum(jnp.searchsorted(pad_end, blk_start, side='right'), N_EXPERTS - 1).astype(jnp.int32)
    nvalid = jnp.clip(counts[blk_e] - (blk_start - pad_start[blk_e]), 0, bm).astype(jnp.int32)
    n_used = (pad_end[-1] // bm).astype(jnp.int32).reshape(1)
    return blk_e, nvalid, n_used, idx.reshape(n_blocks, bm), row_gate.reshape(n_rows, 1)


def _rmsnorm(x, w):
    xf = x.astype(F32)
    y = xf * lax.rsqrt(jnp.mean(xf * xf, axis=-1, keepdims=True) + NORM_EPS)
    return y * w.astype(F32)


def _causal_conv(x, prev, w, b):
    width = w.shape[0]
    L = x.shape[1]
    xp = jnp.concatenate([prev.astype(x.dtype), x], axis=1)
    y = b
    for k in range(width):
        y = y + xp[:, k:k + L] * w[k]
    return y, xp[:, xp.shape[1] - (width - 1):]


def _ssd_scan(x, dt, A, Bm, Cm, h0):
    b, L, H, P = x.shape
    G, N = Bm.shape[2], Bm.shape[3]
    K = H // G
    q = math.gcd(L, SSD_CHUNK)
    nc = L // q
    a = (dt * A).reshape(b, nc, q, G, K)
    xdt = (x * dt[..., None]).reshape(b, nc, q, G, K, P)
    Bc = Bm.reshape(b, nc, q, G, N)
    Cc = Cm.reshape(b, nc, q, G, N)
    a_cs = jnp.cumsum(a, axis=2)
    causal = jnp.tril(jnp.ones((q, q), dtype=bool))[None, None, :, :, None, None]
    seg = a_cs[:, :, :, None] - a_cs[:, :, None, :]
    decay = jnp.exp(jnp.where(causal, seg, -jnp.inf))
    cb = jnp.einsum('bctgn,bcsgn->bctsg', Cc, Bc)
    y_diag = jnp.einsum('bctsgk,bcsgkp->bctgkp', cb[..., None] * decay, xdt)
    to_end = jnp.exp(a_cs[:, :, -1:] - a_cs)
    states = jnp.einsum('bcsgn,bcsgkp->bcgkpn', Bc, xdt * to_end[..., None])
    chunk_decay = jnp.exp(a_cs[:, :, -1])

    def step(h, inp):
        st, dec = inp
        return h * dec[..., None, None] + st, h

    h_last, h_prev = lax.scan(step, h0.reshape(b, G, K, P, N),
                              (jnp.moveaxis(states, 1, 0), jnp.moveaxis(chunk_decay, 1, 0)))
    h_prev = jnp.moveaxis(h_prev, 0, 1)
    y_off = jnp.einsum('bctgn,bcgkpn->bctgkp', Cc, h_prev) * jnp.exp(a_cs)[..., None]
    return (y_diag + y_off).reshape(b, L, H, P), h_last.reshape(b, H, P, N)


def _rg_lru(x, h0, wa, ba, wi, bi, lam):
    b, L, _ = x.shape
    xb = x.reshape(b, L, RG_BLOCKS, RG_BLOCK_DIM)
    r = jax.nn.sigmoid(jnp.einsum('blnd,nde->blne', xb, wa).reshape(b, L, RG_WIDTH) + ba)
    i = jax.nn.sigmoid(jnp.einsum('blnd,nde->blne', xb, wi).reshape(b, L, RG_WIDTH) + bi)
    log_a = -RG_C * r * jax.nn.softplus(-lam.astype(F32))
    a = jnp.exp(log_a)
    u = jnp.sqrt(-jnp.expm1(2.0 * log_a)) * (i * x)
    u = u.at[:, 0].add(a[:, 0] * h0)

    def combine(e1, e2):
        return e1[0] * e2[0], e2[0] * e1[1] + e2[1]

    _, h = lax.associative_scan(combine, (a, u), axis=1)
    return h, h[:, -1]


def _rec_core(z, xbc, dt, gate, xr, conv_ssd0, ssm0, conv_rg0, rg0, conv_w, conv_b, dt_bias, a_log,
              d_skip, norm_w, rg_conv_w, rg_conv_b, wa, ba, wi, bi, lam):
    b, L, _ = z.shape
    xbc, conv_ssd1 = _causal_conv(xbc, conv_ssd0, conv_w, conv_b)
    xbc = jax.nn.silu(xbc)
    xs, Bm, Cm = jnp.split(xbc, [SSD_WIDTH, SSD_WIDTH + SSD_GROUPS * SSD_STATE], axis=-1)
    dt = jax.nn.softplus(dt + dt_bias)
    A = -jnp.exp(a_log)
    xh = xs.reshape(b, L, SSD_HEADS, SSD_HEAD_DIM)
    y, ssm1 = _ssd_scan(xh, dt, A, Bm.reshape(b, L, SSD_GROUPS, SSD_STATE),
                        Cm.reshape(b, L, SSD_GROUPS, SSD_STATE), ssm0)
    y = y + d_skip[:, None] * xh
    y = y.reshape(b, L, SSD_WIDTH) * jax.nn.silu(z)
    y = _rmsnorm(y.reshape(b, L, SSD_GROUPS, -1), norm_w.reshape(SSD_GROUPS, -1)).reshape(b, L, SSD_WIDTH)
    xr, conv_rg1 = _causal_conv(xr, conv_rg0, rg_conv_w, rg_conv_b)
    r_out, rg1 = _rg_lru(xr, rg0, wa, ba, wi, bi, lam)
    r_out = r_out * jax.nn.gelu(gate)
    return y, r_out, conv_ssd1, ssm1, conv_rg1, rg1


def _nsa_split(proj, b, L, q_norm, k_norm):
    q = proj[:, :NSA_Q_WIDTH]
    kv = proj[:, NSA_Q_WIDTH:NSA_Q_WIDTH + 6 * NSA_KV_WIDTH]
    g = proj[:, NSA_Q_WIDTH + 6 * NSA_KV_WIDTH:NSA_Q_WIDTH + 6 * NSA_KV_WIDTH + 3 * NSA_HEADS]
    q = _rmsnorm(q.reshape(b, L, NSA_KV_HEADS, NSA_GROUP, NSA_HEAD_DIM), q_norm) * (NSA_HEAD_DIM ** -0.5)
    kv = kv.reshape(b, L, 6, NSA_KV_HEADS, NSA_HEAD_DIM)
    k_slc = _rmsnorm(kv[:, :, 2], k_norm[1])
    k_win = _rmsnorm(kv[:, :, 4], k_norm[2])
    rows = jnp.stack([kv[:, :, 0], kv[:, :, 1], k_slc, kv[:, :, 3]], axis=2)
    win = jnp.stack([k_win, kv[:, :, 5]], axis=2)
    gates = jax.nn.sigmoid(g).reshape(b, L, NSA_KV_HEADS, NSA_GROUP, 3)
    return q, rows, win, gates


def _masked_softmax(s, mask):
    s = jnp.where(mask, s.astype(F32), -jnp.inf)
    m = jnp.max(s, axis=-1, keepdims=True)
    e = jnp.exp(s - jnp.where(jnp.isfinite(m), m, 0.0))
    d = jnp.sum(e, axis=-1, keepdims=True)
    return e / jnp.where(d > 0, d, 1.0)


def _compress(r, w1, pe, w2):
    b, T, G, dh = r.shape
    n_chunk = T // CMP_STRIDE
    nc = n_chunk - CMP_R + 1
    ch = r[:, :n_chunk * CMP_STRIDE].reshape(b, n_chunk, CMP_STRIDE, G, dh)
    proj = jnp.einsum('bcsgd,rsdh->bcrgh', ch, w1.reshape(CMP_R, CMP_STRIDE, dh, CMP_HID))
    hid = jnp.einsum('ld,ldh->h', pe, w1)
    for rr in range(CMP_R):
        hid = hid + proj[:, rr:rr + nc, rr]
    return jax.nn.gelu(hid) @ w2


def _nsa_context(rows, cmp_w1, cmp_pe, cmp_w2, k_norm_cmp):
    b, T = rows.shape[:2]
    kc = _rmsnorm(_compress(rows[:, :, 0], cmp_w1[0], cmp_pe[0], cmp_w2[0]), k_norm_cmp)
    vc = _compress(rows[:, :, 1], cmp_w1[1], cmp_pe[1], cmp_w2[1])
    ns = -(-T // SEL_BLOCK)
    sel = jnp.pad(rows[:, :, 2:4], ((0, 0), (0, ns * SEL_BLOCK - T), (0, 0), (0, 0), (0, 0)))
    sel = sel.reshape(b, ns, SEL_BLOCK, 2, NSA_KV_HEADS, NSA_HEAD_DIM).transpose(3, 0, 4, 1, 2, 5)
    return kc, vc, sel[0], sel[1]


def _overlap_matrix(nc, ns):
    i = np.arange(nc)[:, None]
    j = np.arange(ns)[None, :]
    ov = (i * CMP_STRIDE < (j + 1) * SEL_BLOCK) & (i * CMP_STRIDE + CMP_LEN > j * SEL_BLOCK)
    return ov.astype(np.float32)


def _nsa_attend(q, gates, t_pos, kc, vc, ks, vs, kw, vw, w_pos):
    b, Q, G, K, dh = q.shape
    nc = kc.shape[1]
    ns = ks.shape[2]
    tq = t_pos[None, :, None, None, None]
    c_end = jnp.arange(nc) * CMP_STRIDE + CMP_LEN - 1
    p_c = _masked_softmax(jnp.einsum('bqgkd,bngd->bqgkn', q, kc), c_end <= tq)
    o_c = jnp.einsum('bqgkn,bngd->bqgkd', p_c, vc)
    imp = jnp.einsum('bqgkn,ns->bqgs', p_c, jnp.asarray(_overlap_matrix(nc, ns)))
    jj = jnp.arange(ns)[None, :]
    jt = (t_pos // SEL_BLOCK)[:, None]
    valid = jj <= jt
    forced = valid & ((jj == 0) | (jj == jt) | (jj == jt - 1))
    imp = jnp.where(forced[None, :, None], jnp.inf, jnp.where(valid[None, :, None], imp, -jnp.inf))
    _, idx = lax.top_k(imp, min(SEL_TOPN, ns))
    n = idx.shape[-1]
    bi = jnp.arange(b)[:, None, None, None]
    gi = jnp.arange(G)[None, None, :, None]
    k_sel = ks[bi, gi, idx].reshape(b, Q, G, n * SEL_BLOCK, dh)
    v_sel = vs[bi, gi, idx].reshape(b, Q, G, n * SEL_BLOCK, dh)
    kpos = (idx[..., None] * SEL_BLOCK + jnp.arange(SEL_BLOCK)).reshape(b, Q, G, 1, n * SEL_BLOCK)
    p_s = _masked_softmax(jnp.einsum('bqgkd,bqgmd->bqgkm', q, k_sel), kpos <= tq)
    o_s = jnp.einsum('bqgkm,bqgmd->bqgkd', p_s, v_sel)
    m_w = (w_pos <= tq) & (w_pos > tq - WINDOW) & (w_pos >= 0)
    p_w = _masked_softmax(jnp.einsum('bqgkd,bwgd->bqgkw', q, kw), m_w)
    o_w = jnp.einsum('bqgkw,bwgd->bqgkd', p_w, vw)
    o = gates[..., 0:1] * o_c + gates[..., 1:2] * o_s + gates[..., 2:3] * o_w
    return o.reshape(b, Q, G * K * dh)


def _nsa_prompt_core(proj, b, L, q_norm, k_norm, cmp_w1, cmp_pe, cmp_w2):
    q, rows, win, gates = _nsa_split(proj, b, L, q_norm, k_norm)
    kc, vc, ks, vs = _nsa_context(rows, cmp_w1, cmp_pe, cmp_w2, k_norm[0])
    win_pad = jnp.pad(win, ((0, 0), (WINDOW, 0), (0, 0), (0, 0), (0, 0)))

    def block(i):
        s = i * NSA_Q_BLOCK
        qb = lax.dynamic_slice_in_dim(q, s, NSA_Q_BLOCK, axis=1)
        gb = lax.dynamic_slice_in_dim(gates, s, NSA_Q_BLOCK, axis=1)
        wb = lax.dynamic_slice_in_dim(win_pad, s, WINDOW + NSA_Q_BLOCK, axis=1)
        t_pos = s + jnp.arange(NSA_Q_BLOCK)
        w_pos = s - WINDOW + jnp.arange(WINDOW + NSA_Q_BLOCK)
        return _nsa_attend(qb, gb, t_pos, kc, vc, ks, vs, wb[:, :, 0], wb[:, :, 1], w_pos)

    o = lax.map(block, jnp.arange(L // NSA_Q_BLOCK))
    o = jnp.moveaxis(o, 0, 1).reshape(b, L, NSA_Q_WIDTH)
    return o, rows, win[:, L - min(WINDOW, L):]


def _nsa_sample_core(proj, b, L, cache, page_table, win_buf, q_norm, k_norm, cmp_w1, cmp_pe, cmp_w2):
    q, rows, win, gates = _nsa_split(proj, b, L, q_norm, k_norm)
    past = page_table.shape[1] * cache.shape[1]
    past_rows = cache[page_table].reshape(b, past, 4, NSA_KV_HEADS, NSA_HEAD_DIM)
    kc, vc, ks, vs = _nsa_context(jnp.concatenate([past_rows, rows], axis=1),
                                  cmp_w1, cmp_pe, cmp_w2, k_norm[0])
    wb_len = win_buf.shape[1]
    wk = jnp.concatenate([win_buf, win], axis=1)
    t_pos = past + jnp.arange(L)
    w_pos = past - wb_len + jnp.arange(wb_len + L)
    o = _nsa_attend(q, gates, t_pos, kc, vc, ks, vs, wk[:, :, 0], wk[:, :, 1], w_pos)
    return o, rows, wk[:, L:]


def _pad_cols(w, n):
    return jnp.pad(w, ((0, 0), (0, n - w.shape[1])))


def kernel(x_prompt, x_sample, cache_nsa_kv, state_nsa_win, state_ssd_conv, state_ssd, state_rg_conv, state_rg, page_table, c_prompt, c_sample, ada_w, ada_b, norm_mix, norm_ffn, rec_w_in, ssd_conv_w, ssd_conv_b, ssd_dt_bias, ssd_a_log, ssd_d, ssd_norm_w, rg_conv_w, rg_conv_b, rg_wa, rg_ba, rg_wi, rg_bi, rg_lambda, rec_w_out, nsa_w_in, nsa_q_norm, nsa_k_norm, cmp_w1, cmp_pe, cmp_w2, nsa_w_out, router_w, router_b, moe_w1, moe_b1, moe_w2, moe_b2):
    bp, L, d = x_prompt.shape
    bs = x_sample.shape[0]
    depth = ada_w.shape[0]
    tp = bp * L
    xp = x_prompt.reshape(tp, d)
    xs = x_sample.reshape(bs, d)

    n_c = bp + bs
    n_c_pad = -(-n_c // 8) * 8
    c_all = jnp.pad(jnp.concatenate([c_prompt, c_sample], axis=0), ((0, n_c_pad - n_c), (0, 0)))
    ada_w_cat = jnp.concatenate([ada_w[i] for i in range(depth)], axis=1).astype(BF16)
    ada_b_cat = jnp.concatenate([ada_b[i] for i in range(depth)], axis=0)[None, :]
    mod_all = adaln_mod(c_all, ada_w_cat, ada_b_cat)

    outs = {k: [] for k in ('rows_p', 'rows_s', 'win_p', 'win_s', 'sconv_p', 'sconv_s', 'ssm_p', 'ssm_s',
                            'rconv_p', 'rconv_s', 'rg_p', 'rg_s')}

    for i in range(depth):
        j = i // 2
        mod_i = mod_all[:, i * 6 * d:(i + 1) * 6 * d]
        mp = [mod_i[:bp, k * d:(k + 1) * d].reshape(bp, 1, d) for k in range(6)]
        ms = [mod_i[bp:bp + bs, k * d:(k + 1) * d].reshape(1, bs, d) for k in range(6)]
        g_mix = norm_mix[i][None, :]
        g_ffn = norm_ffn[i][None, :]

        if i % 2 == 0:
            w_in = rec_w_in[j]
            s0, s1, s2, s3 = 1024, 1024 + SSD_XBC, 1024 + SSD_XBC + SSD_HEADS, 1024 + SSD_XBC + SSD_HEADS + RG_WIDTH
            w_cat = jnp.concatenate([w_in[:, :s1], w_in[:, s2:], _pad_cols(w_in[:, s1:s2], 512)], axis=1).astype(BF16)
            proj_p = mod_matmul(xp, g_mix, mp[1], mp[0], w_cat, ROW_TILE, 512)
            proj_s = mod_matmul(xs, g_mix, ms[1], ms[0], w_cat, bs, 512)

            def split(pr, b_, l_):
                pr = pr.reshape(b_, l_, -1)
                return (pr[..., :1024], pr[..., 1024:3072], pr[..., 5120:5120 + SSD_HEADS],
                        pr[..., 3072:4096], pr[..., 4096:5120])

            wts = (ssd_conv_w[j], ssd_conv_b[j], ssd_dt_bias[j], ssd_a_log[j], ssd_d[j], ssd_norm_w[j],
                   rg_conv_w[j], rg_conv_b[j], rg_wa[j], rg_ba[j], rg_wi[j], rg_bi[j], rg_lambda[j])
            yp, rp, a1, a2, a3, a4 = _rec_core(
                *split(proj_p, bp, L),
                jnp.zeros((bp, SSD_CONV - 1, SSD_XBC), F32),
                jnp.zeros((bp, SSD_HEADS, SSD_HEAD_DIM, SSD_STATE), F32),
                jnp.zeros((bp, 3, RG_WIDTH), F32), jnp.zeros((bp, RG_WIDTH), F32), *wts)
            ys, rs, b1_, b2_, b3_, b4_ = _rec_core(
                *split(proj_s, bs, 1), state_ssd_conv[j], state_ssd[j], state_rg_conv[j], state_rg[j], *wts)
            outs['sconv_p'].append(a1); outs['ssm_p'].append(a2); outs['rconv_p'].append(a3); outs['rg_p'].append(a4)
            outs['sconv_s'].append(b1_); outs['ssm_s'].append(b2_); outs['rconv_s'].append(b3_); outs['rg_s'].append(b4_)
            w_out = rec_w_out[j].astype(BF16)
            w_parts = [w_out[:SSD_WIDTH], w_out[SSD_WIDTH:]]
            xp = mm_residual([yp.reshape(tp, -1), rp.reshape(tp, -1)], w_parts, xp, mp[2], ROW_TILE)
            xs = mm_residual([ys.reshape(bs, -1), rs.reshape(bs, -1)], w_parts, xs, ms[2], bs)
        else:
            w_in = nsa_w_in[j]
            w_cat = _pad_cols(w_in, 3072).astype(BF16)
            proj_p = mod_matmul(xp, g_mix, mp[1], mp[0], w_cat, ROW_TILE, 512)
            proj_s = mod_matmul(xs, g_mix, ms[1], ms[0], w_cat, bs, 512)
            wts = (nsa_q_norm[j], nsa_k_norm[j], cmp_w1[j], cmp_pe[j], cmp_w2[j])
            op, rp, wp = _nsa_prompt_core(proj_p, bp, L, *wts)
            osm, rs, ws = _nsa_sample_core(proj_s, bs, 1, cache_nsa_kv[j], page_table, state_nsa_win[j], *wts)
            outs['rows_p'].append(rp); outs['win_p'].append(wp); outs['rows_s'].append(rs); outs['win_s'].append(ws)
            w_out = nsa_w_out[j].astype(BF16)
            xp = mm_residual([op.reshape(tp, -1)], [w_out], xp, mp[2], ROW_TILE)
            xs = mm_residual([osm.reshape(bs, -1)], [w_out], xs, ms[2], bs)

        rw = _pad_cols(router_w[i], V7X_LANES)
        rwh = rw.astype(BF16)
        rwl = (rw - rwh.astype(F32)).astype(BF16)
        rb = jnp.concatenate([router_b[i], jnp.full((V7X_LANES - N_EXPERTS,), -1e30, F32)])[None, :]
        h_p, e_p, gt_p = moe_router(xp, g_ffn, mp[4], mp[3], rwh, rwl, rb, ROW_TILE)
        h_s, e_s, gt_s = moe_router(xs, g_ffn, ms[4], ms[3], rwh, rwl, rb, bs)
        h_all = jnp.concatenate([h_p, h_s], axis=0)
        top_e = jnp.concatenate([e_p[:, :TOP_K], e_s[:, :TOP_K]], axis=0)
        gate = jnp.concatenate([gt_p[:, :TOP_K], gt_s[:, :TOP_K]], axis=0)
        blk_e, nvalid, n_used, idx, row_gate = _moe_routing(top_e, gate)
        y4 = moe_ffn(h_all, blk_e, nvalid, n_used, idx, row_gate, moe_w1[i], moe_b1[i], moe_w2[i], moe_b2[i])
        y4 = y4.reshape(tp + bs, TOP_K * d)
        xp = moe_combine(xp, mp[5], y4, 0, 256)
        xs = moe_combine(xs, ms[5], y4, tp, bs)

    st = lambda k: jnp.stack(outs[k])
    return (xp.reshape(bp, L, d), xs.reshape(bs, 1, d), st('rows_p'), st('rows_s'), st('win_p'), st('win_s'),
            st('sconv_p'), st('sconv_s'), st('ssm_p'), st('ssm_s'), st('rconv_p'), st('rconv_s'),
            st('rg_p'), st('rg_s'))
```

```python
import functools
import math

import jax
import jax.numpy as jnp
import numpy as np
from jax import lax
from jax.experimental import pallas as pl
from jax.experimental.pallas import tpu as pltpu

F32 = jnp.float32
BF16 = jnp.bfloat16

D_MODEL = 1024
NORM_EPS = 1e-6

SSD_WIDTH = 1024
SSD_HEAD_DIM = 64
SSD_HEADS = 16
SSD_GROUPS = 4
SSD_STATE = 128
SSD_CONV = 4
SSD_CHUNK = 128
SSD_XBC = SSD_WIDTH + 2 * SSD_GROUPS * SSD_STATE

RG_WIDTH = 1024
RG_BLOCKS = 16
RG_BLOCK_DIM = 64
RG_C = 8.0

NSA_HEADS = 16
NSA_KV_HEADS = 4
NSA_HEAD_DIM = 64
NSA_GROUP = 4
NSA_Q_WIDTH = 1024
NSA_KV_WIDTH = 256
CMP_LEN = 32
CMP_STRIDE = 16
CMP_R = 2
CMP_HID = 128
SEL_BLOCK = 64
SEL_TOPN = 16
WINDOW = 512
NSA_Q_BLOCK = 64

N_EXPERTS = 32
TOP_K = 4
D_FF = 1024
SWIGLU_LIMIT = 7.0
SWIGLU_ALPHA = 1.702

V7X_LANES = 128
V7X_VMEM_LIMIT_BYTES = 56 * 1024 * 1024

MOE_BLOCK_ROWS = 256
ROW_TILE = 512


def _cparams(sem, vmem=None):
    return pltpu.CompilerParams(dimension_semantics=sem, vmem_limit_bytes=vmem)


def _adaln_body(c_ref, w_ref, b_ref, o_ref):
    c = c_ref[...]
    s = c * jax.nn.sigmoid(c)
    o_ref[...] = jnp.dot(s.astype(BF16), w_ref[...], preferred_element_type=F32) + b_ref[...]


def adaln_mod(c, w_bf, b):
    r, d = c.shape
    n = w_bf.shape[1]
    tn = 1536
    return pl.pallas_call(
        _adaln_body,
        out_shape=jax.ShapeDtypeStruct((r, n), F32),
        grid=(n // tn,),
        in_specs=[pl.BlockSpec((r, d), lambda j: (0, 0)),
                  pl.BlockSpec((d, tn), lambda j: (0, j)),
                  pl.BlockSpec((1, tn), lambda j: (0, j))],
        out_specs=pl.BlockSpec((r, tn), lambda j: (0, j)),
        compiler_params=_cparams(("arbitrary",)),
        name="adaln_mod",
    )(c, w_bf, b)


def _modulated(x, g, scale, shift):
    ms = jnp.mean(x * x, axis=-1, keepdims=True)
    y = x * lax.rsqrt(ms + NORM_EPS) * g
    return y * (1.0 + scale) + shift


def _mod_matmul_body(x_ref, g_ref, sc_ref, sh_ref, w_ref, o_ref, h_scr):
    @pl.when(pl.program_id(1) == 0)
    def _():
        h_scr[...] = _modulated(x_ref[...], g_ref[...], sc_ref[0], sh_ref[0]).astype(BF16)

    o_ref[...] = jnp.dot(h_scr[...], w_ref[...], preferred_element_type=F32)


def mod_matmul(x, g, scale, shift, w_bf, tm, tn):
    t, d = x.shape
    n = w_bf.shape[1]
    m, r, _ = scale.shape
    rows_per_mod = t // m
    mod_spec = pl.BlockSpec((1, r, d), lambda i, j: ((i * tm) // rows_per_mod, 0, 0))
    return pl.pallas_call(
        _mod_matmul_body,
        out_shape=jax.ShapeDtypeStruct((t, n), F32),
        grid=(t // tm, n // tn),
        in_specs=[pl.BlockSpec((tm, d), lambda i, j: (i, 0)),
                  pl.BlockSpec((1, d), lambda i, j: (0, 0)),
                  mod_spec, mod_spec,
                  pl.BlockSpec((d, tn), lambda i, j: (0, j))],
        out_specs=pl.BlockSpec((tm, tn), lambda i, j: (i, j)),
        scratch_shapes=[pltpu.VMEM((tm, d), BF16)],
        compiler_params=_cparams(("arbitrary", "arbitrary")),
        name="mod_matmul",
    )(x, g, scale, shift, w_bf)


def _mm_residual_body(n_a, *refs):
    a_refs = refs[:n_a]
    w_refs = refs[n_a:2 * n_a]
    x_ref, g_ref, o_ref = refs[2 * n_a:]
    acc = None
    for a_ref, w_ref in zip(a_refs, w_refs):
        p = jnp.dot(a_ref[...].astype(BF16), w_ref[...], preferred_element_type=F32)
        acc = p if acc is None else acc + p
    o_ref[...] = x_ref[...] + g_ref[0] * acc


def mm_residual(a_list, w_list, x, gate, tm):
    t, d = x.shape
    m, r, _ = gate.shape
    rows_per_mod = t // m
    in_specs = [pl.BlockSpec((tm, a.shape[1]), lambda i: (i, 0)) for a in a_list]
    in_specs += [pl.BlockSpec(w.shape, lambda i: (0, 0)) for w in w_list]
    in_specs += [pl.BlockSpec((tm, d), lambda i: (i, 0)),
                 pl.BlockSpec((1, r, d), lambda i: ((i * tm) // rows_per_mod, 0, 0))]
    return pl.pallas_call(
        functools.partial(_mm_residual_body, len(a_list)),
        out_shape=jax.ShapeDtypeStruct((t, d), F32),
        grid=(t // tm,),
        in_specs=in_specs,
        out_specs=pl.BlockSpec((tm, d), lambda i: (i, 0)),
        compiler_params=_cparams(("arbitrary",)),
        name="mm_residual",
    )(*a_list, *w_list, x, gate)


def _router_body(x_ref, g_ref, sc_ref, sh_ref, wh_ref, wl_ref, rb_ref, h_ref, e_ref, gt_ref):
    h = _modulated(x_ref[...], g_ref[...], sc_ref[0], sh_ref[0])
    h_ref[...] = h
    hh = h.astype(BF16)
    hl = (h - hh.astype(F32)).astype(BF16)
    wh = wh_ref[...]
    wl = wl_ref[...]
    logits = (jnp.dot(hh, wh, preferred_element_type=F32)
              + (jnp.dot(hh, wl, preferred_element_type=F32)
                 + jnp.dot(hl, wh, preferred_element_type=F32))) + rb_ref[...]
    lane = lax.broadcasted_iota(jnp.int32, logits.shape, 1)
    neg = jnp.float32(-jnp.inf)
    vals, idxs = [], []
    cur = logits
    for _ in range(TOP_K):
        m = jnp.max(cur, axis=-1, keepdims=True)
        idx = jnp.min(jnp.where(cur == m, lane, V7X_LANES), axis=-1, keepdims=True)
        vals.append(m)
        idxs.append(idx)
        cur = jnp.where(lane == idx, neg, cur)
    exps = [jnp.exp(v - vals[0]) for v in vals]
    den = exps[0] + exps[1] + exps[2] + exps[3]
    e_out = jnp.zeros(logits.shape, jnp.int32)
    g_out = jnp.zeros(logits.shape, F32)
    for k in range(TOP_K):
        e_out = jnp.where(lane == k, idxs[k], e_out)
        g_out = jnp.where(lane == k, exps[k] / den, g_out)
    e_ref[...] = e_out
    gt_ref[...] = g_out


def moe_router(x, g, scale, shift, wh, wl, rb, tm):
    t, d = x.shape
    m, r, _ = scale.shape
    rows_per_mod = t // m
    mod_spec = pl.BlockSpec((1, r, d), lambda i: ((i * tm) // rows_per_mod, 0, 0))
    return pl.pallas_call(
        _router_body,
        out_shape=(jax.ShapeDtypeStruct((t, d), F32),
                   jax.ShapeDtypeStruct((t, V7X_LANES), jnp.int32),
                   jax.ShapeDtypeStruct((t, V7X_LANES), F32)),
        grid=(t // tm,),
        in_specs=[pl.BlockSpec((tm, d), lambda i: (i, 0)),
                  pl.BlockSpec((1, d), lambda i: (0, 0)),
                  mod_spec, mod_spec,
                  pl.BlockSpec((d, V7X_LANES), lambda i: (0, 0)),
                  pl.BlockSpec((d, V7X_LANES), lambda i: (0, 0)),
                  pl.BlockSpec((1, V7X_LANES), lambda i: (0, 0))],
        out_specs=(pl.BlockSpec((tm, d), lambda i: (i, 0)),
                   pl.BlockSpec((tm, V7X_LANES), lambda i: (i, 0)),
                   pl.BlockSpec((tm, V7X_LANES), lambda i: (i, 0))),
        compiler_params=_cparams(("arbitrary",)),
        name="moe_router",
    )(x, g, scale, shift, wh, wl, rb)


def _ffn_body(blk_e_ref, nvalid_ref, nused_ref,
              idx_hbm, h_hbm, gate_ref, w1_ref, b1_ref, w2_ref, b2_ref,
              out_hbm,
              idx_smem, xbuf, ybuf, w1bf, w2bf, sem_idx, sem_g, sem_s):
    bm = MOE_BLOCK_ROWS
    i = pl.program_id(0)
    n_used = nused_ref[0]

    def idx_copy(blk, slot):
        return pltpu.make_async_copy(idx_hbm.at[blk], idx_smem.at[slot], sem_idx.at[slot])

    def gather_copy(tok, slot, r):
        return pltpu.make_async_copy(h_hbm.at[pl.ds(tok, 1)], xbuf.at[slot, pl.ds(r, 1)], sem_g.at[slot])

    def scatter_copy(dst, slot, r):
        return pltpu.make_async_copy(ybuf.at[slot, pl.ds(r, 1)], out_hbm.at[pl.ds(dst, 1)], sem_s.at[slot])

    def start_gather(islot, slot):
        def body(r, c):
            tok = lax.shift_right_logical(jnp.maximum(idx_smem[islot, r], 0), 2)
            gather_copy(tok, slot, r).start()
            return c
        lax.fori_loop(0, bm, body, 0)

    def wait_gather(slot):
        def body(r, c):
            gather_copy(0, slot, r).wait()
            return c
        lax.fori_loop(0, bm, body, 0)

    def start_scatter(islot, slot, n):
        def body(r, c):
            scatter_copy(idx_smem[islot, r], slot, r).start()
            return c
        lax.fori_loop(0, n, body, 0)

    def wait_scatter(slot, n):
        def body(r, c):
            scatter_copy(0, slot, r).wait()
            return c
        lax.fori_loop(0, n, body, 0)

    @pl.when(i < n_used)
    def _():
        slot = i % 2
        islot = i % 3

        @pl.when(i == 0)
        def _():
            idx_copy(0, 0).start()
            idx_copy(0, 0).wait()
            start_gather(0, 0)

            @pl.when(n_used > 1)
            def _():
                idx_copy(1, 1).start()

        @pl.when(i + 2 < n_used)
        def _():
            idx_copy(i + 2, (i + 2) % 3).start()

        @pl.when(i + 1 < n_used)
        def _():
            idx_copy(i + 1, (i + 1) % 3).wait()
            start_gather((i + 1) % 3, 1 - slot)

        @pl.when(jnp.logical_or(i == 0, blk_e_ref[i] != blk_e_ref[jnp.maximum(i - 1, 0)]))
        def _():
            w1bf[...] = w1_ref[0].astype(BF16)
            w2bf[...] = w2_ref[0].astype(BF16)

        wait_gather(slot)

        @pl.when(i >= 2)
        def _():
            wait_scatter(slot, nvalid_ref[jnp.maximum(i - 2, 0)])

        x = xbuf[slot].astype(BF16)
        u = jnp.dot(x, w1bf[...], preferred_element_type=F32) + b1_ref[0]
        gl = jnp.minimum(u[:, :D_FF], SWIGLU_LIMIT)
        lin = jnp.clip(u[:, D_FF:], -SWIGLU_LIMIT, SWIGLU_LIMIT)
        act = gl * jax.nn.sigmoid(SWIGLU_ALPHA * gl) * (lin + 1.0)
        y = jnp.dot(act.astype(BF16), w2bf[...], preferred_element_type=F32) + b2_ref[0]
        ybuf[slot] = y * gate_ref[...]
        start_scatter(islot, slot, nvalid_ref[i])

        @pl.when(i == n_used - 1)
        def _():
            @pl.when(i >= 1)
            def _():
                wait_scatter(1 - slot, nvalid_ref[jnp.maximum(i - 1, 0)])
            wait_scatter(slot, nvalid_ref[i])


def moe_ffn(h_all, blk_e, nvalid, n_used, idx, row_gate, w1, b1, w2, b2):
    t, d = h_all.shape
    bm = MOE_BLOCK_ROWS
    n_blocks = idx.shape[0]
    grid_spec = pltpu.PrefetchScalarGridSpec(
        num_scalar_prefetch=3,
        grid=(n_blocks,),
        in_specs=[pl.BlockSpec(memory_space=pl.ANY),
                  pl.BlockSpec(memory_space=pl.ANY),
                  pl.BlockSpec((bm, 1), lambda i, be, nv, nu: (i, 0)),
                  pl.BlockSpec((1, d, 2 * D_FF), lambda i, be, nv, nu: (be[i], 0, 0)),
                  pl.BlockSpec((1, 1, 2 * D_FF), lambda i, be, nv, nu: (be[i], 0, 0)),
                  pl.BlockSpec((1, D_FF, d), lambda i, be, nv, nu: (be[i], 0, 0)),
                  pl.BlockSpec((1, 1, d), lambda i, be, nv, nu: (be[i], 0, 0))],
        out_specs=pl.BlockSpec(memory_space=pl.ANY),
        scratch_shapes=[pltpu.SMEM((3, bm), jnp.int32),
                        pltpu.VMEM((2, bm, d), F32),
                        pltpu.VMEM((2, bm, d), F32),
                        pltpu.VMEM((d, 2 * D_FF), BF16),
                        pltpu.VMEM((D_FF, d), BF16),
                        pltpu.SemaphoreType.DMA((3,)),
                        pltpu.SemaphoreType.DMA((2,)),
                        pltpu.SemaphoreType.DMA((2,))],
    )
    return pl.pallas_call(
        _ffn_body,
        out_shape=jax.ShapeDtypeStruct((t * TOP_K, d), F32),
        grid_spec=grid_spec,
        compiler_params=_cparams(("arbitrary",), V7X_VMEM_LIMIT_BYTES),
        name="moe_ffn",
    )(blk_e, nvalid, n_used, idx, h_all, row_gate, w1, b1.reshape(N_EXPERTS, 1, -1), w2,
      b2.reshape(N_EXPERTS, 1, -1))


def _moe_combine_body(x_ref, g_ref, y_ref, o_ref):
    y = y_ref[...]
    d = D_MODEL
    acc = (y[:, 0:d] + y[:, d:2 * d]) + (y[:, 2 * d:3 * d] + y[:, 3 * d:4 * d])
    o_ref[...] = x_ref[...] + g_ref[0] * acc


def moe_combine(x, gate, y4, row_off, tm):
    t, d = x.shape
    m, r, _ = gate.shape
    rows_per_mod = t // m
    off = row_off // tm
    return pl.pallas_call(
        _moe_combine_body,
        out_shape=jax.ShapeDtypeStruct((t, d), F32),
        grid=(t // tm,),
        in_specs=[pl.BlockSpec((tm, d), lambda i: (i, 0)),
                  pl.BlockSpec((1, r, d), lambda i: ((i * tm) // rows_per_mod, 0, 0)),
                  pl.BlockSpec((tm, TOP_K * d), lambda i: (i + off, 0))],
        out_specs=pl.BlockSpec((tm, d), lambda i: (i, 0)),
        compiler_params=_cparams(("arbitrary",)),
        name="moe_combine",
    )(x, gate, y4)


def _moe_routing(top_e, gate):
    bm = MOE_BLOCK_ROWS
    t = top_e.shape[0]
    tk = t * TOP_K
    e_flat = top_e.reshape(-1)
    order = jnp.argsort(e_flat).astype(jnp.int32)
    e_sorted = e_flat[order]
    counts = jnp.bincount(e_flat, length=N_EXPERTS).astype(jnp.int32)
    padded = (counts + bm - 1) // bm * bm
    pad_end = jnp.cumsum(padded)
    pad_start = pad_end - padded
    start = jnp.cumsum(counts) - counts
    dest = pad_start[e_sorted] + jnp.arange(tk, dtype=jnp.int32) - start[e_sorted]
    n_blocks = -(-tk // bm) + N_EXPERTS
    n_rows = n_blocks * bm
    idx = jnp.full((n_rows,), -1, jnp.int32).at[dest].set(order)
    row_gate = jnp.zeros((n_rows,), F32).at[dest].set(gate.reshape(-1)[order])
    blk_start = jnp.arange(n_blocks, dtype=jnp.int32) * bm
    blk_e = jnp.minimum(jnp.searchsorted(pad_end, blk_start, side='right'), N_EXPERTS - 1).astype(jnp.int32)
    nvalid = jnp.clip(counts[blk_e] - (blk_start - pad_start[blk_e]), 0, bm).astype(jnp.int32)
    n_used = (pad_end[-1] // bm).astype(jnp.int32).reshape(1)
    return blk_e, nvalid, n_used, idx.reshape(n_blocks, bm), row_gate.reshape(n_rows, 1)


def _rmsnorm(x, w):
    xf = x.astype(F32)
    y = xf * lax.rsqrt(jnp.mean(xf * xf, axis=-1, keepdims=True) + NORM_EPS)
    return y * w.astype(F32)


def _causal_conv(x, prev, w, b):
    width = w.shape[0]
    L = x.shape[1]
    xp = jnp.concatenate([prev.astype(x.dtype), x], axis=1)
    y = b
    for k in range(width):
        y = y + xp[:, k:k + L] * w[k]
    return y, xp[:, xp.shape[1] - (width - 1):]


def _ssd_scan(x, dt, A, Bm, Cm, h0):
    b, L, H, P = x.shape
    G, N = Bm.shape[2], Bm.shape[3]
    K = H // G
    q = math.gcd(L, SSD_CHUNK)
    nc = L // q
    a = (dt * A).reshape(b, nc, q, G, K)
    xdt = (x * dt[..., None]).reshape(b, nc, q, G, K, P)
    Bc = Bm.reshape(b, nc, q, G, N)
    Cc = Cm.reshape(b, nc, q, G, N)
    a_cs = jnp.cumsum(a, axis=2)
    causal = jnp.tril(jnp.ones((q, q), dtype=bool))[None, None, :, :, None, None]
    seg = a_cs[:, :, :, None] - a_cs[:, :, None, :]
    decay = jnp.exp(jnp.where(causal, seg, -jnp.inf))
    cb = jnp.einsum('bctgn,bcsgn->bctsg', Cc, Bc)
    y_diag = jnp.einsum('bctsgk,bcsgkp->bctgkp', cb[..., None] * decay, xdt)
    to_end = jnp.exp(a_cs[:, :, -1:] - a_cs)
    states = jnp.einsum('bcsgn,bcsgkp->bcgkpn', Bc, xdt * to_end[..., None])
    chunk_decay = jnp.exp(a_cs[:, :, -1])

    def step(h, inp):
        st, dec = inp
        return h * dec[..., None, None] + st, h

    h_last, h_prev = lax.scan(step, h0.reshape(b, G, K, P, N),
                              (jnp.moveaxis(states, 1, 0), jnp.moveaxis(chunk_decay, 1, 0)))
    h_prev = jnp.moveaxis(h_prev, 0, 1)
    y_off = jnp.einsum('bctgn,bcgkpn->bctgkp', Cc, h_prev) * jnp.exp(a_cs)[..., None]
    return (y_diag + y_off).reshape(b, L, H, P), h_last.reshape(b, H, P, N)


def _rg_lru(x, h0, wa, ba, wi, bi, lam):
    b, L, _ = x.shape
    xb = x.reshape(b, L, RG_BLOCKS, RG_BLOCK_DIM)
    r = jax.nn.sigmoid(jnp.einsum('blnd,nde->blne', xb, wa).reshape(b, L, RG_WIDTH) + ba)
    i = jax.nn.sigmoid(jnp.einsum('blnd,nde->blne', xb, wi).reshape(b, L, RG_WIDTH) + bi)
    log_a = -RG_C * r * jax.nn.softplus(-lam.astype(F32))
    a = jnp.exp(log_a)
    u = jnp.sqrt(-jnp.expm1(2.0 * log_a)) * (i * x)
    u = u.at[:, 0].add(a[:, 0] * h0)

    def combine(e1, e2):
        return e1[0] * e2[0], e2[0] * e1[1] + e2[1]

    _, h = lax.associative_scan(combine, (a, u), axis=1)
    return h, h[:, -1]


def _rec_core(z, xbc, dt, gate, xr, conv_ssd0, ssm0, conv_rg0, rg0, conv_w, conv_b, dt_bias, a_log,
              d_skip, norm_w, rg_conv_w, rg_conv_b, wa, ba, wi, bi, lam):
    b, L, _ = z.shape
    xbc, conv_ssd1 = _causal_conv(xbc, conv_ssd0, conv_w, conv_b)
    xbc = jax.nn.silu(xbc)
    xs, Bm, Cm = jnp.split(xbc, [SSD_WIDTH, SSD_WIDTH + SSD_GROUPS * SSD_STATE], axis=-1)
    dt = jax.nn.softplus(dt + dt_bias)
    A = -jnp.exp(a_log)
    xh = xs.reshape(b, L, SSD_HEADS, SSD_HEAD_DIM)
    y, ssm1 = _ssd_scan(xh, dt, A, Bm.reshape(b, L, SSD_GROUPS, SSD_STATE),
                        Cm.reshape(b, L, SSD_GROUPS, SSD_STATE), ssm0)
    y = y + d_skip[:, None] * xh
    y = y.reshape(b, L, SSD_WIDTH) * jax.nn.silu(z)
    y = _rmsnorm(y.reshape(b, L, SSD_GROUPS, -1), norm_w.reshape(SSD_GROUPS, -1)).reshape(b, L, SSD_WIDTH)
    xr, conv_rg1 = _causal_conv(xr, conv_rg0, rg_conv_w, rg_conv_b)
    r_out, rg1 = _rg_lru(xr, rg0, wa, ba, wi, bi, lam)
    r_out = r_out * jax.nn.gelu(gate)
    return y, r_out, conv_ssd1, ssm1, conv_rg1, rg1


def _nsa_split(proj, b, L, q_norm, k_norm):
    q = proj[:, :NSA_Q_WIDTH]
    kv = proj[:, NSA_Q_WIDTH:NSA_Q_WIDTH + 6 * NSA_KV_WIDTH]
    g = proj[:, NSA_Q_WIDTH + 6 * NSA_KV_WIDTH:NSA_Q_WIDTH + 6 * NSA_KV_WIDTH + 3 * NSA_HEADS]
    q = _rmsnorm(q.reshape(b, L, NSA_KV_HEADS, NSA_GROUP, NSA_HEAD_DIM), q_norm) * (NSA_HEAD_DIM ** -0.5)
    kv = kv.reshape(b, L, 6, NSA_KV_HEADS, NSA_HEAD_DIM)
    k_slc = _rmsnorm(kv[:, :, 2], k_norm[1])
    k_win = _rmsnorm(kv[:, :, 4], k_norm[2])
    rows = jnp.stack([kv[:, :, 0], kv[:, :, 1], k_slc, kv[:, :, 3]], axis=2)
    win = jnp.stack([k_win, kv[:, :, 5]], axis=2)
    gates = jax.nn.sigmoid(g).reshape(b, L, NSA_KV_HEADS, NSA_GROUP, 3)
    return q, rows, win, gates


def _masked_softmax(s, mask):
    s = jnp.where(mask, s.astype(F32), -jnp.inf)
    m = jnp.max(s, axis=-1, keepdims=True)
    e = jnp.exp(s - jnp.where(jnp.isfinite(m), m, 0.0))
    d = jnp.sum(e, axis=-1, keepdims=True)
    return e / jnp.where(d > 0, d, 1.0)


def _compress(r, w1, pe, w2):
    b, T, G, dh = r.shape
    n_chunk = T // CMP_STRIDE
    nc = n_chunk - CMP_R + 1
    ch = r[:, :n_chunk * CMP_STRIDE].reshape(b, n_chunk, CMP_STRIDE, G, dh)
    proj = jnp.einsum('bcsgd,rsdh->bcrgh', ch, w1.reshape(CMP_R, CMP_STRIDE, dh, CMP_HID))
    hid = jnp.einsum('ld,ldh->h', pe, w1)
    for rr in range(CMP_R):
        hid = hid + proj[:, rr:rr + nc, rr]
    return jax.nn.gelu(hid) @ w2


def _nsa_context(rows, cmp_w1, cmp_pe, cmp_w2, k_norm_cmp):
    b, T = rows.shape[:2]
    kc = _rmsnorm(_compress(rows[:, :, 0], cmp_w1[0], cmp_pe[0], cmp_w2[0]), k_norm_cmp)
    vc = _compress(rows[:, :, 1], cmp_w1[1], cmp_pe[1], cmp_w2[1])
    ns = -(-T // SEL_BLOCK)
    sel = jnp.pad(rows[:, :, 2:4], ((0, 0), (0, ns * SEL_BLOCK - T), (0, 0), (0, 0), (0, 0)))
    sel = sel.reshape(b, ns, SEL_BLOCK, 2, NSA_KV_HEADS, NSA_HEAD_DIM).transpose(3, 0, 4, 1, 2, 5)
    return kc, vc, sel[0], sel[1]


def _overlap_matrix(nc, ns):
    i = np.arange(nc)[:, None]
    j = np.arange(ns)[None, :]
    ov = (i * CMP_STRIDE < (j + 1) * SEL_BLOCK) & (i * CMP_STRIDE + CMP_LEN > j * SEL_BLOCK)
    return ov.astype(np.float32)


def _nsa_attend(q, gates, t_pos, kc, vc, ks, vs, kw, vw, w_pos):
    b, Q, G, K, dh = q.shape
    nc = kc.shape[1]
    ns = ks.shape[2]
    tq = t_pos[None, :, None, None, None]
    c_end = jnp.arange(nc) * CMP_STRIDE + CMP_LEN - 1
    p_c = _masked_softmax(jnp.einsum('bqgkd,bngd->bqgkn', q, kc), c_end <= tq)
    o_c = jnp.einsum('bqgkn,bngd->bqgkd', p_c, vc)
    imp = jnp.einsum('bqgkn,ns->bqgs', p_c, jnp.asarray(_overlap_matrix(nc, ns)))
    jj = jnp.arange(ns)[None, :]
    jt = (t_pos // SEL_BLOCK)[:, None]
    valid = jj <= jt
    forced = valid & ((jj == 0) | (jj == jt) | (jj == jt - 1))
    imp = jnp.where(forced[None, :, None], jnp.inf, jnp.where(valid[None, :, None], imp, -jnp.inf))
    _, idx = lax.top_k(imp, min(SEL_TOPN, ns))
    n = idx.shape[-1]
    bi = jnp.arange(b)[:, None, None, None]
    gi = jnp.arange(G)[None, None, :, None]
    k_sel = ks[bi, gi, idx].reshape(b, Q, G, n * SEL_BLOCK, dh)
    v_sel = vs[bi, gi, idx].reshape(b, Q, G, n * SEL_BLOCK, dh)
    kpos = (idx[..., None] * SEL_BLOCK + jnp.arange(SEL_BLOCK)).reshape(b, Q, G, 1, n * SEL_BLOCK)
    p_s = _masked_softmax(jnp.einsum('bqgkd,bqgmd->bqgkm', q, k_sel), kpos <= tq)
    o_s = jnp.einsum('bqgkm,bqgmd->bqgkd', p_s, v_sel)
    m_w = (w_pos <= tq) & (w_pos > tq - WINDOW) & (w_pos >= 0)
    p_w = _masked_softmax(jnp.einsum('bqgkd,bwgd->bqgkw', q, kw), m_w)
    o_w = jnp.einsum('bqgkw,bwgd->bqgkd', p_w, vw)
    o = gates[..., 0:1] * o_c + gates[..., 1:2] * o_s + gates[..., 2:3] * o_w
    return o.reshape(b, Q, G * K * dh)


def _nsa_prompt_core(proj, b, L, q_norm, k_norm, cmp_w1, cmp_pe, cmp_w2):
    q, rows, win, gates = _nsa_split(proj, b, L, q_norm, k_norm)
    kc, vc, ks, vs = _nsa_context(rows, cmp_w1, cmp_pe, cmp_w2, k_norm[0])
    win_pad = jnp.pad(win, ((0, 0), (WINDOW, 0), (0, 0), (0, 0), (0, 0)))

    def block(i):
        s = i * NSA_Q_BLOCK
        qb = lax.dynamic_slice_in_dim(q, s, NSA_Q_BLOCK, axis=1)
        gb = lax.dynamic_slice_in_dim(gates, s, NSA_Q_BLOCK, axis=1)
        wb = lax.dynamic_slice_in_dim(win_pad, s, WINDOW + NSA_Q_BLOCK, axis=1)
        t_pos = s + jnp.arange(NSA_Q_BLOCK)
        w_pos = s - WINDOW + jnp.arange(WINDOW + NSA_Q_BLOCK)
        return _nsa_attend(qb, gb, t_pos, kc, vc, ks, vs, wb[:, :, 0], wb[:, :, 1], w_pos)

    o = lax.map(block, jnp.arange(L // NSA_Q_BLOCK))
    o = jnp.moveaxis(o, 0, 1).reshape(b, L, NSA_Q_WIDTH)
    return o, rows, win[:, L - min(WINDOW, L):]


def _nsa_sample_core(proj, b, L, cache, page_table, win_buf, q_norm, k_norm, cmp_w1, cmp_pe, cmp_w2):
    q, rows, win, gates = _nsa_split(proj, b, L, q_norm, k_norm)
    past = page_table.shape[1] * cache.shape[1]
    past_rows = cache[page_table].reshape(b, past, 4, NSA_KV_HEADS, NSA_HEAD_DIM)
    kc, vc, ks, vs = _nsa_context(jnp.concatenate([past_rows, rows], axis=1),
                                  cmp_w1, cmp_pe, cmp_w2, k_norm[0])
    wb_len = win_buf.shape[1]
    wk = jnp.concatenate([win_buf, win], axis=1)
    t_pos = past + jnp.arange(L)
    w_pos = past - wb_len + jnp.arange(wb_len + L)
    o = _nsa_attend(q, gates, t_pos, kc, vc, ks, vs, wk[:, :, 0], wk[:, :, 1], w_pos)
    return o, rows, wk[:, L:]


NSA_TQ = 128
NSA_TK_SLC = 512
NSA_TK_WIN = 256
NSA_NS_PAD = 64
SEL_BIAS = -16384.0
MASK_VALUE = -1e30


def _split_bf16(x):
    hi = x.astype(BF16)
    lo = (x - hi.astype(F32)).astype(BF16)
    return hi, lo


def _seg_rms_scale(x, seg, seg_t):
    hi, lo = _split_bf16(x * x)
    ss = jnp.dot(hi, seg, preferred_element_type=F32) + jnp.dot(lo, seg, preferred_element_type=F32)
    r = lax.rsqrt(ss * (1.0 / NSA_HEAD_DIM) + NORM_EPS)
    rh, rl = _split_bf16(r)
    return jnp.dot(rh, seg_t, preferred_element_type=F32) + jnp.dot(rl, seg_t, preferred_element_type=F32)


def _nsa_prep_body(seq_len, p_ref, wq_ref, wks_ref, wkw_ref, segq_ref, segqt_ref, segk_ref, segkt_ref,
                   q_ref, rows_ref, win_ref, kaug_ref, vslc_ref, kwin_ref, vwin_ref, gate_ref):
    tm = p_ref.shape[0]
    dh = NSA_HEAD_DIM
    q = p_ref[:, 0:NSA_Q_WIDTH]
    qn = q * _seg_rms_scale(q, segq_ref[...], segqt_ref[...]) * wq_ref[...]
    kv = [p_ref[:, NSA_Q_WIDTH + NSA_KV_WIDTH * j:NSA_Q_WIDTH + NSA_KV_WIDTH * (j + 1)] for j in range(6)]
    ksl = kv[2] * _seg_rms_scale(kv[2], segk_ref[...], segkt_ref[...]) * wks_ref[...]
    kwn = kv[4] * _seg_rms_scale(kv[4], segk_ref[...], segkt_ref[...]) * wkw_ref[...]
    rows_ref[...] = jnp.concatenate([kv[0], kv[1], ksl, kv[3]], axis=1)
    win_ref[...] = jnp.concatenate([kwn, kv[5]], axis=1)
    gates = jax.nn.sigmoid(p_ref[:, NSA_Q_WIDTH + 6 * NSA_KV_WIDTH:NSA_Q_WIDTH + 6 * NSA_KV_WIDTH + V7X_LANES])
    t0 = (pl.program_id(0) * tm) % seq_len
    tpos = t0 + lax.broadcasted_iota(jnp.int32, (tm, NSA_NS_PAD), 0)
    blk = lax.broadcasted_iota(jnp.int32, (tm, NSA_NS_PAD), 1)
    onehot = jnp.where(blk == lax.shift_right_logical(tpos, 6), 1.0, 0.0).astype(BF16)
    for g in range(NSA_KV_HEADS):
        sl = slice(g * dh, (g + 1) * dh)
        kaug_ref[0, g] = jnp.concatenate([ksl[:, sl].astype(BF16), onehot], axis=1)
        vslc_ref[0, g] = kv[3][:, sl].astype(BF16)
        kwin_ref[0, g] = kwn[:, sl].astype(BF16)
        vwin_ref[0, g] = kv[5][:, sl].astype(BF16)
        gate_ref[0, g] = gates if g == 0 else pltpu.roll(gates, V7X_LANES - 3 * NSA_GROUP * g, 1)
        for k in range(NSA_GROUP):
            c0 = (g * NSA_GROUP + k) * dh
            q_ref[0, g, k] = qn[:, c0:c0 + dh].astype(BF16)


def _head_segments(width):
    lane = np.arange(width)[:, None] // NSA_HEAD_DIM
    seg = (lane == np.arange(V7X_LANES)[None, :]).astype(np.float32)
    return jnp.asarray(seg, BF16), jnp.asarray(seg.T, BF16)


def nsa_prep(proj, b, seq_len, q_norm, k_norm, tm):
    t = proj.shape[0]
    G, K, dh = NSA_KV_HEADS, NSA_GROUP, NSA_HEAD_DIM
    wq = (jnp.tile(q_norm, NSA_HEADS) * (dh ** -0.5))[None, :]
    wks = jnp.tile(k_norm[1], G)[None, :]
    wkw = jnp.tile(k_norm[2], G)[None, :]
    segq, segqt = _head_segments(NSA_Q_WIDTH)
    segk, segkt = _head_segments(NSA_KV_WIDTH)
    tiles_per_seq = seq_len // tm
    bi = lambda i: i // tiles_per_seq
    ti = lambda i: i % tiles_per_seq
    full = lambda a: pl.BlockSpec(a.shape, lambda i: (0,) * a.ndim)
    out_shape = (jax.ShapeDtypeStruct((b, G, K, seq_len, dh), BF16),
                 jax.ShapeDtypeStruct((t, 4 * NSA_KV_WIDTH), F32),
                 jax.ShapeDtypeStruct((t, 2 * NSA_KV_WIDTH), F32),
                 jax.ShapeDtypeStruct((b, G, seq_len, 2 * dh), BF16),
                 jax.ShapeDtypeStruct((b, G, seq_len, dh), BF16),
                 jax.ShapeDtypeStruct((b, G, seq_len, dh), BF16),
                 jax.ShapeDtypeStruct((b, G, seq_len, dh), BF16),
                 jax.ShapeDtypeStruct((b, G, seq_len, V7X_LANES), F32))
    per_g = lambda w: pl.BlockSpec((1, G, tm, w), lambda i: (bi(i), 0, ti(i), 0))
    out_specs = (pl.BlockSpec((1, G, K, tm, dh), lambda i: (bi(i), 0, 0, ti(i), 0)),
                 pl.BlockSpec((tm, 4 * NSA_KV_WIDTH), lambda i: (i, 0)),
                 pl.BlockSpec((tm, 2 * NSA_KV_WIDTH), lambda i: (i, 0)),
                 per_g(2 * dh), per_g(dh), per_g(dh), per_g(dh), per_g(V7X_LANES))
    return pl.pallas_call(
        functools.partial(_nsa_prep_body, seq_len),
        out_shape=out_shape,
        grid=(t // tm,),
        in_specs=[pl.BlockSpec((tm, proj.shape[1]), lambda i: (i, 0)),
                  full(wq), full(wks), full(wkw), full(segq), full(segqt), full(segk), full(segkt)],
        out_specs=out_specs,
        compiler_params=_cparams(("arbitrary",), V7X_VMEM_LIMIT_BYTES),
        name="nsa_prep",
    )(proj, wq, wks, wkw, segq, segqt, segk, segkt)


def _gelu_tanh(x):
    return 0.5 * x * (1.0 + jnp.tanh(math.sqrt(2.0 / math.pi) * (x + 0.044715 * (x * x * x))))


def _nsa_compress_body(n_chunk, x0_ref, x1_ref, x2_ref, x3_ref, wk_ref, wv_ref, pe_ref, w1f_ref, w2_ref, kn_ref,
                       kc_ref, vc_ref, pk_scr, pv_scr):
    s = pl.program_id(1)

    @pl.when(s == 0)
    def _():
        pk_scr[...] = jnp.zeros_like(pk_scr)
        pv_scr[...] = jnp.zeros_like(pv_scr)

    xs = [r[pl.ds(s, n_chunk, stride=CMP_STRIDE), :].astype(BF16) for r in (x0_ref, x1_ref, x2_ref, x3_ref)]
    pk_scr[...] += jnp.dot(jnp.concatenate(xs[0:2], axis=1), wk_ref[0], preferred_element_type=F32)
    pv_scr[...] += jnp.dot(jnp.concatenate(xs[2:4], axis=1), wv_ref[0], preferred_element_type=F32)

    @pl.when(s == CMP_STRIDE - 1)
    def _():
        for kv, p_scr, o_ref in ((0, pk_scr, kc_ref), (1, pv_scr, vc_ref)):
            p = p_scr[...]
            p_next = pltpu.roll(p, n_chunk - 1, 0)
            pe_h = jnp.dot(pe_ref[kv], w1f_ref[kv], preferred_element_type=F32)[0:1, :]
            for g in range(NSA_KV_HEADS):
                c0 = g * 2 * CMP_HID
                hid = pe_h + p[:, c0:c0 + CMP_HID] + p_next[:, c0 + CMP_HID:c0 + 2 * CMP_HID]
                y = jnp.dot(_gelu_tanh(hid).astype(BF16), w2_ref[kv], preferred_element_type=F32)
                if kv == 0:
                    y = y * lax.rsqrt(jnp.mean(y * y, axis=-1, keepdims=True) + NORM_EPS) * kn_ref[...]
                o_ref[0, g] = y.astype(BF16)


def nsa_compress(rows, b, seq_len, cmp_w1, cmp_pe, cmp_w2, k_norm_cmp):
    G, dh = NSA_KV_HEADS, NSA_HEAD_DIM
    n_chunk = seq_len // CMP_STRIDE
    w1 = cmp_w1.reshape(2, CMP_R, CMP_STRIDE, dh, CMP_HID)
    eye = jnp.eye(G, dtype=F32)
    wbd = jnp.einsum('vrsdh,gq->vsgdqrh', w1, eye).reshape(2, CMP_STRIDE, G * dh, G * CMP_R * CMP_HID).astype(BF16)
    pe = jnp.broadcast_to(cmp_pe.reshape(2, 1, CMP_LEN * dh), (2, 8, CMP_LEN * dh)).astype(BF16)
    w1f = cmp_w1.reshape(2, CMP_LEN * dh, CMP_HID).astype(BF16)
    w2 = cmp_w2.astype(BF16)
    kn = k_norm_cmp[None, :]
    full = lambda a: pl.BlockSpec(a.shape, lambda bi, s: (0,) * a.ndim)
    return pl.pallas_call(
        functools.partial(_nsa_compress_body, n_chunk),
        out_shape=(jax.ShapeDtypeStruct((b, G, n_chunk, dh), BF16),
                   jax.ShapeDtypeStruct((b, G, n_chunk, dh), BF16)),
        grid=(b, CMP_STRIDE),
        in_specs=[pl.BlockSpec((seq_len, V7X_LANES), lambda bi, s: (bi, 0)),
                  pl.BlockSpec((seq_len, V7X_LANES), lambda bi, s: (bi, 1)),
                  pl.BlockSpec((seq_len, V7X_LANES), lambda bi, s: (bi, 2)),
                  pl.BlockSpec((seq_len, V7X_LANES), lambda bi, s: (bi, 3)),
                  pl.BlockSpec((1, G * dh, G * CMP_R * CMP_HID), lambda bi, s: (s, 0, 0)),
                  pl.BlockSpec((1, G * dh, G * CMP_R * CMP_HID), lambda bi, s: (s, 0, 0)),
                  full(pe), full(w1f), full(w2), full(kn)],
        out_specs=(pl.BlockSpec((1, G, n_chunk, dh), lambda bi, s: (bi, 0, 0, 0)),
                   pl.BlockSpec((1, G, n_chunk, dh), lambda bi, s: (bi, 0, 0, 0))),
        scratch_shapes=[pltpu.VMEM((n_chunk, G * CMP_R * CMP_HID), F32),
                        pltpu.VMEM((n_chunk, G * CMP_R * CMP_HID), F32)],
        compiler_params=_cparams(("arbitrary", "arbitrary"), V7X_VMEM_LIMIT_BYTES),
        name="nsa_compress",
    )(rows, rows, rows, rows, wbd[0], wbd[1], pe, w1f, w2, kn)


_NT_DIMS = (((1,), (1,)), ((), ()))


def _flash_branch(q2, k_ref, v_ref, lo, hi, tk, mask_fn):
    rows = q2.shape[0]

    def body(j, carry):
        m, l, acc = carry
        k0 = pl.multiple_of(j * tk, tk)
        k = k_ref[0, 0, pl.ds(k0, tk), :]
        v = v_ref[0, 0, pl.ds(k0, tk), :]
        s = lax.dot_general(q2, k, _NT_DIMS, preferred_element_type=F32)
        s = jnp.where(mask_fn(k0), s, MASK_VALUE)
        m_new = jnp.maximum(m, jnp.max(s, axis=-1, keepdims=True))
        alpha = jnp.exp(m - m_new)
        p = jnp.exp(s - m_new)
        l = alpha * l + jnp.sum(p, axis=-1, keepdims=True)
        acc = alpha * acc + jnp.dot(p.astype(BF16), v, preferred_element_type=F32)
        return m_new, l, acc

    init = (jnp.full((rows, 1), MASK_VALUE, F32), jnp.zeros((rows, 1), F32),
            jnp.zeros((rows, NSA_HEAD_DIM), F32))
    _, l, acc = lax.fori_loop(lo, hi, body, init)
    return acc / l


def _nsa_attn_body(n_cmp, q_ref, kc_ref, vc_ref, kaug_ref, vslc_ref, kwin_ref, vwin_ref, gate_ref, ovt_ref, o_ref):
    tq = NSA_TQ
    rows = NSA_GROUP * tq
    q0 = pl.program_id(2) * tq
    q2 = q_ref[0, 0].reshape(rows, NSA_HEAD_DIM)
    row_t = q0 + jnp.bitwise_and(lax.broadcasted_iota(jnp.int32, (rows, 1), 0), tq - 1)

    n_pad = kc_ref.shape[2]
    s = lax.dot_general(q2, kc_ref[0, 0], _NT_DIMS, preferred_element_type=F32)
    n_idx = lax.broadcasted_iota(jnp.int32, (1, n_pad), 1)
    cmask = jnp.logical_and(n_idx * CMP_STRIDE + (CMP_LEN - 1) <= row_t, n_idx < n_cmp)
    s = jnp.where(cmask, s, MASK_VALUE)
    m = jnp.max(s, axis=-1, keepdims=True)
    e = jnp.where(cmask, jnp.exp(s - m), 0.0)
    den = jnp.sum(e, axis=-1, keepdims=True)
    p_c = e / jnp.where(den > 0.0, den, 1.0)
    o_c = jnp.dot(p_c.astype(BF16), vc_ref[0, 0], preferred_element_type=F32)

    p_sum = (p_c[0:tq] + p_c[tq:2 * tq]) + (p_c[2 * tq:3 * tq] + p_c[3 * tq:4 * tq])
    ph, plo = _split_bf16(p_sum)
    ovt = ovt_ref[...]
    imp = (lax.dot_general(ovt, ph, _NT_DIMS, preferred_element_type=F32)
           + lax.dot_general(ovt, plo, _NT_DIMS, preferred_element_type=F32))
    blk = lax.broadcasted_iota(jnp.int32, (NSA_NS_PAD, tq), 0)
    jt = lax.shift_right_logical(q0 + lax.broadcasted_iota(jnp.int32, (NSA_NS_PAD, tq), 1), 6)
    valid = blk <= jt
    forced = jnp.logical_and(valid, jnp.logical_or(blk == 0, jnp.logical_or(blk == jt, blk == jt - 1)))
    eff = jnp.where(forced, jnp.inf, jnp.where(valid, imp, -jnp.inf))
    rank = jnp.zeros((NSA_NS_PAD, tq), jnp.int32)
    for j in range(NSA_NS_PAD):
        other = eff[j:j + 1, :]
        ahead = jnp.logical_or(other > eff, jnp.logical_and(other == eff, blk > j))
        rank = rank + ahead.astype(jnp.int32)
    sel = jnp.logical_and(valid, rank < SEL_TOPN)
    sel_bias = jnp.where(sel, 0.0, SEL_BIAS).T.astype(BF16)

    q_aug = jnp.concatenate([q2, jnp.concatenate([sel_bias] * NSA_GROUP, axis=0)], axis=1)
    hi = (q0 + tq - 1) // NSA_TK_SLC + 1

    def slc_mask(k0):
        kpos = k0 + lax.broadcasted_iota(jnp.int32, (1, NSA_TK_SLC), 1)
        return kpos <= row_t

    o_s = _flash_branch(q_aug, kaug_ref, vslc_ref, 0, hi, NSA_TK_SLC, slc_mask)

    def win_mask(k0):
        kpos = k0 + lax.broadcasted_iota(jnp.int32, (1, NSA_TK_WIN), 1)
        return jnp.logical_and(kpos <= row_t, kpos > row_t - WINDOW)

    lo_w = jnp.maximum(q0 - (WINDOW - 1), 0) // NSA_TK_WIN
    hi_w = (q0 + tq - 1) // NSA_TK_WIN + 1
    o_w = _flash_branch(q2, kwin_ref, vwin_ref, lo_w, hi_w, NSA_TK_WIN, win_mask)

    gt = gate_ref[0, 0]
    outs = []
    for k in range(NSA_GROUP):
        r = slice(k * tq, (k + 1) * tq)
        outs.append(gt[:, 3 * k:3 * k + 1] * o_c[r] + gt[:, 3 * k + 1:3 * k + 2] * o_s[r]
                    + gt[:, 3 * k + 2:3 * k + 3] * o_w[r])
    o_ref[...] = jnp.concatenate(outs, axis=1)


def nsa_attention(q, kc, vc, kaug, vslc, kwin, vwin, gates, b, seq_len):
    G, K, dh = NSA_KV_HEADS, NSA_GROUP, NSA_HEAD_DIM
    tq = NSA_TQ
    nq = seq_len // tq
    n_chunk = kc.shape[2]
    n_cmp = n_chunk - CMP_R + 1
    ns = seq_len // SEL_BLOCK
    ovt = np.zeros((NSA_NS_PAD, n_chunk), np.float32)
    ovt[:ns, :n_cmp] = _overlap_matrix(n_cmp, ns).T
    ovt = jnp.asarray(ovt, BF16)
    seq_spec = lambda w: pl.BlockSpec((1, 1, seq_len, w), lambda bi, g, qi: (bi, g, 0, 0))
    return pl.pallas_call(
        functools.partial(_nsa_attn_body, n_cmp),
        out_shape=jax.ShapeDtypeStruct((b * seq_len, NSA_Q_WIDTH), F32),
        grid=(b, G, nq),
        in_specs=[pl.BlockSpec((1, 1, K, tq, dh), lambda bi, g, qi: (bi, g, 0, qi, 0)),
                  pl.BlockSpec((1, 1, n_chunk, dh), lambda bi, g, qi: (bi, g, 0, 0)),
                  pl.BlockSpec((1, 1, n_chunk, dh), lambda bi, g, qi: (bi, g, 0, 0)),
                  seq_spec(2 * dh), seq_spec(dh), seq_spec(dh), seq_spec(dh),
                  pl.BlockSpec((1, 1, tq, V7X_LANES), lambda bi, g, qi: (bi, g, qi, 0)),
                  pl.BlockSpec(ovt.shape, lambda bi, g, qi: (0, 0))],
        out_specs=pl.BlockSpec((tq, K * dh), lambda bi, g, qi: (bi * nq + qi, g)),
        compiler_params=_cparams(("arbitrary", "arbitrary", "arbitrary"), V7X_VMEM_LIMIT_BYTES),
        name="nsa_attention",
    )(q, kc, vc, kaug, vslc, kwin, vwin, gates, ovt)


def nsa_prompt_pallas(proj, b, seq_len, q_norm, k_norm, cmp_w1, cmp_pe, cmp_w2):
    q, rows, win, kaug, vslc, kwin, vwin, gates = nsa_prep(proj, b, seq_len, q_norm, k_norm, ROW_TILE)
    kc, vc = nsa_compress(rows, b, seq_len, cmp_w1, cmp_pe, cmp_w2, k_norm[0])
    o = nsa_attention(q, kc, vc, kaug, vslc, kwin, vwin, gates, b, seq_len)
    rows_out = rows.reshape(b, seq_len, 4, NSA_KV_HEADS, NSA_HEAD_DIM)
    wlen = min(WINDOW, seq_len)
    win_out = win.reshape(b, seq_len, 2, NSA_KV_HEADS, NSA_HEAD_DIM)[:, seq_len - wlen:]
    return o, rows_out, win_out


def _pad_cols(w, n):
    return jnp.pad(w, ((0, 0), (0, n - w.shape[1])))


def kernel(x_prompt, x_sample, cache_nsa_kv, state_nsa_win, state_ssd_conv, state_ssd, state_rg_conv, state_rg, page_table, c_prompt, c_sample, ada_w, ada_b, norm_mix, norm_ffn, rec_w_in, ssd_conv_w, ssd_conv_b, ssd_dt_bias, ssd_a_log, ssd_d, ssd_norm_w, rg_conv_w, rg_conv_b, rg_wa, rg_ba, rg_wi, rg_bi, rg_lambda, rec_w_out, nsa_w_in, nsa_q_norm, nsa_k_norm, cmp_w1, cmp_pe, cmp_w2, nsa_w_out, router_w, router_b, moe_w1, moe_b1, moe_w2, moe_b2):
    bp, L, d = x_prompt.shape
    bs = x_sample.shape[0]
    depth = ada_w.shape[0]
    tp = bp * L
    xp = x_prompt.reshape(tp, d)
    xs = x_sample.reshape(bs, d)

    n_c = bp + bs
    n_c_pad = -(-n_c // 8) * 8
    c_all = jnp.pad(jnp.concatenate([c_prompt, c_sample], axis=0), ((0, n_c_pad - n_c), (0, 0)))
    ada_w_cat = jnp.concatenate([ada_w[i] for i in range(depth)], axis=1).astype(BF16)
    ada_b_cat = jnp.concatenate([ada_b[i] for i in range(depth)], axis=0)[None, :]
    mod_all = adaln_mod(c_all, ada_w_cat, ada_b_cat)

    outs = {k: [] for k in ('rows_p', 'rows_s', 'win_p', 'win_s', 'sconv_p', 'sconv_s', 'ssm_p', 'ssm_s',
                            'rconv_p', 'rconv_s', 'rg_p', 'rg_s')}

    for i in range(depth):
        j = i // 2
        mod_i = mod_all[:, i * 6 * d:(i + 1) * 6 * d]
        mp = [mod_i[:bp, k * d:(k + 1) * d].reshape(bp, 1, d) for k in range(6)]
        ms = [mod_i[bp:bp + bs, k * d:(k + 1) * d].reshape(1, bs, d) for k in range(6)]
        g_mix = norm_mix[i][None, :]
        g_ffn = norm_ffn[i][None, :]

        if i % 2 == 0:
            w_in = rec_w_in[j]
            s0, s1, s2, s3 = 1024, 1024 + SSD_XBC, 1024 + SSD_XBC + SSD_HEADS, 1024 + SSD_XBC + SSD_HEADS + RG_WIDTH
            w_cat = jnp.concatenate([w_in[:, :s1], w_in[:, s2:], _pad_cols(w_in[:, s1:s2], 512)], axis=1).astype(BF16)
            proj_p = mod_matmul(xp, g_mix, mp[1], mp[0], w_cat, ROW_TILE, 512)
            proj_s = mod_matmul(xs, g_mix, ms[1], ms[0], w_cat, bs, 512)

            def split(pr, b_, l_):
                pr = pr.reshape(b_, l_, -1)
                return (pr[..., :1024], pr[..., 1024:3072], pr[..., 5120:5120 + SSD_HEADS],
                        pr[..., 3072:4096], pr[..., 4096:5120])

            wts = (ssd_conv_w[j], ssd_conv_b[j], ssd_dt_bias[j], ssd_a_log[j], ssd_d[j], ssd_norm_w[j],
                   rg_conv_w[j], rg_conv_b[j], rg_wa[j], rg_ba[j], rg_wi[j], rg_bi[j], rg_lambda[j])
            yp, rp, a1, a2, a3, a4 = _rec_core(
                *split(proj_p, bp, L),
                jnp.zeros((bp, SSD_CONV - 1, SSD_XBC), F32),
                jnp.zeros((bp, SSD_HEADS, SSD_HEAD_DIM, SSD_STATE), F32),
                jnp.zeros((bp, 3, RG_WIDTH), F32), jnp.zeros((bp, RG_WIDTH), F32), *wts)
            ys, rs, b1_, b2_, b3_, b4_ = _rec_core(
                *split(proj_s, bs, 1), state_ssd_conv[j], state_ssd[j], state_rg_conv[j], state_rg[j], *wts)
            outs['sconv_p'].append(a1); outs['ssm_p'].append(a2); outs['rconv_p'].append(a3); outs['rg_p'].append(a4)
            outs['sconv_s'].append(b1_); outs['ssm_s'].append(b2_); outs['rconv_s'].append(b3_); outs['rg_s'].append(b4_)
            w_out = rec_w_out[j].astype(BF16)
            w_parts = [w_out[:SSD_WIDTH], w_out[SSD_WIDTH:]]
            xp = mm_residual([yp.reshape(tp, -1), rp.reshape(tp, -1)], w_parts, xp, mp[2], ROW_TILE)
            xs = mm_residual([ys.reshape(bs, -1), rs.reshape(bs, -1)], w_parts, xs, ms[2], bs)
        else:
            w_in = nsa_w_in[j]
            w_cat = _pad_cols(w_in, 3072).astype(BF16)
            proj_p = mod_matmul(xp, g_mix, mp[1], mp[0], w_cat, ROW_TILE, 512)
            proj_s = mod_matmul(xs, g_mix, ms[1], ms[0], w_cat, bs, 512)
            wts = (nsa_q_norm[j], nsa_k_norm[j], cmp_w1[j], cmp_pe[j], cmp_w2[j])
            op, rp, wp = nsa_prompt_pallas(proj_p, bp, L, *wts)
            osm, rs, ws = _nsa_sample_core(proj_s, bs, 1, cache_nsa_kv[j], page_table, state_nsa_win[j], *wts)
            outs['rows_p'].append(rp); outs['win_p'].append(wp); outs['rows_s'].append(rs); outs['win_s'].append(ws)
            w_out = nsa_w_out[j].astype(BF16)
            xp = mm_residual([op.reshape(tp, -1)], [w_out], xp, mp[2], ROW_TILE)
            xs = mm_residual([osm.reshape(bs, -1)], [w_out], xs, ms[2], bs)

        rw = _pad_cols(router_w[i], V7X_LANES)
        rwh = rw.astype(BF16)
        rwl = (rw - rwh.astype(F32)).astype(BF16)
        rb = jnp.concatenate([router_b[i], jnp.full((V7X_LANES - N_EXPERTS,), -1e30, F32)])[None, :]
        h_p, e_p, gt_p = moe_router(xp, g_ffn, mp[4], mp[3], rwh, rwl, rb, ROW_TILE)
        h_s, e_s, gt_s = moe_router(xs, g_ffn, ms[4], ms[3], rwh, rwl, rb, bs)
        h_all = jnp.concatenate([h_p, h_s], axis=0)
        top_e = jnp.concatenate([e_p[:, :TOP_K], e_s[:, :TOP_K]], axis=0)
        gate = jnp.concatenate([gt_p[:, :TOP_K], gt_s[:, :TOP_K]], axis=0)
        blk_e, nvalid, n_used, idx, row_gate = _moe_routing(top_e, gate)
        y4 = moe_ffn(h_all, blk_e, nvalid, n_used, idx, row_gate, moe_w1[i], moe_b1[i], moe_w2[i], moe_b2[i])
        y4 = y4.reshape(tp + bs, TOP_K * d)
        xp = moe_combine(xp, mp[5], y4, 0, 256)
        xs = moe_combine(xs, ms[5], y4, tp, bs)

    st = lambda k: jnp.stack(outs[k])
    return (xp.reshape(bp, L, d), xs.reshape(bs, 1, d), st('rows_p'), st('rows_s'), st('win_p'), st('win_s'),
            st('sconv_p'), st('sconv_s'), st('ssm_p'), st('ssm_s'), st('rconv_p'), st('rconv_s'),
            st('rg_p'), st('rg_s'))
```

```python
import functools
import math

import jax
import jax.numpy as jnp
import numpy as np
from jax import lax
from jax.experimental import pallas as pl
from jax.experimental.pallas import tpu as pltpu

F32 = jnp.float32
BF16 = jnp.bfloat16

D_MODEL = 1024
NORM_EPS = 1e-6

SSD_WIDTH = 1024
SSD_HEAD_DIM = 64
SSD_HEADS = 16
SSD_GROUPS = 4
SSD_STATE = 128
SSD_CONV = 4
SSD_CHUNK = 128
SSD_XBC = SSD_WIDTH + 2 * SSD_GROUPS * SSD_STATE

RG_WIDTH = 1024
RG_BLOCKS = 16
RG_BLOCK_DIM = 64
RG_C = 8.0

NSA_HEADS = 16
NSA_KV_HEADS = 4
NSA_HEAD_DIM = 64
NSA_GROUP = 4
NSA_Q_WIDTH = 1024
NSA_KV_WIDTH = 256
CMP_LEN = 32
CMP_STRIDE = 16
CMP_R = 2
CMP_HID = 128
SEL_BLOCK = 64
SEL_TOPN = 16
WINDOW = 512
NSA_Q_BLOCK = 64

N_EXPERTS = 32
TOP_K = 4
D_FF = 1024
SWIGLU_LIMIT = 7.0
SWIGLU_ALPHA = 1.702

V7X_LANES = 128
V7X_VMEM_LIMIT_BYTES = 56 * 1024 * 1024

MOE_BLOCK_ROWS = 256
ROW_TILE = 512


def _cparams(sem, vmem=None):
    return pltpu.CompilerParams(dimension_semantics=sem, vmem_limit_bytes=vmem)


def _adaln_body(c_ref, w_ref, b_ref, o_ref):
    c = c_ref[...]
    s = c * jax.nn.sigmoid(c)
    o_ref[...] = jnp.dot(s.astype(BF16), w_ref[...], preferred_element_type=F32) + b_ref[...]


def adaln_mod(c, w_bf, b):
    r, d = c.shape
    n = w_bf.shape[1]
    tn = 1536
    return pl.pallas_call(
        _adaln_body,
        out_shape=jax.ShapeDtypeStruct((r, n), F32),
        grid=(n // tn,),
        in_specs=[pl.BlockSpec((r, d), lambda j: (0, 0)),
                  pl.BlockSpec((d, tn), lambda j: (0, j)),
                  pl.BlockSpec((1, tn), lambda j: (0, j))],
        out_specs=pl.BlockSpec((r, tn), lambda j: (0, j)),
        compiler_params=_cparams(("arbitrary",)),
        name="adaln_mod",
    )(c, w_bf, b)


def _modulated(x, g, scale, shift):
    ms = jnp.mean(x * x, axis=-1, keepdims=True)
    y = x * lax.rsqrt(ms + NORM_EPS) * g
    return y * (1.0 + scale) + shift


def _mod_matmul_body(x_ref, g_ref, sc_ref, sh_ref, w_ref, o_ref, h_scr):
    @pl.when(pl.program_id(1) == 0)
    def _():
        h_scr[...] = _modulated(x_ref[...], g_ref[...], sc_ref[0], sh_ref[0]).astype(BF16)

    o_ref[...] = jnp.dot(h_scr[...], w_ref[...], preferred_element_type=F32)


def mod_matmul(x, g, scale, shift, w_bf, tm, tn):
    t, d = x.shape
    n = w_bf.shape[1]
    m, r, _ = scale.shape
    rows_per_mod = t // m
    mod_spec = pl.BlockSpec((1, r, d), lambda i, j: ((i * tm) // rows_per_mod, 0, 0))
    return pl.pallas_call(
        _mod_matmul_body,
        out_shape=jax.ShapeDtypeStruct((t, n), F32),
        grid=(t // tm, n // tn),
        in_specs=[pl.BlockSpec((tm, d), lambda i, j: (i, 0)),
                  pl.BlockSpec((1, d), lambda i, j: (0, 0)),
                  mod_spec, mod_spec,
                  pl.BlockSpec((d, tn), lambda i, j: (0, j))],
        out_specs=pl.BlockSpec((tm, tn), lambda i, j: (i, j)),
        scratch_shapes=[pltpu.VMEM((tm, d), BF16)],
        compiler_params=_cparams(("arbitrary", "arbitrary")),
        name="mod_matmul",
    )(x, g, scale, shift, w_bf)


def _mm_residual_body(n_a, *refs):
    a_refs = refs[:n_a]
    w_refs = refs[n_a:2 * n_a]
    x_ref, g_ref, o_ref = refs[2 * n_a:]
    acc = None
    for a_ref, w_ref in zip(a_refs, w_refs):
        p = jnp.dot(a_ref[...].astype(BF16), w_ref[...], preferred_element_type=F32)
        acc = p if acc is None else acc + p
    o_ref[...] = x_ref[...] + g_ref[0] * acc


def mm_residual(a_list, w_list, x, gate, tm):
    t, d = x.shape
    m, r, _ = gate.shape
    rows_per_mod = t // m
    in_specs = [pl.BlockSpec((tm, a.shape[1]), lambda i: (i, 0)) for a in a_list]
    in_specs += [pl.BlockSpec(w.shape, lambda i: (0, 0)) for w in w_list]
    in_specs += [pl.BlockSpec((tm, d), lambda i: (i, 0)),
                 pl.BlockSpec((1, r, d), lambda i: ((i * tm) // rows_per_mod, 0, 0))]
    return pl.pallas_call(
        functools.partial(_mm_residual_body, len(a_list)),
        out_shape=jax.ShapeDtypeStruct((t, d), F32),
        grid=(t // tm,),
        in_specs=in_specs,
        out_specs=pl.BlockSpec((tm, d), lambda i: (i, 0)),
        compiler_params=_cparams(("arbitrary",)),
        name="mm_residual",
    )(*a_list, *w_list, x, gate)


def _router_body(x_ref, g_ref, sc_ref, sh_ref, wh_ref, wl_ref, rb_ref, h_ref, e_ref, gt_ref):
    h = _modulated(x_ref[...], g_ref[...], sc_ref[0], sh_ref[0])
    h_ref[...] = h
    hh = h.astype(BF16)
    hl = (h - hh.astype(F32)).astype(BF16)
    wh = wh_ref[...]
    wl = wl_ref[...]
    logits = (jnp.dot(hh, wh, preferred_element_type=F32)
              + (jnp.dot(hh, wl, preferred_element_type=F32)
                 + jnp.dot(hl, wh, preferred_element_type=F32))) + rb_ref[...]
    lane = lax.broadcasted_iota(jnp.int32, logits.shape, 1)
    neg = jnp.float32(-jnp.inf)
    vals, idxs = [], []
    cur = logits
    for _ in range(TOP_K):
        m = jnp.max(cur, axis=-1, keepdims=True)
        idx = jnp.min(jnp.where(cur == m, lane, V7X_LANES), axis=-1, keepdims=True)
        vals.append(m)
        idxs.append(idx)
        cur = jnp.where(lane == idx, neg, cur)
    exps = [jnp.exp(v - vals[0]) for v in vals]
    den = exps[0] + exps[1] + exps[2] + exps[3]
    e_out = jnp.zeros(logits.shape, jnp.int32)
    g_out = jnp.zeros(logits.shape, F32)
    for k in range(TOP_K):
        e_out = jnp.where(lane == k, idxs[k], e_out)
        g_out = jnp.where(lane == k, exps[k] / den, g_out)
    e_ref[...] = e_out
    gt_ref[...] = g_out


def moe_router(x, g, scale, shift, wh, wl, rb, tm):
    t, d = x.shape
    m, r, _ = scale.shape
    rows_per_mod = t // m
    mod_spec = pl.BlockSpec((1, r, d), lambda i: ((i * tm) // rows_per_mod, 0, 0))
    return pl.pallas_call(
        _router_body,
        out_shape=(jax.ShapeDtypeStruct((t, d), F32),
                   jax.ShapeDtypeStruct((t, V7X_LANES), jnp.int32),
                   jax.ShapeDtypeStruct((t, V7X_LANES), F32)),
        grid=(t // tm,),
        in_specs=[pl.BlockSpec((tm, d), lambda i: (i, 0)),
                  pl.BlockSpec((1, d), lambda i: (0, 0)),
                  mod_spec, mod_spec,
                  pl.BlockSpec((d, V7X_LANES), lambda i: (0, 0)),
                  pl.BlockSpec((d, V7X_LANES), lambda i: (0, 0)),
                  pl.BlockSpec((1, V7X_LANES), lambda i: (0, 0))],
        out_specs=(pl.BlockSpec((tm, d), lambda i: (i, 0)),
                   pl.BlockSpec((tm, V7X_LANES), lambda i: (i, 0)),
                   pl.BlockSpec((tm, V7X_LANES), lambda i: (i, 0))),
        compiler_params=_cparams(("arbitrary",)),
        name="moe_router",
    )(x, g, scale, shift, wh, wl, rb)


def _ffn_body(blk_e_ref, nvalid_ref, nused_ref,
              idx_hbm, h_hbm, gate_ref, w1_ref, b1_ref, w2_ref, b2_ref,
              out_hbm,
              idx_smem, xbuf, ybuf, w1bf, w2bf, sem_idx, sem_g, sem_s):
    bm = MOE_BLOCK_ROWS
    i = pl.program_id(0)
    n_used = nused_ref[0]

    def idx_copy(blk, slot):
        return pltpu.make_async_copy(idx_hbm.at[blk], idx_smem.at[slot], sem_idx.at[slot])

    def gather_copy(tok, slot, r):
        return pltpu.make_async_copy(h_hbm.at[pl.ds(tok, 1)], xbuf.at[slot, pl.ds(r, 1)], sem_g.at[slot])

    def scatter_copy(dst, slot, r):
        return pltpu.make_async_copy(ybuf.at[slot, pl.ds(r, 1)], out_hbm.at[pl.ds(dst, 1)], sem_s.at[slot])

    def start_gather(islot, slot):
        def body(r, c):
            tok = lax.shift_right_logical(jnp.maximum(idx_smem[islot, r], 0), 2)
            gather_copy(tok, slot, r).start()
            return c
        lax.fori_loop(0, bm, body, 0)

    def wait_gather(slot):
        def body(r, c):
            gather_copy(0, slot, r).wait()
            return c
        lax.fori_loop(0, bm, body, 0)

    def start_scatter(islot, slot, n):
        def body(r, c):
            scatter_copy(idx_smem[islot, r], slot, r).start()
            return c
        lax.fori_loop(0, n, body, 0)

    def wait_scatter(slot, n):
        def body(r, c):
            scatter_copy(0, slot, r).wait()
            return c
        lax.fori_loop(0, n, body, 0)

    @pl.when(i < n_used)
    def _():
        slot = i % 2
        islot = i % 3

        @pl.when(i == 0)
        def _():
            idx_copy(0, 0).start()
            idx_copy(0, 0).wait()
            start_gather(0, 0)

            @pl.when(n_used > 1)
            def _():
                idx_copy(1, 1).start()

        @pl.when(i + 2 < n_used)
        def _():
            idx_copy(i + 2, (i + 2) % 3).start()

        @pl.when(i + 1 < n_used)
        def _():
            idx_copy(i + 1, (i + 1) % 3).wait()
            start_gather((i + 1) % 3, 1 - slot)

        @pl.when(jnp.logical_or(i == 0, blk_e_ref[i] != blk_e_ref[jnp.maximum(i - 1, 0)]))
        def _():
            w1bf[...] = w1_ref[0].astype(BF16)
            w2bf[...] = w2_ref[0].astype(BF16)

        wait_gather(slot)

        @pl.when(i >= 2)
        def _():
            wait_scatter(slot, nvalid_ref[jnp.maximum(i - 2, 0)])

        x = xbuf[slot].astype(BF16)
        u = jnp.dot(x, w1bf[...], preferred_element_type=F32) + b1_ref[0]
        gl = jnp.minimum(u[:, :D_FF], SWIGLU_LIMIT)
        lin = jnp.clip(u[:, D_FF:], -SWIGLU_LIMIT, SWIGLU_LIMIT)
        act = gl * jax.nn.sigmoid(SWIGLU_ALPHA * gl) * (lin + 1.0)
        y = jnp.dot(act.astype(BF16), w2bf[...], preferred_element_type=F32) + b2_ref[0]
        ybuf[slot] = y * gate_ref[...]
        start_scatter(islot, slot, nvalid_ref[i])

        @pl.when(i == n_used - 1)
        def _():
            @pl.when(i >= 1)
            def _():
                wait_scatter(1 - slot, nvalid_ref[jnp.maximum(i - 1, 0)])
            wait_scatter(slot, nvalid_ref[i])


def moe_ffn(h_all, blk_e, nvalid, n_used, idx, row_gate, w1, b1, w2, b2):
    t, d = h_all.shape
    bm = MOE_BLOCK_ROWS
    n_blocks = idx.shape[0]
    grid_spec = pltpu.PrefetchScalarGridSpec(
        num_scalar_prefetch=3,
        grid=(n_blocks,),
        in_specs=[pl.BlockSpec(memory_space=pl.ANY),
                  pl.BlockSpec(memory_space=pl.ANY),
                  pl.BlockSpec((bm, 1), lambda i, be, nv, nu: (i, 0)),
                  pl.BlockSpec((1, d, 2 * D_FF), lambda i, be, nv, nu: (be[i], 0, 0)),
                  pl.BlockSpec((1, 1, 2 * D_FF), lambda i, be, nv, nu: (be[i], 0, 0)),
                  pl.BlockSpec((1, D_FF, d), lambda i, be, nv, nu: (be[i], 0, 0)),
                  pl.BlockSpec((1, 1, d), lambda i, be, nv, nu: (be[i], 0, 0))],
        out_specs=pl.BlockSpec(memory_space=pl.ANY),
        scratch_shapes=[pltpu.SMEM((3, bm), jnp.int32),
                        pltpu.VMEM((2, bm, d), F32),
                        pltpu.VMEM((2, bm, d), F32),
                        pltpu.VMEM((d, 2 * D_FF), BF16),
                        pltpu.VMEM((D_FF, d), BF16),
                        pltpu.SemaphoreType.DMA((3,)),
                        pltpu.SemaphoreType.DMA((2,)),
                        pltpu.SemaphoreType.DMA((2,))],
    )
    return pl.pallas_call(
        _ffn_body,
        out_shape=jax.ShapeDtypeStruct((t * TOP_K, d), F32),
        grid_spec=grid_spec,
        compiler_params=_cparams(("arbitrary",), V7X_VMEM_LIMIT_BYTES),
        name="moe_ffn",
    )(blk_e, nvalid, n_used, idx, h_all, row_gate, w1, b1.reshape(N_EXPERTS, 1, -1), w2,
      b2.reshape(N_EXPERTS, 1, -1))


def _moe_combine_body(x_ref, g_ref, y_ref, o_ref):
    y = y_ref[...]
    d = D_MODEL
    acc = (y[:, 0:d] + y[:, d:2 * d]) + (y[:, 2 * d:3 * d] + y[:, 3 * d:4 * d])
    o_ref[...] = x_ref[...] + g_ref[0] * acc


def moe_combine(x, gate, y4, row_off, tm):
    t, d = x.shape
    m, r, _ = gate.shape
    rows_per_mod = t // m
    off = row_off // tm
    return pl.pallas_call(
        _moe_combine_body,
        out_shape=jax.ShapeDtypeStruct((t, d), F32),
        grid=(t // tm,),
        in_specs=[pl.BlockSpec((tm, d), lambda i: (i, 0)),
                  pl.BlockSpec((1, r, d), lambda i: ((i * tm) // rows_per_mod, 0, 0)),
                  pl.BlockSpec((tm, TOP_K * d), lambda i: (i + off, 0))],
        out_specs=pl.BlockSpec((tm, d), lambda i: (i, 0)),
        compiler_params=_cparams(("arbitrary",)),
        name="moe_combine",
    )(x, gate, y4)


def _moe_routing(top_e, gate):
    bm = MOE_BLOCK_ROWS
    t = top_e.shape[0]
    tk = t * TOP_K
    e_flat = top_e.reshape(-1)
    order = jnp.argsort(e_flat).astype(jnp.int32)
    e_sorted = e_flat[order]
    counts = jnp.bincount(e_flat, length=N_EXPERTS).astype(jnp.int32)
    padded = (counts + bm - 1) // bm * bm
    pad_end = jnp.cumsum(padded)
    pad_start = pad_end - padded
    start = jnp.cumsum(counts) - counts
    dest = pad_start[e_sorted] + jnp.arange(tk, dtype=jnp.int32) - start[e_sorted]
    n_blocks = -(-tk // bm) + N_EXPERTS
    n_rows = n_blocks * bm
    idx = jnp.full((n_rows,), -1, jnp.int32).at[dest].set(order)
    row_gate = jnp.zeros((n_rows,), F32).at[dest].set(gate.reshape(-1)[order])
    blk_start = jnp.arange(n_blocks, dtype=jnp.int32) * bm
    blk_e = jnp.minimum(jnp.searchsorted(pad_end, blk_start, side='right'), N_EXPERTS - 1).astype(jnp.int32)
    nvalid = jnp.clip(counts[blk_e] - (blk_start - pad_start[blk_e]), 0, bm).astype(jnp.int32)
    n_used = (pad_end[-1] // bm).astype(jnp.int32).reshape(1)
    return blk_e, nvalid, n_used, idx.reshape(n_blocks, bm), row_gate.reshape(n_rows, 1)


def _rmsnorm(x, w):
    xf = x.astype(F32)
    y = xf * lax.rsqrt(jnp.mean(xf * xf, axis=-1, keepdims=True) + NORM_EPS)
    return y * w.astype(F32)


def _causal_conv(x, prev, w, b):
    width = w.shape[0]
    L = x.shape[1]
    xp = jnp.concatenate([prev.astype(x.dtype), x], axis=1)
    y = b
    for k in range(width):
        y = y + xp[:, k:k + L] * w[k]
    return y, xp[:, xp.shape[1] - (width - 1):]


def _ssd_scan(x, dt, A, Bm, Cm, h0):
    b, L, H, P = x.shape
    G, N = Bm.shape[2], Bm.shape[3]
    K = H // G
    q = math.gcd(L, SSD_CHUNK)
    nc = L // q
    a = (dt * A).reshape(b, nc, q, G, K)
    xdt = (x * dt[..., None]).reshape(b, nc, q, G, K, P)
    Bc = Bm.reshape(b, nc, q, G, N)
    Cc = Cm.reshape(b, nc, q, G, N)
    a_cs = jnp.cumsum(a, axis=2)
    causal = jnp.tril(jnp.ones((q, q), dtype=bool))[None, None, :, :, None, None]
    seg = a_cs[:, :, :, None] - a_cs[:, :, None, :]
    decay = jnp.exp(jnp.where(causal, seg, -jnp.inf))
    cb = jnp.einsum('bctgn,bcsgn->bctsg', Cc, Bc)
    y_diag = jnp.einsum('bctsgk,bcsgkp->bctgkp', cb[..., None] * decay, xdt)
    to_end = jnp.exp(a_cs[:, :, -1:] - a_cs)
    states = jnp.einsum('bcsgn,bcsgkp->bcgkpn', Bc, xdt * to_end[..., None])
    chunk_decay = jnp.exp(a_cs[:, :, -1])

    def step(h, inp):
        st, dec = inp
        return h * dec[..., None, None] + st, h

    h_last, h_prev = lax.scan(step, h0.reshape(b, G, K, P, N),
                              (jnp.moveaxis(states, 1, 0), jnp.moveaxis(chunk_decay, 1, 0)))
    h_prev = jnp.moveaxis(h_prev, 0, 1)
    y_off = jnp.einsum('bctgn,bcgkpn->bctgkp', Cc, h_prev) * jnp.exp(a_cs)[..., None]
    return (y_diag + y_off).reshape(b, L, H, P), h_last.reshape(b, H, P, N)


def _rg_lru(x, h0, wa, ba, wi, bi, lam):
    b, L, _ = x.shape
    xb = x.reshape(b, L, RG_BLOCKS, RG_BLOCK_DIM)
    r = jax.nn.sigmoid(jnp.einsum('blnd,nde->blne', xb, wa).reshape(b, L, RG_WIDTH) + ba)
    i = jax.nn.sigmoid(jnp.einsum('blnd,nde->blne', xb, wi).reshape(b, L, RG_WIDTH) + bi)
    log_a = -RG_C * r * jax.nn.softplus(-lam.astype(F32))
    a = jnp.exp(log_a)
    u = jnp.sqrt(-jnp.expm1(2.0 * log_a)) * (i * x)
    u = u.at[:, 0].add(a[:, 0] * h0)

    def combine(e1, e2):
        return e1[0] * e2[0], e2[0] * e1[1] + e2[1]

    _, h = lax.associative_scan(combine, (a, u), axis=1)
    return h, h[:, -1]


def _rec_core(z, xbc, dt, gate, xr, conv_ssd0, ssm0, conv_rg0, rg0, conv_w, conv_b, dt_bias, a_log,
              d_skip, norm_w, rg_conv_w, rg_conv_b, wa, ba, wi, bi, lam):
    b, L, _ = z.shape
    xbc, conv_ssd1 = _causal_conv(xbc, conv_ssd0, conv_w, conv_b)
    xbc = jax.nn.silu(xbc)
    xs, Bm, Cm = jnp.split(xbc, [SSD_WIDTH, SSD_WIDTH + SSD_GROUPS * SSD_STATE], axis=-1)
    dt = jax.nn.softplus(dt + dt_bias)
    A = -jnp.exp(a_log)
    xh = xs.reshape(b, L, SSD_HEADS, SSD_HEAD_DIM)
    y, ssm1 = _ssd_scan(xh, dt, A, Bm.reshape(b, L, SSD_GROUPS, SSD_STATE),
                        Cm.reshape(b, L, SSD_GROUPS, SSD_STATE), ssm0)
    y = y + d_skip[:, None] * xh
    y = y.reshape(b, L, SSD_WIDTH) * jax.nn.silu(z)
    y = _rmsnorm(y.reshape(b, L, SSD_GROUPS, -1), norm_w.reshape(SSD_GROUPS, -1)).reshape(b, L, SSD_WIDTH)
    xr, conv_rg1 = _causal_conv(xr, conv_rg0, rg_conv_w, rg_conv_b)
    r_out, rg1 = _rg_lru(xr, rg0, wa, ba, wi, bi, lam)
    r_out = r_out * jax.nn.gelu(gate)
    return y, r_out, conv_ssd1, ssm1, conv_rg1, rg1


CONV_TAIL = 8
RG_TILE = 256


def _split3_bf16(x):
    h = x.astype(BF16)
    r = x - h.astype(F32)
    m = r.astype(BF16)
    return h, m, (r - m.astype(F32)).astype(BF16)


def _dot3(parts, w, dims=None):
    if dims is None:
        outs = [jnp.dot(p, w, preferred_element_type=F32) for p in parts]
    else:
        outs = [lax.dot_general(w, p, dims, preferred_element_type=F32) for p in parts]
    return (outs[0] + outs[1]) + outs[2]


def _softplus(x):
    return jnp.maximum(x, 0.0) + jnp.log(1.0 + jnp.exp(-jnp.abs(x)))


def _silu(x):
    return x * jax.nn.sigmoid(x)


def _group_rmsnorm(y, w, n_groups):
    width = y.shape[1] // n_groups
    outs = []
    for g in range(n_groups):
        yg = y[:, g * width:(g + 1) * width]
        outs.append(yg * lax.rsqrt(jnp.mean(yg * yg, axis=-1, keepdims=True) + NORM_EPS))
    return jnp.concatenate(outs, axis=1) * w


def _conv_tile(xbuf, cw_ref, cb_ref, rows):
    y = cb_ref[...]
    for k in range(SSD_CONV):
        y = y + cw_ref[k:k + 1, :] * xbuf[pl.ds(CONV_TAIL - (SSD_CONV - 1) + k, rows), :]
    return y


def _ssd_prompt_body(z_ref, xs_ref, bc_ref, dt_ref, cw_ref, cb_ref, dtb_ref, a_ref, dexp_ref, nw_ref, tri_ref,
                     y_ref, conv_ref, state_ref, xbuf, h_scr):
    q = SSD_CHUNK
    c = pl.program_id(1)
    last = pl.num_programs(1) - 1
    P, N = SSD_HEAD_DIM, SSD_STATE

    @pl.when(c == 0)
    def _():
        xbuf[0:CONV_TAIL, :] = jnp.zeros((CONV_TAIL, SSD_XBC), F32)
        h_scr[...] = jnp.zeros_like(h_scr)

    xbuf[CONV_TAIL:CONV_TAIL + q, 0:SSD_WIDTH] = xs_ref[...]
    xbuf[CONV_TAIL:CONV_TAIL + q, SSD_WIDTH:SSD_XBC] = bc_ref[...]
    xc = _silu(_conv_tile(xbuf, cw_ref, cb_ref, q))

    @pl.when(c == last)
    def _():
        conv_ref[0] = xbuf[CONV_TAIL + q - (SSD_CONV - 1):CONV_TAIL + q, :]

    xbuf[0:CONV_TAIL, :] = xbuf[q:q + CONV_TAIL, :]

    xs = xc[:, 0:SSD_WIDTH]
    bm = xc[:, SSD_WIDTH:SSD_WIDTH + SSD_GROUPS * N].astype(BF16)
    cm = xc[:, SSD_WIDTH + SSD_GROUPS * N:SSD_XBC].astype(BF16)
    dt = _softplus(dt_ref[...] + dtb_ref[...])
    a = dt * a_ref[...]
    a_cs = _dot3(_split3_bf16(a), tri_ref[...], dims=(((1,), (0,)), ((), ())))
    a_cs_t = a_cs.T
    dt_t = dt.T
    a_end_t = a_cs_t[:, q - 1:q]
    w_t = dt_t * jnp.exp(a_end_t - a_cs_t)
    ea = jnp.exp(a_cs)
    xs_t = xs.T
    row = lax.broadcasted_iota(jnp.int32, (q, q), 0)
    col = lax.broadcasted_iota(jnp.int32, (q, q), 1)
    causal = col <= row
    heads_per_group = SSD_HEADS // SSD_GROUPS
    ys = []
    for g in range(SSD_GROUPS):
        bg = bm[:, g * N:(g + 1) * N]
        cg = cm[:, g * N:(g + 1) * N]
        cb = lax.dot_general(cg, bg, _NT_DIMS_SSD, preferred_element_type=F32)
        for k in range(heads_per_group):
            h = g * heads_per_group + k
            seg = a_cs[:, h:h + 1] - a_cs_t[h:h + 1, :]
            decay = jnp.exp(jnp.where(causal, seg, MASK_NEG))
            xh = xs[:, h * P:(h + 1) * P]
            xdt = (xh * dt[:, h:h + 1]).astype(BF16)
            y_diag = jnp.dot((cb * decay).astype(BF16), xdt, preferred_element_type=F32)
            h_prev = h_scr[h]
            y_off = lax.dot_general(cg, h_prev.astype(BF16), _NT_DIMS_SSD,
                                    preferred_element_type=F32) * ea[:, h:h + 1]
            st = jnp.dot((xs_t[h * P:(h + 1) * P, :] * w_t[h:h + 1, :]).astype(BF16), bg,
                         preferred_element_type=F32)
            h_scr[h] = h_prev * jnp.exp(a_end_t[h:h + 1, :]) + st
            ys.append(y_diag + y_off)
    y = jnp.concatenate(ys, axis=1) + dexp_ref[...] * xs
    y = y * _silu(z_ref[...])
    y_ref[...] = _group_rmsnorm(y, nw_ref[...], SSD_GROUPS)

    @pl.when(c == last)
    def _():
        state_ref[0] = h_scr[...]


_NT_DIMS_SSD = (((1,), (1,)), ((), ()))
MASK_NEG = -1e30


def _pad_lanes(v, n=V7X_LANES):
    return jnp.pad(v, (0, n - v.shape[0]))[None, :]


def ssd_prompt(proj, b, seq_len, conv_w, conv_b, dt_bias, a_log, d_skip, norm_w):
    q = SSD_CHUNK
    nc = seq_len // q
    t = b * seq_len
    tri = jnp.asarray(np.tril(np.ones((q, q), np.float32)), BF16)
    a_neg = _pad_lanes(-jnp.exp(a_log))
    dtb = _pad_lanes(dt_bias)
    dexp = jnp.repeat(d_skip, SSD_HEAD_DIM)[None, :]
    nw = norm_w[None, :]
    cb = conv_b[None, :]
    colblk = lambda j, w=SSD_WIDTH: pl.BlockSpec((q, w), lambda bi, c: (bi * nc + c, j))
    full = lambda a: pl.BlockSpec(a.shape, lambda bi, c: (0,) * a.ndim)
    return pl.pallas_call(
        _ssd_prompt_body,
        out_shape=(jax.ShapeDtypeStruct((t, SSD_WIDTH), F32),
                   jax.ShapeDtypeStruct((b, SSD_CONV - 1, SSD_XBC), F32),
                   jax.ShapeDtypeStruct((b, SSD_HEADS, SSD_HEAD_DIM, SSD_STATE), F32)),
        grid=(b, nc),
        in_specs=[colblk(0), colblk(1), colblk(2),
                  pl.BlockSpec((q, V7X_LANES), lambda bi, c: (bi * nc + c, 5 * SSD_WIDTH // V7X_LANES)),
                  full(conv_w), full(cb), full(dtb), full(a_neg), full(dexp), full(nw), full(tri)],
        out_specs=(pl.BlockSpec((q, SSD_WIDTH), lambda bi, c: (bi * nc + c, 0)),
                   pl.BlockSpec((1, SSD_CONV - 1, SSD_XBC), lambda bi, c: (bi, 0, 0)),
                   pl.BlockSpec((1, SSD_HEADS, SSD_HEAD_DIM, SSD_STATE), lambda bi, c: (bi, 0, 0, 0))),
        scratch_shapes=[pltpu.VMEM((CONV_TAIL + q, SSD_XBC), F32),
                        pltpu.VMEM((SSD_HEADS, SSD_HEAD_DIM, SSD_STATE), F32)],
        compiler_params=_cparams(("arbitrary", "arbitrary"), V7X_VMEM_LIMIT_BYTES),
        name="ssd_prompt",
    )(proj, proj, proj, proj, conv_w, cb, dtb, a_neg, dexp, nw, tri)


def _ssd_sample_pre_body(xs_ref, bc_ref, dt_ref, s0_ref, s1_ref, s2_ref, cw_ref, cb_ref, dtb_ref, a_ref, dexp_ref,
                         hexp_ref, yd_ref, xdt_ref, b_ref, c_ref, ea_ref, eaexp_ref):
    N = SSD_STATE
    x_new = jnp.concatenate([xs_ref[...], bc_ref[...]], axis=1)
    y = (cb_ref[...] + cw_ref[0:1, :] * s0_ref[...] + cw_ref[1:2, :] * s1_ref[...]
         + cw_ref[2:3, :] * s2_ref[...] + cw_ref[3:4, :] * x_new)
    xc = _silu(y)
    xs = xc[:, 0:SSD_WIDTH]
    bm = xc[:, SSD_WIDTH:SSD_WIDTH + SSD_GROUPS * N]
    cm = xc[:, SSD_WIDTH + SSD_GROUPS * N:SSD_XBC]
    dt = _softplus(dt_ref[...] + dtb_ref[...])
    ea = jnp.exp(dt * a_ref[...])
    hexp = hexp_ref[...]
    dt_exp = _dot3(_split3_bf16(dt), hexp)
    xdt = (xs * dt_exp).astype(BF16)
    bb = bm.astype(BF16)
    cc = cm.astype(BF16)
    prod = bb.astype(F32) * cc.astype(F32)
    hw = SSD_WIDTH // SSD_GROUPS
    cb = jnp.concatenate(
        [jnp.broadcast_to(jnp.sum(prod[:, g * N:(g + 1) * N], axis=-1, keepdims=True), (xs.shape[0], hw))
         for g in range(SSD_GROUPS)], axis=1)
    yd_ref[...] = cb.astype(BF16).astype(F32) * xdt.astype(F32) + dexp_ref[...] * xs
    xdt_ref[...] = xdt.astype(F32)
    b_ref[...] = bb.astype(F32)
    c_ref[...] = cc.astype(F32)
    ea_ref[...] = ea
    eaexp_ref[...] = _dot3(_split3_bf16(ea), hexp)


def _ssd_sample_state_body(ea_smem, xdt_ref, b_ref, c_ref, yd_ref, eaexp_ref, z_ref, nw_ref, h0_ref,
                           y_ref, h1_ref):
    i = pl.program_id(0)
    N = SSD_STATE
    gw = SSD_WIDTH // SSD_GROUPS
    heads_per_group = SSD_HEADS // SSD_GROUPS
    row0 = lax.broadcasted_iota(jnp.int32, (8, 1), 0) == 0
    y_off = []
    for g in range(SSD_GROUPS):
        x8 = jnp.broadcast_to(xdt_ref[0, :, g * gw:(g + 1) * gw], (8, gw)).astype(BF16)
        b8 = jnp.where(row0, jnp.broadcast_to(b_ref[0, :, g * N:(g + 1) * N], (8, N)), 0.0).astype(BF16)
        c8 = jnp.broadcast_to(c_ref[0, :, g * N:(g + 1) * N], (8, N)).astype(BF16)
        h0g = h0_ref[0, g * gw:(g + 1) * gw, :]
        st = lax.dot_general(x8, b8, (((0,), (0,)), ((), ())), preferred_element_type=F32)
        yo = lax.dot_general(c8, h0g.astype(BF16), _NT_DIMS_SSD, preferred_element_type=F32)
        y_off.append(yo[0:1, :])
        for k in range(heads_per_group):
            h = g * heads_per_group + k
            r = slice(k * SSD_HEAD_DIM, (k + 1) * SSD_HEAD_DIM)
            h1_ref[0, g * gw + k * SSD_HEAD_DIM:g * gw + (k + 1) * SSD_HEAD_DIM, :] = (
                h0g[r, :] * ea_smem[i, h] + st[r, :])
    y = jnp.concatenate(y_off, axis=1) * eaexp_ref[0] + yd_ref[0]
    y = y * _silu(z_ref[0])
    y_ref[0] = _group_rmsnorm(y, nw_ref[...], SSD_GROUPS)


def ssd_sample(proj, conv_state, ssm_state, conv_w, conv_b, dt_bias, a_log, d_skip, norm_w):
    bsz = proj.shape[0]
    H, P, N = SSD_HEADS, SSD_HEAD_DIM, SSD_STATE
    a_neg = _pad_lanes(-jnp.exp(a_log))
    dtb = _pad_lanes(dt_bias)
    dexp = jnp.repeat(d_skip, P)[None, :]
    hexp = jnp.asarray((np.arange(V7X_LANES)[:, None] == (np.arange(H * P)[None, :] // P)).astype(np.float32), BF16)
    cb = conv_b[None, :]
    s0, s1, s2 = conv_state[:, 0], conv_state[:, 1], conv_state[:, 2]
    blk = lambda j, w: pl.BlockSpec((bsz, w), lambda i: (0, j))
    full = lambda a: pl.BlockSpec(a.shape, lambda i: (0,) * a.ndim)
    o = lambda w, dt_: jax.ShapeDtypeStruct((bsz, w), dt_)
    yd, xdt, bb, cc, ea, eaexp = pl.pallas_call(
        _ssd_sample_pre_body,
        out_shape=(o(SSD_WIDTH, F32), o(SSD_WIDTH, F32), o(SSD_GROUPS * N, F32), o(SSD_GROUPS * N, F32),
                   o(V7X_LANES, F32), o(SSD_WIDTH, F32)),
        grid=(1,),
        in_specs=[blk(1, SSD_WIDTH), blk(2, SSD_WIDTH), blk(5 * SSD_WIDTH // V7X_LANES, V7X_LANES),
                  full(s0), full(s1), full(s2), full(conv_w), full(cb), full(dtb), full(a_neg), full(dexp), full(hexp)],
        out_specs=(full(o(SSD_WIDTH, F32)), full(o(SSD_WIDTH, F32)), full(o(SSD_GROUPS * N, F32)),
                   full(o(SSD_GROUPS * N, F32)), full(o(V7X_LANES, F32)), full(o(SSD_WIDTH, F32))),
        compiler_params=_cparams(("arbitrary",)),
        name="ssd_sample_pre",
    )(proj, proj, proj, s0, s1, s2, conv_w, cb, dtb, a_neg, dexp, hexp)
    x_new = jnp.concatenate([proj[:, SSD_WIDTH:2 * SSD_WIDTH], proj[:, 2 * SSD_WIDTH:3 * SSD_WIDTH]], axis=1)
    conv_new = jnp.stack([s1, s2, x_new], axis=1)
    z3 = proj[:, 0:SSD_WIDTH].reshape(bsz, 1, SSD_WIDTH)
    row = lambda w: pl.BlockSpec((1, 1, w), lambda i, ea_: (i, 0, 0))
    nw = norm_w[None, :]
    grid_spec = pltpu.PrefetchScalarGridSpec(
        num_scalar_prefetch=1,
        grid=(bsz,),
        in_specs=[row(SSD_WIDTH), row(SSD_GROUPS * N), row(SSD_GROUPS * N), row(SSD_WIDTH), row(SSD_WIDTH),
                  row(SSD_WIDTH), pl.BlockSpec(nw.shape, lambda i, ea_: (0, 0)),
                  pl.BlockSpec((1, H * P, N), lambda i, ea_: (i, 0, 0))],
        out_specs=(row(SSD_WIDTH), pl.BlockSpec((1, H * P, N), lambda i, ea_: (i, 0, 0))),
    )
    r3 = lambda a: a.reshape(bsz, 1, a.shape[1])
    y, h1 = pl.pallas_call(
        _ssd_sample_state_body,
        out_shape=(jax.ShapeDtypeStruct((bsz, 1, SSD_WIDTH), F32), jax.ShapeDtypeStruct((bsz, H * P, N), F32)),
        grid_spec=grid_spec,
        compiler_params=_cparams(("arbitrary",)),
        name="ssd_sample_state",
    )(ea[:, :H], r3(xdt), r3(bb), r3(cc), r3(yd), r3(eaexp), z3, nw, ssm_state.reshape(bsz, H * P, N))
    return y.reshape(bsz, SSD_WIDTH), conv_new, h1.reshape(bsz, H, P, N)


def _rg_gates(xc, wa_ref, ba_ref, wi_ref, bi_ref, sp_ref):
    xb = xc.astype(BF16)
    r = jax.nn.sigmoid(jnp.dot(xb, wa_ref[...], preferred_element_type=F32) + ba_ref[...])
    ig = jax.nn.sigmoid(jnp.dot(xb, wi_ref[...], preferred_element_type=F32) + bi_ref[...])
    log_a = -RG_C * r * sp_ref[...]
    a = jnp.exp(log_a)
    u = jnp.sqrt(1.0 - jnp.exp(2.0 * log_a)) * (ig * xc)
    return a, u


def _rg_prompt_body(gate_ref, xr_ref, cw_ref, cb_ref, wa_ref, ba_ref, wi_ref, bi_ref, sp_ref,
                    y_ref, conv_ref, state_ref, xbuf, h_scr):
    rows = RG_TILE
    c = pl.program_id(1)
    last = pl.num_programs(1) - 1

    @pl.when(c == 0)
    def _():
        xbuf[0:CONV_TAIL, :] = jnp.zeros((CONV_TAIL, RG_WIDTH), F32)
        h_scr[...] = jnp.zeros_like(h_scr)

    xbuf[CONV_TAIL:CONV_TAIL + rows, :] = xr_ref[...]
    xc = _conv_tile(xbuf, cw_ref, cb_ref, rows)

    @pl.when(c == last)
    def _():
        conv_ref[0] = xbuf[CONV_TAIL + rows - (SSD_CONV - 1):CONV_TAIL + rows, :]

    xbuf[0:CONV_TAIL, :] = xbuf[rows:rows + CONV_TAIL, :]
    a, u = _rg_gates(xc, wa_ref, ba_ref, wi_ref, bi_ref, sp_ref)
    t_idx = lax.broadcasted_iota(jnp.int32, (rows, 1), 0)
    d = 1
    while d < rows:
        keep = t_idx >= d
        a_sh = jnp.where(keep, pltpu.roll(a, d, 0), 1.0)
        u_sh = jnp.where(keep, pltpu.roll(u, d, 0), 0.0)
        u = u + a * u_sh
        a = a * a_sh
        d *= 2
    h = u + a * h_scr[0:1, :]
    h_scr[0:1, :] = h[rows - 1:rows, :]
    y_ref[...] = h * _gelu_tanh(gate_ref[...])

    @pl.when(c == last)
    def _():
        state_ref[0] = h[rows - 1:rows, :]


def _rg_weights(wa, ba, wi, bi, lam):
    eye = jnp.eye(RG_BLOCKS, dtype=F32)
    bd = lambda w: jnp.einsum('nde,nm->ndme', w, eye).reshape(RG_WIDTH, RG_WIDTH).astype(BF16)
    return bd(wa), ba[None, :], bd(wi), bi[None, :], jax.nn.softplus(-lam)[None, :]


def rg_prompt(proj, b, seq_len, conv_w, conv_b, wa, ba, wi, bi, lam):
    rows = RG_TILE
    nt = seq_len // rows
    t = b * seq_len
    wts = _rg_weights(wa, ba, wi, bi, lam)
    cb = conv_b[None, :]
    full = lambda a: pl.BlockSpec(a.shape, lambda bi_, c: (0,) * a.ndim)
    return pl.pallas_call(
        _rg_prompt_body,
        out_shape=(jax.ShapeDtypeStruct((t, RG_WIDTH), F32),
                   jax.ShapeDtypeStruct((b, SSD_CONV - 1, RG_WIDTH), F32),
                   jax.ShapeDtypeStruct((b, 1, RG_WIDTH), F32)),
        grid=(b, nt),
        in_specs=[pl.BlockSpec((rows, RG_WIDTH), lambda bi_, c: (bi_ * nt + c, 3)),
                  pl.BlockSpec((rows, RG_WIDTH), lambda bi_, c: (bi_ * nt + c, 4)),
                  full(conv_w), full(cb)] + [full(w) for w in wts],
        out_specs=(pl.BlockSpec((rows, RG_WIDTH), lambda bi_, c: (bi_ * nt + c, 0)),
                   pl.BlockSpec((1, SSD_CONV - 1, RG_WIDTH), lambda bi_, c: (bi_, 0, 0)),
                   pl.BlockSpec((1, 1, RG_WIDTH), lambda bi_, c: (bi_, 0, 0))),
        scratch_shapes=[pltpu.VMEM((CONV_TAIL + rows, RG_WIDTH), F32), pltpu.VMEM((8, RG_WIDTH), F32)],
        compiler_params=_cparams(("arbitrary", "arbitrary"), V7X_VMEM_LIMIT_BYTES),
        name="rg_prompt",
    )(proj, proj, conv_w, cb, *wts)


def _rg_sample_body(gate_ref, xr_ref, s0_ref, s1_ref, s2_ref, h0_ref, cw_ref, cb_ref, wa_ref, ba_ref, wi_ref, bi_ref,
                    sp_ref, y_ref, h1_ref):
    xc = (cb_ref[...] + cw_ref[0:1, :] * s0_ref[...] + cw_ref[1:2, :] * s1_ref[...]
          + cw_ref[2:3, :] * s2_ref[...] + cw_ref[3:4, :] * xr_ref[...])
    a, u = _rg_gates(xc, wa_ref, ba_ref, wi_ref, bi_ref, sp_ref)
    h = a * h0_ref[...] + u
    h1_ref[...] = h
    y_ref[...] = h * _gelu_tanh(gate_ref[...])


def rg_sample(proj, conv_state, h0, conv_w, conv_b, wa, ba, wi, bi, lam):
    bsz = proj.shape[0]
    wts = _rg_weights(wa, ba, wi, bi, lam)
    cb = conv_b[None, :]
    s0, s1, s2 = conv_state[:, 0], conv_state[:, 1], conv_state[:, 2]
    full = lambda a: pl.BlockSpec(a.shape, lambda i: (0,) * a.ndim)
    out = jax.ShapeDtypeStruct((bsz, RG_WIDTH), F32)
    y, h1 = pl.pallas_call(
        _rg_sample_body,
        out_shape=(out, out),
        grid=(1,),
        in_specs=[pl.BlockSpec((bsz, RG_WIDTH), lambda i: (0, 3)), pl.BlockSpec((bsz, RG_WIDTH), lambda i: (0, 4)),
                  full(s0), full(s1), full(s2), full(h0), full(conv_w), full(cb)] + [full(w) for w in wts],
        out_specs=(full(out), full(out)),
        compiler_params=_cparams(("arbitrary",)),
        name="rg_sample",
    )(proj, proj, s0, s1, s2, h0, conv_w, cb, *wts)
    conv_new = jnp.stack([s1, s2, proj[:, 4 * RG_WIDTH:5 * RG_WIDTH]], axis=1)
    return y, conv_new, h1


def _nsa_split(proj, b, L, q_norm, k_norm):
    q = proj[:, :NSA_Q_WIDTH]
    kv = proj[:, NSA_Q_WIDTH:NSA_Q_WIDTH + 6 * NSA_KV_WIDTH]
    g = proj[:, NSA_Q_WIDTH + 6 * NSA_KV_WIDTH:NSA_Q_WIDTH + 6 * NSA_KV_WIDTH + 3 * NSA_HEADS]
    q = _rmsnorm(q.reshape(b, L, NSA_KV_HEADS, NSA_GROUP, NSA_HEAD_DIM), q_norm) * (NSA_HEAD_DIM ** -0.5)
    kv = kv.reshape(b, L, 6, NSA_KV_HEADS, NSA_HEAD_DIM)
    k_slc = _rmsnorm(kv[:, :, 2], k_norm[1])
    k_win = _rmsnorm(kv[:, :, 4], k_norm[2])
    rows = jnp.stack([kv[:, :, 0], kv[:, :, 1], k_slc, kv[:, :, 3]], axis=2)
    win = jnp.stack([k_win, kv[:, :, 5]], axis=2)
    gates = jax.nn.sigmoid(g).reshape(b, L, NSA_KV_HEADS, NSA_GROUP, 3)
    return q, rows, win, gates


def _masked_softmax(s, mask):
    s = jnp.where(mask, s.astype(F32), -jnp.inf)
    m = jnp.max(s, axis=-1, keepdims=True)
    e = jnp.exp(s - jnp.where(jnp.isfinite(m), m, 0.0))
    d = jnp.sum(e, axis=-1, keepdims=True)
    return e / jnp.where(d > 0, d, 1.0)


def _compress(r, w1, pe, w2):
    b, T, G, dh = r.shape
    n_chunk = T // CMP_STRIDE
    nc = n_chunk - CMP_R + 1
    ch = r[:, :n_chunk * CMP_STRIDE].reshape(b, n_chunk, CMP_STRIDE, G, dh)
    proj = jnp.einsum('bcsgd,rsdh->bcrgh', ch, w1.reshape(CMP_R, CMP_STRIDE, dh, CMP_HID))
    hid = jnp.einsum('ld,ldh->h', pe, w1)
    for rr in range(CMP_R):
        hid = hid + proj[:, rr:rr + nc, rr]
    return jax.nn.gelu(hid) @ w2


def _nsa_context(rows, cmp_w1, cmp_pe, cmp_w2, k_norm_cmp):
    b, T = rows.shape[:2]
    kc = _rmsnorm(_compress(rows[:, :, 0], cmp_w1[0], cmp_pe[0], cmp_w2[0]), k_norm_cmp)
    vc = _compress(rows[:, :, 1], cmp_w1[1], cmp_pe[1], cmp_w2[1])
    ns = -(-T // SEL_BLOCK)
    sel = jnp.pad(rows[:, :, 2:4], ((0, 0), (0, ns * SEL_BLOCK - T), (0, 0), (0, 0), (0, 0)))
    sel = sel.reshape(b, ns, SEL_BLOCK, 2, NSA_KV_HEADS, NSA_HEAD_DIM).transpose(3, 0, 4, 1, 2, 5)
    return kc, vc, sel[0], sel[1]


def _overlap_matrix(nc, ns):
    i = np.arange(nc)[:, None]
    j = np.arange(ns)[None, :]
    ov = (i * CMP_STRIDE < (j + 1) * SEL_BLOCK) & (i * CMP_STRIDE + CMP_LEN > j * SEL_BLOCK)
    return ov.astype(np.float32)


def _nsa_attend(q, gates, t_pos, kc, vc, ks, vs, kw, vw, w_pos):
    b, Q, G, K, dh = q.shape
    nc = kc.shape[1]
    ns = ks.shape[2]
    tq = t_pos[None, :, None, None, None]
    c_end = jnp.arange(nc) * CMP_STRIDE + CMP_LEN - 1
    p_c = _masked_softmax(jnp.einsum('bqgkd,bngd->bqgkn', q, kc), c_end <= tq)
    o_c = jnp.einsum('bqgkn,bngd->bqgkd', p_c, vc)
    imp = jnp.einsum('bqgkn,ns->bqgs', p_c, jnp.asarray(_overlap_matrix(nc, ns)))
    jj = jnp.arange(ns)[None, :]
    jt = (t_pos // SEL_BLOCK)[:, None]
    valid = jj <= jt
    forced = valid & ((jj == 0) | (jj == jt) | (jj == jt - 1))
    imp = jnp.where(forced[None, :, None], jnp.inf, jnp.where(valid[None, :, None], imp, -jnp.inf))
    _, idx = lax.top_k(imp, min(SEL_TOPN, ns))
    n = idx.shape[-1]
    bi = jnp.arange(b)[:, None, None, None]
    gi = jnp.arange(G)[None, None, :, None]
    k_sel = ks[bi, gi, idx].reshape(b, Q, G, n * SEL_BLOCK, dh)
    v_sel = vs[bi, gi, idx].reshape(b, Q, G, n * SEL_BLOCK, dh)
    kpos = (idx[..., None] * SEL_BLOCK + jnp.arange(SEL_BLOCK)).reshape(b, Q, G, 1, n * SEL_BLOCK)
    p_s = _masked_softmax(jnp.einsum('bqgkd,bqgmd->bqgkm', q, k_sel), kpos <= tq)
    o_s = jnp.einsum('bqgkm,bqgmd->bqgkd', p_s, v_sel)
    m_w = (w_pos <= tq) & (w_pos > tq - WINDOW) & (w_pos >= 0)
    p_w = _masked_softmax(jnp.einsum('bqgkd,bwgd->bqgkw', q, kw), m_w)
    o_w = jnp.einsum('bqgkw,bwgd->bqgkd', p_w, vw)
    o = gates[..., 0:1] * o_c + gates[..., 1:2] * o_s + gates[..., 2:3] * o_w
    return o.reshape(b, Q, G * K * dh)


def _nsa_prompt_core(proj, b, L, q_norm, k_norm, cmp_w1, cmp_pe, cmp_w2):
    q, rows, win, gates = _nsa_split(proj, b, L, q_norm, k_norm)
    kc, vc, ks, vs = _nsa_context(rows, cmp_w1, cmp_pe, cmp_w2, k_norm[0])
    win_pad = jnp.pad(win, ((0, 0), (WINDOW, 0), (0, 0), (0, 0), (0, 0)))

    def block(i):
        s = i * NSA_Q_BLOCK
        qb = lax.dynamic_slice_in_dim(q, s, NSA_Q_BLOCK, axis=1)
        gb = lax.dynamic_slice_in_dim(gates, s, NSA_Q_BLOCK, axis=1)
        wb = lax.dynamic_slice_in_dim(win_pad, s, WINDOW + NSA_Q_BLOCK, axis=1)
        t_pos = s + jnp.arange(NSA_Q_BLOCK)
        w_pos = s - WINDOW + jnp.arange(WINDOW + NSA_Q_BLOCK)
        return _nsa_attend(qb, gb, t_pos, kc, vc, ks, vs, wb[:, :, 0], wb[:, :, 1], w_pos)

    o = lax.map(block, jnp.arange(L // NSA_Q_BLOCK))
    o = jnp.moveaxis(o, 0, 1).reshape(b, L, NSA_Q_WIDTH)
    return o, rows, win[:, L - min(WINDOW, L):]


def _nsa_sample_core(proj, b, L, cache, page_table, win_buf, q_norm, k_norm, cmp_w1, cmp_pe, cmp_w2):
    q, rows, win, gates = _nsa_split(proj, b, L, q_norm, k_norm)
    past = page_table.shape[1] * cache.shape[1]
    past_rows = cache[page_table].reshape(b, past, 4, NSA_KV_HEADS, NSA_HEAD_DIM)
    kc, vc, ks, vs = _nsa_context(jnp.concatenate([past_rows, rows], axis=1),
                                  cmp_w1, cmp_pe, cmp_w2, k_norm[0])
    wb_len = win_buf.shape[1]
    wk = jnp.concatenate([win_buf, win], axis=1)
    t_pos = past + jnp.arange(L)
    w_pos = past - wb_len + jnp.arange(wb_len + L)
    o = _nsa_attend(q, gates, t_pos, kc, vc, ks, vs, wk[:, :, 0], wk[:, :, 1], w_pos)
    return o, rows, wk[:, L:]


NSA_TQ = 128
NSA_TK_SLC = 512
NSA_TK_WIN = 256
NSA_NS_PAD = 64
SEL_BIAS = -16384.0
MASK_VALUE = -1e30


def _split_bf16(x):
    hi = x.astype(BF16)
    lo = (x - hi.astype(F32)).astype(BF16)
    return hi, lo


def _seg_rms_scale(x, seg, seg_t):
    hi, lo = _split_bf16(x * x)
    ss = jnp.dot(hi, seg, preferred_element_type=F32) + jnp.dot(lo, seg, preferred_element_type=F32)
    r = lax.rsqrt(ss * (1.0 / NSA_HEAD_DIM) + NORM_EPS)
    rh, rl = _split_bf16(r)
    return jnp.dot(rh, seg_t, preferred_element_type=F32) + jnp.dot(rl, seg_t, preferred_element_type=F32)


def _nsa_prep_body(seq_len, p_ref, wq_ref, wks_ref, wkw_ref, segq_ref, segqt_ref, segk_ref, segkt_ref,
                   q_ref, rows_ref, win_ref, kaug_ref, vslc_ref, kwin_ref, vwin_ref, gate_ref):
    tm = p_ref.shape[0]
    dh = NSA_HEAD_DIM
    q = p_ref[:, 0:NSA_Q_WIDTH]
    qn = q * _seg_rms_scale(q, segq_ref[...], segqt_ref[...]) * wq_ref[...]
    kv = [p_ref[:, NSA_Q_WIDTH + NSA_KV_WIDTH * j:NSA_Q_WIDTH + NSA_KV_WIDTH * (j + 1)] for j in range(6)]
    ksl = kv[2] * _seg_rms_scale(kv[2], segk_ref[...], segkt_ref[...]) * wks_ref[...]
    kwn = kv[4] * _seg_rms_scale(kv[4], segk_ref[...], segkt_ref[...]) * wkw_ref[...]
    rows_ref[...] = jnp.concatenate([kv[0], kv[1], ksl, kv[3]], axis=1)
    win_ref[...] = jnp.concatenate([kwn, kv[5]], axis=1)
    gates = jax.nn.sigmoid(p_ref[:, NSA_Q_WIDTH + 6 * NSA_KV_WIDTH:NSA_Q_WIDTH + 6 * NSA_KV_WIDTH + V7X_LANES])
    t0 = (pl.program_id(0) * tm) % seq_len
    tpos = t0 + lax.broadcasted_iota(jnp.int32, (tm, NSA_NS_PAD), 0)
    blk = lax.broadcasted_iota(jnp.int32, (tm, NSA_NS_PAD), 1)
    onehot = jnp.where(blk == lax.shift_right_logical(tpos, 6), 1.0, 0.0).astype(BF16)
    for g in range(NSA_KV_HEADS):
        sl = slice(g * dh, (g + 1) * dh)
        kaug_ref[0, g] = jnp.concatenate([ksl[:, sl].astype(BF16), onehot], axis=1)
        vslc_ref[0, g] = kv[3][:, sl].astype(BF16)
        kwin_ref[0, g] = kwn[:, sl].astype(BF16)
        vwin_ref[0, g] = kv[5][:, sl].astype(BF16)
        gate_ref[0, g] = gates if g == 0 else pltpu.roll(gates, V7X_LANES - 3 * NSA_GROUP * g, 1)
        for k in range(NSA_GROUP):
            c0 = (g * NSA_GROUP + k) * dh
            q_ref[0, g, k] = qn[:, c0:c0 + dh].astype(BF16)


def _head_segments(width):
    lane = np.arange(width)[:, None] // NSA_HEAD_DIM
    seg = (lane == np.arange(V7X_LANES)[None, :]).astype(np.float32)
    return jnp.asarray(seg, BF16), jnp.asarray(seg.T, BF16)


def nsa_prep(proj, b, seq_len, q_norm, k_norm, tm):
    t = proj.shape[0]
    G, K, dh = NSA_KV_HEADS, NSA_GROUP, NSA_HEAD_DIM
    wq = (jnp.tile(q_norm, NSA_HEADS) * (dh ** -0.5))[None, :]
    wks = jnp.tile(k_norm[1], G)[None, :]
    wkw = jnp.tile(k_norm[2], G)[None, :]
    segq, segqt = _head_segments(NSA_Q_WIDTH)
    segk, segkt = _head_segments(NSA_KV_WIDTH)
    tiles_per_seq = seq_len // tm
    bi = lambda i: i // tiles_per_seq
    ti = lambda i: i % tiles_per_seq
    full = lambda a: pl.BlockSpec(a.shape, lambda i: (0,) * a.ndim)
    out_shape = (jax.ShapeDtypeStruct((b, G, K, seq_len, dh), BF16),
                 jax.ShapeDtypeStruct((t, 4 * NSA_KV_WIDTH), F32),
                 jax.ShapeDtypeStruct((t, 2 * NSA_KV_WIDTH), F32),
                 jax.ShapeDtypeStruct((b, G, seq_len, 2 * dh), BF16),
                 jax.ShapeDtypeStruct((b, G, seq_len, dh), BF16),
                 jax.ShapeDtypeStruct((b, G, seq_len, dh), BF16),
                 jax.ShapeDtypeStruct((b, G, seq_len, dh), BF16),
                 jax.ShapeDtypeStruct((b, G, seq_len, V7X_LANES), F32))
    per_g = lambda w: pl.BlockSpec((1, G, tm, w), lambda i: (bi(i), 0, ti(i), 0))
    out_specs = (pl.BlockSpec((1, G, K, tm, dh), lambda i: (bi(i), 0, 0, ti(i), 0)),
                 pl.BlockSpec((tm, 4 * NSA_KV_WIDTH), lambda i: (i, 0)),
                 pl.BlockSpec((tm, 2 * NSA_KV_WIDTH), lambda i: (i, 0)),
                 per_g(2 * dh), per_g(dh), per_g(dh), per_g(dh), per_g(V7X_LANES))
    return pl.pallas_call(
        functools.partial(_nsa_prep_body, seq_len),
        out_shape=out_shape,
        grid=(t // tm,),
        in_specs=[pl.BlockSpec((tm, proj.shape[1]), lambda i: (i, 0)),
                  full(wq), full(wks), full(wkw), full(segq), full(segqt), full(segk), full(segkt)],
        out_specs=out_specs,
        compiler_params=_cparams(("arbitrary",), V7X_VMEM_LIMIT_BYTES),
        name="nsa_prep",
    )(proj, wq, wks, wkw, segq, segqt, segk, segkt)


def _gelu_tanh(x):
    return 0.5 * x * (1.0 + jnp.tanh(math.sqrt(2.0 / math.pi) * (x + 0.044715 * (x * x * x))))


def _nsa_compress_body(n_chunk, x0_ref, x1_ref, x2_ref, x3_ref, wk_ref, wv_ref, pe_ref, w1f_ref, w2_ref, kn_ref,
                       kc_ref, vc_ref, pk_scr, pv_scr):
    s = pl.program_id(1)

    @pl.when(s == 0)
    def _():
        pk_scr[...] = jnp.zeros_like(pk_scr)
        pv_scr[...] = jnp.zeros_like(pv_scr)

    xs = [r[pl.ds(s, n_chunk, stride=CMP_STRIDE), :].astype(BF16) for r in (x0_ref, x1_ref, x2_ref, x3_ref)]
    pk_scr[...] += jnp.dot(jnp.concatenate(xs[0:2], axis=1), wk_ref[0], preferred_element_type=F32)
    pv_scr[...] += jnp.dot(jnp.concatenate(xs[2:4], axis=1), wv_ref[0], preferred_element_type=F32)

    @pl.when(s == CMP_STRIDE - 1)
    def _():
        for kv, p_scr, o_ref in ((0, pk_scr, kc_ref), (1, pv_scr, vc_ref)):
            p = p_scr[...]
            p_next = pltpu.roll(p, n_chunk - 1, 0)
            pe_h = jnp.dot(pe_ref[kv], w1f_ref[kv], preferred_element_type=F32)[0:1, :]
            for g in range(NSA_KV_HEADS):
                c0 = g * 2 * CMP_HID
                hid = pe_h + p[:, c0:c0 + CMP_HID] + p_next[:, c0 + CMP_HID:c0 + 2 * CMP_HID]
                y = jnp.dot(_gelu_tanh(hid).astype(BF16), w2_ref[kv], preferred_element_type=F32)
                if kv == 0:
                    y = y * lax.rsqrt(jnp.mean(y * y, axis=-1, keepdims=True) + NORM_EPS) * kn_ref[...]
                o_ref[0, g] = y.astype(BF16)


def nsa_compress(rows, b, seq_len, cmp_w1, cmp_pe, cmp_w2, k_norm_cmp):
    G, dh = NSA_KV_HEADS, NSA_HEAD_DIM
    n_chunk = seq_len // CMP_STRIDE
    w1 = cmp_w1.reshape(2, CMP_R, CMP_STRIDE, dh, CMP_HID)
    eye = jnp.eye(G, dtype=F32)
    wbd = jnp.einsum('vrsdh,gq->vsgdqrh', w1, eye).reshape(2, CMP_STRIDE, G * dh, G * CMP_R * CMP_HID).astype(BF16)
    pe = jnp.broadcast_to(cmp_pe.reshape(2, 1, CMP_LEN * dh), (2, 8, CMP_LEN * dh)).astype(BF16)
    w1f = cmp_w1.reshape(2, CMP_LEN * dh, CMP_HID).astype(BF16)
    w2 = cmp_w2.astype(BF16)
    kn = k_norm_cmp[None, :]
    full = lambda a: pl.BlockSpec(a.shape, lambda bi, s: (0,) * a.ndim)
    return pl.pallas_call(
        functools.partial(_nsa_compress_body, n_chunk),
        out_shape=(jax.ShapeDtypeStruct((b, G, n_chunk, dh), BF16),
                   jax.ShapeDtypeStruct((b, G, n_chunk, dh), BF16)),
        grid=(b, CMP_STRIDE),
        in_specs=[pl.BlockSpec((seq_len, V7X_LANES), lambda bi, s: (bi, 0)),
                  pl.BlockSpec((seq_len, V7X_LANES), lambda bi, s: (bi, 1)),
                  pl.BlockSpec((seq_len, V7X_LANES), lambda bi, s: (bi, 2)),
                  pl.BlockSpec((seq_len, V7X_LANES), lambda bi, s: (bi, 3)),
                  pl.BlockSpec((1, G * dh, G * CMP_R * CMP_HID), lambda bi, s: (s, 0, 0)),
                  pl.BlockSpec((1, G * dh, G * CMP_R * CMP_HID), lambda bi, s: (s, 0, 0)),
                  full(pe), full(w1f), full(w2), full(kn)],
        out_specs=(pl.BlockSpec((1, G, n_chunk, dh), lambda bi, s: (bi, 0, 0, 0)),
                   pl.BlockSpec((1, G, n_chunk, dh), lambda bi, s: (bi, 0, 0, 0))),
        scratch_shapes=[pltpu.VMEM((n_chunk, G * CMP_R * CMP_HID), F32),
                        pltpu.VMEM((n_chunk, G * CMP_R * CMP_HID), F32)],
        compiler_params=_cparams(("arbitrary", "arbitrary"), V7X_VMEM_LIMIT_BYTES),
        name="nsa_compress",
    )(rows, rows, rows, rows, wbd[0], wbd[1], pe, w1f, w2, kn)


_NT_DIMS = (((1,), (1,)), ((), ()))


def _flash_branch(q2, k_ref, v_ref, lo, hi, tk, mask_fn):
    rows = q2.shape[0]

    def body(j, carry):
        m, l, acc = carry
        k0 = pl.multiple_of(j * tk, tk)
        k = k_ref[0, 0, pl.ds(k0, tk), :]
        v = v_ref[0, 0, pl.ds(k0, tk), :]
        s = lax.dot_general(q2, k, _NT_DIMS, preferred_element_type=F32)
        s = jnp.where(mask_fn(k0), s, MASK_VALUE)
        m_new = jnp.maximum(m, jnp.max(s, axis=-1, keepdims=True))
        alpha = jnp.exp(m - m_new)
        p = jnp.exp(s - m_new)
        l = alpha * l + jnp.sum(p, axis=-1, keepdims=True)
        acc = alpha * acc + jnp.dot(p.astype(BF16), v, preferred_element_type=F32)
        return m_new, l, acc

    init = (jnp.full((rows, 1), MASK_VALUE, F32), jnp.zeros((rows, 1), F32),
            jnp.zeros((rows, NSA_HEAD_DIM), F32))
    _, l, acc = lax.fori_loop(lo, hi, body, init)
    return acc / l


def _nsa_attn_body(n_cmp, q_ref, kc_ref, vc_ref, kaug_ref, vslc_ref, kwin_ref, vwin_ref, gate_ref, ovt_ref, o_ref):
    tq = NSA_TQ
    rows = NSA_GROUP * tq
    q0 = pl.program_id(2) * tq
    q2 = q_ref[0, 0].reshape(rows, NSA_HEAD_DIM)
    row_t = q0 + jnp.bitwise_and(lax.broadcasted_iota(jnp.int32, (rows, 1), 0), tq - 1)

    n_pad = kc_ref.shape[2]
    s = lax.dot_general(q2, kc_ref[0, 0], _NT_DIMS, preferred_element_type=F32)
    n_idx = lax.broadcasted_iota(jnp.int32, (1, n_pad), 1)
    cmask = jnp.logical_and(n_idx * CMP_STRIDE + (CMP_LEN - 1) <= row_t, n_idx < n_cmp)
    s = jnp.where(cmask, s, MASK_VALUE)
    m = jnp.max(s, axis=-1, keepdims=True)
    e = jnp.where(cmask, jnp.exp(s - m), 0.0)
    den = jnp.sum(e, axis=-1, keepdims=True)
    p_c = e / jnp.where(den > 0.0, den, 1.0)
    o_c = jnp.dot(p_c.astype(BF16), vc_ref[0, 0], preferred_element_type=F32)

    p_sum = (p_c[0:tq] + p_c[tq:2 * tq]) + (p_c[2 * tq:3 * tq] + p_c[3 * tq:4 * tq])
    ph, plo = _split_bf16(p_sum)
    ovt = ovt_ref[...]
    imp = (lax.dot_general(ovt, ph, _NT_DIMS, preferred_element_type=F32)
           + lax.dot_general(ovt, plo, _NT_DIMS, preferred_element_type=F32))
    blk = lax.broadcasted_iota(jnp.int32, (NSA_NS_PAD, tq), 0)
    jt = lax.shift_right_logical(q0 + lax.broadcasted_iota(jnp.int32, (NSA_NS_PAD, tq), 1), 6)
    valid = blk <= jt
    forced = jnp.logical_and(valid, jnp.logical_or(blk == 0, jnp.logical_or(blk == jt, blk == jt - 1)))
    eff = jnp.where(forced, jnp.inf, jnp.where(valid, imp, -jnp.inf))
    rank = jnp.zeros((NSA_NS_PAD, tq), jnp.int32)
    for j in range(NSA_NS_PAD):
        other = eff[j:j + 1, :]
        ahead = jnp.logical_or(other > eff, jnp.logical_and(other == eff, blk > j))
        rank = rank + ahead.astype(jnp.int32)
    sel = jnp.logical_and(valid, rank < SEL_TOPN)
    sel_bias = jnp.where(sel, 0.0, SEL_BIAS).T.astype(BF16)

    q_aug = jnp.concatenate([q2, jnp.concatenate([sel_bias] * NSA_GROUP, axis=0)], axis=1)
    hi = (q0 + tq - 1) // NSA_TK_SLC + 1

    def slc_mask(k0):
        kpos = k0 + lax.broadcasted_iota(jnp.int32, (1, NSA_TK_SLC), 1)
        return kpos <= row_t

    o_s = _flash_branch(q_aug, kaug_ref, vslc_ref, 0, hi, NSA_TK_SLC, slc_mask)

    def win_mask(k0):
        kpos = k0 + lax.broadcasted_iota(jnp.int32, (1, NSA_TK_WIN), 1)
        return jnp.logical_and(kpos <= row_t, kpos > row_t - WINDOW)

    lo_w = jnp.maximum(q0 - (WINDOW - 1), 0) // NSA_TK_WIN
    hi_w = (q0 + tq - 1) // NSA_TK_WIN + 1
    o_w = _flash_branch(q2, kwin_ref, vwin_ref, lo_w, hi_w, NSA_TK_WIN, win_mask)

    gt = gate_ref[0, 0]
    outs = []
    for k in range(NSA_GROUP):
        r = slice(k * tq, (k + 1) * tq)
        outs.append(gt[:, 3 * k:3 * k + 1] * o_c[r] + gt[:, 3 * k + 1:3 * k + 2] * o_s[r]
                    + gt[:, 3 * k + 2:3 * k + 3] * o_w[r])
    o_ref[...] = jnp.concatenate(outs, axis=1)


def nsa_attention(q, kc, vc, kaug, vslc, kwin, vwin, gates, b, seq_len):
    G, K, dh = NSA_KV_HEADS, NSA_GROUP, NSA_HEAD_DIM
    tq = NSA_TQ
    nq = seq_len // tq
    n_chunk = kc.shape[2]
    n_cmp = n_chunk - CMP_R + 1
    ns = seq_len // SEL_BLOCK
    ovt = np.zeros((NSA_NS_PAD, n_chunk), np.float32)
    ovt[:ns, :n_cmp] = _overlap_matrix(n_cmp, ns).T
    ovt = jnp.asarray(ovt, BF16)
    seq_spec = lambda w: pl.BlockSpec((1, 1, seq_len, w), lambda bi, g, qi: (bi, g, 0, 0))
    return pl.pallas_call(
        functools.partial(_nsa_attn_body, n_cmp),
        out_shape=jax.ShapeDtypeStruct((b * seq_len, NSA_Q_WIDTH), F32),
        grid=(b, G, nq),
        in_specs=[pl.BlockSpec((1, 1, K, tq, dh), lambda bi, g, qi: (bi, g, 0, qi, 0)),
                  pl.BlockSpec((1, 1, n_chunk, dh), lambda bi, g, qi: (bi, g, 0, 0)),
                  pl.BlockSpec((1, 1, n_chunk, dh), lambda bi, g, qi: (bi, g, 0, 0)),
                  seq_spec(2 * dh), seq_spec(dh), seq_spec(dh), seq_spec(dh),
                  pl.BlockSpec((1, 1, tq, V7X_LANES), lambda bi, g, qi: (bi, g, qi, 0)),
                  pl.BlockSpec(ovt.shape, lambda bi, g, qi: (0, 0))],
        out_specs=pl.BlockSpec((tq, K * dh), lambda bi, g, qi: (bi * nq + qi, g)),
        compiler_params=_cparams(("arbitrary", "arbitrary", "arbitrary"), V7X_VMEM_LIMIT_BYTES),
        name="nsa_attention",
    )(q, kc, vc, kaug, vslc, kwin, vwin, gates, ovt)


def nsa_prompt_pallas(proj, b, seq_len, q_norm, k_norm, cmp_w1, cmp_pe, cmp_w2):
    q, rows, win, kaug, vslc, kwin, vwin, gates = nsa_prep(proj, b, seq_len, q_norm, k_norm, ROW_TILE)
    kc, vc = nsa_compress(rows, b, seq_len, cmp_w1, cmp_pe, cmp_w2, k_norm[0])
    o = nsa_attention(q, kc, vc, kaug, vslc, kwin, vwin, gates, b, seq_len)
    rows_out = rows.reshape(b, seq_len, 4, NSA_KV_HEADS, NSA_HEAD_DIM)
    wlen = min(WINDOW, seq_len)
    win_out = win.reshape(b, seq_len, 2, NSA_KV_HEADS, NSA_HEAD_DIM)[:, seq_len - wlen:]
    return o, rows_out, win_out


def _diag_heads(o_full):
    g_row = lax.shift_right_logical(lax.broadcasted_iota(jnp.int32, (NSA_HEADS, 1), 0), 2)
    out = jnp.zeros((NSA_HEADS, NSA_HEAD_DIM), F32)
    for g in range(NSA_KV_HEADS):
        out = out + jnp.where(g_row == g, o_full[:, g * NSA_HEAD_DIM:(g + 1) * NSA_HEAD_DIM], 0.0)
    return out


def _nsa_sample_body(n_pages, page_rows, pt_ref,
                     cache_hbm, qbd_ref, rown_ref, winn_ref, winbuf_ref, wc_ref, pe_ref, w1f_ref, w2_ref, kn_ref,
                     gsum_ref, ovs_ref, rep_ref, eblk_ref,
                     oc_ref, os_ref, ow_ref,
                     cmp_buf, slc_buf, sem):
    i = pl.program_id(0)
    nb = pl.num_programs(0)
    slot = i % 2
    past = n_pages * page_rows
    n_chunk = past // CMP_STRIDE
    n_cmp = n_chunk - CMP_R + 1
    t_pos = past
    kvw = NSA_KV_WIDTH

    def page_copies(bi, sl):
        copies = []
        for p in range(n_pages):
            pg = pt_ref[bi, p]
            rows = pl.ds(p * page_rows, page_rows)
            for j in range(4):
                copies.append(pltpu.make_async_copy(cache_hbm.at[pg, :, pl.ds(j * V7X_LANES, V7X_LANES)],
                                                    cmp_buf.at[sl, j, rows, :], sem.at[sl]))
            copies.append(pltpu.make_async_copy(cache_hbm.at[pg, :, pl.ds(2 * kvw, 2 * kvw)],
                                                slc_buf.at[sl, rows, :], sem.at[sl]))
        return copies

    @pl.when(i == 0)
    def _():
        for c in page_copies(0, 0):
            c.start()

    @pl.when(i + 1 < nb)
    def _():
        for c in page_copies(i + 1, 1 - slot):
            c.start()

    for c in page_copies(i, slot):
        c.wait()

    parts = []
    for j in range(4):
        acc = None
        for s in range(CMP_STRIDE):
            xs = cmp_buf[slot, j, pl.ds(s, n_chunk, stride=CMP_STRIDE), :].astype(BF16)
            d = jnp.dot(xs, wc_ref[j // 2, s], preferred_element_type=F32)
            acc = d if acc is None else acc + d
        parts.append(acc)
    slabs = []
    for kv in range(2):
        p = jnp.concatenate(parts[2 * kv:2 * kv + 2], axis=1)
        p_next = pltpu.roll(p, n_chunk - 1, 0)
        pe_h = jnp.dot(pe_ref[kv], w1f_ref[kv], preferred_element_type=F32)[0:1, :]
        ys = []
        for g in range(NSA_KV_HEADS):
            c0 = g * 2 * CMP_HID
            hid = pe_h + p[:, c0:c0 + CMP_HID] + p_next[:, c0 + CMP_HID:c0 + 2 * CMP_HID]
            y = jnp.dot(_gelu_tanh(hid).astype(BF16), w2_ref[kv], preferred_element_type=F32)
            if kv == 0:
                y = y * lax.rsqrt(jnp.mean(y * y, axis=-1, keepdims=True) + NORM_EPS) * kn_ref[...]
            ys.append(y)
        slabs.append(jnp.concatenate(ys, axis=1).astype(BF16))
    kc, vc = slabs

    qbd = qbd_ref[0]
    qf = qbd.astype(F32)

    s_c = lax.dot_general(qbd, kc, _NT_DIMS, preferred_element_type=F32)
    n_idx = lax.broadcasted_iota(jnp.int32, (1, n_chunk), 1)
    cmask = jnp.logical_and(n_idx * CMP_STRIDE + (CMP_LEN - 1) <= t_pos, n_idx < n_cmp)
    s_c = jnp.where(cmask, s_c, MASK_VALUE)
    m = jnp.max(s_c, axis=-1, keepdims=True)
    e = jnp.where(cmask, jnp.exp(s_c - m), 0.0)
    den = jnp.sum(e, axis=-1, keepdims=True)
    p_c = e / jnp.where(den > 0.0, den, 1.0)
    oc_ref[0] = _diag_heads(jnp.dot(p_c.astype(BF16), vc, preferred_element_type=F32))

    gsum = gsum_ref[...]
    ph, plo = _split_bf16(p_c)
    p_sum = jnp.dot(gsum, ph, preferred_element_type=F32) + jnp.dot(gsum, plo, preferred_element_type=F32)
    sh, slo = _split_bf16(p_sum)
    ovs = ovs_ref[...]
    imp = jnp.dot(sh, ovs, preferred_element_type=F32) + jnp.dot(slo, ovs, preferred_element_type=F32)
    blk = lax.broadcasted_iota(jnp.int32, imp.shape, 1)
    jt = t_pos // SEL_BLOCK
    valid = blk <= jt
    forced = jnp.logical_and(valid, jnp.logical_or(blk == 0, jnp.logical_or(blk == jt, blk == jt - 1)))
    eff = jnp.where(forced, jnp.inf, jnp.where(valid, imp, -jnp.inf))
    rank = jnp.zeros(imp.shape, jnp.int32)
    for j in range(jt + 1):
        other = eff[:, j:j + 1]
        ahead = jnp.logical_or(other > eff, jnp.logical_and(other == eff, blk > j))
        rank = rank + ahead.astype(jnp.int32)
    sel = jnp.logical_and(valid, rank < SEL_TOPN)
    sel_bias = jnp.where(sel, 0.0, SEL_BIAS).astype(BF16)
    bias_h = jnp.dot(rep_ref[...], sel_bias, preferred_element_type=F32).astype(BF16)
    bias_keys = jnp.dot(bias_h, eblk_ref[...], preferred_element_type=F32)

    rn = rown_ref[0]
    ks = slc_buf[slot, :, 0:kvw].astype(BF16)
    vs = slc_buf[slot, :, kvw:2 * kvw].astype(BF16)
    s_s = lax.dot_general(qbd, ks, _NT_DIMS, preferred_element_type=F32) + bias_keys
    ks_new = rn[:, 2 * kvw:3 * kvw].astype(BF16).astype(F32)
    vs_new = rn[:, 3 * kvw:4 * kvw].astype(BF16).astype(F32)
    s_new = jnp.sum(qf * ks_new, axis=-1, keepdims=True)
    m = jnp.maximum(jnp.max(s_s, axis=-1, keepdims=True), s_new)
    p = jnp.exp(s_s - m)
    p_new = jnp.exp(s_new - m)
    den = jnp.sum(p, axis=-1, keepdims=True) + p_new
    o_full = jnp.dot(p.astype(BF16), vs, preferred_element_type=F32) + p_new.astype(BF16).astype(F32) * vs_new
    os_ref[0] = _diag_heads(o_full) / den

    wb = winbuf_ref[0]
    wn = winn_ref[0]
    wb_len = wb.shape[0]
    kw = wb[:, 0:kvw].astype(BF16)
    vw = wb[:, kvw:2 * kvw].astype(BF16)
    s_w = lax.dot_general(qbd, kw, _NT_DIMS, preferred_element_type=F32)
    w_idx = lax.broadcasted_iota(jnp.int32, (1, wb_len), 1)
    w_pos = t_pos - wb_len + w_idx
    wmask = jnp.logical_and(w_pos > t_pos - WINDOW, w_pos >= 0)
    s_w = jnp.where(wmask, s_w, MASK_VALUE)
    kw_new = wn[:, 0:kvw].astype(BF16).astype(F32)
    vw_new = wn[:, kvw:2 * kvw].astype(BF16).astype(F32)
    s_new = jnp.sum(qf * kw_new, axis=-1, keepdims=True)
    m = jnp.maximum(jnp.max(s_w, axis=-1, keepdims=True), s_new)
    p = jnp.where(wmask, jnp.exp(s_w - m), 0.0)
    p_new = jnp.exp(s_new - m)
    den = jnp.sum(p, axis=-1, keepdims=True) + p_new
    o_full = jnp.dot(p.astype(BF16), vw, preferred_element_type=F32) + p_new.astype(BF16).astype(F32) * vw_new
    ow_ref[0] = _diag_heads(o_full) / den


def nsa_sample_attention(cache, page_table, win_buf, q, rows_new, win_new, cmp_w1, cmp_pe, cmp_w2, k_norm_cmp):
    G, K, dh = NSA_KV_HEADS, NSA_GROUP, NSA_HEAD_DIM
    bsz, n_pages = page_table.shape
    n_phys, page_rows = cache.shape[0], cache.shape[1]
    past = n_pages * page_rows
    n_chunk = past // CMP_STRIDE
    n_cmp = n_chunk - CMP_R + 1
    ns = -(-(past + 1) // SEL_BLOCK)
    cache3 = cache.reshape(n_phys, page_rows, 4 * G * dh)
    wb_len = win_buf.shape[1]
    win3 = win_buf.reshape(bsz, wb_len, 2 * G * dh)
    qh = jnp.transpose(q[0], (2, 0, 1, 3)).astype(F32)
    qbd = jnp.einsum('bgkd,gq->bgkqd', qh, jnp.eye(G, dtype=F32)).reshape(bsz, G * K, G * dh).astype(BF16)
    w1 = cmp_w1.reshape(2, CMP_R, CMP_STRIDE, dh, CMP_HID)
    wc = jnp.einsum('vrsdh,pq->vspdqrh', w1, jnp.eye(2, dtype=F32)).reshape(
        2, CMP_STRIDE, 2 * dh, 2 * CMP_R * CMP_HID).astype(BF16)
    pe = jnp.broadcast_to(cmp_pe.reshape(2, 1, CMP_LEN * dh), (2, 8, CMP_LEN * dh)).astype(BF16)
    w1f = cmp_w1.reshape(2, CMP_LEN * dh, CMP_HID).astype(BF16)
    w2 = cmp_w2.astype(BF16)
    kn = k_norm_cmp[None, :]
    gsum = np.zeros((8, G * K), np.float32)
    gsum[np.arange(G * K) // K, np.arange(G * K)] = 1.0
    ovs = np.zeros((n_chunk, NSA_NS_PAD), np.float32)
    ovs[:n_cmp, :ns] = _overlap_matrix(n_cmp, ns)
    eblk = (np.arange(NSA_NS_PAD)[:, None] == (np.arange(past)[None, :] // SEL_BLOCK)).astype(np.float32)
    gsum, ovs, eblk = jnp.asarray(gsum, BF16), jnp.asarray(ovs, BF16), jnp.asarray(eblk, BF16)
    rep = gsum.T
    full = lambda a: pl.BlockSpec(a.shape, lambda i, pt: (0,) * a.ndim)
    grid_spec = pltpu.PrefetchScalarGridSpec(
        num_scalar_prefetch=1,
        grid=(bsz,),
        in_specs=[pl.BlockSpec(memory_space=pl.ANY),
                  pl.BlockSpec((1, G * K, G * dh), lambda i, pt: (i, 0, 0)),
                  pl.BlockSpec((1, 1, 4 * G * dh), lambda i, pt: (i, 0, 0)),
                  pl.BlockSpec((1, 1, 2 * G * dh), lambda i, pt: (i, 0, 0)),
                  pl.BlockSpec((1, wb_len, 2 * G * dh), lambda i, pt: (i, 0, 0)),
                  full(wc), full(pe), full(w1f), full(w2), full(kn), full(gsum), full(ovs), full(rep), full(eblk)],
        out_specs=[pl.BlockSpec((1, G * K, dh), lambda i, pt: (i, 0, 0))] * 3,
        scratch_shapes=[pltpu.VMEM((2, 4, past, V7X_LANES), F32),
                        pltpu.VMEM((2, past, 2 * G * dh), F32),
                        pltpu.SemaphoreType.DMA((2,))],
    )
    out = jax.ShapeDtypeStruct((bsz, G * K, dh), F32)
    o_c, o_s, o_w = pl.pallas_call(
        functools.partial(_nsa_sample_body, n_pages, page_rows),
        out_shape=(out, out, out),
        grid_spec=grid_spec,
        compiler_params=_cparams(("arbitrary",), V7X_VMEM_LIMIT_BYTES),
        name="nsa_sample_attention",
    )(page_table, cache3, qbd, rows_new.reshape(bsz, 1, -1), win_new.reshape(bsz, 1, -1), win3,
      wc, pe, w1f, w2, kn, gsum, ovs, rep, eblk)
    return (o_c.reshape(bsz, -1), o_s.reshape(bsz, -1), o_w.reshape(bsz, -1))


def _mm_residual_gated_body(oc_ref, os_ref, ow_ref, gc_ref, gs_ref, gw_ref, w_ref, x_ref, g_ref, o_ref):
    a = gc_ref[...] * oc_ref[...] + gs_ref[...] * os_ref[...] + gw_ref[...] * ow_ref[...]
    acc = jnp.dot(a.astype(BF16), w_ref[...], preferred_element_type=F32)
    o_ref[...] = x_ref[...] + g_ref[0] * acc


def mm_residual_gated(branches, gates, w_bf, x, gate_mod):
    t, d = x.shape
    full2 = lambda a: pl.BlockSpec(a.shape, lambda i: (0,) * a.ndim)
    args = (*branches, *gates, w_bf, x, gate_mod)
    return pl.pallas_call(
        _mm_residual_gated_body,
        out_shape=jax.ShapeDtypeStruct((t, d), F32),
        grid=(1,),
        in_specs=[full2(a) for a in args],
        out_specs=pl.BlockSpec((t, d), lambda i: (0, 0)),
        compiler_params=_cparams(("arbitrary",)),
        name="mm_residual_gated",
    )(*args)


def nsa_sample_pallas(proj, bsz, cache, page_table, win_buf, q_norm, k_norm, cmp_w1, cmp_pe, cmp_w2):
    G, K, dh = NSA_KV_HEADS, NSA_GROUP, NSA_HEAD_DIM
    q, rows, win, _, _, _, _, gates = nsa_prep(proj, 1, bsz, q_norm, k_norm, bsz)
    branches = nsa_sample_attention(cache, page_table, win_buf, q, rows, win, cmp_w1, cmp_pe, cmp_w2, k_norm[0])
    g3 = jnp.transpose(gates[0, :, :, :3 * K], (1, 0, 2)).reshape(bsz, G, K, 3)
    gexp = [jnp.repeat(g3[..., br].reshape(bsz, G * K), dh, axis=1) for br in range(3)]
    rows_out = rows.reshape(bsz, 1, 4, G, dh)
    win_out = jnp.concatenate([win_buf[:, 1:], win.reshape(bsz, 1, 2, G, dh)], axis=1)
    return branches, gexp, rows_out, win_out


def _pad_cols(w, n):
    return jnp.pad(w, ((0, 0), (0, n - w.shape[1])))


def kernel(x_prompt, x_sample, cache_nsa_kv, state_nsa_win, state_ssd_conv, state_ssd, state_rg_conv, state_rg, page_table, c_prompt, c_sample, ada_w, ada_b, norm_mix, norm_ffn, rec_w_in, ssd_conv_w, ssd_conv_b, ssd_dt_bias, ssd_a_log, ssd_d, ssd_norm_w, rg_conv_w, rg_conv_b, rg_wa, rg_ba, rg_wi, rg_bi, rg_lambda, rec_w_out, nsa_w_in, nsa_q_norm, nsa_k_norm, cmp_w1, cmp_pe, cmp_w2, nsa_w_out, router_w, router_b, moe_w1, moe_b1, moe_w2, moe_b2):
    bp, L, d = x_prompt.shape
    bs = x_sample.shape[0]
    depth = ada_w.shape[0]
    tp = bp * L
    xp = x_prompt.reshape(tp, d)
    xs = x_sample.reshape(bs, d)

    n_c = bp + bs
    n_c_pad = -(-n_c // 8) * 8
    c_all = jnp.pad(jnp.concatenate([c_prompt, c_sample], axis=0), ((0, n_c_pad - n_c), (0, 0)))
    ada_w_cat = jnp.concatenate([ada_w[i] for i in range(depth)], axis=1).astype(BF16)
    ada_b_cat = jnp.concatenate([ada_b[i] for i in range(depth)], axis=0)[None, :]
    mod_all = adaln_mod(c_all, ada_w_cat, ada_b_cat)

    outs = {k: [] for k in ('rows_p', 'rows_s', 'win_p', 'win_s', 'sconv_p', 'sconv_s', 'ssm_p', 'ssm_s',
                            'rconv_p', 'rconv_s', 'rg_p', 'rg_s')}

    for i in range(depth):
        j = i // 2
        mod_i = mod_all[:, i * 6 * d:(i + 1) * 6 * d]
        mp = [mod_i[:bp, k * d:(k + 1) * d].reshape(bp, 1, d) for k in range(6)]
        ms = [mod_i[bp:bp + bs, k * d:(k + 1) * d].reshape(1, bs, d) for k in range(6)]
        g_mix = norm_mix[i][None, :]
        g_ffn = norm_ffn[i][None, :]

        if i % 2 == 0:
            w_in = rec_w_in[j]
            s0, s1, s2, s3 = 1024, 1024 + SSD_XBC, 1024 + SSD_XBC + SSD_HEADS, 1024 + SSD_XBC + SSD_HEADS + RG_WIDTH
            w_cat = jnp.concatenate([w_in[:, :s1], w_in[:, s2:], _pad_cols(w_in[:, s1:s2], 512)], axis=1).astype(BF16)
            proj_p = mod_matmul(xp, g_mix, mp[1], mp[0], w_cat, ROW_TILE, 512)
            proj_s = mod_matmul(xs, g_mix, ms[1], ms[0], w_cat, bs, 512)

            ssd_w = (ssd_conv_w[j], ssd_conv_b[j], ssd_dt_bias[j], ssd_a_log[j], ssd_d[j], ssd_norm_w[j])
            rg_w = (rg_conv_w[j], rg_conv_b[j], rg_wa[j], rg_ba[j], rg_wi[j], rg_bi[j], rg_lambda[j])
            yp, a1, a2 = ssd_prompt(proj_p, bp, L, *ssd_w)
            rp, a3, a4 = rg_prompt(proj_p, bp, L, *rg_w)
            a4 = a4.reshape(bp, RG_WIDTH)
            ys, b1_, b2_ = ssd_sample(proj_s, state_ssd_conv[j], state_ssd[j], *ssd_w)
            rs, b3_, b4_ = rg_sample(proj_s, state_rg_conv[j], state_rg[j], *rg_w)
            outs['sconv_p'].append(a1); outs['ssm_p'].append(a2); outs['rconv_p'].append(a3); outs['rg_p'].append(a4)
            outs['sconv_s'].append(b1_); outs['ssm_s'].append(b2_); outs['rconv_s'].append(b3_); outs['rg_s'].append(b4_)
            w_out = rec_w_out[j].astype(BF16)
            w_parts = [w_out[:SSD_WIDTH], w_out[SSD_WIDTH:]]
            xp = mm_residual([yp.reshape(tp, -1), rp.reshape(tp, -1)], w_parts, xp, mp[2], ROW_TILE)
            xs = mm_residual([ys.reshape(bs, -1), rs.reshape(bs, -1)], w_parts, xs, ms[2], bs)
        else:
            w_in = nsa_w_in[j]
            w_cat = _pad_cols(w_in, 3072).astype(BF16)
            proj_p = mod_matmul(xp, g_mix, mp[1], mp[0], w_cat, ROW_TILE, 512)
            proj_s = mod_matmul(xs, g_mix, ms[1], ms[0], w_cat, bs, 512)
            wts = (nsa_q_norm[j], nsa_k_norm[j], cmp_w1[j], cmp_pe[j], cmp_w2[j])
            op, rp, wp = nsa_prompt_pallas(proj_p, bp, L, *wts)
            br_s, gexp_s, rs, ws = nsa_sample_pallas(proj_s, bs, cache_nsa_kv[j], page_table, state_nsa_win[j], *wts)
            outs['rows_p'].append(rp); outs['win_p'].append(wp); outs['rows_s'].append(rs); outs['win_s'].append(ws)
            w_out = nsa_w_out[j].astype(BF16)
            xp = mm_residual([op.reshape(tp, -1)], [w_out], xp, mp[2], ROW_TILE)
            xs = mm_residual_gated(br_s, gexp_s, w_out, xs, ms[2])

        rw = _pad_cols(router_w[i], V7X_LANES)
        rwh = rw.astype(BF16)
        rwl = (rw - rwh.astype(F32)).astype(BF16)
        rb = jnp.concatenate([router_b[i], jnp.full((V7X_LANES - N_EXPERTS,), -1e30, F32)])[None, :]
        h_p, e_p, gt_p = moe_router(xp, g_ffn, mp[4], mp[3], rwh, rwl, rb, ROW_TILE)
        h_s, e_s, gt_s = moe_router(xs, g_ffn, ms[4], ms[3], rwh, rwl, rb, bs)
        h_all = jnp.concatenate([h_p, h_s], axis=0)
        top_e = jnp.concatenate([e_p[:, :TOP_K], e_s[:, :TOP_K]], axis=0)
        gate = jnp.concatenate([gt_p[:, :TOP_K], gt_s[:, :TOP_K]], axis=0)
        blk_e, nvalid, n_used, idx, row_gate = _moe_routing(top_e, gate)
        y4 = moe_ffn(h_all, blk_e, nvalid, n_used, idx, row_gate, moe_w1[i], moe_b1[i], moe_w2[i], moe_b2[i])
        y4 = y4.reshape(tp + bs, TOP_K * d)
        xp = moe_combine(xp, mp[5], y4, 0, 256)
        xs = moe_combine(xs, ms[5], y4, tp, bs)

    st = lambda k: jnp.stack(outs[k])
    return (xp.reshape(bp, L, d), xs.reshape(bs, 1, d), st('rows_p'), st('rows_s'), st('win_p'), st('win_s'),
            st('sconv_p'), st('sconv_s'), st('ssm_p'), st('ssm_s'), st('rconv_p'), st('rconv_s'),
            st('rg_p'), st('rg_s'))
```

```python
import functools
import math

import jax
import jax.numpy as jnp
import numpy as np
from jax import lax
from jax.experimental import pallas as pl
from jax.experimental.pallas import tpu as pltpu

F32 = jnp.float32
BF16 = jnp.bfloat16

D_MODEL = 1024
NORM_EPS = 1e-6

SSD_WIDTH = 1024
SSD_HEAD_DIM = 64
SSD_HEADS = 16
SSD_GROUPS = 4
SSD_STATE = 128
SSD_CONV = 4
SSD_CHUNK = 128
SSD_XBC = SSD_WIDTH + 2 * SSD_GROUPS * SSD_STATE

RG_WIDTH = 1024
RG_BLOCKS = 16
RG_BLOCK_DIM = 64
RG_C = 8.0

NSA_HEADS = 16
NSA_KV_HEADS = 4
NSA_HEAD_DIM = 64
NSA_GROUP = 4
NSA_Q_WIDTH = 1024
NSA_KV_WIDTH = 256
CMP_LEN = 32
CMP_STRIDE = 16
CMP_R = 2
CMP_HID = 128
SEL_BLOCK = 64
SEL_TOPN = 16
WINDOW = 512
NSA_Q_BLOCK = 64

N_EXPERTS = 32
TOP_K = 4
D_FF = 1024
SWIGLU_LIMIT = 7.0
SWIGLU_ALPHA = 1.702

V7X_LANES = 128
V7X_VMEM_LIMIT_BYTES = 56 * 1024 * 1024

MOE_BLOCK_ROWS = 256
MOE_DMA_UNROLL = 16
ROW_TILE = 512


def _cparams(sem, vmem=None):
    return pltpu.CompilerParams(dimension_semantics=sem, vmem_limit_bytes=vmem)


def _adaln_body(c_ref, w_ref, b_ref, o_ref):
    c = c_ref[...]
    s = c * jax.nn.sigmoid(c)
    o_ref[...] = jnp.dot(s.astype(BF16), w_ref[...], preferred_element_type=F32) + b_ref[...]


def adaln_mod(c, w_bf, b):
    r, d = c.shape
    n = w_bf.shape[1]
    tn = 1536
    return pl.pallas_call(
        _adaln_body,
        out_shape=jax.ShapeDtypeStruct((r, n), F32),
        grid=(n // tn,),
        in_specs=[pl.BlockSpec((r, d), lambda j: (0, 0)),
                  pl.BlockSpec((d, tn), lambda j: (0, j)),
                  pl.BlockSpec((1, tn), lambda j: (0, j))],
        out_specs=pl.BlockSpec((r, tn), lambda j: (0, j)),
        compiler_params=_cparams(("arbitrary",)),
        name="adaln_mod",
    )(c, w_bf, b)


def _modulated(x, g, scale, shift):
    ms = jnp.mean(x * x, axis=-1, keepdims=True)
    y = x * lax.rsqrt(ms + NORM_EPS) * g
    return y * (1.0 + scale) + shift


def _mod_matmul_body(x_ref, g_ref, sc_ref, sh_ref, w_ref, o_ref, h_scr):
    @pl.when(pl.program_id(1) == 0)
    def _():
        h_scr[...] = _modulated(x_ref[...], g_ref[...], sc_ref[0], sh_ref[0]).astype(BF16)

    o_ref[...] = jnp.dot(h_scr[...], w_ref[...], preferred_element_type=F32)


def mod_matmul(x, g, scale, shift, w_bf, tm, tn):
    t, d = x.shape
    n = w_bf.shape[1]
    m, r, _ = scale.shape
    rows_per_mod = t // m
    mod_spec = pl.BlockSpec((1, r, d), lambda i, j: ((i * tm) // rows_per_mod, 0, 0))
    return pl.pallas_call(
        _mod_matmul_body,
        out_shape=jax.ShapeDtypeStruct((t, n), F32),
        grid=(t // tm, n // tn),
        in_specs=[pl.BlockSpec((tm, d), lambda i, j: (i, 0)),
                  pl.BlockSpec((1, d), lambda i, j: (0, 0)),
                  mod_spec, mod_spec,
                  pl.BlockSpec((d, tn), lambda i, j: (0, j))],
        out_specs=pl.BlockSpec((tm, tn), lambda i, j: (i, j)),
        scratch_shapes=[pltpu.VMEM((tm, d), BF16)],
        compiler_params=_cparams(("arbitrary", "arbitrary")),
        name="mod_matmul",
    )(x, g, scale, shift, w_bf)


def _mm_residual_body(n_a, *refs):
    a_refs = refs[:n_a]
    w_refs = refs[n_a:2 * n_a]
    x_ref, g_ref, o_ref = refs[2 * n_a:]
    acc = None
    for a_ref, w_ref in zip(a_refs, w_refs):
        p = jnp.dot(a_ref[...].astype(BF16), w_ref[...], preferred_element_type=F32)
        acc = p if acc is None else acc + p
    o_ref[...] = x_ref[...] + g_ref[0] * acc


def mm_residual(a_list, w_list, x, gate, tm):
    t, d = x.shape
    m, r, _ = gate.shape
    rows_per_mod = t // m
    in_specs = [pl.BlockSpec((tm, a.shape[1]), lambda i: (i, 0)) for a in a_list]
    in_specs += [pl.BlockSpec(w.shape, lambda i: (0, 0)) for w in w_list]
    in_specs += [pl.BlockSpec((tm, d), lambda i: (i, 0)),
                 pl.BlockSpec((1, r, d), lambda i: ((i * tm) // rows_per_mod, 0, 0))]
    return pl.pallas_call(
        functools.partial(_mm_residual_body, len(a_list)),
        out_shape=jax.ShapeDtypeStruct((t, d), F32),
        grid=(t // tm,),
        in_specs=in_specs,
        out_specs=pl.BlockSpec((tm, d), lambda i: (i, 0)),
        compiler_params=_cparams(("arbitrary",)),
        name="mm_residual",
    )(*a_list, *w_list, x, gate)


def _router_body(x_ref, g_ref, sc_ref, sh_ref, wh_ref, wl_ref, rb_ref, h_ref, e_ref, gt_ref):
    h = _modulated(x_ref[...], g_ref[...], sc_ref[0], sh_ref[0])
    h_ref[...] = h
    hh = h.astype(BF16)
    hl = (h - hh.astype(F32)).astype(BF16)
    wh = wh_ref[...]
    wl = wl_ref[...]
    logits = (jnp.dot(hh, wh, preferred_element_type=F32)
              + (jnp.dot(hh, wl, preferred_element_type=F32)
                 + jnp.dot(hl, wh, preferred_element_type=F32))) + rb_ref[...]
    lane = lax.broadcasted_iota(jnp.int32, logits.shape, 1)
    neg = jnp.float32(-jnp.inf)
    vals, idxs = [], []
    cur = logits
    for _ in range(TOP_K):
        m = jnp.max(cur, axis=-1, keepdims=True)
        idx = jnp.min(jnp.where(cur == m, lane, V7X_LANES), axis=-1, keepdims=True)
        vals.append(m)
        idxs.append(idx)
        cur = jnp.where(lane == idx, neg, cur)
    exps = [jnp.exp(v - vals[0]) for v in vals]
    den = exps[0] + exps[1] + exps[2] + exps[3]
    e_out = jnp.zeros(logits.shape, jnp.int32)
    g_out = jnp.zeros(logits.shape, F32)
    for k in range(TOP_K):
        e_out = jnp.where(lane == k, idxs[k], e_out)
        g_out = jnp.where(lane == k, exps[k] / den, g_out)
    e_ref[...] = e_out
    gt_ref[...] = g_out


def moe_router(x, g, scale, shift, wh, wl, rb, tm):
    t, d = x.shape
    m, r, _ = scale.shape
    rows_per_mod = t // m
    mod_spec = pl.BlockSpec((1, r, d), lambda i: ((i * tm) // rows_per_mod, 0, 0))
    return pl.pallas_call(
        _router_body,
        out_shape=(jax.ShapeDtypeStruct((t, d), F32),
                   jax.ShapeDtypeStruct((t, V7X_LANES), jnp.int32),
                   jax.ShapeDtypeStruct((t, V7X_LANES), F32)),
        grid=(t // tm,),
        in_specs=[pl.BlockSpec((tm, d), lambda i: (i, 0)),
                  pl.BlockSpec((1, d), lambda i: (0, 0)),
                  mod_spec, mod_spec,
                  pl.BlockSpec((d, V7X_LANES), lambda i: (0, 0)),
                  pl.BlockSpec((d, V7X_LANES), lambda i: (0, 0)),
                  pl.BlockSpec((1, V7X_LANES), lambda i: (0, 0))],
        out_specs=(pl.BlockSpec((tm, d), lambda i: (i, 0)),
                   pl.BlockSpec((tm, V7X_LANES), lambda i: (i, 0)),
                   pl.BlockSpec((tm, V7X_LANES), lambda i: (i, 0))),
        compiler_params=_cparams(("arbitrary",)),
        name="moe_router",
    )(x, g, scale, shift, wh, wl, rb)


def _ffn_body(blk_e_ref, nused_ref,
              idx_hbm, h_hbm, gate_ref, w1_ref, b1_ref, w2_ref, b2_ref,
              out_hbm,
              idx_smem, xbuf, ybuf, w1bf, w2bf, sem_idx, sem_g, sem_s):
    bm = MOE_BLOCK_ROWS
    i = pl.program_id(0)
    n_used = nused_ref[0]

    def idx_copy(blk, slot):
        return pltpu.make_async_copy(idx_hbm.at[blk], idx_smem.at[slot], sem_idx.at[slot])

    def gather_copy(tok, slot, r):
        return pltpu.make_async_copy(h_hbm.at[pl.ds(tok, 1)], xbuf.at[slot, pl.ds(r, 1)], sem_g.at[slot])

    def scatter_copy(dst, slot, r):
        return pltpu.make_async_copy(ybuf.at[slot, pl.ds(r, 1)], out_hbm.at[pl.ds(dst, 1)], sem_s.at[slot])

    def start_gather(islot, slot):
        for r in range(bm):
            gather_copy(idx_smem[islot, r], slot, r).start()

    def wait_gather(slot):
        def body(r, c):
            gather_copy(0, slot, 0).wait()
            return c
        lax.fori_loop(0, bm, body, 0, unroll=MOE_DMA_UNROLL)

    def start_scatter(islot, slot):
        for r in range(bm):
            scatter_copy(idx_smem[islot, bm + r], slot, r).start()

    def wait_scatter(slot):
        def body(r, c):
            scatter_copy(0, slot, 0).wait()
            return c
        lax.fori_loop(0, bm, body, 0, unroll=MOE_DMA_UNROLL)

    @pl.when(i < n_used)
    def _():
        slot = i % 2
        islot = i % 3

        @pl.when(i == 0)
        def _():
            ybuf[...] = jnp.zeros_like(ybuf)
            for sl in range(2):
                tail = pltpu.make_async_copy(ybuf.at[sl], out_hbm.at[pl.ds(out_hbm.shape[0] - (2 - sl) * bm, bm)],
                                             sem_s.at[sl])
                tail.start()
                tail.wait()
            idx_copy(0, 0).start()
            idx_copy(0, 0).wait()
            start_gather(0, 0)

            @pl.when(n_used > 1)
            def _():
                idx_copy(1, 1).start()

        @pl.when(i + 2 < n_used)
        def _():
            idx_copy(i + 2, (i + 2) % 3).start()

        @pl.when(i + 1 < n_used)
        def _():
            idx_copy(i + 1, (i + 1) % 3).wait()
            start_gather((i + 1) % 3, 1 - slot)

        @pl.when(jnp.logical_or(i == 0, blk_e_ref[i] != blk_e_ref[jnp.maximum(i - 1, 0)]))
        def _():
            w1bf[...] = w1_ref[0].astype(BF16)
            w2bf[...] = w2_ref[0].astype(BF16)

        wait_gather(slot)

        @pl.when(i >= 2)
        def _():
            wait_scatter(slot)

        x = xbuf[slot].astype(BF16)
        u = jnp.dot(x, w1bf[...], preferred_element_type=F32) + b1_ref[0]
        gl = jnp.minimum(u[:, :D_FF], SWIGLU_LIMIT)
        lin = jnp.clip(u[:, D_FF:], -SWIGLU_LIMIT, SWIGLU_LIMIT)
        act = gl * jax.nn.sigmoid(SWIGLU_ALPHA * gl) * (lin + 1.0)
        y = jnp.dot(act.astype(BF16), w2bf[...], preferred_element_type=F32) + b2_ref[0]
        ybuf[slot] = y * gate_ref[...]
        start_scatter(islot, slot)

        @pl.when(i == n_used - 1)
        def _():
            @pl.when(i >= 1)
            def _():
                wait_scatter(1 - slot)
            wait_scatter(slot)


def moe_ffn(h_all, blk_e, n_used, idx, row_gate, w1, b1, w2, b2):
    t, d = h_all.shape
    bm = MOE_BLOCK_ROWS
    n_blocks = idx.shape[0]
    grid_spec = pltpu.PrefetchScalarGridSpec(
        num_scalar_prefetch=2,
        grid=(n_blocks,),
        in_specs=[pl.BlockSpec(memory_space=pl.ANY),
                  pl.BlockSpec(memory_space=pl.ANY),
                  pl.BlockSpec((bm, 1), lambda i, be, nu: (i, 0)),
                  pl.BlockSpec((1, d, 2 * D_FF), lambda i, be, nu: (be[i], 0, 0)),
                  pl.BlockSpec((1, 1, 2 * D_FF), lambda i, be, nu: (be[i], 0, 0)),
                  pl.BlockSpec((1, D_FF, d), lambda i, be, nu: (be[i], 0, 0)),
                  pl.BlockSpec((1, 1, d), lambda i, be, nu: (be[i], 0, 0))],
        out_specs=pl.BlockSpec(memory_space=pl.ANY),
        scratch_shapes=[pltpu.SMEM((3, 2 * bm), jnp.int32),
                        pltpu.VMEM((2, bm, d), F32),
                        pltpu.VMEM((2, bm, d), F32),
                        pltpu.VMEM((d, 2 * D_FF), BF16),
                        pltpu.VMEM((D_FF, d), BF16),
                        pltpu.SemaphoreType.DMA((3,)),
                        pltpu.SemaphoreType.DMA((2,)),
                        pltpu.SemaphoreType.DMA((2,))],
    )
    return pl.pallas_call(
        _ffn_body,
        out_shape=jax.ShapeDtypeStruct((t * TOP_K + 2 * bm, d), F32),
        grid_spec=grid_spec,
        compiler_params=_cparams(("arbitrary",), V7X_VMEM_LIMIT_BYTES),
        name="moe_ffn",
    )(blk_e, n_used, idx, h_all, row_gate, w1, b1.reshape(N_EXPERTS, 1, -1), w2,
      b2.reshape(N_EXPERTS, 1, -1))


def _moe_combine_body(x_ref, g_ref, y0_ref, y1_ref, y2_ref, y3_ref, o_ref):
    acc = (y0_ref[...] + y1_ref[...]) + (y2_ref[...] + y3_ref[...])
    o_ref[...] = x_ref[...] + g_ref[0] * acc


def moe_combine(x, gate, y4, t_all, row_off, tm):
    t, d = x.shape
    m, r, _ = gate.shape
    rows_per_mod = t // m
    y_spec = lambda k: pl.BlockSpec((tm, d), lambda i: ((k * t_all + row_off) // tm + i, 0))
    return pl.pallas_call(
        _moe_combine_body,
        out_shape=jax.ShapeDtypeStruct((t, d), F32),
        grid=(t // tm,),
        in_specs=[pl.BlockSpec((tm, d), lambda i: (i, 0)),
                  pl.BlockSpec((1, r, d), lambda i: ((i * tm) // rows_per_mod, 0, 0)),
                  y_spec(0), y_spec(1), y_spec(2), y_spec(3)],
        out_specs=pl.BlockSpec((tm, d), lambda i: (i, 0)),
        compiler_params=_cparams(("arbitrary",)),
        name="moe_combine",
    )(x, gate, y4, y4, y4, y4)


def _moe_routing(top_e, gate):
    bm = MOE_BLOCK_ROWS
    t = top_e.shape[0]
    tk = t * TOP_K
    e_flat = top_e.reshape(-1)
    order = jnp.argsort(e_flat).astype(jnp.int32)
    experts = jnp.arange(N_EXPERTS, dtype=jnp.int32)
    counts = jnp.sum((e_flat[:, None] == experts[None, :]).astype(jnp.int32), axis=0)
    padded = (counts + bm - 1) // bm * bm
    pad_end = jnp.cumsum(padded)
    pad_start = pad_end - padded
    start = jnp.cumsum(counts) - counts
    n_blocks = -(-tk // bm) + N_EXPERTS
    n_rows = n_blocks * bm
    blk_start = jnp.arange(n_blocks, dtype=jnp.int32) * bm
    blk_e = jnp.minimum(jnp.sum((pad_end[None, :] <= blk_start[:, None]).astype(jnp.int32), axis=1),
                        N_EXPERTS - 1)
    n_used = (pad_end[-1] // bm).astype(jnp.int32)
    row = jnp.arange(n_rows, dtype=jnp.int32)
    row_e = blk_e[row // bm]
    off = row - pad_start[row_e]
    valid = jnp.logical_and(off < counts[row_e], row < n_used * bm)
    src = order[jnp.clip(start[row_e] + off, 0, tk - 1)]
    row_gate = jnp.where(valid, gate.reshape(-1)[src], 0.0)
    src_tok = jnp.where(valid, src // TOP_K, 0)
    pad_row = tk + ((row // bm) % 2) * bm + row % bm
    dst_row = jnp.where(valid, (src % TOP_K) * t + src // TOP_K, pad_row)
    idx = jnp.concatenate([src_tok.reshape(n_blocks, bm), dst_row.reshape(n_blocks, bm)], axis=1)
    return blk_e, n_used.reshape(1), idx.astype(jnp.int32), row_gate.reshape(n_rows, 1)


def _rmsnorm(x, w):
    xf = x.astype(F32)
    y = xf * lax.rsqrt(jnp.mean(xf * xf, axis=-1, keepdims=True) + NORM_EPS)
    return y * w.astype(F32)


def _causal_conv(x, prev, w, b):
    width = w.shape[0]
    L = x.shape[1]
    xp = jnp.concatenate([prev.astype(x.dtype), x], axis=1)
    y = b
    for k in range(width):
        y = y + xp[:, k:k + L] * w[k]
    return y, xp[:, xp.shape[1] - (width - 1):]


def _ssd_scan(x, dt, A, Bm, Cm, h0):
    b, L, H, P = x.shape
    G, N = Bm.shape[2], Bm.shape[3]
    K = H // G
    q = math.gcd(L, SSD_CHUNK)
    nc = L // q
    a = (dt * A).reshape(b, nc, q, G, K)
    xdt = (x * dt[..., None]).reshape(b, nc, q, G, K, P)
    Bc = Bm.reshape(b, nc, q, G, N)
    Cc = Cm.reshape(b, nc, q, G, N)
    a_cs = jnp.cumsum(a, axis=2)
    causal = jnp.tril(jnp.ones((q, q), dtype=bool))[None, None, :, :, None, None]
    seg = a_cs[:, :, :, None] - a_cs[:, :, None, :]
    decay = jnp.exp(jnp.where(causal, seg, -jnp.inf))
    cb = jnp.einsum('bctgn,bcsgn->bctsg', Cc, Bc)
    y_diag = jnp.einsum('bctsgk,bcsgkp->bctgkp', cb[..., None] * decay, xdt)
    to_end = jnp.exp(a_cs[:, :, -1:] - a_cs)
    states = jnp.einsum('bcsgn,bcsgkp->bcgkpn', Bc, xdt * to_end[..., None])
    chunk_decay = jnp.exp(a_cs[:, :, -1])

    def step(h, inp):
        st, dec = inp
        return h * dec[..., None, None] + st, h

    h_last, h_prev = lax.scan(step, h0.reshape(b, G, K, P, N),
                              (jnp.moveaxis(states, 1, 0), jnp.moveaxis(chunk_decay, 1, 0)))
    h_prev = jnp.moveaxis(h_prev, 0, 1)
    y_off = jnp.einsum('bctgn,bcgkpn->bctgkp', Cc, h_prev) * jnp.exp(a_cs)[..., None]
    return (y_diag + y_off).reshape(b, L, H, P), h_last.reshape(b, H, P, N)


def _rg_lru(x, h0, wa, ba, wi, bi, lam):
    b, L, _ = x.shape
    xb = x.reshape(b, L, RG_BLOCKS, RG_BLOCK_DIM)
    r = jax.nn.sigmoid(jnp.einsum('blnd,nde->blne', xb, wa).reshape(b, L, RG_WIDTH) + ba)
    i = jax.nn.sigmoid(jnp.einsum('blnd,nde->blne', xb, wi).reshape(b, L, RG_WIDTH) + bi)
    log_a = -RG_C * r * jax.nn.softplus(-lam.astype(F32))
    a = jnp.exp(log_a)
    u = jnp.sqrt(-jnp.expm1(2.0 * log_a)) * (i * x)
    u = u.at[:, 0].add(a[:, 0] * h0)

    def combine(e1, e2):
        return e1[0] * e2[0], e2[0] * e1[1] + e2[1]

    _, h = lax.associative_scan(combine, (a, u), axis=1)
    return h, h[:, -1]


def _rec_core(z, xbc, dt, gate, xr, conv_ssd0, ssm0, conv_rg0, rg0, conv_w, conv_b, dt_bias, a_log,
              d_skip, norm_w, rg_conv_w, rg_conv_b, wa, ba, wi, bi, lam):
    b, L, _ = z.shape
    xbc, conv_ssd1 = _causal_conv(xbc, conv_ssd0, conv_w, conv_b)
    xbc = jax.nn.silu(xbc)
    xs, Bm, Cm = jnp.split(xbc, [SSD_WIDTH, SSD_WIDTH + SSD_GROUPS * SSD_STATE], axis=-1)
    dt = jax.nn.softplus(dt + dt_bias)
    A = -jnp.exp(a_log)
    xh = xs.reshape(b, L, SSD_HEADS, SSD_HEAD_DIM)
    y, ssm1 = _ssd_scan(xh, dt, A, Bm.reshape(b, L, SSD_GROUPS, SSD_STATE),
                        Cm.reshape(b, L, SSD_GROUPS, SSD_STATE), ssm0)
    y = y + d_skip[:, None] * xh
    y = y.reshape(b, L, SSD_WIDTH) * jax.nn.silu(z)
    y = _rmsnorm(y.reshape(b, L, SSD_GROUPS, -1), norm_w.reshape(SSD_GROUPS, -1)).reshape(b, L, SSD_WIDTH)
    xr, conv_rg1 = _causal_conv(xr, conv_rg0, rg_conv_w, rg_conv_b)
    r_out, rg1 = _rg_lru(xr, rg0, wa, ba, wi, bi, lam)
    r_out = r_out * jax.nn.gelu(gate)
    return y, r_out, conv_ssd1, ssm1, conv_rg1, rg1


CONV_TAIL = 8
RG_TILE = 256


def _split3_bf16(x):
    h = x.astype(BF16)
    r = x - h.astype(F32)
    m = r.astype(BF16)
    return h, m, (r - m.astype(F32)).astype(BF16)


def _dot3(parts, w, dims=None):
    if dims is None:
        outs = [jnp.dot(p, w, preferred_element_type=F32) for p in parts]
    else:
        outs = [lax.dot_general(w, p, dims, preferred_element_type=F32) for p in parts]
    return (outs[0] + outs[1]) + outs[2]


def _softplus(x):
    return jnp.maximum(x, 0.0) + jnp.log(1.0 + jnp.exp(-jnp.abs(x)))


def _silu(x):
    return x * jax.nn.sigmoid(x)


def _group_rmsnorm(y, w, n_groups):
    width = y.shape[1] // n_groups
    outs = []
    for g in range(n_groups):
        yg = y[:, g * width:(g + 1) * width]
        outs.append(yg * lax.rsqrt(jnp.mean(yg * yg, axis=-1, keepdims=True) + NORM_EPS))
    return jnp.concatenate(outs, axis=1) * w


def _conv_tile(xbuf, cw_ref, cb_ref, rows):
    y = cb_ref[...]
    for k in range(SSD_CONV):
        y = y + cw_ref[k:k + 1, :] * xbuf[pl.ds(CONV_TAIL - (SSD_CONV - 1) + k, rows), :]
    return y


def _ssd_prompt_body(z_ref, xs_ref, bc_ref, dt_ref, cw_ref, cb_ref, dtb_ref, a_ref, dexp_ref, nw_ref, tri_ref,
                     y_ref, conv_ref, state_ref, xbuf, h_scr):
    q = SSD_CHUNK
    c = pl.program_id(1)
    last = pl.num_programs(1) - 1
    P, N = SSD_HEAD_DIM, SSD_STATE

    @pl.when(c == 0)
    def _():
        xbuf[0:CONV_TAIL, :] = jnp.zeros((CONV_TAIL, SSD_XBC), F32)
        h_scr[...] = jnp.zeros_like(h_scr)

    xbuf[CONV_TAIL:CONV_TAIL + q, 0:SSD_WIDTH] = xs_ref[...]
    xbuf[CONV_TAIL:CONV_TAIL + q, SSD_WIDTH:SSD_XBC] = bc_ref[...]
    xc = _silu(_conv_tile(xbuf, cw_ref, cb_ref, q))

    @pl.when(c == last)
    def _():
        conv_ref[0] = xbuf[CONV_TAIL + q - (SSD_CONV - 1):CONV_TAIL + q, :]

    xbuf[0:CONV_TAIL, :] = xbuf[q:q + CONV_TAIL, :]

    xs = xc[:, 0:SSD_WIDTH]
    bm = xc[:, SSD_WIDTH:SSD_WIDTH + SSD_GROUPS * N].astype(BF16)
    cm = xc[:, SSD_WIDTH + SSD_GROUPS * N:SSD_XBC].astype(BF16)
    dt = _softplus(dt_ref[...] + dtb_ref[...])
    a = dt * a_ref[...]
    a_cs = _dot3(_split3_bf16(a), tri_ref[...], dims=(((1,), (0,)), ((), ())))
    a_cs_t = a_cs.T
    dt_t = dt.T
    a_end_t = a_cs_t[:, q - 1:q]
    w_t = dt_t * jnp.exp(a_end_t - a_cs_t)
    ea = jnp.exp(a_cs)
    xs_t = xs.T
    row = lax.broadcasted_iota(jnp.int32, (q, q), 0)
    col = lax.broadcasted_iota(jnp.int32, (q, q), 1)
    causal = col <= row
    heads_per_group = SSD_HEADS // SSD_GROUPS
    ys = []
    for g in range(SSD_GROUPS):
        bg = bm[:, g * N:(g + 1) * N]
        cg = cm[:, g * N:(g + 1) * N]
        cb = lax.dot_general(cg, bg, _NT_DIMS_SSD, preferred_element_type=F32)
        for k in range(heads_per_group):
            h = g * heads_per_group + k
            seg = a_cs[:, h:h + 1] - a_cs_t[h:h + 1, :]
            decay = jnp.exp(jnp.where(causal, seg, MASK_NEG))
            xh = xs[:, h * P:(h + 1) * P]
            xdt = (xh * dt[:, h:h + 1]).astype(BF16)
            y_diag = jnp.dot((cb * decay).astype(BF16), xdt, preferred_element_type=F32)
            h_prev = h_scr[h]
            y_off = lax.dot_general(cg, h_prev.astype(BF16), _NT_DIMS_SSD,
                                    preferred_element_type=F32) * ea[:, h:h + 1]
            st = jnp.dot((xs_t[h * P:(h + 1) * P, :] * w_t[h:h + 1, :]).astype(BF16), bg,
                         preferred_element_type=F32)
            h_scr[h] = h_prev * jnp.exp(a_end_t[h:h + 1, :]) + st
            ys.append(y_diag + y_off)
    y = jnp.concatenate(ys, axis=1) + dexp_ref[...] * xs
    y = y * _silu(z_ref[...])
    y_ref[...] = _group_rmsnorm(y, nw_ref[...], SSD_GROUPS)

    @pl.when(c == last)
    def _():
        state_ref[0] = h_scr[...]


_NT_DIMS_SSD = (((1,), (1,)), ((), ()))
MASK_NEG = -1e30


def _pad_lanes(v, n=V7X_LANES):
    return jnp.pad(v, (0, n - v.shape[0]))[None, :]


def ssd_prompt(proj, b, seq_len, conv_w, conv_b, dt_bias, a_log, d_skip, norm_w):
    q = SSD_CHUNK
    nc = seq_len // q
    t = b * seq_len
    tri = jnp.asarray(np.tril(np.ones((q, q), np.float32)), BF16)
    a_neg = _pad_lanes(-jnp.exp(a_log))
    dtb = _pad_lanes(dt_bias)
    dexp = jnp.repeat(d_skip, SSD_HEAD_DIM)[None, :]
    nw = norm_w[None, :]
    cb = conv_b[None, :]
    colblk = lambda j, w=SSD_WIDTH: pl.BlockSpec((q, w), lambda bi, c: (bi * nc + c, j))
    full = lambda a: pl.BlockSpec(a.shape, lambda bi, c: (0,) * a.ndim)
    return pl.pallas_call(
        _ssd_prompt_body,
        out_shape=(jax.ShapeDtypeStruct((t, SSD_WIDTH), F32),
                   jax.ShapeDtypeStruct((b, SSD_CONV - 1, SSD_XBC), F32),
                   jax.ShapeDtypeStruct((b, SSD_HEADS, SSD_HEAD_DIM, SSD_STATE), F32)),
        grid=(b, nc),
        in_specs=[colblk(0), colblk(1), colblk(2),
                  pl.BlockSpec((q, V7X_LANES), lambda bi, c: (bi * nc + c, 5 * SSD_WIDTH // V7X_LANES)),
                  full(conv_w), full(cb), full(dtb), full(a_neg), full(dexp), full(nw), full(tri)],
        out_specs=(pl.BlockSpec((q, SSD_WIDTH), lambda bi, c: (bi * nc + c, 0)),
                   pl.BlockSpec((1, SSD_CONV - 1, SSD_XBC), lambda bi, c: (bi, 0, 0)),
                   pl.BlockSpec((1, SSD_HEADS, SSD_HEAD_DIM, SSD_STATE), lambda bi, c: (bi, 0, 0, 0))),
        scratch_shapes=[pltpu.VMEM((CONV_TAIL + q, SSD_XBC), F32),
                        pltpu.VMEM((SSD_HEADS, SSD_HEAD_DIM, SSD_STATE), F32)],
        compiler_params=_cparams(("arbitrary", "arbitrary"), V7X_VMEM_LIMIT_BYTES),
        name="ssd_prompt",
    )(proj, proj, proj, proj, conv_w, cb, dtb, a_neg, dexp, nw, tri)


def _ssd_sample_pre_body(xs_ref, bc_ref, dt_ref, s0_ref, s1_ref, s2_ref, cw_ref, cb_ref, dtb_ref, a_ref, dexp_ref,
                         hexp_ref, yd_ref, xdt_ref, b_ref, c_ref, ea_ref, eaexp_ref):
    N = SSD_STATE
    x_new = jnp.concatenate([xs_ref[...], bc_ref[...]], axis=1)
    y = (cb_ref[...] + cw_ref[0:1, :] * s0_ref[...] + cw_ref[1:2, :] * s1_ref[...]
         + cw_ref[2:3, :] * s2_ref[...] + cw_ref[3:4, :] * x_new)
    xc = _silu(y)
    xs = xc[:, 0:SSD_WIDTH]
    bm = xc[:, SSD_WIDTH:SSD_WIDTH + SSD_GROUPS * N]
    cm = xc[:, SSD_WIDTH + SSD_GROUPS * N:SSD_XBC]
    dt = _softplus(dt_ref[...] + dtb_ref[...])
    ea = jnp.exp(dt * a_ref[...])
    hexp = hexp_ref[...]
    dt_exp = _dot3(_split3_bf16(dt), hexp)
    xdt = (xs * dt_exp).astype(BF16)
    bb = bm.astype(BF16)
    cc = cm.astype(BF16)
    prod = bb.astype(F32) * cc.astype(F32)
    hw = SSD_WIDTH // SSD_GROUPS
    cb = jnp.concatenate(
        [jnp.broadcast_to(jnp.sum(prod[:, g * N:(g + 1) * N], axis=-1, keepdims=True), (xs.shape[0], hw))
         for g in range(SSD_GROUPS)], axis=1)
    yd_ref[...] = cb.astype(BF16).astype(F32) * xdt.astype(F32) + dexp_ref[...] * xs
    xdt_ref[...] = xdt.astype(F32)
    b_ref[...] = bb.astype(F32)
    c_ref[...] = cc.astype(F32)
    ea_ref[...] = ea
    eaexp_ref[...] = _dot3(_split3_bf16(ea), hexp)


def _ssd_sample_state_body(ea_smem, xdt_ref, b_ref, c_ref, yd_ref, eaexp_ref, z_ref, nw_ref, h0_ref,
                           y_ref, h1_ref):
    i = pl.program_id(0)
    N = SSD_STATE
    gw = SSD_WIDTH // SSD_GROUPS
    heads_per_group = SSD_HEADS // SSD_GROUPS
    row0 = lax.broadcasted_iota(jnp.int32, (8, 1), 0) == 0
    y_off = []
    for g in range(SSD_GROUPS):
        x8 = jnp.broadcast_to(xdt_ref[0, :, g * gw:(g + 1) * gw], (8, gw)).astype(BF16)
        b8 = jnp.where(row0, jnp.broadcast_to(b_ref[0, :, g * N:(g + 1) * N], (8, N)), 0.0).astype(BF16)
        c8 = jnp.broadcast_to(c_ref[0, :, g * N:(g + 1) * N], (8, N)).astype(BF16)
        h0g = h0_ref[0, g * gw:(g + 1) * gw, :]
        st = lax.dot_general(x8, b8, (((0,), (0,)), ((), ())), preferred_element_type=F32)
        yo = lax.dot_general(c8, h0g.astype(BF16), _NT_DIMS_SSD, preferred_element_type=F32)
        y_off.append(yo[0:1, :])
        for k in range(heads_per_group):
            h = g * heads_per_group + k
            r = slice(k * SSD_HEAD_DIM, (k + 1) * SSD_HEAD_DIM)
            h1_ref[0, g * gw + k * SSD_HEAD_DIM:g * gw + (k + 1) * SSD_HEAD_DIM, :] = (
                h0g[r, :] * ea_smem[i, h] + st[r, :])
    y = jnp.concatenate(y_off, axis=1) * eaexp_ref[0] + yd_ref[0]
    y = y * _silu(z_ref[0])
    y_ref[0] = _group_rmsnorm(y, nw_ref[...], SSD_GROUPS)


def ssd_sample(proj, conv_state, ssm_state, conv_w, conv_b, dt_bias, a_log, d_skip, norm_w):
    bsz = proj.shape[0]
    H, P, N = SSD_HEADS, SSD_HEAD_DIM, SSD_STATE
    a_neg = _pad_lanes(-jnp.exp(a_log))
    dtb = _pad_lanes(dt_bias)
    dexp = jnp.repeat(d_skip, P)[None, :]
    hexp = jnp.asarray((np.arange(V7X_LANES)[:, None] == (np.arange(H * P)[None, :] // P)).astype(np.float32), BF16)
    cb = conv_b[None, :]
    s0, s1, s2 = conv_state[:, 0], conv_state[:, 1], conv_state[:, 2]
    blk = lambda j, w: pl.BlockSpec((bsz, w), lambda i: (0, j))
    full = lambda a: pl.BlockSpec(a.shape, lambda i: (0,) * a.ndim)
    o = lambda w, dt_: jax.ShapeDtypeStruct((bsz, w), dt_)
    yd, xdt, bb, cc, ea, eaexp = pl.pallas_call(
        _ssd_sample_pre_body,
        out_shape=(o(SSD_WIDTH, F32), o(SSD_WIDTH, F32), o(SSD_GROUPS * N, F32), o(SSD_GROUPS * N, F32),
                   o(V7X_LANES, F32), o(SSD_WIDTH, F32)),
        grid=(1,),
        in_specs=[blk(1, SSD_WIDTH), blk(2, SSD_WIDTH), blk(5 * SSD_WIDTH // V7X_LANES, V7X_LANES),
                  full(s0), full(s1), full(s2), full(conv_w), full(cb), full(dtb), full(a_neg), full(dexp), full(hexp)],
        out_specs=(full(o(SSD_WIDTH, F32)), full(o(SSD_WIDTH, F32)), full(o(SSD_GROUPS * N, F32)),
                   full(o(SSD_GROUPS * N, F32)), full(o(V7X_LANES, F32)), full(o(SSD_WIDTH, F32))),
        compiler_params=_cparams(("arbitrary",)),
        name="ssd_sample_pre",
    )(proj, proj, proj, s0, s1, s2, conv_w, cb, dtb, a_neg, dexp, hexp)
    x_new = jnp.concatenate([proj[:, SSD_WIDTH:2 * SSD_WIDTH], proj[:, 2 * SSD_WIDTH:3 * SSD_WIDTH]], axis=1)
    conv_new = jnp.stack([s1, s2, x_new], axis=1)
    z3 = proj[:, 0:SSD_WIDTH].reshape(bsz, 1, SSD_WIDTH)
    row = lambda w: pl.BlockSpec((1, 1, w), lambda i, ea_: (i, 0, 0))
    nw = norm_w[None, :]
    grid_spec = pltpu.PrefetchScalarGridSpec(
        num_scalar_prefetch=1,
        grid=(bsz,),
        in_specs=[row(SSD_WIDTH), row(SSD_GROUPS * N), row(SSD_GROUPS * N), row(SSD_WIDTH), row(SSD_WIDTH),
                  row(SSD_WIDTH), pl.BlockSpec(nw.shape, lambda i, ea_: (0, 0)),
                  pl.BlockSpec((1, H * P, N), lambda i, ea_: (i, 0, 0))],
        out_specs=(row(SSD_WIDTH), pl.BlockSpec((1, H * P, N), lambda i, ea_: (i, 0, 0))),
    )
    r3 = lambda a: a.reshape(bsz, 1, a.shape[1])
    y, h1 = pl.pallas_call(
        _ssd_sample_state_body,
        out_shape=(jax.ShapeDtypeStruct((bsz, 1, SSD_WIDTH), F32), jax.ShapeDtypeStruct((bsz, H * P, N), F32)),
        grid_spec=grid_spec,
        compiler_params=_cparams(("arbitrary",)),
        name="ssd_sample_state",
    )(ea[:, :H], r3(xdt), r3(bb), r3(cc), r3(yd), r3(eaexp), z3, nw, ssm_state.reshape(bsz, H * P, N))
    return y.reshape(bsz, SSD_WIDTH), conv_new, h1.reshape(bsz, H, P, N)


def _rg_gates(xc, wa_ref, ba_ref, wi_ref, bi_ref, sp_ref):
    xb = xc.astype(BF16)
    r = jax.nn.sigmoid(jnp.dot(xb, wa_ref[...], preferred_element_type=F32) + ba_ref[...])
    ig = jax.nn.sigmoid(jnp.dot(xb, wi_ref[...], preferred_element_type=F32) + bi_ref[...])
    log_a = -RG_C * r * sp_ref[...]
    a = jnp.exp(log_a)
    u = jnp.sqrt(1.0 - jnp.exp(2.0 * log_a)) * (ig * xc)
    return a, u


def _rg_prompt_body(gate_ref, xr_ref, cw_ref, cb_ref, wa_ref, ba_ref, wi_ref, bi_ref, sp_ref,
                    y_ref, conv_ref, state_ref, xbuf, h_scr):
    rows = RG_TILE
    c = pl.program_id(1)
    last = pl.num_programs(1) - 1

    @pl.when(c == 0)
    def _():
        xbuf[0:CONV_TAIL, :] = jnp.zeros((CONV_TAIL, RG_WIDTH), F32)
        h_scr[...] = jnp.zeros_like(h_scr)

    xbuf[CONV_TAIL:CONV_TAIL + rows, :] = xr_ref[...]
    xc = _conv_tile(xbuf, cw_ref, cb_ref, rows)

    @pl.when(c == last)
    def _():
        conv_ref[0] = xbuf[CONV_TAIL + rows - (SSD_CONV - 1):CONV_TAIL + rows, :]

    xbuf[0:CONV_TAIL, :] = xbuf[rows:rows + CONV_TAIL, :]
    a, u = _rg_gates(xc, wa_ref, ba_ref, wi_ref, bi_ref, sp_ref)
    t_idx = lax.broadcasted_iota(jnp.int32, (rows, 1), 0)
    d = 1
    while d < rows:
        keep = t_idx >= d
        a_sh = jnp.where(keep, pltpu.roll(a, d, 0), 1.0)
        u_sh = jnp.where(keep, pltpu.roll(u, d, 0), 0.0)
        u = u + a * u_sh
        a = a * a_sh
        d *= 2
    h = u + a * h_scr[0:1, :]
    h_scr[0:1, :] = h[rows - 1:rows, :]
    y_ref[...] = h * _gelu_tanh(gate_ref[...])

    @pl.when(c == last)
    def _():
        state_ref[0] = h[rows - 1:rows, :]


def _rg_weights(wa, ba, wi, bi, lam):
    eye = jnp.eye(RG_BLOCKS, dtype=F32)
    bd = lambda w: jnp.einsum('nde,nm->ndme', w, eye).reshape(RG_WIDTH, RG_WIDTH).astype(BF16)
    return bd(wa), ba[None, :], bd(wi), bi[None, :], jax.nn.softplus(-lam)[None, :]


def rg_prompt(proj, b, seq_len, conv_w, conv_b, wa, ba, wi, bi, lam):
    rows = RG_TILE
    nt = seq_len // rows
    t = b * seq_len
    wts = _rg_weights(wa, ba, wi, bi, lam)
    cb = conv_b[None, :]
    full = lambda a: pl.BlockSpec(a.shape, lambda bi_, c: (0,) * a.ndim)
    return pl.pallas_call(
        _rg_prompt_body,
        out_shape=(jax.ShapeDtypeStruct((t, RG_WIDTH), F32),
                   jax.ShapeDtypeStruct((b, SSD_CONV - 1, RG_WIDTH), F32),
                   jax.ShapeDtypeStruct((b, 1, RG_WIDTH), F32)),
        grid=(b, nt),
        in_specs=[pl.BlockSpec((rows, RG_WIDTH), lambda bi_, c: (bi_ * nt + c, 3)),
                  pl.BlockSpec((rows, RG_WIDTH), lambda bi_, c: (bi_ * nt + c, 4)),
                  full(conv_w), full(cb)] + [full(w) for w in wts],
        out_specs=(pl.BlockSpec((rows, RG_WIDTH), lambda bi_, c: (bi_ * nt + c, 0)),
                   pl.BlockSpec((1, SSD_CONV - 1, RG_WIDTH), lambda bi_, c: (bi_, 0, 0)),
                   pl.BlockSpec((1, 1, RG_WIDTH), lambda bi_, c: (bi_, 0, 0))),
        scratch_shapes=[pltpu.VMEM((CONV_TAIL + rows, RG_WIDTH), F32), pltpu.VMEM((8, RG_WIDTH), F32)],
        compiler_params=_cparams(("arbitrary", "arbitrary"), V7X_VMEM_LIMIT_BYTES),
        name="rg_prompt",
    )(proj, proj, conv_w, cb, *wts)


def _rg_sample_body(gate_ref, xr_ref, s0_ref, s1_ref, s2_ref, h0_ref, cw_ref, cb_ref, wa_ref, ba_ref, wi_ref, bi_ref,
                    sp_ref, y_ref, h1_ref):
    xc = (cb_ref[...] + cw_ref[0:1, :] * s0_ref[...] + cw_ref[1:2, :] * s1_ref[...]
          + cw_ref[2:3, :] * s2_ref[...] + cw_ref[3:4, :] * xr_ref[...])
    a, u = _rg_gates(xc, wa_ref, ba_ref, wi_ref, bi_ref, sp_ref)
    h = a * h0_ref[...] + u
    h1_ref[...] = h
    y_ref[...] = h * _gelu_tanh(gate_ref[...])


def rg_sample(proj, conv_state, h0, conv_w, conv_b, wa, ba, wi, bi, lam):
    bsz = proj.shape[0]
    wts = _rg_weights(wa, ba, wi, bi, lam)
    cb = conv_b[None, :]
    s0, s1, s2 = conv_state[:, 0], conv_state[:, 1], conv_state[:, 2]
    full = lambda a: pl.BlockSpec(a.shape, lambda i: (0,) * a.ndim)
    out = jax.ShapeDtypeStruct((bsz, RG_WIDTH), F32)
    y, h1 = pl.pallas_call(
        _rg_sample_body,
        out_shape=(out, out),
        grid=(1,),
        in_specs=[pl.BlockSpec((bsz, RG_WIDTH), lambda i: (0, 3)), pl.BlockSpec((bsz, RG_WIDTH), lambda i: (0, 4)),
                  full(s0), full(s1), full(s2), full(h0), full(conv_w), full(cb)] + [full(w) for w in wts],
        out_specs=(full(out), full(out)),
        compiler_params=_cparams(("arbitrary",)),
        name="rg_sample",
    )(proj, proj, s0, s1, s2, h0, conv_w, cb, *wts)
    conv_new = jnp.stack([s1, s2, proj[:, 4 * RG_WIDTH:5 * RG_WIDTH]], axis=1)
    return y, conv_new, h1


def _nsa_split(proj, b, L, q_norm, k_norm):
    q = proj[:, :NSA_Q_WIDTH]
    kv = proj[:, NSA_Q_WIDTH:NSA_Q_WIDTH + 6 * NSA_KV_WIDTH]
    g = proj[:, NSA_Q_WIDTH + 6 * NSA_KV_WIDTH:NSA_Q_WIDTH + 6 * NSA_KV_WIDTH + 3 * NSA_HEADS]
    q = _rmsnorm(q.reshape(b, L, NSA_KV_HEADS, NSA_GROUP, NSA_HEAD_DIM), q_norm) * (NSA_HEAD_DIM ** -0.5)
    kv = kv.reshape(b, L, 6, NSA_KV_HEADS, NSA_HEAD_DIM)
    k_slc = _rmsnorm(kv[:, :, 2], k_norm[1])
    k_win = _rmsnorm(kv[:, :, 4], k_norm[2])
    rows = jnp.stack([kv[:, :, 0], kv[:, :, 1], k_slc, kv[:, :, 3]], axis=2)
    win = jnp.stack([k_win, kv[:, :, 5]], axis=2)
    gates = jax.nn.sigmoid(g).reshape(b, L, NSA_KV_HEADS, NSA_GROUP, 3)
    return q, rows, win, gates


def _masked_softmax(s, mask):
    s = jnp.where(mask, s.astype(F32), -jnp.inf)
    m = jnp.max(s, axis=-1, keepdims=True)
    e = jnp.exp(s - jnp.where(jnp.isfinite(m), m, 0.0))
    d = jnp.sum(e, axis=-1, keepdims=True)
    return e / jnp.where(d > 0, d, 1.0)


def _compress(r, w1, pe, w2):
    b, T, G, dh = r.shape
    n_chunk = T // CMP_STRIDE
    nc = n_chunk - CMP_R + 1
    ch = r[:, :n_chunk * CMP_STRIDE].reshape(b, n_chunk, CMP_STRIDE, G, dh)
    proj = jnp.einsum('bcsgd,rsdh->bcrgh', ch, w1.reshape(CMP_R, CMP_STRIDE, dh, CMP_HID))
    hid = jnp.einsum('ld,ldh->h', pe, w1)
    for rr in range(CMP_R):
        hid = hid + proj[:, rr:rr + nc, rr]
    return jax.nn.gelu(hid) @ w2


def _nsa_context(rows, cmp_w1, cmp_pe, cmp_w2, k_norm_cmp):
    b, T = rows.shape[:2]
    kc = _rmsnorm(_compress(rows[:, :, 0], cmp_w1[0], cmp_pe[0], cmp_w2[0]), k_norm_cmp)
    vc = _compress(rows[:, :, 1], cmp_w1[1], cmp_pe[1], cmp_w2[1])
    ns = -(-T // SEL_BLOCK)
    sel = jnp.pad(rows[:, :, 2:4], ((0, 0), (0, ns * SEL_BLOCK - T), (0, 0), (0, 0), (0, 0)))
    sel = sel.reshape(b, ns, SEL_BLOCK, 2, NSA_KV_HEADS, NSA_HEAD_DIM).transpose(3, 0, 4, 1, 2, 5)
    return kc, vc, sel[0], sel[1]


def _overlap_matrix(nc, ns):
    i = np.arange(nc)[:, None]
    j = np.arange(ns)[None, :]
    ov = (i * CMP_STRIDE < (j + 1) * SEL_BLOCK) & (i * CMP_STRIDE + CMP_LEN > j * SEL_BLOCK)
    return ov.astype(np.float32)


def _nsa_attend(q, gates, t_pos, kc, vc, ks, vs, kw, vw, w_pos):
    b, Q, G, K, dh = q.shape
    nc = kc.shape[1]
    ns = ks.shape[2]
    tq = t_pos[None, :, None, None, None]
    c_end = jnp.arange(nc) * CMP_STRIDE + CMP_LEN - 1
    p_c = _masked_softmax(jnp.einsum('bqgkd,bngd->bqgkn', q, kc), c_end <= tq)
    o_c = jnp.einsum('bqgkn,bngd->bqgkd', p_c, vc)
    imp = jnp.einsum('bqgkn,ns->bqgs', p_c, jnp.asarray(_overlap_matrix(nc, ns)))
    jj = jnp.arange(ns)[None, :]
    jt = (t_pos // SEL_BLOCK)[:, None]
    valid = jj <= jt
    forced = valid & ((jj == 0) | (jj == jt) | (jj == jt - 1))
    imp = jnp.where(forced[None, :, None], jnp.inf, jnp.where(valid[None, :, None], imp, -jnp.inf))
    _, idx = lax.top_k(imp, min(SEL_TOPN, ns))
    n = idx.shape[-1]
    bi = jnp.arange(b)[:, None, None, None]
    gi = jnp.arange(G)[None, None, :, None]
    k_sel = ks[bi, gi, idx].reshape(b, Q, G, n * SEL_BLOCK, dh)
    v_sel = vs[bi, gi, idx].reshape(b, Q, G, n * SEL_BLOCK, dh)
    kpos = (idx[..., None] * SEL_BLOCK + jnp.arange(SEL_BLOCK)).reshape(b, Q, G, 1, n * SEL_BLOCK)
    p_s = _masked_softmax(jnp.einsum('bqgkd,bqgmd->bqgkm', q, k_sel), kpos <= tq)
    o_s = jnp.einsum('bqgkm,bqgmd->bqgkd', p_s, v_sel)
    m_w = (w_pos <= tq) & (w_pos > tq - WINDOW) & (w_pos >= 0)
    p_w = _masked_softmax(jnp.einsum('bqgkd,bwgd->bqgkw', q, kw), m_w)
    o_w = jnp.einsum('bqgkw,bwgd->bqgkd', p_w, vw)
    o = gates[..., 0:1] * o_c + gates[..., 1:2] * o_s + gates[..., 2:3] * o_w
    return o.reshape(b, Q, G * K * dh)


def _nsa_prompt_core(proj, b, L, q_norm, k_norm, cmp_w1, cmp_pe, cmp_w2):
    q, rows, win, gates = _nsa_split(proj, b, L, q_norm, k_norm)
    kc, vc, ks, vs = _nsa_context(rows, cmp_w1, cmp_pe, cmp_w2, k_norm[0])
    win_pad = jnp.pad(win, ((0, 0), (WINDOW, 0), (0, 0), (0, 0), (0, 0)))

    def block(i):
        s = i * NSA_Q_BLOCK
        qb = lax.dynamic_slice_in_dim(q, s, NSA_Q_BLOCK, axis=1)
        gb = lax.dynamic_slice_in_dim(gates, s, NSA_Q_BLOCK, axis=1)
        wb = lax.dynamic_slice_in_dim(win_pad, s, WINDOW + NSA_Q_BLOCK, axis=1)
        t_pos = s + jnp.arange(NSA_Q_BLOCK)
        w_pos = s - WINDOW + jnp.arange(WINDOW + NSA_Q_BLOCK)
        return _nsa_attend(qb, gb, t_pos, kc, vc, ks, vs, wb[:, :, 0], wb[:, :, 1], w_pos)

    o = lax.map(block, jnp.arange(L // NSA_Q_BLOCK))
    o = jnp.moveaxis(o, 0, 1).reshape(b, L, NSA_Q_WIDTH)
    return o, rows, win[:, L - min(WINDOW, L):]


def _nsa_sample_core(proj, b, L, cache, page_table, win_buf, q_norm, k_norm, cmp_w1, cmp_pe, cmp_w2):
    q, rows, win, gates = _nsa_split(proj, b, L, q_norm, k_norm)
    past = page_table.shape[1] * cache.shape[1]
    past_rows = cache[page_table].reshape(b, past, 4, NSA_KV_HEADS, NSA_HEAD_DIM)
    kc, vc, ks, vs = _nsa_context(jnp.concatenate([past_rows, rows], axis=1),
                                  cmp_w1, cmp_pe, cmp_w2, k_norm[0])
    wb_len = win_buf.shape[1]
    wk = jnp.concatenate([win_buf, win], axis=1)
    t_pos = past + jnp.arange(L)
    w_pos = past - wb_len + jnp.arange(wb_len + L)
    o = _nsa_attend(q, gates, t_pos, kc, vc, ks, vs, wk[:, :, 0], wk[:, :, 1], w_pos)
    return o, rows, wk[:, L:]


NSA_TQ = 128
NSA_TK_SLC = 1024
NSA_NS_PAD = 64
SEL_BIAS = -16384.0
MASK_VALUE = -1e30


def _split_bf16(x):
    hi = x.astype(BF16)
    lo = (x - hi.astype(F32)).astype(BF16)
    return hi, lo


def _seg_rms_scale(x, seg, seg_t):
    hi, lo = _split_bf16(x * x)
    ss = jnp.dot(hi, seg, preferred_element_type=F32) + jnp.dot(lo, seg, preferred_element_type=F32)
    r = lax.rsqrt(ss * (1.0 / NSA_HEAD_DIM) + NORM_EPS)
    rh, rl = _split_bf16(r)
    return jnp.dot(rh, seg_t, preferred_element_type=F32) + jnp.dot(rl, seg_t, preferred_element_type=F32)


def _nsa_prep_body(seq_len, p_ref, wq_ref, wks_ref, wkw_ref, segq_ref, segqt_ref, segk_ref, segkt_ref,
                   q_ref, rows_ref, win_ref, kaug_ref, vslc_ref, kwin_ref, vwin_ref, gate_ref):
    tm = p_ref.shape[0]
    dh = NSA_HEAD_DIM
    q = p_ref[:, 0:NSA_Q_WIDTH]
    qn = q * _seg_rms_scale(q, segq_ref[...], segqt_ref[...]) * wq_ref[...]
    kv = [p_ref[:, NSA_Q_WIDTH + NSA_KV_WIDTH * j:NSA_Q_WIDTH + NSA_KV_WIDTH * (j + 1)] for j in range(6)]
    ksl = kv[2] * _seg_rms_scale(kv[2], segk_ref[...], segkt_ref[...]) * wks_ref[...]
    kwn = kv[4] * _seg_rms_scale(kv[4], segk_ref[...], segkt_ref[...]) * wkw_ref[...]
    rows_ref[...] = jnp.concatenate([kv[0], kv[1], ksl, kv[3]], axis=1)
    win_ref[...] = jnp.concatenate([kwn, kv[5]], axis=1)
    gates = jax.nn.sigmoid(p_ref[:, NSA_Q_WIDTH + 6 * NSA_KV_WIDTH:NSA_Q_WIDTH + 6 * NSA_KV_WIDTH + V7X_LANES])
    t0 = (pl.program_id(0) * tm) % seq_len
    tpos = t0 + lax.broadcasted_iota(jnp.int32, (tm, NSA_NS_PAD), 0)
    blk = lax.broadcasted_iota(jnp.int32, (tm, NSA_NS_PAD), 1)
    onehot = jnp.where(blk == lax.shift_right_logical(tpos, 6), 1.0, 0.0).astype(BF16)
    for g in range(NSA_KV_HEADS):
        sl = slice(g * dh, (g + 1) * dh)
        kaug_ref[0, g] = jnp.concatenate([ksl[:, sl].astype(BF16), onehot], axis=1)
        vslc_ref[0, g] = kv[3][:, sl].astype(BF16)
        kwin_ref[0, g] = kwn[:, sl].astype(BF16)
        vwin_ref[0, g] = kv[5][:, sl].astype(BF16)
        gate_ref[0, g] = gates if g == 0 else pltpu.roll(gates, V7X_LANES - 3 * NSA_GROUP * g, 1)
        for k in range(NSA_GROUP):
            c0 = (g * NSA_GROUP + k) * dh
            q_ref[0, g, k] = qn[:, c0:c0 + dh].astype(BF16)


def _head_segments(width):
    lane = np.arange(width)[:, None] // NSA_HEAD_DIM
    seg = (lane == np.arange(V7X_LANES)[None, :]).astype(np.float32)
    return jnp.asarray(seg, BF16), jnp.asarray(seg.T, BF16)


def nsa_prep(proj, b, seq_len, q_norm, k_norm, tm):
    t = proj.shape[0]
    G, K, dh = NSA_KV_HEADS, NSA_GROUP, NSA_HEAD_DIM
    wq = (jnp.tile(q_norm, NSA_HEADS) * (dh ** -0.5))[None, :]
    wks = jnp.tile(k_norm[1], G)[None, :]
    wkw = jnp.tile(k_norm[2], G)[None, :]
    segq, segqt = _head_segments(NSA_Q_WIDTH)
    segk, segkt = _head_segments(NSA_KV_WIDTH)
    tiles_per_seq = seq_len // tm
    bi = lambda i: i // tiles_per_seq
    ti = lambda i: i % tiles_per_seq
    full = lambda a: pl.BlockSpec(a.shape, lambda i: (0,) * a.ndim)
    out_shape = (jax.ShapeDtypeStruct((b, G, K, seq_len, dh), BF16),
                 jax.ShapeDtypeStruct((t, 4 * NSA_KV_WIDTH), F32),
                 jax.ShapeDtypeStruct((t, 2 * NSA_KV_WIDTH), F32),
                 jax.ShapeDtypeStruct((b, G, seq_len, 2 * dh), BF16),
                 jax.ShapeDtypeStruct((b, G, seq_len, dh), BF16),
                 jax.ShapeDtypeStruct((b, G, seq_len, dh), BF16),
                 jax.ShapeDtypeStruct((b, G, seq_len, dh), BF16),
                 jax.ShapeDtypeStruct((b, G, seq_len, V7X_LANES), F32))
    per_g = lambda w: pl.BlockSpec((1, G, tm, w), lambda i: (bi(i), 0, ti(i), 0))
    out_specs = (pl.BlockSpec((1, G, K, tm, dh), lambda i: (bi(i), 0, 0, ti(i), 0)),
                 pl.BlockSpec((tm, 4 * NSA_KV_WIDTH), lambda i: (i, 0)),
                 pl.BlockSpec((tm, 2 * NSA_KV_WIDTH), lambda i: (i, 0)),
                 per_g(2 * dh), per_g(dh), per_g(dh), per_g(dh), per_g(V7X_LANES))
    return pl.pallas_call(
        functools.partial(_nsa_prep_body, seq_len),
        out_shape=out_shape,
        grid=(t // tm,),
        in_specs=[pl.BlockSpec((tm, proj.shape[1]), lambda i: (i, 0)),
                  full(wq), full(wks), full(wkw), full(segq), full(segqt), full(segk), full(segkt)],
        out_specs=out_specs,
        compiler_params=_cparams(("arbitrary",), V7X_VMEM_LIMIT_BYTES),
        name="nsa_prep",
    )(proj, wq, wks, wkw, segq, segqt, segk, segkt)


def _gelu_tanh(x):
    return 0.5 * x * (1.0 + jnp.tanh(math.sqrt(2.0 / math.pi) * (x + 0.044715 * (x * x * x))))


def _nsa_compress_body(n_chunk, x0_ref, x1_ref, x2_ref, x3_ref, wk_ref, wv_ref, pe_ref, w1f_ref, w2_ref, kn_ref,
                       kc_ref, vc_ref, pk_scr, pv_scr):
    s = pl.program_id(1)

    @pl.when(s == 0)
    def _():
        pk_scr[...] = jnp.zeros_like(pk_scr)
        pv_scr[...] = jnp.zeros_like(pv_scr)

    xs = [r[pl.ds(s, n_chunk, stride=CMP_STRIDE), :].astype(BF16) for r in (x0_ref, x1_ref, x2_ref, x3_ref)]
    pk_scr[...] += jnp.dot(jnp.concatenate(xs[0:2], axis=1), wk_ref[0], preferred_element_type=F32)
    pv_scr[...] += jnp.dot(jnp.concatenate(xs[2:4], axis=1), wv_ref[0], preferred_element_type=F32)

    @pl.when(s == CMP_STRIDE - 1)
    def _():
        for kv, p_scr, o_ref in ((0, pk_scr, kc_ref), (1, pv_scr, vc_ref)):
            p = p_scr[...]
            p_next = pltpu.roll(p, n_chunk - 1, 0)
            pe_h = jnp.dot(pe_ref[kv], w1f_ref[kv], preferred_element_type=F32)[0:1, :]
            for g in range(NSA_KV_HEADS):
                c0 = g * 2 * CMP_HID
                hid = pe_h + p[:, c0:c0 + CMP_HID] + p_next[:, c0 + CMP_HID:c0 + 2 * CMP_HID]
                y = jnp.dot(_gelu_tanh(hid).astype(BF16), w2_ref[kv], preferred_element_type=F32)
                if kv == 0:
                    y = y * lax.rsqrt(jnp.mean(y * y, axis=-1, keepdims=True) + NORM_EPS) * kn_ref[...]
                o_ref[0, g] = y.astype(BF16)


def nsa_compress(rows, b, seq_len, cmp_w1, cmp_pe, cmp_w2, k_norm_cmp):
    G, dh = NSA_KV_HEADS, NSA_HEAD_DIM
    n_chunk = seq_len // CMP_STRIDE
    w1 = cmp_w1.reshape(2, CMP_R, CMP_STRIDE, dh, CMP_HID)
    eye = jnp.eye(G, dtype=F32)
    wbd = jnp.einsum('vrsdh,gq->vsgdqrh', w1, eye).reshape(2, CMP_STRIDE, G * dh, G * CMP_R * CMP_HID).astype(BF16)
    pe = jnp.broadcast_to(cmp_pe.reshape(2, 1, CMP_LEN * dh), (2, 8, CMP_LEN * dh)).astype(BF16)
    w1f = cmp_w1.reshape(2, CMP_LEN * dh, CMP_HID).astype(BF16)
    w2 = cmp_w2.astype(BF16)
    kn = k_norm_cmp[None, :]
    full = lambda a: pl.BlockSpec(a.shape, lambda bi, s: (0,) * a.ndim)
    return pl.pallas_call(
        functools.partial(_nsa_compress_body, n_chunk),
        out_shape=(jax.ShapeDtypeStruct((b, G, n_chunk, dh), BF16),
                   jax.ShapeDtypeStruct((b, G, n_chunk, dh), BF16)),
        grid=(b, CMP_STRIDE),
        in_specs=[pl.BlockSpec((seq_len, V7X_LANES), lambda bi, s: (bi, 0)),
                  pl.BlockSpec((seq_len, V7X_LANES), lambda bi, s: (bi, 1)),
                  pl.BlockSpec((seq_len, V7X_LANES), lambda bi, s: (bi, 2)),
                  pl.BlockSpec((seq_len, V7X_LANES), lambda bi, s: (bi, 3)),
                  pl.BlockSpec((1, G * dh, G * CMP_R * CMP_HID), lambda bi, s: (s, 0, 0)),
                  pl.BlockSpec((1, G * dh, G * CMP_R * CMP_HID), lambda bi, s: (s, 0, 0)),
                  full(pe), full(w1f), full(w2), full(kn)],
        out_specs=(pl.BlockSpec((1, G, n_chunk, dh), lambda bi, s: (bi, 0, 0, 0)),
                   pl.BlockSpec((1, G, n_chunk, dh), lambda bi, s: (bi, 0, 0, 0))),
        scratch_shapes=[pltpu.VMEM((n_chunk, G * CMP_R * CMP_HID), F32),
                        pltpu.VMEM((n_chunk, G * CMP_R * CMP_HID), F32)],
        compiler_params=_cparams(("arbitrary", "arbitrary"), V7X_VMEM_LIMIT_BYTES),
        name="nsa_compress",
    )(rows, rows, rows, rows, wbd[0], wbd[1], pe, w1f, w2, kn)


_NT_DIMS = (((1,), (1,)), ((), ()))


def _flash_branch(q2, k_ref, v_ref, n_tiles, tk, last_mask_fn):
    rows = q2.shape[0]

    def step(j, carry, mask_fn):
        m, l, acc = carry
        k0 = pl.multiple_of(j * tk, tk)
        k = k_ref[0, 0, pl.ds(k0, tk), :]
        v = v_ref[0, 0, pl.ds(k0, tk), :]
        s = lax.dot_general(q2, k, _NT_DIMS, preferred_element_type=F32)
        if mask_fn is not None:
            s = jnp.where(mask_fn(k0), s, MASK_VALUE)
        m_new = jnp.maximum(m, jnp.max(s, axis=-1, keepdims=True))
        alpha = jnp.exp(m - m_new)
        p = jnp.exp(s - m_new)
        l = alpha * l + jnp.sum(p, axis=-1, keepdims=True)
        acc = alpha * acc + jnp.dot(p.astype(BF16), v, preferred_element_type=F32)
        return m_new, l, acc

    init = (jnp.full((rows, 1), MASK_VALUE, F32), jnp.zeros((rows, 1), F32),
            jnp.zeros((rows, NSA_HEAD_DIM), F32))
    carry = lax.fori_loop(0, n_tiles - 1, lambda j, c: step(j, c, None), init)
    _, l, acc = step(n_tiles - 1, carry, last_mask_fn)
    return acc / l


def _nsa_attn_body(n_cmp, q_ref, kc_ref, vc_ref, kaug_ref, vslc_ref, kwin_ref, vwin_ref, gate_ref, ovt_ref, o_ref):
    tq = NSA_TQ
    rows = NSA_GROUP * tq
    q0 = pl.program_id(2) * tq
    q2 = q_ref[0, 0].reshape(rows, NSA_HEAD_DIM)
    row_t = q0 + jnp.bitwise_and(lax.broadcasted_iota(jnp.int32, (rows, 1), 0), tq - 1)

    n_pad = kc_ref.shape[2]
    s = lax.dot_general(q2, kc_ref[0, 0], _NT_DIMS, preferred_element_type=F32)
    n_idx = lax.broadcasted_iota(jnp.int32, (1, n_pad), 1)
    cmask = jnp.logical_and(n_idx * CMP_STRIDE + (CMP_LEN - 1) <= row_t, n_idx < n_cmp)
    s = jnp.where(cmask, s, MASK_VALUE)
    m = jnp.max(s, axis=-1, keepdims=True)
    e = jnp.where(cmask, jnp.exp(s - m), 0.0)
    den = jnp.sum(e, axis=-1, keepdims=True)
    p_c = e / jnp.where(den > 0.0, den, 1.0)
    o_c = jnp.dot(p_c.astype(BF16), vc_ref[0, 0], preferred_element_type=F32)

    p_sum = (p_c[0:tq] + p_c[tq:2 * tq]) + (p_c[2 * tq:3 * tq] + p_c[3 * tq:4 * tq])
    ph, plo = _split_bf16(p_sum)
    ovt = ovt_ref[...]
    imp = (lax.dot_general(ovt, ph, _NT_DIMS, preferred_element_type=F32)
           + lax.dot_general(ovt, plo, _NT_DIMS, preferred_element_type=F32))
    blk = lax.broadcasted_iota(jnp.int32, (NSA_NS_PAD, tq), 0)
    jt = lax.shift_right_logical(q0 + lax.broadcasted_iota(jnp.int32, (NSA_NS_PAD, tq), 1), 6)
    valid = blk <= jt
    forced = jnp.logical_and(valid, jnp.logical_or(blk == 0, jnp.logical_or(blk == jt, blk == jt - 1)))
    eff = jnp.where(forced, jnp.inf, jnp.where(valid, imp, -jnp.inf))
    rank = jnp.zeros((NSA_NS_PAD, tq), jnp.int32)
    for j in range(NSA_NS_PAD):
        other = eff[j:j + 1, :]
        ahead = jnp.logical_or(other > eff, jnp.logical_and(other == eff, blk > j))
        rank = rank + ahead.astype(jnp.int32)
    sel = jnp.logical_and(valid, rank < SEL_TOPN)
    sel_bias = jnp.where(sel, 0.0, SEL_BIAS).T.astype(BF16)

    q_aug = jnp.concatenate([q2, jnp.concatenate([sel_bias] * NSA_GROUP, axis=0)], axis=1)
    hi = (q0 + tq - 1) // NSA_TK_SLC + 1

    def slc_mask(k0):
        kpos = k0 + lax.broadcasted_iota(jnp.int32, (1, NSA_TK_SLC), 1)
        return kpos <= row_t

    o_s = _flash_branch(q_aug, kaug_ref, vslc_ref, hi, NSA_TK_SLC, slc_mask)

    span = WINDOW + tq
    w0 = pl.multiple_of(jnp.maximum(q0 - WINDOW, 0), tq)
    kw = kwin_ref[0, 0, pl.ds(w0, span), :]
    vw = vwin_ref[0, 0, pl.ds(w0, span), :]
    s_w = lax.dot_general(q2, kw, _NT_DIMS, preferred_element_type=F32)
    kpos = w0 + lax.broadcasted_iota(jnp.int32, (1, span), 1)
    wmask = jnp.logical_and(kpos <= row_t, kpos > row_t - WINDOW)
    s_w = jnp.where(wmask, s_w, MASK_VALUE)
    p_w = jnp.exp(s_w - jnp.max(s_w, axis=-1, keepdims=True))
    o_w = (jnp.dot(p_w.astype(BF16), vw, preferred_element_type=F32)
           / jnp.sum(p_w, axis=-1, keepdims=True))

    gt = gate_ref[0, 0]
    outs = []
    for k in range(NSA_GROUP):
        r = slice(k * tq, (k + 1) * tq)
        outs.append(gt[:, 3 * k:3 * k + 1] * o_c[r] + gt[:, 3 * k + 1:3 * k + 2] * o_s[r]
                    + gt[:, 3 * k + 2:3 * k + 3] * o_w[r])
    o_ref[...] = jnp.concatenate(outs, axis=1)


def nsa_attention(q, kc, vc, kaug, vslc, kwin, vwin, gates, b, seq_len):
    G, K, dh = NSA_KV_HEADS, NSA_GROUP, NSA_HEAD_DIM
    tq = NSA_TQ
    nq = seq_len // tq
    n_chunk = kc.shape[2]
    n_cmp = n_chunk - CMP_R + 1
    ns = seq_len // SEL_BLOCK
    ovt = np.zeros((NSA_NS_PAD, n_chunk), np.float32)
    ovt[:ns, :n_cmp] = _overlap_matrix(n_cmp, ns).T
    ovt = jnp.asarray(ovt, BF16)
    seq_spec = lambda w: pl.BlockSpec((1, 1, seq_len, w), lambda bi, g, qi: (bi, g, 0, 0))
    return pl.pallas_call(
        functools.partial(_nsa_attn_body, n_cmp),
        out_shape=jax.ShapeDtypeStruct((b * seq_len, NSA_Q_WIDTH), F32),
        grid=(b, G, nq),
        in_specs=[pl.BlockSpec((1, 1, K, tq, dh), lambda bi, g, qi: (bi, g, 0, qi, 0)),
                  pl.BlockSpec((1, 1, n_chunk, dh), lambda bi, g, qi: (bi, g, 0, 0)),
                  pl.BlockSpec((1, 1, n_chunk, dh), lambda bi, g, qi: (bi, g, 0, 0)),
                  seq_spec(2 * dh), seq_spec(dh), seq_spec(dh), seq_spec(dh),
                  pl.BlockSpec((1, 1, tq, V7X_LANES), lambda bi, g, qi: (bi, g, qi, 0)),
                  pl.BlockSpec(ovt.shape, lambda bi, g, qi: (0, 0))],
        out_specs=pl.BlockSpec((tq, K * dh), lambda bi, g, qi: (bi * nq + qi, g)),
        compiler_params=_cparams(("arbitrary", "arbitrary", "arbitrary"), V7X_VMEM_LIMIT_BYTES),
        name="nsa_attention",
    )(q, kc, vc, kaug, vslc, kwin, vwin, gates, ovt)


def nsa_prompt_pallas(proj, b, seq_len, q_norm, k_norm, cmp_w1, cmp_pe, cmp_w2):
    q, rows, win, kaug, vslc, kwin, vwin, gates = nsa_prep(proj, b, seq_len, q_norm, k_norm, ROW_TILE)
    kc, vc = nsa_compress(rows, b, seq_len, cmp_w1, cmp_pe, cmp_w2, k_norm[0])
    o = nsa_attention(q, kc, vc, kaug, vslc, kwin, vwin, gates, b, seq_len)
    rows_out = rows.reshape(b, seq_len, 4, NSA_KV_HEADS, NSA_HEAD_DIM)
    wlen = min(WINDOW, seq_len)
    win_out = win.reshape(b, seq_len, 2, NSA_KV_HEADS, NSA_HEAD_DIM)[:, seq_len - wlen:]
    return o, rows_out, win_out


def _diag_heads(o_full):
    g_row = lax.shift_right_logical(lax.broadcasted_iota(jnp.int32, (NSA_HEADS, 1), 0), 2)
    out = jnp.zeros((NSA_HEADS, NSA_HEAD_DIM), F32)
    for g in range(NSA_KV_HEADS):
        out = out + jnp.where(g_row == g, o_full[:, g * NSA_HEAD_DIM:(g + 1) * NSA_HEAD_DIM], 0.0)
    return out


def _nsa_sample_body(n_pages, page_rows, pt_ref,
                     cache_hbm, qbd_ref, rown_ref, winn_ref, winbuf_ref, wc_ref, pe_ref, w1f_ref, w2_ref, kn_ref,
                     gsum_ref, ovs_ref, rep_ref, eblk_ref,
                     oc_ref, os_ref, ow_ref,
                     cmp_buf, slc_buf, sem):
    i = pl.program_id(0)
    nb = pl.num_programs(0)
    slot = i % 2
    past = n_pages * page_rows
    n_chunk = past // CMP_STRIDE
    n_cmp = n_chunk - CMP_R + 1
    t_pos = past
    kvw = NSA_KV_WIDTH

    def page_copies(bi, sl):
        copies = []
        for p in range(n_pages):
            pg = pt_ref[bi, p]
            rows = pl.ds(p * page_rows, page_rows)
            for j in range(4):
                copies.append(pltpu.make_async_copy(cache_hbm.at[pg, :, pl.ds(j * V7X_LANES, V7X_LANES)],
                                                    cmp_buf.at[sl, j, rows, :], sem.at[sl]))
            copies.append(pltpu.make_async_copy(cache_hbm.at[pg, :, pl.ds(2 * kvw, 2 * kvw)],
                                                slc_buf.at[sl, rows, :], sem.at[sl]))
        return copies

    @pl.when(i == 0)
    def _():
        for c in page_copies(0, 0):
            c.start()

    @pl.when(i + 1 < nb)
    def _():
        for c in page_copies(i + 1, 1 - slot):
            c.start()

    for c in page_copies(i, slot):
        c.wait()

    parts = []
    for j in range(4):
        acc = None
        for s in range(CMP_STRIDE):
            xs = cmp_buf[slot, j, pl.ds(s, n_chunk, stride=CMP_STRIDE), :].astype(BF16)
            d = jnp.dot(xs, wc_ref[j // 2, s], preferred_element_type=F32)
            acc = d if acc is None else acc + d
        parts.append(acc)
    slabs = []
    for kv in range(2):
        p = jnp.concatenate(parts[2 * kv:2 * kv + 2], axis=1)
        p_next = pltpu.roll(p, n_chunk - 1, 0)
        pe_h = jnp.dot(pe_ref[kv], w1f_ref[kv], preferred_element_type=F32)[0:1, :]
        ys = []
        for g in range(NSA_KV_HEADS):
            c0 = g * 2 * CMP_HID
            hid = pe_h + p[:, c0:c0 + CMP_HID] + p_next[:, c0 + CMP_HID:c0 + 2 * CMP_HID]
            y = jnp.dot(_gelu_tanh(hid).astype(BF16), w2_ref[kv], preferred_element_type=F32)
            if kv == 0:
                y = y * lax.rsqrt(jnp.mean(y * y, axis=-1, keepdims=True) + NORM_EPS) * kn_ref[...]
            ys.append(y)
        slabs.append(jnp.concatenate(ys, axis=1).astype(BF16))
    kc, vc = slabs

    qbd = qbd_ref[0]
    qf = qbd.astype(F32)

    s_c = lax.dot_general(qbd, kc, _NT_DIMS, preferred_element_type=F32)
    n_idx = lax.broadcasted_iota(jnp.int32, (1, n_chunk), 1)
    cmask = jnp.logical_and(n_idx * CMP_STRIDE + (CMP_LEN - 1) <= t_pos, n_idx < n_cmp)
    s_c = jnp.where(cmask, s_c, MASK_VALUE)
    m = jnp.max(s_c, axis=-1, keepdims=True)
    e = jnp.where(cmask, jnp.exp(s_c - m), 0.0)
    den = jnp.sum(e, axis=-1, keepdims=True)
    p_c = e / jnp.where(den > 0.0, den, 1.0)
    oc_ref[0] = _diag_heads(jnp.dot(p_c.astype(BF16), vc, preferred_element_type=F32))

    gsum = gsum_ref[...]
    ph, plo = _split_bf16(p_c)
    p_sum = jnp.dot(gsum, ph, preferred_element_type=F32) + jnp.dot(gsum, plo, preferred_element_type=F32)
    sh, slo = _split_bf16(p_sum)
    ovs = ovs_ref[...]
    imp = jnp.dot(sh, ovs, preferred_element_type=F32) + jnp.dot(slo, ovs, preferred_element_type=F32)
    blk = lax.broadcasted_iota(jnp.int32, imp.shape, 1)
    jt = t_pos // SEL_BLOCK
    valid = blk <= jt
    forced = jnp.logical_and(valid, jnp.logical_or(blk == 0, jnp.logical_or(blk == jt, blk == jt - 1)))
    eff = jnp.where(forced, jnp.inf, jnp.where(valid, imp, -jnp.inf))
    rank = jnp.zeros(imp.shape, jnp.int32)
    for j in range(jt + 1):
        other = eff[:, j:j + 1]
        ahead = jnp.logical_or(other > eff, jnp.logical_and(other == eff, blk > j))
        rank = rank + ahead.astype(jnp.int32)
    sel = jnp.logical_and(valid, rank < SEL_TOPN)
    sel_bias = jnp.where(sel, 0.0, SEL_BIAS).astype(BF16)
    bias_h = jnp.dot(rep_ref[...], sel_bias, preferred_element_type=F32).astype(BF16)
    bias_keys = jnp.dot(bias_h, eblk_ref[...], preferred_element_type=F32)

    rn = rown_ref[0]
    ks = slc_buf[slot, :, 0:kvw].astype(BF16)
    vs = slc_buf[slot, :, kvw:2 * kvw].astype(BF16)
    s_s = lax.dot_general(qbd, ks, _NT_DIMS, preferred_element_type=F32) + bias_keys
    ks_new = rn[:, 2 * kvw:3 * kvw].astype(BF16).astype(F32)
    vs_new = rn[:, 3 * kvw:4 * kvw].astype(BF16).astype(F32)
    s_new = jnp.sum(qf * ks_new, axis=-1, keepdims=True)
    m = jnp.maximum(jnp.max(s_s, axis=-1, keepdims=True), s_new)
    p = jnp.exp(s_s - m)
    p_new = jnp.exp(s_new - m)
    den = jnp.sum(p, axis=-1, keepdims=True) + p_new
    o_full = jnp.dot(p.astype(BF16), vs, preferred_element_type=F32) + p_new.astype(BF16).astype(F32) * vs_new
    os_ref[0] = _diag_heads(o_full) / den

    wb = winbuf_ref[0]
    wn = winn_ref[0]
    wb_len = wb.shape[0]
    kw = wb[:, 0:kvw].astype(BF16)
    vw = wb[:, kvw:2 * kvw].astype(BF16)
    s_w = lax.dot_general(qbd, kw, _NT_DIMS, preferred_element_type=F32)
    w_idx = lax.broadcasted_iota(jnp.int32, (1, wb_len), 1)
    w_pos = t_pos - wb_len + w_idx
    wmask = jnp.logical_and(w_pos > t_pos - WINDOW, w_pos >= 0)
    s_w = jnp.where(wmask, s_w, MASK_VALUE)
    kw_new = wn[:, 0:kvw].astype(BF16).astype(F32)
    vw_new = wn[:, kvw:2 * kvw].astype(BF16).astype(F32)
    s_new = jnp.sum(qf * kw_new, axis=-1, keepdims=True)
    m = jnp.maximum(jnp.max(s_w, axis=-1, keepdims=True), s_new)
    p = jnp.where(wmask, jnp.exp(s_w - m), 0.0)
    p_new = jnp.exp(s_new - m)
    den = jnp.sum(p, axis=-1, keepdims=True) + p_new
    o_full = jnp.dot(p.astype(BF16), vw, preferred_element_type=F32) + p_new.astype(BF16).astype(F32) * vw_new
    ow_ref[0] = _diag_heads(o_full) / den


def nsa_sample_attention(cache, page_table, win_buf, q, rows_new, win_new, cmp_w1, cmp_pe, cmp_w2, k_norm_cmp):
    G, K, dh = NSA_KV_HEADS, NSA_GROUP, NSA_HEAD_DIM
    bsz, n_pages = page_table.shape
    n_phys, page_rows = cache.shape[0], cache.shape[1]
    past = n_pages * page_rows
    n_chunk = past // CMP_STRIDE
    n_cmp = n_chunk - CMP_R + 1
    ns = -(-(past + 1) // SEL_BLOCK)
    cache3 = cache.reshape(n_phys, page_rows, 4 * G * dh)
    wb_len = win_buf.shape[1]
    win3 = win_buf.reshape(bsz, wb_len, 2 * G * dh)
    qh = jnp.transpose(q[0], (2, 0, 1, 3)).astype(F32)
    qbd = jnp.einsum('bgkd,gq->bgkqd', qh, jnp.eye(G, dtype=F32)).reshape(bsz, G * K, G * dh).astype(BF16)
    w1 = cmp_w1.reshape(2, CMP_R, CMP_STRIDE, dh, CMP_HID)
    wc = jnp.einsum('vrsdh,pq->vspdqrh', w1, jnp.eye(2, dtype=F32)).reshape(
        2, CMP_STRIDE, 2 * dh, 2 * CMP_R * CMP_HID).astype(BF16)
    pe = jnp.broadcast_to(cmp_pe.reshape(2, 1, CMP_LEN * dh), (2, 8, CMP_LEN * dh)).astype(BF16)
    w1f = cmp_w1.reshape(2, CMP_LEN * dh, CMP_HID).astype(BF16)
    w2 = cmp_w2.astype(BF16)
    kn = k_norm_cmp[None, :]
    gsum = np.zeros((8, G * K), np.float32)
    gsum[np.arange(G * K) // K, np.arange(G * K)] = 1.0
    ovs = np.zeros((n_chunk, NSA_NS_PAD), np.float32)
    ovs[:n_cmp, :ns] = _overlap_matrix(n_cmp, ns)
    eblk = (np.arange(NSA_NS_PAD)[:, None] == (np.arange(past)[None, :] // SEL_BLOCK)).astype(np.float32)
    gsum, ovs, eblk = jnp.asarray(gsum, BF16), jnp.asarray(ovs, BF16), jnp.asarray(eblk, BF16)
    rep = gsum.T
    full = lambda a: pl.BlockSpec(a.shape, lambda i, pt: (0,) * a.ndim)
    grid_spec = pltpu.PrefetchScalarGridSpec(
        num_scalar_prefetch=1,
        grid=(bsz,),
        in_specs=[pl.BlockSpec(memory_space=pl.ANY),
                  pl.BlockSpec((1, G * K, G * dh), lambda i, pt: (i, 0, 0)),
                  pl.BlockSpec((1, 1, 4 * G * dh), lambda i, pt: (i, 0, 0)),
                  pl.BlockSpec((1, 1, 2 * G * dh), lambda i, pt: (i, 0, 0)),
                  pl.BlockSpec((1, wb_len, 2 * G * dh), lambda i, pt: (i, 0, 0)),
                  full(wc), full(pe), full(w1f), full(w2), full(kn), full(gsum), full(ovs), full(rep), full(eblk)],
        out_specs=[pl.BlockSpec((1, G * K, dh), lambda i, pt: (i, 0, 0))] * 3,
        scratch_shapes=[pltpu.VMEM((2, 4, past, V7X_LANES), F32),
                        pltpu.VMEM((2, past, 2 * G * dh), F32),
                        pltpu.SemaphoreType.DMA((2,))],
    )
    out = jax.ShapeDtypeStruct((bsz, G * K, dh), F32)
    o_c, o_s, o_w = pl.pallas_call(
        functools.partial(_nsa_sample_body, n_pages, page_rows),
        out_shape=(out, out, out),
        grid_spec=grid_spec,
        compiler_params=_cparams(("arbitrary",), V7X_VMEM_LIMIT_BYTES),
        name="nsa_sample_attention",
    )(page_table, cache3, qbd, rows_new.reshape(bsz, 1, -1), win_new.reshape(bsz, 1, -1), win3,
      wc, pe, w1f, w2, kn, gsum, ovs, rep, eblk)
    return (o_c.reshape(bsz, -1), o_s.reshape(bsz, -1), o_w.reshape(bsz, -1))


def _mm_residual_gated_body(oc_ref, os_ref, ow_ref, gc_ref, gs_ref, gw_ref, w_ref, x_ref, g_ref, o_ref):
    a = gc_ref[...] * oc_ref[...] + gs_ref[...] * os_ref[...] + gw_ref[...] * ow_ref[...]
    acc = jnp.dot(a.astype(BF16), w_ref[...], preferred_element_type=F32)
    o_ref[...] = x_ref[...] + g_ref[0] * acc


def mm_residual_gated(branches, gates, w_bf, x, gate_mod):
    t, d = x.shape
    full2 = lambda a: pl.BlockSpec(a.shape, lambda i: (0,) * a.ndim)
    args = (*branches, *gates, w_bf, x, gate_mod)
    return pl.pallas_call(
        _mm_residual_gated_body,
        out_shape=jax.ShapeDtypeStruct((t, d), F32),
        grid=(1,),
        in_specs=[full2(a) for a in args],
        out_specs=pl.BlockSpec((t, d), lambda i: (0, 0)),
        compiler_params=_cparams(("arbitrary",)),
        name="mm_residual_gated",
    )(*args)


def nsa_sample_pallas(proj, bsz, cache, page_table, win_buf, q_norm, k_norm, cmp_w1, cmp_pe, cmp_w2):
    G, K, dh = NSA_KV_HEADS, NSA_GROUP, NSA_HEAD_DIM
    q, rows, win, _, _, _, _, gates = nsa_prep(proj, 1, bsz, q_norm, k_norm, bsz)
    branches = nsa_sample_attention(cache, page_table, win_buf, q, rows, win, cmp_w1, cmp_pe, cmp_w2, k_norm[0])
    g3 = jnp.transpose(gates[0, :, :, :3 * K], (1, 0, 2)).reshape(bsz, G, K, 3)
    gexp = [jnp.repeat(g3[..., br].reshape(bsz, G * K), dh, axis=1) for br in range(3)]
    rows_out = rows.reshape(bsz, 1, 4, G, dh)
    win_out = jnp.concatenate([win_buf[:, 1:], win.reshape(bsz, 1, 2, G, dh)], axis=1)
    return branches, gexp, rows_out, win_out


def _pad_cols(w, n):
    return jnp.pad(w, ((0, 0), (0, n - w.shape[1])))


def kernel(x_prompt, x_sample, cache_nsa_kv, state_nsa_win, state_ssd_conv, state_ssd, state_rg_conv, state_rg, page_table, c_prompt, c_sample, ada_w, ada_b, norm_mix, norm_ffn, rec_w_in, ssd_conv_w, ssd_conv_b, ssd_dt_bias, ssd_a_log, ssd_d, ssd_norm_w, rg_conv_w, rg_conv_b, rg_wa, rg_ba, rg_wi, rg_bi, rg_lambda, rec_w_out, nsa_w_in, nsa_q_norm, nsa_k_norm, cmp_w1, cmp_pe, cmp_w2, nsa_w_out, router_w, router_b, moe_w1, moe_b1, moe_w2, moe_b2):
    bp, L, d = x_prompt.shape
    bs = x_sample.shape[0]
    depth = ada_w.shape[0]
    tp = bp * L
    xp = x_prompt.reshape(tp, d)
    xs = x_sample.reshape(bs, d)

    n_c = bp + bs
    n_c_pad = -(-n_c // 8) * 8
    c_all = jnp.pad(jnp.concatenate([c_prompt, c_sample], axis=0), ((0, n_c_pad - n_c), (0, 0)))
    ada_w_cat = jnp.concatenate([ada_w[i] for i in range(depth)], axis=1).astype(BF16)
    ada_b_cat = jnp.concatenate([ada_b[i] for i in range(depth)], axis=0)[None, :]
    mod_all = adaln_mod(c_all, ada_w_cat, ada_b_cat)

    outs = {k: [] for k in ('rows_p', 'rows_s', 'win_p', 'win_s', 'sconv_p', 'sconv_s', 'ssm_p', 'ssm_s',
                            'rconv_p', 'rconv_s', 'rg_p', 'rg_s')}

    for i in range(depth):
        j = i // 2
        mod_i = mod_all[:, i * 6 * d:(i + 1) * 6 * d]
        mp = [mod_i[:bp, k * d:(k + 1) * d].reshape(bp, 1, d) for k in range(6)]
        ms = [mod_i[bp:bp + bs, k * d:(k + 1) * d].reshape(1, bs, d) for k in range(6)]
        g_mix = norm_mix[i][None, :]
        g_ffn = norm_ffn[i][None, :]

        if i % 2 == 0:
            w_in = rec_w_in[j]
            s0, s1, s2, s3 = 1024, 1024 + SSD_XBC, 1024 + SSD_XBC + SSD_HEADS, 1024 + SSD_XBC + SSD_HEADS + RG_WIDTH
            w_cat = jnp.concatenate([w_in[:, :s1], w_in[:, s2:], _pad_cols(w_in[:, s1:s2], 512)], axis=1).astype(BF16)
            proj_p = mod_matmul(xp, g_mix, mp[1], mp[0], w_cat, ROW_TILE, 512)
            proj_s = mod_matmul(xs, g_mix, ms[1], ms[0], w_cat, bs, 512)

            ssd_w = (ssd_conv_w[j], ssd_conv_b[j], ssd_dt_bias[j], ssd_a_log[j], ssd_d[j], ssd_norm_w[j])
            rg_w = (rg_conv_w[j], rg_conv_b[j], rg_wa[j], rg_ba[j], rg_wi[j], rg_bi[j], rg_lambda[j])
            yp, a1, a2 = ssd_prompt(proj_p, bp, L, *ssd_w)
            rp, a3, a4 = rg_prompt(proj_p, bp, L, *rg_w)
            a4 = a4.reshape(bp, RG_WIDTH)
            ys, b1_, b2_ = ssd_sample(proj_s, state_ssd_conv[j], state_ssd[j], *ssd_w)
            rs, b3_, b4_ = rg_sample(proj_s, state_rg_conv[j], state_rg[j], *rg_w)
            outs['sconv_p'].append(a1); outs['ssm_p'].append(a2); outs['rconv_p'].append(a3); outs['rg_p'].append(a4)
            outs['sconv_s'].append(b1_); outs['ssm_s'].append(b2_); outs['rconv_s'].append(b3_); outs['rg_s'].append(b4_)
            w_out = rec_w_out[j].astype(BF16)
            w_parts = [w_out[:SSD_WIDTH], w_out[SSD_WIDTH:]]
            xp = mm_residual([yp.reshape(tp, -1), rp.reshape(tp, -1)], w_parts, xp, mp[2], ROW_TILE)
            xs = mm_residual([ys.reshape(bs, -1), rs.reshape(bs, -1)], w_parts, xs, ms[2], bs)
        else:
            w_in = nsa_w_in[j]
            w_cat = _pad_cols(w_in, 3072).astype(BF16)
            proj_p = mod_matmul(xp, g_mix, mp[1], mp[0], w_cat, ROW_TILE, 512)
            proj_s = mod_matmul(xs, g_mix, ms[1], ms[0], w_cat, bs, 512)
            wts = (nsa_q_norm[j], nsa_k_norm[j], cmp_w1[j], cmp_pe[j], cmp_w2[j])
            op, rp, wp = nsa_prompt_pallas(proj_p, bp, L, *wts)
            br_s, gexp_s, rs, ws = nsa_sample_pallas(proj_s, bs, cache_nsa_kv[j], page_table, state_nsa_win[j], *wts)
            outs['rows_p'].append(rp); outs['win_p'].append(wp); outs['rows_s'].append(rs); outs['win_s'].append(ws)
            w_out = nsa_w_out[j].astype(BF16)
            xp = mm_residual([op.reshape(tp, -1)], [w_out], xp, mp[2], ROW_TILE)
            xs = mm_residual_gated(br_s, gexp_s, w_out, xs, ms[2])

        rw = _pad_cols(router_w[i], V7X_LANES)
        rwh = rw.astype(BF16)
        rwl = (rw - rwh.astype(F32)).astype(BF16)
        rb = jnp.concatenate([router_b[i], jnp.full((V7X_LANES - N_EXPERTS,), -1e30, F32)])[None, :]
        h_p, e_p, gt_p = moe_router(xp, g_ffn, mp[4], mp[3], rwh, rwl, rb, ROW_TILE)
        h_s, e_s, gt_s = moe_router(xs, g_ffn, ms[4], ms[3], rwh, rwl, rb, bs)
        h_all = jnp.concatenate([h_p, h_s], axis=0)
        top_e = jnp.concatenate([e_p[:, :TOP_K], e_s[:, :TOP_K]], axis=0)
        gate = jnp.concatenate([gt_p[:, :TOP_K], gt_s[:, :TOP_K]], axis=0)
        blk_e, n_used, idx, row_gate = _moe_routing(top_e, gate)
        y4 = moe_ffn(h_all, blk_e, n_used, idx, row_gate, moe_w1[i], moe_b1[i], moe_w2[i], moe_b2[i])
        xp = moe_combine(xp, mp[5], y4, tp + bs, 0, bs)
        xs = moe_combine(xs, ms[5], y4, tp + bs, tp, bs)

    st = lambda k: jnp.stack(outs[k])
    return (xp.reshape(bp, L, d), xs.reshape(bs, 1, d), st('rows_p'), st('rows_s'), st('win_p'), st('win_s'),
            st('sconv_p'), st('sconv_s'), st('ssm_p'), st('ssm_s'), st('rconv_p'), st('rconv_s'),
            st('rg_p'), st('rg_s'))
```

```python
import functools
import math

import jax
import jax.numpy as jnp
import numpy as np
from jax import lax
from jax.experimental import pallas as pl
from jax.experimental.pallas import tpu as pltpu

F32 = jnp.float32
BF16 = jnp.bfloat16

D_MODEL = 1024
NORM_EPS = 1e-6

SSD_WIDTH = 1024
SSD_HEAD_DIM = 64
SSD_HEADS = 16
SSD_GROUPS = 4
SSD_STATE = 128
SSD_CONV = 4
SSD_CHUNK = 128
SSD_XBC = SSD_WIDTH + 2 * SSD_GROUPS * SSD_STATE

RG_WIDTH = 1024
RG_BLOCKS = 16
RG_BLOCK_DIM = 64
RG_C = 8.0

NSA_HEADS = 16
NSA_KV_HEADS = 4
NSA_HEAD_DIM = 64
NSA_GROUP = 4
NSA_Q_WIDTH = 1024
NSA_KV_WIDTH = 256
CMP_LEN = 32
CMP_STRIDE = 16
CMP_R = 2
CMP_HID = 128
SEL_BLOCK = 64
SEL_TOPN = 16
WINDOW = 512
NSA_Q_BLOCK = 64

N_EXPERTS = 32
TOP_K = 4
D_FF = 1024
SWIGLU_LIMIT = 7.0
SWIGLU_ALPHA = 1.702

V7X_LANES = 128
V7X_VMEM_LIMIT_BYTES = 56 * 1024 * 1024

MOE_BLOCK_ROWS = 256
MOE_DMA_UNROLL = 16
ROW_TILE = 512
PROJ_ROW_TILE = 1024


def _cparams(sem, vmem=None):
    return pltpu.CompilerParams(dimension_semantics=sem, vmem_limit_bytes=vmem)


def _adaln_body(c_ref, w_ref, b_ref, o_ref):
    c = c_ref[...]
    s = c * jax.nn.sigmoid(c)
    o_ref[...] = jnp.dot(s.astype(BF16), w_ref[...], preferred_element_type=F32) + b_ref[...]


def adaln_mod(c, w_bf, b):
    r, d = c.shape
    n = w_bf.shape[1]
    tn = 1536
    return pl.pallas_call(
        _adaln_body,
        out_shape=jax.ShapeDtypeStruct((r, n), F32),
        grid=(n // tn,),
        in_specs=[pl.BlockSpec((r, d), lambda j: (0, 0)),
                  pl.BlockSpec((d, tn), lambda j: (0, j)),
                  pl.BlockSpec((1, tn), lambda j: (0, j))],
        out_specs=pl.BlockSpec((r, tn), lambda j: (0, j)),
        compiler_params=_cparams(("arbitrary",)),
        name="adaln_mod",
    )(c, w_bf, b)


def _modulated(x, g, scale, shift):
    ms = jnp.mean(x * x, axis=-1, keepdims=True)
    y = x * lax.rsqrt(ms + NORM_EPS) * g
    return y * (1.0 + scale) + shift


def _mod_matmul_body(x_ref, g_ref, sc_ref, sh_ref, w_ref, o_ref, h_scr):
    @pl.when(pl.program_id(1) == 0)
    def _():
        h_scr[...] = _modulated(x_ref[...], g_ref[...], sc_ref[0], sh_ref[0]).astype(BF16)

    o_ref[...] = jnp.dot(h_scr[...], w_ref[...], preferred_element_type=F32)


def mod_matmul(x, g, scale, shift, w_bf, tm, tn):
    t, d = x.shape
    n = w_bf.shape[1]
    m, r, _ = scale.shape
    rows_per_mod = t // m
    mod_spec = pl.BlockSpec((1, r, d), lambda i, j: ((i * tm) // rows_per_mod, 0, 0))
    return pl.pallas_call(
        _mod_matmul_body,
        out_shape=jax.ShapeDtypeStruct((t, n), F32),
        grid=(t // tm, n // tn),
        in_specs=[pl.BlockSpec((tm, d), lambda i, j: (i, 0)),
                  pl.BlockSpec((1, d), lambda i, j: (0, 0)),
                  mod_spec, mod_spec,
                  pl.BlockSpec((d, tn), lambda i, j: (0, j))],
        out_specs=pl.BlockSpec((tm, tn), lambda i, j: (i, j)),
        scratch_shapes=[pltpu.VMEM((tm, d), BF16)],
        compiler_params=_cparams(("arbitrary", "arbitrary")),
        name="mod_matmul",
    )(x, g, scale, shift, w_bf)


def _mm_residual_body(n_a, *refs):
    a_refs = refs[:n_a]
    w_refs = refs[n_a:2 * n_a]
    x_ref, g_ref, o_ref = refs[2 * n_a:]
    acc = None
    for a_ref, w_ref in zip(a_refs, w_refs):
        p = jnp.dot(a_ref[...].astype(BF16), w_ref[...], preferred_element_type=F32)
        acc = p if acc is None else acc + p
    o_ref[...] = x_ref[...] + g_ref[0] * acc


def mm_residual(a_list, w_list, x, gate, tm):
    t, d = x.shape
    m, r, _ = gate.shape
    rows_per_mod = t // m
    in_specs = [pl.BlockSpec((tm, a.shape[1]), lambda i: (i, 0)) for a in a_list]
    in_specs += [pl.BlockSpec(w.shape, lambda i: (0, 0)) for w in w_list]
    in_specs += [pl.BlockSpec((tm, d), lambda i: (i, 0)),
                 pl.BlockSpec((1, r, d), lambda i: ((i * tm) // rows_per_mod, 0, 0))]
    return pl.pallas_call(
        functools.partial(_mm_residual_body, len(a_list)),
        out_shape=jax.ShapeDtypeStruct((t, d), F32),
        grid=(t // tm,),
        in_specs=in_specs,
        out_specs=pl.BlockSpec((tm, d), lambda i: (i, 0)),
        compiler_params=_cparams(("arbitrary",)),
        name="mm_residual",
    )(*a_list, *w_list, x, gate)


def _router_body(x_ref, g_ref, sc_ref, sh_ref, wh_ref, wl_ref, rb_ref, base_ref, tri_ref,
                 h_ref, e_ref, gt_ref, rank_ref, cnt_ref, carry):
    h = _modulated(x_ref[...], g_ref[...], sc_ref[0], sh_ref[0])
    h_ref[...] = h
    hh = h.astype(BF16)
    hl = (h - hh.astype(F32)).astype(BF16)
    wh = wh_ref[...]
    wl = wl_ref[...]
    logits = (jnp.dot(hh, wh, preferred_element_type=F32)
              + (jnp.dot(hh, wl, preferred_element_type=F32)
                 + jnp.dot(hl, wh, preferred_element_type=F32))) + rb_ref[...]
    lane = lax.broadcasted_iota(jnp.int32, logits.shape, 1)
    neg = jnp.float32(-jnp.inf)
    vals, idxs = [], []
    cur = logits
    for _ in range(TOP_K):
        m = jnp.max(cur, axis=-1, keepdims=True)
        idx = jnp.min(jnp.where(cur == m, lane, V7X_LANES), axis=-1, keepdims=True)
        vals.append(m)
        idxs.append(idx)
        cur = jnp.where(lane == idx, neg, cur)
    exps = [jnp.exp(v - vals[0]) for v in vals]
    den = exps[0] + exps[1] + exps[2] + exps[3]
    e_out = jnp.zeros(logits.shape, jnp.int32)
    g_out = jnp.zeros(logits.shape, F32)
    hot = jnp.zeros(logits.shape, F32)
    for k in range(TOP_K):
        e_out = jnp.where(lane == k, idxs[k], e_out)
        g_out = jnp.where(lane == k, exps[k] / den, g_out)
        hot = hot + jnp.where(lane == idxs[k], 1.0, 0.0)
    e_ref[...] = e_out
    gt_ref[...] = g_out

    @pl.when(pl.program_id(0) == 0)
    def _():
        carry[...] = jnp.broadcast_to(base_ref[...], carry.shape)

    before = jnp.dot(tri_ref[...], hot.astype(BF16), preferred_element_type=F32) + carry[0:1, :]
    rank_out = jnp.zeros(logits.shape, F32)
    for k in range(TOP_K):
        r_k = jnp.sum(jnp.where(lane == idxs[k], before, 0.0), axis=-1, keepdims=True)
        rank_out = jnp.where(lane == k, r_k, rank_out)
    rank_ref[...] = rank_out
    total = carry[0:1, :] + jnp.sum(hot, axis=0, keepdims=True)
    carry[0:1, :] = total
    cnt_ref[0] = total


def moe_router(x, g, scale, shift, wh, wl, rb, base, tm):
    t, d = x.shape
    m, r, _ = scale.shape
    rows_per_mod = t // m
    mod_spec = pl.BlockSpec((1, r, d), lambda i: ((i * tm) // rows_per_mod, 0, 0))
    tri = jnp.asarray(np.tril(np.ones((tm, tm), np.float32), -1), BF16)
    lanes = lambda dt_: jax.ShapeDtypeStruct((t, V7X_LANES), dt_)
    lane_spec = pl.BlockSpec((tm, V7X_LANES), lambda i: (i, 0))
    return pl.pallas_call(
        _router_body,
        out_shape=(jax.ShapeDtypeStruct((t, d), F32), lanes(jnp.int32), lanes(F32), lanes(F32),
                   jax.ShapeDtypeStruct((t // tm, 1, V7X_LANES), F32)),
        grid=(t // tm,),
        in_specs=[pl.BlockSpec((tm, d), lambda i: (i, 0)),
                  pl.BlockSpec((1, d), lambda i: (0, 0)),
                  mod_spec, mod_spec,
                  pl.BlockSpec((d, V7X_LANES), lambda i: (0, 0)),
                  pl.BlockSpec((d, V7X_LANES), lambda i: (0, 0)),
                  pl.BlockSpec((1, V7X_LANES), lambda i: (0, 0)),
                  pl.BlockSpec((1, V7X_LANES), lambda i: (0, 0)),
                  pl.BlockSpec((tm, tm), lambda i: (0, 0))],
        out_specs=(pl.BlockSpec((tm, d), lambda i: (i, 0)), lane_spec, lane_spec, lane_spec,
                   pl.BlockSpec((1, 1, V7X_LANES), lambda i: (i, 0, 0))),
        scratch_shapes=[pltpu.VMEM((8, V7X_LANES), F32)],
        compiler_params=_cparams(("arbitrary",)),
        name="moe_router",
    )(x, g, scale, shift, wh, wl, rb, base, tri)


def _moe_dest_body(e_ref, rank_ref, pstart_ref, o_ref):
    e = e_ref[...]
    lane = lax.broadcasted_iota(jnp.int32, e.shape, 1)
    pstart = pstart_ref[...]
    out = jnp.zeros(e.shape, F32)
    for k in range(TOP_K):
        start_k = jnp.sum(jnp.where(lane == e[:, k:k + 1], pstart, 0.0), axis=-1, keepdims=True)
        out = jnp.where(lane == k, start_k, out)
    o_ref[...] = (out + rank_ref[...]).astype(jnp.int32)


def moe_dest(top_e, rank, pad_start, tm):
    t = top_e.shape[0]
    spec = pl.BlockSpec((tm, V7X_LANES), lambda i: (i, 0))
    return pl.pallas_call(
        _moe_dest_body,
        out_shape=jax.ShapeDtypeStruct((t, V7X_LANES), jnp.int32),
        grid=(t // tm,),
        in_specs=[spec, spec, pl.BlockSpec((1, V7X_LANES), lambda i: (0, 0))],
        out_specs=spec,
        compiler_params=_cparams(("arbitrary",)),
        name="moe_dest",
    )(top_e, rank, pad_start)


def _ffn_body(blk_e_ref, nused_ref,
              idx_hbm, h_hbm, w1_ref, b1_ref, w2_ref, b2_ref,
              out_hbm,
              idx_smem, xbuf, ybuf, w1bf, w2bf, sem_idx, sem_g, sem_s):
    bm = MOE_BLOCK_ROWS
    i = pl.program_id(0)
    n_used = nused_ref[0]

    def idx_copy(blk, slot):
        return pltpu.make_async_copy(idx_hbm.at[blk], idx_smem.at[slot], sem_idx.at[slot])

    def gather_copy(tok, slot, r):
        return pltpu.make_async_copy(h_hbm.at[pl.ds(tok, 1)], xbuf.at[slot, pl.ds(r, 1)], sem_g.at[slot])

    def scatter_copy(dst, slot, r):
        return pltpu.make_async_copy(ybuf.at[slot, pl.ds(r, 1)], out_hbm.at[pl.ds(dst, 1)], sem_s.at[slot])

    def start_gather(islot, slot):
        for r in range(bm):
            gather_copy(idx_smem[islot, r], slot, r).start()

    def wait_gather(slot):
        def body(r, c):
            gather_copy(0, slot, 0).wait()
            return c
        lax.fori_loop(0, bm, body, 0, unroll=MOE_DMA_UNROLL)

    def start_scatter(islot, slot):
        for r in range(bm):
            scatter_copy(idx_smem[islot, bm + r], slot, r).start()

    def wait_scatter(slot):
        def body(r, c):
            scatter_copy(0, slot, 0).wait()
            return c
        lax.fori_loop(0, bm, body, 0, unroll=MOE_DMA_UNROLL)

    @pl.when(i < n_used)
    def _():
        slot = i % 2
        islot = i % 3

        @pl.when(i == 0)
        def _():
            ybuf[...] = jnp.zeros_like(ybuf)
            for sl in range(2):
                tail = pltpu.make_async_copy(ybuf.at[sl], out_hbm.at[pl.ds(out_hbm.shape[0] - (2 - sl) * bm, bm)],
                                             sem_s.at[sl])
                tail.start()
                tail.wait()
            idx_copy(0, 0).start()
            idx_copy(0, 0).wait()
            start_gather(0, 0)

            @pl.when(n_used > 1)
            def _():
                idx_copy(1, 1).start()

        @pl.when(i + 2 < n_used)
        def _():
            idx_copy(i + 2, (i + 2) % 3).start()

        @pl.when(i + 1 < n_used)
        def _():
            idx_copy(i + 1, (i + 1) % 3).wait()
            start_gather((i + 1) % 3, 1 - slot)

        @pl.when(jnp.logical_or(i == 0, blk_e_ref[i] != blk_e_ref[jnp.maximum(i - 1, 0)]))
        def _():
            w1bf[...] = w1_ref[0].astype(BF16)
            w2bf[...] = w2_ref[0].astype(BF16)

        wait_gather(slot)

        @pl.when(i >= 2)
        def _():
            wait_scatter(slot)

        x = xbuf[slot].astype(BF16)
        u = jnp.dot(x, w1bf[...], preferred_element_type=F32) + b1_ref[0]
        gl = jnp.minimum(u[:, :D_FF], SWIGLU_LIMIT)
        lin = jnp.clip(u[:, D_FF:], -SWIGLU_LIMIT, SWIGLU_LIMIT)
        act = gl * jax.nn.sigmoid(SWIGLU_ALPHA * gl) * (lin + 1.0)
        ybuf[slot] = jnp.dot(act.astype(BF16), w2bf[...], preferred_element_type=F32) + b2_ref[0]
        start_scatter(islot, slot)

        @pl.when(i == n_used - 1)
        def _():
            @pl.when(i >= 1)
            def _():
                wait_scatter(1 - slot)
            wait_scatter(slot)


def moe_ffn(h_all, blk_e, n_used, idx, w1, b1, w2, b2):
    t, d = h_all.shape
    bm = MOE_BLOCK_ROWS
    n_blocks = idx.shape[0]
    grid_spec = pltpu.PrefetchScalarGridSpec(
        num_scalar_prefetch=2,
        grid=(n_blocks,),
        in_specs=[pl.BlockSpec(memory_space=pl.ANY),
                  pl.BlockSpec(memory_space=pl.ANY),
                  pl.BlockSpec((1, d, 2 * D_FF), lambda i, be, nu: (be[i], 0, 0)),
                  pl.BlockSpec((1, 1, 2 * D_FF), lambda i, be, nu: (be[i], 0, 0)),
                  pl.BlockSpec((1, D_FF, d), lambda i, be, nu: (be[i], 0, 0)),
                  pl.BlockSpec((1, 1, d), lambda i, be, nu: (be[i], 0, 0))],
        out_specs=pl.BlockSpec(memory_space=pl.ANY),
        scratch_shapes=[pltpu.SMEM((3, 2 * bm), jnp.int32),
                        pltpu.VMEM((2, bm, d), F32),
                        pltpu.VMEM((2, bm, d), F32),
                        pltpu.VMEM((d, 2 * D_FF), BF16),
                        pltpu.VMEM((D_FF, d), BF16),
                        pltpu.SemaphoreType.DMA((3,)),
                        pltpu.SemaphoreType.DMA((2,)),
                        pltpu.SemaphoreType.DMA((2,))],
    )
    return pl.pallas_call(
        _ffn_body,
        out_shape=jax.ShapeDtypeStruct((t * TOP_K + 2 * bm, d), F32),
        grid_spec=grid_spec,
        compiler_params=_cparams(("arbitrary",), V7X_VMEM_LIMIT_BYTES),
        name="moe_ffn",
    )(blk_e, n_used, idx, h_all, w1, b1.reshape(N_EXPERTS, 1, -1), w2, b2.reshape(N_EXPERTS, 1, -1))


def _moe_combine_body(x_ref, g_ref, rg_ref, y0_ref, y1_ref, y2_ref, y3_ref, o_ref):
    rg = rg_ref[...]
    acc = ((rg[:, 0:1] * y0_ref[...] + rg[:, 1:2] * y1_ref[...])
           + (rg[:, 2:3] * y2_ref[...] + rg[:, 3:4] * y3_ref[...]))
    o_ref[...] = x_ref[...] + g_ref[0] * acc


def moe_combine(x, gate, router_gate, y4, t_all, row_off, tm):
    t, d = x.shape
    m, r, _ = gate.shape
    rows_per_mod = t // m
    y_spec = lambda k: pl.BlockSpec((tm, d), lambda i: ((k * t_all + row_off) // tm + i, 0))
    return pl.pallas_call(
        _moe_combine_body,
        out_shape=jax.ShapeDtypeStruct((t, d), F32),
        grid=(t // tm,),
        in_specs=[pl.BlockSpec((tm, d), lambda i: (i, 0)),
                  pl.BlockSpec((1, r, d), lambda i: ((i * tm) // rows_per_mod, 0, 0)),
                  pl.BlockSpec((tm, V7X_LANES), lambda i: (i, 0)),
                  y_spec(0), y_spec(1), y_spec(2), y_spec(3)],
        out_specs=pl.BlockSpec((tm, d), lambda i: (i, 0)),
        compiler_params=_cparams(("arbitrary",)),
        name="moe_combine",
    )(x, gate, router_gate, y4, y4, y4, y4)


def _moe_block_layout(counts, tk):
    bm = MOE_BLOCK_ROWS
    padded = (counts + bm - 1) // bm * bm
    pad_end = jnp.cumsum(padded)
    pad_start = pad_end - padded
    n_blocks = -(-tk // bm) + N_EXPERTS
    blk_start = jnp.arange(n_blocks, dtype=jnp.int32) * bm
    blk_e = jnp.minimum(jnp.sum((pad_end[None, :] <= blk_start[:, None]).astype(jnp.int32), axis=1),
                        N_EXPERTS - 1)
    n_used = (pad_end[-1] // bm).astype(jnp.int32).reshape(1)
    return padded, pad_start, blk_e, n_used, n_blocks


def _moe_row_tables(dest, counts, padded, pad_start, n_blocks, t):
    bm = MOE_BLOCK_ROWS
    tk = t * TOP_K
    n_rows = n_blocks * bm
    big = jnp.int32(2 ** 30)
    p = jnp.arange(bm, dtype=jnp.int32)[None, :]
    e = jnp.arange(N_EXPERTS, dtype=jnp.int32)[:, None]
    pad_keys = jnp.where(p < (padded - counts)[:, None], (pad_start + counts)[:, None] + p, big + e * bm + p)
    n_fill = n_rows - tk - N_EXPERTS * bm
    keys = jnp.concatenate([dest.reshape(tk), pad_keys.reshape(-1), big + N_EXPERTS * bm + jnp.arange(n_fill, dtype=jnp.int32)])
    vals = jnp.concatenate([jnp.arange(tk, dtype=jnp.int32), jnp.full((n_rows - tk,), -1, jnp.int32)])
    _, v = lax.sort((keys, vals), num_keys=1)
    valid = v >= 0
    row = jnp.arange(n_rows, dtype=jnp.int32)
    src_tok = jnp.where(valid, v // TOP_K, 0)
    pad_row = tk + ((row // bm) % 2) * bm + row % bm
    dst_row = jnp.where(valid, (v % TOP_K) * t + v // TOP_K, pad_row)
    return jnp.concatenate([src_tok.reshape(n_blocks, bm), dst_row.reshape(n_blocks, bm)], axis=1)


def _rmsnorm(x, w):
    xf = x.astype(F32)
    y = xf * lax.rsqrt(jnp.mean(xf * xf, axis=-1, keepdims=True) + NORM_EPS)
    return y * w.astype(F32)


def _causal_conv(x, prev, w, b):
    width = w.shape[0]
    L = x.shape[1]
    xp = jnp.concatenate([prev.astype(x.dtype), x], axis=1)
    y = b
    for k in range(width):
        y = y + xp[:, k:k + L] * w[k]
    return y, xp[:, xp.shape[1] - (width - 1):]


def _ssd_scan(x, dt, A, Bm, Cm, h0):
    b, L, H, P = x.shape
    G, N = Bm.shape[2], Bm.shape[3]
    K = H // G
    q = math.gcd(L, SSD_CHUNK)
    nc = L // q
    a = (dt * A).reshape(b, nc, q, G, K)
    xdt = (x * dt[..., None]).reshape(b, nc, q, G, K, P)
    Bc = Bm.reshape(b, nc, q, G, N)
    Cc = Cm.reshape(b, nc, q, G, N)
    a_cs = jnp.cumsum(a, axis=2)
    causal = jnp.tril(jnp.ones((q, q), dtype=bool))[None, None, :, :, None, None]
    seg = a_cs[:, :, :, None] - a_cs[:, :, None, :]
    decay = jnp.exp(jnp.where(causal, seg, -jnp.inf))
    cb = jnp.einsum('bctgn,bcsgn->bctsg', Cc, Bc)
    y_diag = jnp.einsum('bctsgk,bcsgkp->bctgkp', cb[..., None] * decay, xdt)
    to_end = jnp.exp(a_cs[:, :, -1:] - a_cs)
    states = jnp.einsum('bcsgn,bcsgkp->bcgkpn', Bc, xdt * to_end[..., None])
    chunk_decay = jnp.exp(a_cs[:, :, -1])

    def step(h, inp):
        st, dec = inp
        return h * dec[..., None, None] + st, h

    h_last, h_prev = lax.scan(step, h0.reshape(b, G, K, P, N),
                              (jnp.moveaxis(states, 1, 0), jnp.moveaxis(chunk_decay, 1, 0)))
    h_prev = jnp.moveaxis(h_prev, 0, 1)
    y_off = jnp.einsum('bctgn,bcgkpn->bctgkp', Cc, h_prev) * jnp.exp(a_cs)[..., None]
    return (y_diag + y_off).reshape(b, L, H, P), h_last.reshape(b, H, P, N)


def _rg_lru(x, h0, wa, ba, wi, bi, lam):
    b, L, _ = x.shape
    xb = x.reshape(b, L, RG_BLOCKS, RG_BLOCK_DIM)
    r = jax.nn.sigmoid(jnp.einsum('blnd,nde->blne', xb, wa).reshape(b, L, RG_WIDTH) + ba)
    i = jax.nn.sigmoid(jnp.einsum('blnd,nde->blne', xb, wi).reshape(b, L, RG_WIDTH) + bi)
    log_a = -RG_C * r * jax.nn.softplus(-lam.astype(F32))
    a = jnp.exp(log_a)
    u = jnp.sqrt(-jnp.expm1(2.0 * log_a)) * (i * x)
    u = u.at[:, 0].add(a[:, 0] * h0)

    def combine(e1, e2):
        return e1[0] * e2[0], e2[0] * e1[1] + e2[1]

    _, h = lax.associative_scan(combine, (a, u), axis=1)
    return h, h[:, -1]


def _rec_core(z, xbc, dt, gate, xr, conv_ssd0, ssm0, conv_rg0, rg0, conv_w, conv_b, dt_bias, a_log,
              d_skip, norm_w, rg_conv_w, rg_conv_b, wa, ba, wi, bi, lam):
    b, L, _ = z.shape
    xbc, conv_ssd1 = _causal_conv(xbc, conv_ssd0, conv_w, conv_b)
    xbc = jax.nn.silu(xbc)
    xs, Bm, Cm = jnp.split(xbc, [SSD_WIDTH, SSD_WIDTH + SSD_GROUPS * SSD_STATE], axis=-1)
    dt = jax.nn.softplus(dt + dt_bias)
    A = -jnp.exp(a_log)
    xh = xs.reshape(b, L, SSD_HEADS, SSD_HEAD_DIM)
    y, ssm1 = _ssd_scan(xh, dt, A, Bm.reshape(b, L, SSD_GROUPS, SSD_STATE),
                        Cm.reshape(b, L, SSD_GROUPS, SSD_STATE), ssm0)
    y = y + d_skip[:, None] * xh
    y = y.reshape(b, L, SSD_WIDTH) * jax.nn.silu(z)
    y = _rmsnorm(y.reshape(b, L, SSD_GROUPS, -1), norm_w.reshape(SSD_GROUPS, -1)).reshape(b, L, SSD_WIDTH)
    xr, conv_rg1 = _causal_conv(xr, conv_rg0, rg_conv_w, rg_conv_b)
    r_out, rg1 = _rg_lru(xr, rg0, wa, ba, wi, bi, lam)
    r_out = r_out * jax.nn.gelu(gate)
    return y, r_out, conv_ssd1, ssm1, conv_rg1, rg1


CONV_TAIL = 8
RG_TILE = 256


def _split3_bf16(x):
    h = x.astype(BF16)
    r = x - h.astype(F32)
    m = r.astype(BF16)
    return h, m, (r - m.astype(F32)).astype(BF16)


def _dot3(parts, w, dims=None):
    if dims is None:
        outs = [jnp.dot(p, w, preferred_element_type=F32) for p in parts]
    else:
        outs = [lax.dot_general(w, p, dims, preferred_element_type=F32) for p in parts]
    return (outs[0] + outs[1]) + outs[2]


def _softplus(x):
    return jnp.maximum(x, 0.0) + jnp.log(1.0 + jnp.exp(-jnp.abs(x)))


def _silu(x):
    return x * jax.nn.sigmoid(x)


def _group_rmsnorm(y, w, n_groups):
    width = y.shape[1] // n_groups
    outs = []
    for g in range(n_groups):
        yg = y[:, g * width:(g + 1) * width]
        outs.append(yg * lax.rsqrt(jnp.mean(yg * yg, axis=-1, keepdims=True) + NORM_EPS))
    return jnp.concatenate(outs, axis=1) * w


def _conv_tile(xbuf, cw_ref, cb_ref, rows):
    y = cb_ref[...]
    for k in range(SSD_CONV):
        y = y + cw_ref[k:k + 1, :] * xbuf[pl.ds(CONV_TAIL - (SSD_CONV - 1) + k, rows), :]
    return y


def _ssd_prompt_body(z_ref, xs_ref, bc_ref, dt_ref, cw_ref, cb_ref, dtb_ref, a_ref, dexp_ref, nw_ref, tri_ref,
                     y_ref, conv_ref, state_ref, xbuf, h_scr):
    q = SSD_CHUNK
    c = pl.program_id(1)
    last = pl.num_programs(1) - 1
    P, N = SSD_HEAD_DIM, SSD_STATE

    @pl.when(c == 0)
    def _():
        xbuf[0:CONV_TAIL, :] = jnp.zeros((CONV_TAIL, SSD_XBC), F32)
        h_scr[...] = jnp.zeros_like(h_scr)

    xbuf[CONV_TAIL:CONV_TAIL + q, 0:SSD_WIDTH] = xs_ref[...]
    xbuf[CONV_TAIL:CONV_TAIL + q, SSD_WIDTH:SSD_XBC] = bc_ref[...]
    xc = _silu(_conv_tile(xbuf, cw_ref, cb_ref, q))

    @pl.when(c == last)
    def _():
        conv_ref[0] = xbuf[CONV_TAIL + q - (SSD_CONV - 1):CONV_TAIL + q, :]

    xbuf[0:CONV_TAIL, :] = xbuf[q:q + CONV_TAIL, :]

    xs = xc[:, 0:SSD_WIDTH]
    bm = xc[:, SSD_WIDTH:SSD_WIDTH + SSD_GROUPS * N].astype(BF16)
    cm = xc[:, SSD_WIDTH + SSD_GROUPS * N:SSD_XBC].astype(BF16)
    dt = _softplus(dt_ref[...] + dtb_ref[...])
    a = dt * a_ref[...]
    a_cs = _dot3(_split3_bf16(a), tri_ref[...], dims=(((1,), (0,)), ((), ())))
    a_cs_t = a_cs.T
    dt_t = dt.T
    a_end_t = a_cs_t[:, q - 1:q]
    w_t = dt_t * jnp.exp(a_end_t - a_cs_t)
    ea = jnp.exp(a_cs)
    xs_t = xs.T
    row = lax.broadcasted_iota(jnp.int32, (q, q), 0)
    col = lax.broadcasted_iota(jnp.int32, (q, q), 1)
    causal = col <= row
    heads_per_group = SSD_HEADS // SSD_GROUPS
    ys = []
    for g in range(SSD_GROUPS):
        bg = bm[:, g * N:(g + 1) * N]
        cg = cm[:, g * N:(g + 1) * N]
        cb = lax.dot_general(cg, bg, _NT_DIMS_SSD, preferred_element_type=F32)
        for k in range(heads_per_group):
            h = g * heads_per_group + k
            seg = a_cs[:, h:h + 1] - a_cs_t[h:h + 1, :]
            decay = jnp.exp(jnp.where(causal, seg, MASK_NEG))
            xh = xs[:, h * P:(h + 1) * P]
            xdt = (xh * dt[:, h:h + 1]).astype(BF16)
            y_diag = jnp.dot((cb * decay).astype(BF16), xdt, preferred_element_type=F32)
            h_prev = h_scr[h]
            y_off = lax.dot_general(cg, h_prev.astype(BF16), _NT_DIMS_SSD,
                                    preferred_element_type=F32) * ea[:, h:h + 1]
            st = jnp.dot((xs_t[h * P:(h + 1) * P, :] * w_t[h:h + 1, :]).astype(BF16), bg,
                         preferred_element_type=F32)
            h_scr[h] = h_prev * jnp.exp(a_end_t[h:h + 1, :]) + st
            ys.append(y_diag + y_off)
    y = jnp.concatenate(ys, axis=1) + dexp_ref[...] * xs
    y = y * _silu(z_ref[...])
    y_ref[...] = _group_rmsnorm(y, nw_ref[...], SSD_GROUPS)

    @pl.when(c == last)
    def _():
        state_ref[0] = h_scr[...]


_NT_DIMS_SSD = (((1,), (1,)), ((), ()))
MASK_NEG = -1e30


def _pad_lanes(v, n=V7X_LANES):
    return jnp.pad(v, (0, n - v.shape[0]))[None, :]


def ssd_prompt(proj, b, seq_len, conv_w, conv_b, dt_bias, a_log, d_skip, norm_w):
    q = SSD_CHUNK
    nc = seq_len // q
    t = b * seq_len
    tri = jnp.asarray(np.tril(np.ones((q, q), np.float32)), BF16)
    a_neg = _pad_lanes(-jnp.exp(a_log))
    dtb = _pad_lanes(dt_bias)
    dexp = jnp.repeat(d_skip, SSD_HEAD_DIM)[None, :]
    nw = norm_w[None, :]
    cb = conv_b[None, :]
    colblk = lambda j, w=SSD_WIDTH: pl.BlockSpec((q, w), lambda bi, c: (bi * nc + c, j))
    full = lambda a: pl.BlockSpec(a.shape, lambda bi, c: (0,) * a.ndim)
    return pl.pallas_call(
        _ssd_prompt_body,
        out_shape=(jax.ShapeDtypeStruct((t, SSD_WIDTH), F32),
                   jax.ShapeDtypeStruct((b, SSD_CONV - 1, SSD_XBC), F32),
                   jax.ShapeDtypeStruct((b, SSD_HEADS, SSD_HEAD_DIM, SSD_STATE), F32)),
        grid=(b, nc),
        in_specs=[colblk(0), colblk(1), colblk(2),
                  pl.BlockSpec((q, V7X_LANES), lambda bi, c: (bi * nc + c, 5 * SSD_WIDTH // V7X_LANES)),
                  full(conv_w), full(cb), full(dtb), full(a_neg), full(dexp), full(nw), full(tri)],
        out_specs=(pl.BlockSpec((q, SSD_WIDTH), lambda bi, c: (bi * nc + c, 0)),
                   pl.BlockSpec((1, SSD_CONV - 1, SSD_XBC), lambda bi, c: (bi, 0, 0)),
                   pl.BlockSpec((1, SSD_HEADS, SSD_HEAD_DIM, SSD_STATE), lambda bi, c: (bi, 0, 0, 0))),
        scratch_shapes=[pltpu.VMEM((CONV_TAIL + q, SSD_XBC), F32),
                        pltpu.VMEM((SSD_HEADS, SSD_HEAD_DIM, SSD_STATE), F32)],
        compiler_params=_cparams(("arbitrary", "arbitrary"), V7X_VMEM_LIMIT_BYTES),
        name="ssd_prompt",
    )(proj, proj, proj, proj, conv_w, cb, dtb, a_neg, dexp, nw, tri)


def _ssd_sample_pre_body(xs_ref, bc_ref, dt_ref, s0_ref, s1_ref, s2_ref, cw_ref, cb_ref, dtb_ref, a_ref, dexp_ref,
                         hexp_ref, yd_ref, xdt_ref, b_ref, c_ref, ea_ref, eaexp_ref):
    N = SSD_STATE
    x_new = jnp.concatenate([xs_ref[...], bc_ref[...]], axis=1)
    y = (cb_ref[...] + cw_ref[0:1, :] * s0_ref[...] + cw_ref[1:2, :] * s1_ref[...]
         + cw_ref[2:3, :] * s2_ref[...] + cw_ref[3:4, :] * x_new)
    xc = _silu(y)
    xs = xc[:, 0:SSD_WIDTH]
    bm = xc[:, SSD_WIDTH:SSD_WIDTH + SSD_GROUPS * N]
    cm = xc[:, SSD_WIDTH + SSD_GROUPS * N:SSD_XBC]
    dt = _softplus(dt_ref[...] + dtb_ref[...])
    ea = jnp.exp(dt * a_ref[...])
    hexp = hexp_ref[...]
    dt_exp = _dot3(_split3_bf16(dt), hexp)
    xdt = (xs * dt_exp).astype(BF16)
    bb = bm.astype(BF16)
    cc = cm.astype(BF16)
    prod = bb.astype(F32) * cc.astype(F32)
    hw = SSD_WIDTH // SSD_GROUPS
    cb = jnp.concatenate(
        [jnp.broadcast_to(jnp.sum(prod[:, g * N:(g + 1) * N], axis=-1, keepdims=True), (xs.shape[0], hw))
         for g in range(SSD_GROUPS)], axis=1)
    yd_ref[...] = cb.astype(BF16).astype(F32) * xdt.astype(F32) + dexp_ref[...] * xs
    xdt_ref[...] = xdt.astype(F32)
    b_ref[...] = bb.astype(F32)
    c_ref[...] = cc.astype(F32)
    ea_ref[...] = ea
    eaexp_ref[...] = _dot3(_split3_bf16(ea), hexp)


def _ssd_sample_state_body(ea_smem, xdt_ref, b_ref, c_ref, yd_ref, eaexp_ref, z_ref, nw_ref, h0_ref,
                           y_ref, h1_ref):
    i = pl.program_id(0)
    N = SSD_STATE
    gw = SSD_WIDTH // SSD_GROUPS
    heads_per_group = SSD_HEADS // SSD_GROUPS
    row0 = lax.broadcasted_iota(jnp.int32, (8, 1), 0) == 0
    y_off = []
    for g in range(SSD_GROUPS):
        x8 = jnp.broadcast_to(xdt_ref[0, :, g * gw:(g + 1) * gw], (8, gw)).astype(BF16)
        b8 = jnp.where(row0, jnp.broadcast_to(b_ref[0, :, g * N:(g + 1) * N], (8, N)), 0.0).astype(BF16)
        c8 = jnp.broadcast_to(c_ref[0, :, g * N:(g + 1) * N], (8, N)).astype(BF16)
        h0g = h0_ref[0, g * gw:(g + 1) * gw, :]
        st = lax.dot_general(x8, b8, (((0,), (0,)), ((), ())), preferred_element_type=F32)
        yo = lax.dot_general(c8, h0g.astype(BF16), _NT_DIMS_SSD, preferred_element_type=F32)
        y_off.append(yo[0:1, :])
        for k in range(heads_per_group):
            h = g * heads_per_group + k
            r = slice(k * SSD_HEAD_DIM, (k + 1) * SSD_HEAD_DIM)
            h1_ref[0, g * gw + k * SSD_HEAD_DIM:g * gw + (k + 1) * SSD_HEAD_DIM, :] = (
                h0g[r, :] * ea_smem[i, h] + st[r, :])
    y = jnp.concatenate(y_off, axis=1) * eaexp_ref[0] + yd_ref[0]
    y = y * _silu(z_ref[0])
    y_ref[0] = _group_rmsnorm(y, nw_ref[...], SSD_GROUPS)


def ssd_sample(proj, conv_state, ssm_state, conv_w, conv_b, dt_bias, a_log, d_skip, norm_w):
    bsz = proj.shape[0]
    H, P, N = SSD_HEADS, SSD_HEAD_DIM, SSD_STATE
    a_neg = _pad_lanes(-jnp.exp(a_log))
    dtb = _pad_lanes(dt_bias)
    dexp = jnp.repeat(d_skip, P)[None, :]
    hexp = jnp.asarray((np.arange(V7X_LANES)[:, None] == (np.arange(H * P)[None, :] // P)).astype(np.float32), BF16)
    cb = conv_b[None, :]
    s0, s1, s2 = conv_state[:, 0], conv_state[:, 1], conv_state[:, 2]
    blk = lambda j, w: pl.BlockSpec((bsz, w), lambda i: (0, j))
    full = lambda a: pl.BlockSpec(a.shape, lambda i: (0,) * a.ndim)
    o = lambda w, dt_: jax.ShapeDtypeStruct((bsz, w), dt_)
    yd, xdt, bb, cc, ea, eaexp = pl.pallas_call(
        _ssd_sample_pre_body,
        out_shape=(o(SSD_WIDTH, F32), o(SSD_WIDTH, F32), o(SSD_GROUPS * N, F32), o(SSD_GROUPS * N, F32),
                   o(V7X_LANES, F32), o(SSD_WIDTH, F32)),
        grid=(1,),
        in_specs=[blk(1, SSD_WIDTH), blk(2, SSD_WIDTH), blk(5 * SSD_WIDTH // V7X_LANES, V7X_LANES),
                  full(s0), full(s1), full(s2), full(conv_w), full(cb), full(dtb), full(a_neg), full(dexp), full(hexp)],
        out_specs=(full(o(SSD_WIDTH, F32)), full(o(SSD_WIDTH, F32)), full(o(SSD_GROUPS * N, F32)),
                   full(o(SSD_GROUPS * N, F32)), full(o(V7X_LANES, F32)), full(o(SSD_WIDTH, F32))),
        compiler_params=_cparams(("arbitrary",)),
        name="ssd_sample_pre",
    )(proj, proj, proj, s0, s1, s2, conv_w, cb, dtb, a_neg, dexp, hexp)
    x_new = jnp.concatenate([proj[:, SSD_WIDTH:2 * SSD_WIDTH], proj[:, 2 * SSD_WIDTH:3 * SSD_WIDTH]], axis=1)
    conv_new = jnp.stack([s1, s2, x_new], axis=1)
    z3 = proj[:, 0:SSD_WIDTH].reshape(bsz, 1, SSD_WIDTH)
    row = lambda w: pl.BlockSpec((1, 1, w), lambda i, ea_: (i, 0, 0))
    nw = norm_w[None, :]
    grid_spec = pltpu.PrefetchScalarGridSpec(
        num_scalar_prefetch=1,
        grid=(bsz,),
        in_specs=[row(SSD_WIDTH), row(SSD_GROUPS * N), row(SSD_GROUPS * N), row(SSD_WIDTH), row(SSD_WIDTH),
                  row(SSD_WIDTH), pl.BlockSpec(nw.shape, lambda i, ea_: (0, 0)),
                  pl.BlockSpec((1, H * P, N), lambda i, ea_: (i, 0, 0))],
        out_specs=(row(SSD_WIDTH), pl.BlockSpec((1, H * P, N), lambda i, ea_: (i, 0, 0))),
    )
    r3 = lambda a: a.reshape(bsz, 1, a.shape[1])
    y, h1 = pl.pallas_call(
        _ssd_sample_state_body,
        out_shape=(jax.ShapeDtypeStruct((bsz, 1, SSD_WIDTH), F32), jax.ShapeDtypeStruct((bsz, H * P, N), F32)),
        grid_spec=grid_spec,
        compiler_params=_cparams(("arbitrary",)),
        name="ssd_sample_state",
    )(ea[:, :H], r3(xdt), r3(bb), r3(cc), r3(yd), r3(eaexp), z3, nw, ssm_state.reshape(bsz, H * P, N))
    return y.reshape(bsz, SSD_WIDTH), conv_new, h1.reshape(bsz, H, P, N)


def _rg_gates(xc, wa_ref, ba_ref, wi_ref, bi_ref, sp_ref):
    xb = xc.astype(BF16)
    r = jax.nn.sigmoid(jnp.dot(xb, wa_ref[...], preferred_element_type=F32) + ba_ref[...])
    ig = jax.nn.sigmoid(jnp.dot(xb, wi_ref[...], preferred_element_type=F32) + bi_ref[...])
    log_a = -RG_C * r * sp_ref[...]
    a = jnp.exp(log_a)
    u = jnp.sqrt(1.0 - jnp.exp(2.0 * log_a)) * (ig * xc)
    return a, u


def _rg_prompt_body(gate_ref, xr_ref, cw_ref, cb_ref, wa_ref, ba_ref, wi_ref, bi_ref, sp_ref,
                    y_ref, conv_ref, state_ref, xbuf, h_scr):
    rows = RG_TILE
    c = pl.program_id(1)
    last = pl.num_programs(1) - 1

    @pl.when(c == 0)
    def _():
        xbuf[0:CONV_TAIL, :] = jnp.zeros((CONV_TAIL, RG_WIDTH), F32)
        h_scr[...] = jnp.zeros_like(h_scr)

    xbuf[CONV_TAIL:CONV_TAIL + rows, :] = xr_ref[...]
    xc = _conv_tile(xbuf, cw_ref, cb_ref, rows)

    @pl.when(c == last)
    def _():
        conv_ref[0] = xbuf[CONV_TAIL + rows - (SSD_CONV - 1):CONV_TAIL + rows, :]

    xbuf[0:CONV_TAIL, :] = xbuf[rows:rows + CONV_TAIL, :]
    a, u = _rg_gates(xc, wa_ref, ba_ref, wi_ref, bi_ref, sp_ref)
    t_idx = lax.broadcasted_iota(jnp.int32, (rows, 1), 0)
    d = 1
    while d < rows:
        keep = t_idx >= d
        a_sh = jnp.where(keep, pltpu.roll(a, d, 0), 1.0)
        u_sh = jnp.where(keep, pltpu.roll(u, d, 0), 0.0)
        u = u + a * u_sh
        a = a * a_sh
        d *= 2
    h = u + a * h_scr[0:1, :]
    h_scr[0:1, :] = h[rows - 1:rows, :]
    y_ref[...] = h * _gelu_tanh(gate_ref[...])

    @pl.when(c == last)
    def _():
        state_ref[0] = h[rows - 1:rows, :]


def _rg_weights(wa, ba, wi, bi, lam):
    eye = jnp.eye(RG_BLOCKS, dtype=F32)
    bd = lambda w: jnp.einsum('nde,nm->ndme', w, eye).reshape(RG_WIDTH, RG_WIDTH).astype(BF16)
    return bd(wa), ba[None, :], bd(wi), bi[None, :], jax.nn.softplus(-lam)[None, :]


def rg_prompt(proj, b, seq_len, conv_w, conv_b, wa, ba, wi, bi, lam):
    rows = RG_TILE
    nt = seq_len // rows
    t = b * seq_len
    wts = _rg_weights(wa, ba, wi, bi, lam)
    cb = conv_b[None, :]
    full = lambda a: pl.BlockSpec(a.shape, lambda bi_, c: (0,) * a.ndim)
    return pl.pallas_call(
        _rg_prompt_body,
        out_shape=(jax.ShapeDtypeStruct((t, RG_WIDTH), F32),
                   jax.ShapeDtypeStruct((b, SSD_CONV - 1, RG_WIDTH), F32),
                   jax.ShapeDtypeStruct((b, 1, RG_WIDTH), F32)),
        grid=(b, nt),
        in_specs=[pl.BlockSpec((rows, RG_WIDTH), lambda bi_, c: (bi_ * nt + c, 3)),
                  pl.BlockSpec((rows, RG_WIDTH), lambda bi_, c: (bi_ * nt + c, 4)),
                  full(conv_w), full(cb)] + [full(w) for w in wts],
        out_specs=(pl.BlockSpec((rows, RG_WIDTH), lambda bi_, c: (bi_ * nt + c, 0)),
                   pl.BlockSpec((1, SSD_CONV - 1, RG_WIDTH), lambda bi_, c: (bi_, 0, 0)),
                   pl.BlockSpec((1, 1, RG_WIDTH), lambda bi_, c: (bi_, 0, 0))),
        scratch_shapes=[pltpu.VMEM((CONV_TAIL + rows, RG_WIDTH), F32), pltpu.VMEM((8, RG_WIDTH), F32)],
        compiler_params=_cparams(("arbitrary", "arbitrary"), V7X_VMEM_LIMIT_BYTES),
        name="rg_prompt",
    )(proj, proj, conv_w, cb, *wts)


def _rg_sample_body(gate_ref, xr_ref, s0_ref, s1_ref, s2_ref, h0_ref, cw_ref, cb_ref, wa_ref, ba_ref, wi_ref, bi_ref,
                    sp_ref, y_ref, h1_ref):
    xc = (cb_ref[...] + cw_ref[0:1, :] * s0_ref[...] + cw_ref[1:2, :] * s1_ref[...]
          + cw_ref[2:3, :] * s2_ref[...] + cw_ref[3:4, :] * xr_ref[...])
    a, u = _rg_gates(xc, wa_ref, ba_ref, wi_ref, bi_ref, sp_ref)
    h = a * h0_ref[...] + u
    h1_ref[...] = h
    y_ref[...] = h * _gelu_tanh(gate_ref[...])


def rg_sample(proj, conv_state, h0, conv_w, conv_b, wa, ba, wi, bi, lam):
    bsz = proj.shape[0]
    wts = _rg_weights(wa, ba, wi, bi, lam)
    cb = conv_b[None, :]
    s0, s1, s2 = conv_state[:, 0], conv_state[:, 1], conv_state[:, 2]
    full = lambda a: pl.BlockSpec(a.shape, lambda i: (0,) * a.ndim)
    out = jax.ShapeDtypeStruct((bsz, RG_WIDTH), F32)
    y, h1 = pl.pallas_call(
        _rg_sample_body,
        out_shape=(out, out),
        grid=(1,),
        in_specs=[pl.BlockSpec((bsz, RG_WIDTH), lambda i: (0, 3)), pl.BlockSpec((bsz, RG_WIDTH), lambda i: (0, 4)),
                  full(s0), full(s1), full(s2), full(h0), full(conv_w), full(cb)] + [full(w) for w in wts],
        out_specs=(full(out), full(out)),
        compiler_params=_cparams(("arbitrary",)),
        name="rg_sample",
    )(proj, proj, s0, s1, s2, h0, conv_w, cb, *wts)
    conv_new = jnp.stack([s1, s2, proj[:, 4 * RG_WIDTH:5 * RG_WIDTH]], axis=1)
    return y, conv_new, h1


def _nsa_split(proj, b, L, q_norm, k_norm):
    q = proj[:, :NSA_Q_WIDTH]
    kv = proj[:, NSA_Q_WIDTH:NSA_Q_WIDTH + 6 * NSA_KV_WIDTH]
    g = proj[:, NSA_Q_WIDTH + 6 * NSA_KV_WIDTH:NSA_Q_WIDTH + 6 * NSA_KV_WIDTH + 3 * NSA_HEADS]
    q = _rmsnorm(q.reshape(b, L, NSA_KV_HEADS, NSA_GROUP, NSA_HEAD_DIM), q_norm) * (NSA_HEAD_DIM ** -0.5)
    kv = kv.reshape(b, L, 6, NSA_KV_HEADS, NSA_HEAD_DIM)
    k_slc = _rmsnorm(kv[:, :, 2], k_norm[1])
    k_win = _rmsnorm(kv[:, :, 4], k_norm[2])
    rows = jnp.stack([kv[:, :, 0], kv[:, :, 1], k_slc, kv[:, :, 3]], axis=2)
    win = jnp.stack([k_win, kv[:, :, 5]], axis=2)
    gates = jax.nn.sigmoid(g).reshape(b, L, NSA_KV_HEADS, NSA_GROUP, 3)
    return q, rows, win, gates


def _masked_softmax(s, mask):
    s = jnp.where(mask, s.astype(F32), -jnp.inf)
    m = jnp.max(s, axis=-1, keepdims=True)
    e = jnp.exp(s - jnp.where(jnp.isfinite(m), m, 0.0))
    d = jnp.sum(e, axis=-1, keepdims=True)
    return e / jnp.where(d > 0, d, 1.0)


def _compress(r, w1, pe, w2):
    b, T, G, dh = r.shape
    n_chunk = T // CMP_STRIDE
    nc = n_chunk - CMP_R + 1
    ch = r[:, :n_chunk * CMP_STRIDE].reshape(b, n_chunk, CMP_STRIDE, G, dh)
    proj = jnp.einsum('bcsgd,rsdh->bcrgh', ch, w1.reshape(CMP_R, CMP_STRIDE, dh, CMP_HID))
    hid = jnp.einsum('ld,ldh->h', pe, w1)
    for rr in range(CMP_R):
        hid = hid + proj[:, rr:rr + nc, rr]
    return jax.nn.gelu(hid) @ w2


def _nsa_context(rows, cmp_w1, cmp_pe, cmp_w2, k_norm_cmp):
    b, T = rows.shape[:2]
    kc = _rmsnorm(_compress(rows[:, :, 0], cmp_w1[0], cmp_pe[0], cmp_w2[0]), k_norm_cmp)
    vc = _compress(rows[:, :, 1], cmp_w1[1], cmp_pe[1], cmp_w2[1])
    ns = -(-T // SEL_BLOCK)
    sel = jnp.pad(rows[:, :, 2:4], ((0, 0), (0, ns * SEL_BLOCK - T), (0, 0), (0, 0), (0, 0)))
    sel = sel.reshape(b, ns, SEL_BLOCK, 2, NSA_KV_HEADS, NSA_HEAD_DIM).transpose(3, 0, 4, 1, 2, 5)
    return kc, vc, sel[0], sel[1]


def _overlap_matrix(nc, ns):
    i = np.arange(nc)[:, None]
    j = np.arange(ns)[None, :]
    ov = (i * CMP_STRIDE < (j + 1) * SEL_BLOCK) & (i * CMP_STRIDE + CMP_LEN > j * SEL_BLOCK)
    return ov.astype(np.float32)


def _nsa_attend(q, gates, t_pos, kc, vc, ks, vs, kw, vw, w_pos):
    b, Q, G, K, dh = q.shape
    nc = kc.shape[1]
    ns = ks.shape[2]
    tq = t_pos[None, :, None, None, None]
    c_end = jnp.arange(nc) * CMP_STRIDE + CMP_LEN - 1
    p_c = _masked_softmax(jnp.einsum('bqgkd,bngd->bqgkn', q, kc), c_end <= tq)
    o_c = jnp.einsum('bqgkn,bngd->bqgkd', p_c, vc)
    imp = jnp.einsum('bqgkn,ns->bqgs', p_c, jnp.asarray(_overlap_matrix(nc, ns)))
    jj = jnp.arange(ns)[None, :]
    jt = (t_pos // SEL_BLOCK)[:, None]
    valid = jj <= jt
    forced = valid & ((jj == 0) | (jj == jt) | (jj == jt - 1))
    imp = jnp.where(forced[None, :, None], jnp.inf, jnp.where(valid[None, :, None], imp, -jnp.inf))
    _, idx = lax.top_k(imp, min(SEL_TOPN, ns))
    n = idx.shape[-1]
    bi = jnp.arange(b)[:, None, None, None]
    gi = jnp.arange(G)[None, None, :, None]
    k_sel = ks[bi, gi, idx].reshape(b, Q, G, n * SEL_BLOCK, dh)
    v_sel = vs[bi, gi, idx].reshape(b, Q, G, n * SEL_BLOCK, dh)
    kpos = (idx[..., None] * SEL_BLOCK + jnp.arange(SEL_BLOCK)).reshape(b, Q, G, 1, n * SEL_BLOCK)
    p_s = _masked_softmax(jnp.einsum('bqgkd,bqgmd->bqgkm', q, k_sel), kpos <= tq)
    o_s = jnp.einsum('bqgkm,bqgmd->bqgkd', p_s, v_sel)
    m_w = (w_pos <= tq) & (w_pos > tq - WINDOW) & (w_pos >= 0)
    p_w = _masked_softmax(jnp.einsum('bqgkd,bwgd->bqgkw', q, kw), m_w)
    o_w = jnp.einsum('bqgkw,bwgd->bqgkd', p_w, vw)
    o = gates[..., 0:1] * o_c + gates[..., 1:2] * o_s + gates[..., 2:3] * o_w
    return o.reshape(b, Q, G * K * dh)


def _nsa_prompt_core(proj, b, L, q_norm, k_norm, cmp_w1, cmp_pe, cmp_w2):
    q, rows, win, gates = _nsa_split(proj, b, L, q_norm, k_norm)
    kc, vc, ks, vs = _nsa_context(rows, cmp_w1, cmp_pe, cmp_w2, k_norm[0])
    win_pad = jnp.pad(win, ((0, 0), (WINDOW, 0), (0, 0), (0, 0), (0, 0)))

    def block(i):
        s = i * NSA_Q_BLOCK
        qb = lax.dynamic_slice_in_dim(q, s, NSA_Q_BLOCK, axis=1)
        gb = lax.dynamic_slice_in_dim(gates, s, NSA_Q_BLOCK, axis=1)
        wb = lax.dynamic_slice_in_dim(win_pad, s, WINDOW + NSA_Q_BLOCK, axis=1)
        t_pos = s + jnp.arange(NSA_Q_BLOCK)
        w_pos = s - WINDOW + jnp.arange(WINDOW + NSA_Q_BLOCK)
        return _nsa_attend(qb, gb, t_pos, kc, vc, ks, vs, wb[:, :, 0], wb[:, :, 1], w_pos)

    o = lax.map(block, jnp.arange(L // NSA_Q_BLOCK))
    o = jnp.moveaxis(o, 0, 1).reshape(b, L, NSA_Q_WIDTH)
    return o, rows, win[:, L - min(WINDOW, L):]


def _nsa_sample_core(proj, b, L, cache, page_table, win_buf, q_norm, k_norm, cmp_w1, cmp_pe, cmp_w2):
    q, rows, win, gates = _nsa_split(proj, b, L, q_norm, k_norm)
    past = page_table.shape[1] * cache.shape[1]
    past_rows = cache[page_table].reshape(b, past, 4, NSA_KV_HEADS, NSA_HEAD_DIM)
    kc, vc, ks, vs = _nsa_context(jnp.concatenate([past_rows, rows], axis=1),
                                  cmp_w1, cmp_pe, cmp_w2, k_norm[0])
    wb_len = win_buf.shape[1]
    wk = jnp.concatenate([win_buf, win], axis=1)
    t_pos = past + jnp.arange(L)
    w_pos = past - wb_len + jnp.arange(wb_len + L)
    o = _nsa_attend(q, gates, t_pos, kc, vc, ks, vs, wk[:, :, 0], wk[:, :, 1], w_pos)
    return o, rows, wk[:, L:]


NSA_TQ = 128
NSA_TK_SLC = 1024
NSA_NS_PAD = 64
SEL_BIAS = -16384.0
MASK_VALUE = -1e30


def _split_bf16(x):
    hi = x.astype(BF16)
    lo = (x - hi.astype(F32)).astype(BF16)
    return hi, lo


def _seg_rms_scale(x, seg, seg_t):
    hi, lo = _split_bf16(x * x)
    ss = jnp.dot(hi, seg, preferred_element_type=F32) + jnp.dot(lo, seg, preferred_element_type=F32)
    r = lax.rsqrt(ss * (1.0 / NSA_HEAD_DIM) + NORM_EPS)
    rh, rl = _split_bf16(r)
    return jnp.dot(rh, seg_t, preferred_element_type=F32) + jnp.dot(rl, seg_t, preferred_element_type=F32)


def _nsa_prep_body(seq_len, p_ref, wq_ref, wks_ref, wkw_ref, segq_ref, segqt_ref, segk_ref, segkt_ref,
                   q_ref, rows_ref, win_ref, kaug_ref, vslc_ref, kwin_ref, vwin_ref, gate_ref):
    tm = p_ref.shape[0]
    dh = NSA_HEAD_DIM
    q = p_ref[:, 0:NSA_Q_WIDTH]
    qn = q * _seg_rms_scale(q, segq_ref[...], segqt_ref[...]) * wq_ref[...]
    kv = [p_ref[:, NSA_Q_WIDTH + NSA_KV_WIDTH * j:NSA_Q_WIDTH + NSA_KV_WIDTH * (j + 1)] for j in range(6)]
    ksl = kv[2] * _seg_rms_scale(kv[2], segk_ref[...], segkt_ref[...]) * wks_ref[...]
    kwn = kv[4] * _seg_rms_scale(kv[4], segk_ref[...], segkt_ref[...]) * wkw_ref[...]
    rows_ref[...] = jnp.concatenate([kv[0], kv[1], ksl, kv[3]], axis=1)
    win_ref[...] = jnp.concatenate([kwn, kv[5]], axis=1)
    gates = jax.nn.sigmoid(p_ref[:, NSA_Q_WIDTH + 6 * NSA_KV_WIDTH:NSA_Q_WIDTH + 6 * NSA_KV_WIDTH + V7X_LANES])
    t0 = (pl.program_id(0) * tm) % seq_len
    tpos = t0 + lax.broadcasted_iota(jnp.int32, (tm, NSA_NS_PAD), 0)
    blk = lax.broadcasted_iota(jnp.int32, (tm, NSA_NS_PAD), 1)
    onehot = jnp.where(blk == lax.shift_right_logical(tpos, 6), 1.0, 0.0).astype(BF16)
    for g in range(NSA_KV_HEADS):
        sl = slice(g * dh, (g + 1) * dh)
        kaug_ref[0, g] = jnp.concatenate([ksl[:, sl].astype(BF16), onehot], axis=1)
        vslc_ref[0, g] = kv[3][:, sl].astype(BF16)
        kwin_ref[0, g] = kwn[:, sl].astype(BF16)
        vwin_ref[0, g] = kv[5][:, sl].astype(BF16)
        gate_ref[0, g] = gates if g == 0 else pltpu.roll(gates, V7X_LANES - 3 * NSA_GROUP * g, 1)
        for k in range(NSA_GROUP):
            c0 = (g * NSA_GROUP + k) * dh
            q_ref[0, g, k] = qn[:, c0:c0 + dh].astype(BF16)


def _head_segments(width):
    lane = np.arange(width)[:, None] // NSA_HEAD_DIM
    seg = (lane == np.arange(V7X_LANES)[None, :]).astype(np.float32)
    return jnp.asarray(seg, BF16), jnp.asarray(seg.T, BF16)


def nsa_prep(proj, b, seq_len, q_norm, k_norm, tm):
    t = proj.shape[0]
    G, K, dh = NSA_KV_HEADS, NSA_GROUP, NSA_HEAD_DIM
    wq = (jnp.tile(q_norm, NSA_HEADS) * (dh ** -0.5))[None, :]
    wks = jnp.tile(k_norm[1], G)[None, :]
    wkw = jnp.tile(k_norm[2], G)[None, :]
    segq, segqt = _head_segments(NSA_Q_WIDTH)
    segk, segkt = _head_segments(NSA_KV_WIDTH)
    tiles_per_seq = seq_len // tm
    bi = lambda i: i // tiles_per_seq
    ti = lambda i: i % tiles_per_seq
    full = lambda a: pl.BlockSpec(a.shape, lambda i: (0,) * a.ndim)
    out_shape = (jax.ShapeDtypeStruct((b, G, K, seq_len, dh), BF16),
                 jax.ShapeDtypeStruct((t, 4 * NSA_KV_WIDTH), F32),
                 jax.ShapeDtypeStruct((t, 2 * NSA_KV_WIDTH), F32),
                 jax.ShapeDtypeStruct((b, G, seq_len, 2 * dh), BF16),
                 jax.ShapeDtypeStruct((b, G, seq_len, dh), BF16),
                 jax.ShapeDtypeStruct((b, G, seq_len, dh), BF16),
                 jax.ShapeDtypeStruct((b, G, seq_len, dh), BF16),
                 jax.ShapeDtypeStruct((b, G, seq_len, V7X_LANES), F32))
    per_g = lambda w: pl.BlockSpec((1, G, tm, w), lambda i: (bi(i), 0, ti(i), 0))
    out_specs = (pl.BlockSpec((1, G, K, tm, dh), lambda i: (bi(i), 0, 0, ti(i), 0)),
                 pl.BlockSpec((tm, 4 * NSA_KV_WIDTH), lambda i: (i, 0)),
                 pl.BlockSpec((tm, 2 * NSA_KV_WIDTH), lambda i: (i, 0)),
                 per_g(2 * dh), per_g(dh), per_g(dh), per_g(dh), per_g(V7X_LANES))
    return pl.pallas_call(
        functools.partial(_nsa_prep_body, seq_len),
        out_shape=out_shape,
        grid=(t // tm,),
        in_specs=[pl.BlockSpec((tm, proj.shape[1]), lambda i: (i, 0)),
                  full(wq), full(wks), full(wkw), full(segq), full(segqt), full(segk), full(segkt)],
        out_specs=out_specs,
        compiler_params=_cparams(("arbitrary",), V7X_VMEM_LIMIT_BYTES),
        name="nsa_prep",
    )(proj, wq, wks, wkw, segq, segqt, segk, segkt)


def _gelu_tanh(x):
    return 0.5 * x * (1.0 + jnp.tanh(math.sqrt(2.0 / math.pi) * (x + 0.044715 * (x * x * x))))


def _nsa_compress_body(n_chunk, x0_ref, x1_ref, x2_ref, x3_ref, wk_ref, wv_ref, pe_ref, w1f_ref, w2_ref, kn_ref,
                       kc_ref, vc_ref, pk_scr, pv_scr):
    s = pl.program_id(1)

    @pl.when(s == 0)
    def _():
        pk_scr[...] = jnp.zeros_like(pk_scr)
        pv_scr[...] = jnp.zeros_like(pv_scr)

    xs = [r[pl.ds(s, n_chunk, stride=CMP_STRIDE), :].astype(BF16) for r in (x0_ref, x1_ref, x2_ref, x3_ref)]
    pk_scr[...] += jnp.dot(jnp.concatenate(xs[0:2], axis=1), wk_ref[0], preferred_element_type=F32)
    pv_scr[...] += jnp.dot(jnp.concatenate(xs[2:4], axis=1), wv_ref[0], preferred_element_type=F32)

    @pl.when(s == CMP_STRIDE - 1)
    def _():
        for kv, p_scr, o_ref in ((0, pk_scr, kc_ref), (1, pv_scr, vc_ref)):
            p = p_scr[...]
            p_next = pltpu.roll(p, n_chunk - 1, 0)
            pe_h = jnp.dot(pe_ref[kv], w1f_ref[kv], preferred_element_type=F32)[0:1, :]
            for g in range(NSA_KV_HEADS):
                c0 = g * 2 * CMP_HID
                hid = pe_h + p[:, c0:c0 + CMP_HID] + p_next[:, c0 + CMP_HID:c0 + 2 * CMP_HID]
                y = jnp.dot(_gelu_tanh(hid).astype(BF16), w2_ref[kv], preferred_element_type=F32)
                if kv == 0:
                    y = y * lax.rsqrt(jnp.mean(y * y, axis=-1, keepdims=True) + NORM_EPS) * kn_ref[...]
                o_ref[0, g] = y.astype(BF16)


def nsa_compress(rows, b, seq_len, cmp_w1, cmp_pe, cmp_w2, k_norm_cmp):
    G, dh = NSA_KV_HEADS, NSA_HEAD_DIM
    n_chunk = seq_len // CMP_STRIDE
    w1 = cmp_w1.reshape(2, CMP_R, CMP_STRIDE, dh, CMP_HID)
    eye = jnp.eye(G, dtype=F32)
    wbd = jnp.einsum('vrsdh,gq->vsgdqrh', w1, eye).reshape(2, CMP_STRIDE, G * dh, G * CMP_R * CMP_HID).astype(BF16)
    pe = jnp.broadcast_to(cmp_pe.reshape(2, 1, CMP_LEN * dh), (2, 8, CMP_LEN * dh)).astype(BF16)
    w1f = cmp_w1.reshape(2, CMP_LEN * dh, CMP_HID).astype(BF16)
    w2 = cmp_w2.astype(BF16)
    kn = k_norm_cmp[None, :]
    full = lambda a: pl.BlockSpec(a.shape, lambda bi, s: (0,) * a.ndim)
    return pl.pallas_call(
        functools.partial(_nsa_compress_body, n_chunk),
        out_shape=(jax.ShapeDtypeStruct((b, G, n_chunk, dh), BF16),
                   jax.ShapeDtypeStruct((b, G, n_chunk, dh), BF16)),
        grid=(b, CMP_STRIDE),
        in_specs=[pl.BlockSpec((seq_len, V7X_LANES), lambda bi, s: (bi, 0)),
                  pl.BlockSpec((seq_len, V7X_LANES), lambda bi, s: (bi, 1)),
                  pl.BlockSpec((seq_len, V7X_LANES), lambda bi, s: (bi, 2)),
                  pl.BlockSpec((seq_len, V7X_LANES), lambda bi, s: (bi, 3)),
                  pl.BlockSpec((1, G * dh, G * CMP_R * CMP_HID), lambda bi, s: (s, 0, 0)),
                  pl.BlockSpec((1, G * dh, G * CMP_R * CMP_HID), lambda bi, s: (s, 0, 0)),
                  full(pe), full(w1f), full(w2), full(kn)],
        out_specs=(pl.BlockSpec((1, G, n_chunk, dh), lambda bi, s: (bi, 0, 0, 0)),
                   pl.BlockSpec((1, G, n_chunk, dh), lambda bi, s: (bi, 0, 0, 0))),
        scratch_shapes=[pltpu.VMEM((n_chunk, G * CMP_R * CMP_HID), F32),
                        pltpu.VMEM((n_chunk, G * CMP_R * CMP_HID), F32)],
        compiler_params=_cparams(("arbitrary", "arbitrary"), V7X_VMEM_LIMIT_BYTES),
        name="nsa_compress",
    )(rows, rows, rows, rows, wbd[0], wbd[1], pe, w1f, w2, kn)


_NT_DIMS = (((1,), (1,)), ((), ()))


def _flash_branch(q2, k_ref, v_ref, n_tiles, tk, last_mask_fn):
    rows = q2.shape[0]

    def step(j, carry, mask_fn):
        m, l, acc = carry
        k0 = pl.multiple_of(j * tk, tk)
        k = k_ref[0, 0, pl.ds(k0, tk), :]
        v = v_ref[0, 0, pl.ds(k0, tk), :]
        s = lax.dot_general(q2, k, _NT_DIMS, preferred_element_type=F32)
        if mask_fn is not None:
            s = jnp.where(mask_fn(k0), s, MASK_VALUE)
        m_new = jnp.maximum(m, jnp.max(s, axis=-1, keepdims=True))
        alpha = jnp.exp(m - m_new)
        p = jnp.exp(s - m_new)
        l = alpha * l + jnp.sum(p, axis=-1, keepdims=True)
        acc = alpha * acc + jnp.dot(p.astype(BF16), v, preferred_element_type=F32)
        return m_new, l, acc

    init = (jnp.full((rows, 1), MASK_VALUE, F32), jnp.zeros((rows, 1), F32),
            jnp.zeros((rows, NSA_HEAD_DIM), F32))
    carry = lax.fori_loop(0, n_tiles - 1, lambda j, c: step(j, c, None), init)
    _, l, acc = step(n_tiles - 1, carry, last_mask_fn)
    return acc / l


def _nsa_attn_body(n_cmp, q_ref, kc_ref, vc_ref, kaug_ref, vslc_ref, kwin_ref, vwin_ref, gate_ref, ovt_ref, o_ref):
    tq = NSA_TQ
    rows = NSA_GROUP * tq
    q0 = pl.program_id(2) * tq
    q2 = q_ref[0, 0].reshape(rows, NSA_HEAD_DIM)
    row_t = q0 + jnp.bitwise_and(lax.broadcasted_iota(jnp.int32, (rows, 1), 0), tq - 1)

    n_pad = kc_ref.shape[2]
    s = lax.dot_general(q2, kc_ref[0, 0], _NT_DIMS, preferred_element_type=F32)
    n_idx = lax.broadcasted_iota(jnp.int32, (1, n_pad), 1)
    cmask = jnp.logical_and(n_idx * CMP_STRIDE + (CMP_LEN - 1) <= row_t, n_idx < n_cmp)
    s = jnp.where(cmask, s, MASK_VALUE)
    m = jnp.max(s, axis=-1, keepdims=True)
    e = jnp.where(cmask, jnp.exp(s - m), 0.0)
    den = jnp.sum(e, axis=-1, keepdims=True)
    p_c = e / jnp.where(den > 0.0, den, 1.0)
    o_c = jnp.dot(p_c.astype(BF16), vc_ref[0, 0], preferred_element_type=F32)

    p_sum = (p_c[0:tq] + p_c[tq:2 * tq]) + (p_c[2 * tq:3 * tq] + p_c[3 * tq:4 * tq])
    ph, plo = _split_bf16(p_sum)
    ovt = ovt_ref[...]
    imp = (lax.dot_general(ovt, ph, _NT_DIMS, preferred_element_type=F32)
           + lax.dot_general(ovt, plo, _NT_DIMS, preferred_element_type=F32))
    blk = lax.broadcasted_iota(jnp.int32, (NSA_NS_PAD, tq), 0)
    jt = lax.shift_right_logical(q0 + lax.broadcasted_iota(jnp.int32, (NSA_NS_PAD, tq), 1), 6)
    valid = blk <= jt
    forced = jnp.logical_and(valid, jnp.logical_or(blk == 0, jnp.logical_or(blk == jt, blk == jt - 1)))
    eff = jnp.where(forced, jnp.inf, jnp.where(valid, imp, -jnp.inf))
    rank = jnp.zeros((NSA_NS_PAD, tq), jnp.int32)
    for j in range(NSA_NS_PAD):
        other = eff[j:j + 1, :]
        ahead = jnp.logical_or(other > eff, jnp.logical_and(other == eff, blk > j))
        rank = rank + ahead.astype(jnp.int32)
    sel = jnp.logical_and(valid, rank < SEL_TOPN)
    sel_bias = jnp.where(sel, 0.0, SEL_BIAS).T.astype(BF16)

    q_aug = jnp.concatenate([q2, jnp.concatenate([sel_bias] * NSA_GROUP, axis=0)], axis=1)
    hi = (q0 + tq - 1) // NSA_TK_SLC + 1

    def slc_mask(k0):
        kpos = k0 + lax.broadcasted_iota(jnp.int32, (1, NSA_TK_SLC), 1)
        return kpos <= row_t

    o_s = _flash_branch(q_aug, kaug_ref, vslc_ref, hi, NSA_TK_SLC, slc_mask)

    span = WINDOW + tq
    w0 = pl.multiple_of(jnp.maximum(q0 - WINDOW, 0), tq)
    kw = kwin_ref[0, 0, pl.ds(w0, span), :]
    vw = vwin_ref[0, 0, pl.ds(w0, span), :]
    s_w = lax.dot_general(q2, kw, _NT_DIMS, preferred_element_type=F32)
    kpos = w0 + lax.broadcasted_iota(jnp.int32, (1, span), 1)
    wmask = jnp.logical_and(kpos <= row_t, kpos > row_t - WINDOW)
    s_w = jnp.where(wmask, s_w, MASK_VALUE)
    p_w = jnp.exp(s_w - jnp.max(s_w, axis=-1, keepdims=True))
    o_w = (jnp.dot(p_w.astype(BF16), vw, preferred_element_type=F32)
           / jnp.sum(p_w, axis=-1, keepdims=True))

    gt = gate_ref[0, 0]
    outs = []
    for k in range(NSA_GROUP):
        r = slice(k * tq, (k + 1) * tq)
        outs.append(gt[:, 3 * k:3 * k + 1] * o_c[r] + gt[:, 3 * k + 1:3 * k + 2] * o_s[r]
                    + gt[:, 3 * k + 2:3 * k + 3] * o_w[r])
    o_ref[...] = jnp.concatenate(outs, axis=1)


def nsa_attention(q, kc, vc, kaug, vslc, kwin, vwin, gates, b, seq_len):
    G, K, dh = NSA_KV_HEADS, NSA_GROUP, NSA_HEAD_DIM
    tq = NSA_TQ
    nq = seq_len // tq
    n_chunk = kc.shape[2]
    n_cmp = n_chunk - CMP_R + 1
    ns = seq_len // SEL_BLOCK
    ovt = np.zeros((NSA_NS_PAD, n_chunk), np.float32)
    ovt[:ns, :n_cmp] = _overlap_matrix(n_cmp, ns).T
    ovt = jnp.asarray(ovt, BF16)
    seq_spec = lambda w: pl.BlockSpec((1, 1, seq_len, w), lambda bi, g, qi: (bi, g, 0, 0))
    return pl.pallas_call(
        functools.partial(_nsa_attn_body, n_cmp),
        out_shape=jax.ShapeDtypeStruct((b * seq_len, NSA_Q_WIDTH), F32),
        grid=(b, G, nq),
        in_specs=[pl.BlockSpec((1, 1, K, tq, dh), lambda bi, g, qi: (bi, g, 0, qi, 0)),
                  pl.BlockSpec((1, 1, n_chunk, dh), lambda bi, g, qi: (bi, g, 0, 0)),
                  pl.BlockSpec((1, 1, n_chunk, dh), lambda bi, g, qi: (bi, g, 0, 0)),
                  seq_spec(2 * dh), seq_spec(dh), seq_spec(dh), seq_spec(dh),
                  pl.BlockSpec((1, 1, tq, V7X_LANES), lambda bi, g, qi: (bi, g, qi, 0)),
                  pl.BlockSpec(ovt.shape, lambda bi, g, qi: (0, 0))],
        out_specs=pl.BlockSpec((tq, K * dh), lambda bi, g, qi: (bi * nq + qi, g)),
        compiler_params=_cparams(("arbitrary", "arbitrary", "arbitrary"), V7X_VMEM_LIMIT_BYTES),
        name="nsa_attention",
    )(q, kc, vc, kaug, vslc, kwin, vwin, gates, ovt)


def nsa_prompt_pallas(proj, b, seq_len, q_norm, k_norm, cmp_w1, cmp_pe, cmp_w2):
    q, rows, win, kaug, vslc, kwin, vwin, gates = nsa_prep(proj, b, seq_len, q_norm, k_norm, ROW_TILE)
    kc, vc = nsa_compress(rows, b, seq_len, cmp_w1, cmp_pe, cmp_w2, k_norm[0])
    o = nsa_attention(q, kc, vc, kaug, vslc, kwin, vwin, gates, b, seq_len)
    rows_out = rows.reshape(b, seq_len, 4, NSA_KV_HEADS, NSA_HEAD_DIM)
    wlen = min(WINDOW, seq_len)
    win_out = win.reshape(b, seq_len, 2, NSA_KV_HEADS, NSA_HEAD_DIM)[:, seq_len - wlen:]
    return o, rows_out, win_out


def _diag_heads(o_full):
    g_row = lax.shift_right_logical(lax.broadcasted_iota(jnp.int32, (NSA_HEADS, 1), 0), 2)
    out = jnp.zeros((NSA_HEADS, NSA_HEAD_DIM), F32)
    for g in range(NSA_KV_HEADS):
        out = out + jnp.where(g_row == g, o_full[:, g * NSA_HEAD_DIM:(g + 1) * NSA_HEAD_DIM], 0.0)
    return out


def _nsa_sample_body(n_pages, page_rows, pt_ref,
                     cache_hbm, qbd_ref, rown_ref, winn_ref, winbuf_ref, wc_ref, pe_ref, w1f_ref, w2_ref, kn_ref,
                     gsum_ref, ovs_ref, rep_ref, eblk_ref,
                     oc_ref, os_ref, ow_ref,
                     cmp_buf, slc_buf, sem):
    i = pl.program_id(0)
    nb = pl.num_programs(0)
    slot = i % 2
    past = n_pages * page_rows
    n_chunk = past // CMP_STRIDE
    n_cmp = n_chunk - CMP_R + 1
    t_pos = past
    kvw = NSA_KV_WIDTH

    def page_copies(bi, sl):
        copies = []
        for p in range(n_pages):
            pg = pt_ref[bi, p]
            rows = pl.ds(p * page_rows, page_rows)
            for j in range(4):
                copies.append(pltpu.make_async_copy(cache_hbm.at[pg, :, pl.ds(j * V7X_LANES, V7X_LANES)],
                                                    cmp_buf.at[sl, j, rows, :], sem.at[sl]))
            copies.append(pltpu.make_async_copy(cache_hbm.at[pg, :, pl.ds(2 * kvw, 2 * kvw)],
                                                slc_buf.at[sl, rows, :], sem.at[sl]))
        return copies

    @pl.when(i == 0)
    def _():
        for c in page_copies(0, 0):
            c.start()

    @pl.when(i + 1 < nb)
    def _():
        for c in page_copies(i + 1, 1 - slot):
            c.start()

    for c in page_copies(i, slot):
        c.wait()

    parts = []
    for j in range(4):
        acc = None
        for s in range(CMP_STRIDE):
            xs = cmp_buf[slot, j, pl.ds(s, n_chunk, stride=CMP_STRIDE), :].astype(BF16)
            d = jnp.dot(xs, wc_ref[j // 2, s], preferred_element_type=F32)
            acc = d if acc is None else acc + d
        parts.append(acc)
    slabs = []
    for kv in range(2):
        p = jnp.concatenate(parts[2 * kv:2 * kv + 2], axis=1)
        p_next = pltpu.roll(p, n_chunk - 1, 0)
        pe_h = jnp.dot(pe_ref[kv], w1f_ref[kv], preferred_element_type=F32)[0:1, :]
        ys = []
        for g in range(NSA_KV_HEADS):
            c0 = g * 2 * CMP_HID
            hid = pe_h + p[:, c0:c0 + CMP_HID] + p_next[:, c0 + CMP_HID:c0 + 2 * CMP_HID]
            y = jnp.dot(_gelu_tanh(hid).astype(BF16), w2_ref[kv], preferred_element_type=F32)
            if kv == 0:
                y = y * lax.rsqrt(jnp.mean(y * y, axis=-1, keepdims=True) + NORM_EPS) * kn_ref[...]
            ys.append(y)
        slabs.append(jnp.concatenate(ys, axis=1).astype(BF16))
    kc, vc = slabs

    qbd = qbd_ref[0]
    qf = qbd.astype(F32)

    s_c = lax.dot_general(qbd, kc, _NT_DIMS, preferred_element_type=F32)
    n_idx = lax.broadcasted_iota(jnp.int32, (1, n_chunk), 1)
    cmask = jnp.logical_and(n_idx * CMP_STRIDE + (CMP_LEN - 1) <= t_pos, n_idx < n_cmp)
    s_c = jnp.where(cmask, s_c, MASK_VALUE)
    m = jnp.max(s_c, axis=-1, keepdims=True)
    e = jnp.where(cmask, jnp.exp(s_c - m), 0.0)
    den = jnp.sum(e, axis=-1, keepdims=True)
    p_c = e / jnp.where(den > 0.0, den, 1.0)
    oc_ref[0] = _diag_heads(jnp.dot(p_c.astype(BF16), vc, preferred_element_type=F32))

    gsum = gsum_ref[...]
    ph, plo = _split_bf16(p_c)
    p_sum = jnp.dot(gsum, ph, preferred_element_type=F32) + jnp.dot(gsum, plo, preferred_element_type=F32)
    sh, slo = _split_bf16(p_sum)
    ovs = ovs_ref[...]
    imp = jnp.dot(sh, ovs, preferred_element_type=F32) + jnp.dot(slo, ovs, preferred_element_type=F32)
    blk = lax.broadcasted_iota(jnp.int32, imp.shape, 1)
    jt = t_pos // SEL_BLOCK
    valid = blk <= jt
    forced = jnp.logical_and(valid, jnp.logical_or(blk == 0, jnp.logical_or(blk == jt, blk == jt - 1)))
    eff = jnp.where(forced, jnp.inf, jnp.where(valid, imp, -jnp.inf))
    rank = jnp.zeros(imp.shape, jnp.int32)
    for j in range(jt + 1):
        other = eff[:, j:j + 1]
        ahead = jnp.logical_or(other > eff, jnp.logical_and(other == eff, blk > j))
        rank = rank + ahead.astype(jnp.int32)
    sel = jnp.logical_and(valid, rank < SEL_TOPN)
    sel_bias = jnp.where(sel, 0.0, SEL_BIAS).astype(BF16)
    bias_h = jnp.dot(rep_ref[...], sel_bias, preferred_element_type=F32).astype(BF16)
    bias_keys = jnp.dot(bias_h, eblk_ref[...], preferred_element_type=F32)

    rn = rown_ref[0]
    ks = slc_buf[slot, :, 0:kvw].astype(BF16)
    vs = slc_buf[slot, :, kvw:2 * kvw].astype(BF16)
    s_s = lax.dot_general(qbd, ks, _NT_DIMS, preferred_element_type=F32) + bias_keys
    ks_new = rn[:, 2 * kvw:3 * kvw].astype(BF16).astype(F32)
    vs_new = rn[:, 3 * kvw:4 * kvw].astype(BF16).astype(F32)
    s_new = jnp.sum(qf * ks_new, axis=-1, keepdims=True)
    m = jnp.maximum(jnp.max(s_s, axis=-1, keepdims=True), s_new)
    p = jnp.exp(s_s - m)
    p_new = jnp.exp(s_new - m)
    den = jnp.sum(p, axis=-1, keepdims=True) + p_new
    o_full = jnp.dot(p.astype(BF16), vs, preferred_element_type=F32) + p_new.astype(BF16).astype(F32) * vs_new
    os_ref[0] = _diag_heads(o_full) / den

    wb = winbuf_ref[0]
    wn = winn_ref[0]
    wb_len = wb.shape[0]
    kw = wb[:, 0:kvw].astype(BF16)
    vw = wb[:, kvw:2 * kvw].astype(BF16)
    s_w = lax.dot_general(qbd, kw, _NT_DIMS, preferred_element_type=F32)
    w_idx = lax.broadcasted_iota(jnp.int32, (1, wb_len), 1)
    w_pos = t_pos - wb_len + w_idx
    wmask = jnp.logical_and(w_pos > t_pos - WINDOW, w_pos >= 0)
    s_w = jnp.where(wmask, s_w, MASK_VALUE)
    kw_new = wn[:, 0:kvw].astype(BF16).astype(F32)
    vw_new = wn[:, kvw:2 * kvw].astype(BF16).astype(F32)
    s_new = jnp.sum(qf * kw_new, axis=-1, keepdims=True)
    m = jnp.maximum(jnp.max(s_w, axis=-1, keepdims=True), s_new)
    p = jnp.where(wmask, jnp.exp(s_w - m), 0.0)
    p_new = jnp.exp(s_new - m)
    den = jnp.sum(p, axis=-1, keepdims=True) + p_new
    o_full = jnp.dot(p.astype(BF16), vw, preferred_element_type=F32) + p_new.astype(BF16).astype(F32) * vw_new
    ow_ref[0] = _diag_heads(o_full) / den


def nsa_sample_attention(cache, page_table, win_buf, q, rows_new, win_new, cmp_w1, cmp_pe, cmp_w2, k_norm_cmp):
    G, K, dh = NSA_KV_HEADS, NSA_GROUP, NSA_HEAD_DIM
    bsz, n_pages = page_table.shape
    n_phys, page_rows = cache.shape[0], cache.shape[1]
    past = n_pages * page_rows
    n_chunk = past // CMP_STRIDE
    n_cmp = n_chunk - CMP_R + 1
    ns = -(-(past + 1) // SEL_BLOCK)
    cache3 = cache.reshape(n_phys, page_rows, 4 * G * dh)
    wb_len = win_buf.shape[1]
    win3 = win_buf.reshape(bsz, wb_len, 2 * G * dh)
    qh = jnp.transpose(q[0], (2, 0, 1, 3)).astype(F32)
    qbd = jnp.einsum('bgkd,gq->bgkqd', qh, jnp.eye(G, dtype=F32)).reshape(bsz, G * K, G * dh).astype(BF16)
    w1 = cmp_w1.reshape(2, CMP_R, CMP_STRIDE, dh, CMP_HID)
    wc = jnp.einsum('vrsdh,pq->vspdqrh', w1, jnp.eye(2, dtype=F32)).reshape(
        2, CMP_STRIDE, 2 * dh, 2 * CMP_R * CMP_HID).astype(BF16)
    pe = jnp.broadcast_to(cmp_pe.reshape(2, 1, CMP_LEN * dh), (2, 8, CMP_LEN * dh)).astype(BF16)
    w1f = cmp_w1.reshape(2, CMP_LEN * dh, CMP_HID).astype(BF16)
    w2 = cmp_w2.astype(BF16)
    kn = k_norm_cmp[None, :]
    gsum = np.zeros((8, G * K), np.float32)
    gsum[np.arange(G * K) // K, np.arange(G * K)] = 1.0
    ovs = np.zeros((n_chunk, NSA_NS_PAD), np.float32)
    ovs[:n_cmp, :ns] = _overlap_matrix(n_cmp, ns)
    eblk = (np.arange(NSA_NS_PAD)[:, None] == (np.arange(past)[None, :] // SEL_BLOCK)).astype(np.float32)
    gsum, ovs, eblk = jnp.asarray(gsum, BF16), jnp.asarray(ovs, BF16), jnp.asarray(eblk, BF16)
    rep = gsum.T
    full = lambda a: pl.BlockSpec(a.shape, lambda i, pt: (0,) * a.ndim)
    grid_spec = pltpu.PrefetchScalarGridSpec(
        num_scalar_prefetch=1,
        grid=(bsz,),
        in_specs=[pl.BlockSpec(memory_space=pl.ANY),
                  pl.BlockSpec((1, G * K, G * dh), lambda i, pt: (i, 0, 0)),
                  pl.BlockSpec((1, 1, 4 * G * dh), lambda i, pt: (i, 0, 0)),
                  pl.BlockSpec((1, 1, 2 * G * dh), lambda i, pt: (i, 0, 0)),
                  pl.BlockSpec((1, wb_len, 2 * G * dh), lambda i, pt: (i, 0, 0)),
                  full(wc), full(pe), full(w1f), full(w2), full(kn), full(gsum), full(ovs), full(rep), full(eblk)],
        out_specs=[pl.BlockSpec((1, G * K, dh), lambda i, pt: (i, 0, 0))] * 3,
        scratch_shapes=[pltpu.VMEM((2, 4, past, V7X_LANES), F32),
                        pltpu.VMEM((2, past, 2 * G * dh), F32),
                        pltpu.SemaphoreType.DMA((2,))],
    )
    out = jax.ShapeDtypeStruct((bsz, G * K, dh), F32)
    o_c, o_s, o_w = pl.pallas_call(
        functools.partial(_nsa_sample_body, n_pages, page_rows),
        out_shape=(out, out, out),
        grid_spec=grid_spec,
        compiler_params=_cparams(("arbitrary",), V7X_VMEM_LIMIT_BYTES),
        name="nsa_sample_attention",
    )(page_table, cache3, qbd, rows_new.reshape(bsz, 1, -1), win_new.reshape(bsz, 1, -1), win3,
      wc, pe, w1f, w2, kn, gsum, ovs, rep, eblk)
    return (o_c.reshape(bsz, -1), o_s.reshape(bsz, -1), o_w.reshape(bsz, -1))


def _mm_residual_gated_body(oc_ref, os_ref, ow_ref, gc_ref, gs_ref, gw_ref, w_ref, x_ref, g_ref, o_ref):
    a = gc_ref[...] * oc_ref[...] + gs_ref[...] * os_ref[...] + gw_ref[...] * ow_ref[...]
    acc = jnp.dot(a.astype(BF16), w_ref[...], preferred_element_type=F32)
    o_ref[...] = x_ref[...] + g_ref[0] * acc


def mm_residual_gated(branches, gates, w_bf, x, gate_mod):
    t, d = x.shape
    full2 = lambda a: pl.BlockSpec(a.shape, lambda i: (0,) * a.ndim)
    args = (*branches, *gates, w_bf, x, gate_mod)
    return pl.pallas_call(
        _mm_residual_gated_body,
        out_shape=jax.ShapeDtypeStruct((t, d), F32),
        grid=(1,),
        in_specs=[full2(a) for a in args],
        out_specs=pl.BlockSpec((t, d), lambda i: (0, 0)),
        compiler_params=_cparams(("arbitrary",)),
        name="mm_residual_gated",
    )(*args)


def nsa_sample_pallas(proj, bsz, cache, page_table, win_buf, q_norm, k_norm, cmp_w1, cmp_pe, cmp_w2):
    G, K, dh = NSA_KV_HEADS, NSA_GROUP, NSA_HEAD_DIM
    q, rows, win, _, _, _, _, gates = nsa_prep(proj, 1, bsz, q_norm, k_norm, bsz)
    branches = nsa_sample_attention(cache, page_table, win_buf, q, rows, win, cmp_w1, cmp_pe, cmp_w2, k_norm[0])
    g3 = jnp.transpose(gates[0, :, :, :3 * K], (1, 0, 2)).reshape(bsz, G, K, 3)
    gexp = [jnp.repeat(g3[..., br].reshape(bsz, G * K), dh, axis=1) for br in range(3)]
    rows_out = rows.reshape(bsz, 1, 4, G, dh)
    win_out = jnp.concatenate([win_buf[:, 1:], win.reshape(bsz, 1, 2, G, dh)], axis=1)
    return branches, gexp, rows_out, win_out


def _pad_cols(w, n):
    return jnp.pad(w, ((0, 0), (0, n - w.shape[1])))


def kernel(x_prompt, x_sample, cache_nsa_kv, state_nsa_win, state_ssd_conv, state_ssd, state_rg_conv, state_rg, page_table, c_prompt, c_sample, ada_w, ada_b, norm_mix, norm_ffn, rec_w_in, ssd_conv_w, ssd_conv_b, ssd_dt_bias, ssd_a_log, ssd_d, ssd_norm_w, rg_conv_w, rg_conv_b, rg_wa, rg_ba, rg_wi, rg_bi, rg_lambda, rec_w_out, nsa_w_in, nsa_q_norm, nsa_k_norm, cmp_w1, cmp_pe, cmp_w2, nsa_w_out, router_w, router_b, moe_w1, moe_b1, moe_w2, moe_b2):
    bp, L, d = x_prompt.shape
    bs = x_sample.shape[0]
    depth = ada_w.shape[0]
    tp = bp * L
    xp = x_prompt.reshape(tp, d)
    xs = x_sample.reshape(bs, d)

    n_c = bp + bs
    n_c_pad = -(-n_c // 8) * 8
    c_all = jnp.pad(jnp.concatenate([c_prompt, c_sample], axis=0), ((0, n_c_pad - n_c), (0, 0)))
    ada_w_cat = jnp.concatenate([ada_w[i] for i in range(depth)], axis=1).astype(BF16)
    ada_b_cat = jnp.concatenate([ada_b[i] for i in range(depth)], axis=0)[None, :]
    mod_all = adaln_mod(c_all, ada_w_cat, ada_b_cat)

    outs = {k: [] for k in ('rows_p', 'rows_s', 'win_p', 'win_s', 'sconv_p', 'sconv_s', 'ssm_p', 'ssm_s',
                            'rconv_p', 'rconv_s', 'rg_p', 'rg_s')}

    for i in range(depth):
        j = i // 2
        mod_i = mod_all[:, i * 6 * d:(i + 1) * 6 * d]
        mp = [mod_i[:bp, k * d:(k + 1) * d].reshape(bp, 1, d) for k in range(6)]
        ms = [mod_i[bp:bp + bs, k * d:(k + 1) * d].reshape(1, bs, d) for k in range(6)]
        g_mix = norm_mix[i][None, :]
        g_ffn = norm_ffn[i][None, :]

        if i % 2 == 0:
            w_in = rec_w_in[j]
            s0, s1, s2, s3 = 1024, 1024 + SSD_XBC, 1024 + SSD_XBC + SSD_HEADS, 1024 + SSD_XBC + SSD_HEADS + RG_WIDTH
            w_cat = jnp.concatenate([w_in[:, :s1], w_in[:, s2:], _pad_cols(w_in[:, s1:s2], 512)], axis=1).astype(BF16)
            proj_p = mod_matmul(xp, g_mix, mp[1], mp[0], w_cat, PROJ_ROW_TILE, 512)
            proj_s = mod_matmul(xs, g_mix, ms[1], ms[0], w_cat, bs, 512)

            ssd_w = (ssd_conv_w[j], ssd_conv_b[j], ssd_dt_bias[j], ssd_a_log[j], ssd_d[j], ssd_norm_w[j])
            rg_w = (rg_conv_w[j], rg_conv_b[j], rg_wa[j], rg_ba[j], rg_wi[j], rg_bi[j], rg_lambda[j])
            yp, a1, a2 = ssd_prompt(proj_p, bp, L, *ssd_w)
            rp, a3, a4 = rg_prompt(proj_p, bp, L, *rg_w)
            a4 = a4.reshape(bp, RG_WIDTH)
            ys, b1_, b2_ = ssd_sample(proj_s, state_ssd_conv[j], state_ssd[j], *ssd_w)
            rs, b3_, b4_ = rg_sample(proj_s, state_rg_conv[j], state_rg[j], *rg_w)
            outs['sconv_p'].append(a1); outs['ssm_p'].append(a2); outs['rconv_p'].append(a3); outs['rg_p'].append(a4)
            outs['sconv_s'].append(b1_); outs['ssm_s'].append(b2_); outs['rconv_s'].append(b3_); outs['rg_s'].append(b4_)
            w_out = rec_w_out[j].astype(BF16)
            w_parts = [w_out[:SSD_WIDTH], w_out[SSD_WIDTH:]]
            xp = mm_residual([yp.reshape(tp, -1), rp.reshape(tp, -1)], w_parts, xp, mp[2], ROW_TILE)
            xs = mm_residual([ys.reshape(bs, -1), rs.reshape(bs, -1)], w_parts, xs, ms[2], bs)
        else:
            w_in = nsa_w_in[j]
            w_cat = _pad_cols(w_in, 3072).astype(BF16)
            proj_p = mod_matmul(xp, g_mix, mp[1], mp[0], w_cat, PROJ_ROW_TILE, 512)
            proj_s = mod_matmul(xs, g_mix, ms[1], ms[0], w_cat, bs, 512)
            wts = (nsa_q_norm[j], nsa_k_norm[j], cmp_w1[j], cmp_pe[j], cmp_w2[j])
            op, rp, wp = nsa_prompt_pallas(proj_p, bp, L, *wts)
            br_s, gexp_s, rs, ws = nsa_sample_pallas(proj_s, bs, cache_nsa_kv[j], page_table, state_nsa_win[j], *wts)
            outs['rows_p'].append(rp); outs['win_p'].append(wp); outs['rows_s'].append(rs); outs['win_s'].append(ws)
            w_out = nsa_w_out[j].astype(BF16)
            xp = mm_residual([op.reshape(tp, -1)], [w_out], xp, mp[2], ROW_TILE)
            xs = mm_residual_gated(br_s, gexp_s, w_out, xs, ms[2])

        rw = _pad_cols(router_w[i], V7X_LANES)
        rwh = rw.astype(BF16)
        rwl = (rw - rwh.astype(F32)).astype(BF16)
        rb = jnp.concatenate([router_b[i], jnp.full((V7X_LANES - N_EXPERTS,), -1e30, F32)])[None, :]
        zero_base = jnp.zeros((1, V7X_LANES), F32)
        h_p, e_p, gt_p, rk_p, cnt_p = moe_router(xp, g_ffn, mp[4], mp[3], rwh, rwl, rb, zero_base, ROW_TILE)
        h_s, e_s, gt_s, rk_s, cnt_s = moe_router(xs, g_ffn, ms[4], ms[3], rwh, rwl, rb, cnt_p[-1], bs)
        h_all = jnp.concatenate([h_p, h_s], axis=0)
        t_all = tp + bs
        counts = cnt_s[-1, 0, :N_EXPERTS].astype(jnp.int32)
        padded, pad_start, blk_e, n_used, n_blocks = _moe_block_layout(counts, t_all * TOP_K)
        pstart = jnp.pad(pad_start.astype(F32), (0, V7X_LANES - N_EXPERTS))[None, :]
        dest = moe_dest(jnp.concatenate([e_p, e_s], axis=0), jnp.concatenate([rk_p, rk_s], axis=0), pstart, bs)
        idx = _moe_row_tables(dest[:, :TOP_K], counts, padded, pad_start, n_blocks, t_all)
        y4 = moe_ffn(h_all, blk_e, n_used, idx, moe_w1[i], moe_b1[i], moe_w2[i], moe_b2[i])
        xp = moe_combine(xp, mp[5], gt_p, y4, tp + bs, 0, bs)
        xs = moe_combine(xs, ms[5], gt_s, y4, tp + bs, tp, bs)

    st = lambda k: jnp.stack(outs[k])
    return (xp.reshape(bp, L, d), xs.reshape(bs, 1, d), st('rows_p'), st('rows_s'), st('win_p'), st('win_s'),
            st('sconv_p'), st('sconv_s'), st('ssm_p'), st('ssm_s'), st('rconv_p'), st('rconv_s'),
            st('rg_p'), st('rg_s'))
```

```python
import functools
import math

import jax
import jax.numpy as jnp
import numpy as np
from jax import lax
from jax.experimental import pallas as pl
from jax.experimental.pallas import tpu as pltpu

F32 = jnp.float32
BF16 = jnp.bfloat16

D_MODEL = 1024
NORM_EPS = 1e-6

SSD_WIDTH = 1024
SSD_HEAD_DIM = 64
SSD_HEADS = 16
SSD_GROUPS = 4
SSD_STATE = 128
SSD_CONV = 4
SSD_CHUNK = 128
SSD_XBC = SSD_WIDTH + 2 * SSD_GROUPS * SSD_STATE

RG_WIDTH = 1024
RG_BLOCKS = 16
RG_BLOCK_DIM = 64
RG_C = 8.0

NSA_HEADS = 16
NSA_KV_HEADS = 4
NSA_HEAD_DIM = 64
NSA_GROUP = 4
NSA_Q_WIDTH = 1024
NSA_KV_WIDTH = 256
CMP_LEN = 32
CMP_STRIDE = 16
CMP_R = 2
CMP_HID = 128
SEL_BLOCK = 64
SEL_TOPN = 16
WINDOW = 512
NSA_Q_BLOCK = 64

N_EXPERTS = 32
TOP_K = 4
D_FF = 1024
SWIGLU_LIMIT = 7.0
SWIGLU_ALPHA = 1.702

V7X_LANES = 128
V7X_VMEM_LIMIT_BYTES = 56 * 1024 * 1024

MOE_BLOCK_ROWS = 256
MOE_DMA_UNROLL = 16
ROW_TILE = 512
PROJ_ROW_TILE = 1024


def _cparams(sem, vmem=None):
    return pltpu.CompilerParams(dimension_semantics=sem, vmem_limit_bytes=vmem)


def _adaln_body(c_ref, w_ref, b_ref, o_ref):
    c = c_ref[...]
    s = c * jax.nn.sigmoid(c)
    o_ref[...] = jnp.dot(s.astype(BF16), w_ref[...], preferred_element_type=F32) + b_ref[...]


def adaln_mod(c, w_bf, b):
    r, d = c.shape
    n = w_bf.shape[1]
    tn = 1536
    return pl.pallas_call(
        _adaln_body,
        out_shape=jax.ShapeDtypeStruct((r, n), F32),
        grid=(n // tn,),
        in_specs=[pl.BlockSpec((r, d), lambda j: (0, 0)),
                  pl.BlockSpec((d, tn), lambda j: (0, j)),
                  pl.BlockSpec((1, tn), lambda j: (0, j))],
        out_specs=pl.BlockSpec((r, tn), lambda j: (0, j)),
        compiler_params=_cparams(("arbitrary",)),
        name="adaln_mod",
    )(c, w_bf, b)


def _modulated(x, g, scale, shift):
    ms = jnp.mean(x * x, axis=-1, keepdims=True)
    y = x * lax.rsqrt(ms + NORM_EPS) * g
    return y * (1.0 + scale) + shift


def _mod_matmul_body(x_ref, g_ref, sc_ref, sh_ref, w_ref, o_ref, h_scr):
    @pl.when(pl.program_id(1) == 0)
    def _():
        h_scr[...] = _modulated(x_ref[...], g_ref[...], sc_ref[0], sh_ref[0]).astype(BF16)

    o_ref[...] = jnp.dot(h_scr[...], w_ref[...], preferred_element_type=F32)


def mod_matmul(x, g, scale, shift, w_bf, tm, tn):
    t, d = x.shape
    n = w_bf.shape[1]
    m, r, _ = scale.shape
    rows_per_mod = t // m
    mod_spec = pl.BlockSpec((1, r, d), lambda i, j: ((i * tm) // rows_per_mod, 0, 0))
    return pl.pallas_call(
        _mod_matmul_body,
        out_shape=jax.ShapeDtypeStruct((t, n), F32),
        grid=(t // tm, n // tn),
        in_specs=[pl.BlockSpec((tm, d), lambda i, j: (i, 0)),
                  pl.BlockSpec((1, d), lambda i, j: (0, 0)),
                  mod_spec, mod_spec,
                  pl.BlockSpec((d, tn), lambda i, j: (0, j))],
        out_specs=pl.BlockSpec((tm, tn), lambda i, j: (i, j)),
        scratch_shapes=[pltpu.VMEM((tm, d), BF16)],
        compiler_params=_cparams(("arbitrary", "arbitrary")),
        name="mod_matmul",
    )(x, g, scale, shift, w_bf)


def _mm_residual_body(n_a, *refs):
    a_refs = refs[:n_a]
    w_refs = refs[n_a:2 * n_a]
    x_ref, g_ref, o_ref = refs[2 * n_a:]
    acc = None
    for a_ref, w_ref in zip(a_refs, w_refs):
        p = jnp.dot(a_ref[...].astype(BF16), w_ref[...], preferred_element_type=F32)
        acc = p if acc is None else acc + p
    o_ref[...] = x_ref[...] + g_ref[0] * acc


def mm_residual(a_list, w_list, x, gate, tm):
    t, d = x.shape
    m, r, _ = gate.shape
    rows_per_mod = t // m
    in_specs = [pl.BlockSpec((tm, a.shape[1]), lambda i: (i, 0)) for a in a_list]
    in_specs += [pl.BlockSpec(w.shape, lambda i: (0, 0)) for w in w_list]
    in_specs += [pl.BlockSpec((tm, d), lambda i: (i, 0)),
                 pl.BlockSpec((1, r, d), lambda i: ((i * tm) // rows_per_mod, 0, 0))]
    return pl.pallas_call(
        functools.partial(_mm_residual_body, len(a_list)),
        out_shape=jax.ShapeDtypeStruct((t, d), F32),
        grid=(t // tm,),
        in_specs=in_specs,
        out_specs=pl.BlockSpec((tm, d), lambda i: (i, 0)),
        compiler_params=_cparams(("arbitrary",)),
        name="mm_residual",
    )(*a_list, *w_list, x, gate)


def _router_body(x_ref, g_ref, sc_ref, sh_ref, wh_ref, wl_ref, rb_ref, base_ref, tri_ref,
                 h_ref, e_ref, gt_ref, rank_ref, cnt_ref, carry):
    h = _modulated(x_ref[...], g_ref[...], sc_ref[0], sh_ref[0])
    h_ref[...] = h
    hh = h.astype(BF16)
    hl = (h - hh.astype(F32)).astype(BF16)
    wh = wh_ref[...]
    wl = wl_ref[...]
    logits = (jnp.dot(hh, wh, preferred_element_type=F32)
              + (jnp.dot(hh, wl, preferred_element_type=F32)
                 + jnp.dot(hl, wh, preferred_element_type=F32))) + rb_ref[...]
    lane = lax.broadcasted_iota(jnp.int32, logits.shape, 1)
    neg = jnp.float32(-jnp.inf)
    vals, idxs = [], []
    cur = logits
    for _ in range(TOP_K):
        m = jnp.max(cur, axis=-1, keepdims=True)
        idx = jnp.min(jnp.where(cur == m, lane, V7X_LANES), axis=-1, keepdims=True)
        vals.append(m)
        idxs.append(idx)
        cur = jnp.where(lane == idx, neg, cur)
    exps = [jnp.exp(v - vals[0]) for v in vals]
    den = exps[0] + exps[1] + exps[2] + exps[3]
    e_out = jnp.zeros(logits.shape, jnp.int32)
    g_out = jnp.zeros(logits.shape, F32)
    hot = jnp.zeros(logits.shape, F32)
    for k in range(TOP_K):
        e_out = jnp.where(lane == k, idxs[k], e_out)
        g_out = jnp.where(lane == k, exps[k] / den, g_out)
        hot = hot + jnp.where(lane == idxs[k], 1.0, 0.0)
    e_ref[...] = e_out
    gt_ref[...] = g_out

    @pl.when(pl.program_id(0) == 0)
    def _():
        carry[...] = jnp.broadcast_to(base_ref[...], carry.shape)

    before = jnp.dot(tri_ref[...], hot.astype(BF16), preferred_element_type=F32) + carry[0:1, :]
    rank_out = jnp.zeros(logits.shape, F32)
    for k in range(TOP_K):
        r_k = jnp.sum(jnp.where(lane == idxs[k], before, 0.0), axis=-1, keepdims=True)
        rank_out = jnp.where(lane == k, r_k, rank_out)
    rank_ref[...] = rank_out
    total = carry[0:1, :] + jnp.sum(hot, axis=0, keepdims=True)
    carry[0:1, :] = total
    cnt_ref[0] = total


def moe_router(x, g, scale, shift, wh, wl, rb, base, tm):
    t, d = x.shape
    m, r, _ = scale.shape
    rows_per_mod = t // m
    mod_spec = pl.BlockSpec((1, r, d), lambda i: ((i * tm) // rows_per_mod, 0, 0))
    tri = jnp.asarray(np.tril(np.ones((tm, tm), np.float32), -1), BF16)
    lanes = lambda dt_: jax.ShapeDtypeStruct((t, V7X_LANES), dt_)
    lane_spec = pl.BlockSpec((tm, V7X_LANES), lambda i: (i, 0))
    return pl.pallas_call(
        _router_body,
        out_shape=(jax.ShapeDtypeStruct((t, d), F32), lanes(jnp.int32), lanes(F32), lanes(F32),
                   jax.ShapeDtypeStruct((t // tm, 1, V7X_LANES), F32)),
        grid=(t // tm,),
        in_specs=[pl.BlockSpec((tm, d), lambda i: (i, 0)),
                  pl.BlockSpec((1, d), lambda i: (0, 0)),
                  mod_spec, mod_spec,
                  pl.BlockSpec((d, V7X_LANES), lambda i: (0, 0)),
                  pl.BlockSpec((d, V7X_LANES), lambda i: (0, 0)),
                  pl.BlockSpec((1, V7X_LANES), lambda i: (0, 0)),
                  pl.BlockSpec((1, V7X_LANES), lambda i: (0, 0)),
                  pl.BlockSpec((tm, tm), lambda i: (0, 0))],
        out_specs=(pl.BlockSpec((tm, d), lambda i: (i, 0)), lane_spec, lane_spec, lane_spec,
                   pl.BlockSpec((1, 1, V7X_LANES), lambda i: (i, 0, 0))),
        scratch_shapes=[pltpu.VMEM((8, V7X_LANES), F32)],
        compiler_params=_cparams(("arbitrary",)),
        name="moe_router",
    )(x, g, scale, shift, wh, wl, rb, base, tri)


def _moe_dest_body(e_ref, rank_ref, pstart_ref, o_ref):
    e = e_ref[...]
    lane = lax.broadcasted_iota(jnp.int32, e.shape, 1)
    pstart = pstart_ref[...]
    out = jnp.zeros(e.shape, F32)
    for k in range(TOP_K):
        start_k = jnp.sum(jnp.where(lane == e[:, k:k + 1], pstart, 0.0), axis=-1, keepdims=True)
        out = jnp.where(lane == k, start_k, out)
    o_ref[...] = (out + rank_ref[...]).astype(jnp.int32)


def moe_dest(top_e, rank, pad_start, tm):
    t = top_e.shape[0]
    spec = pl.BlockSpec((tm, V7X_LANES), lambda i: (i, 0))
    return pl.pallas_call(
        _moe_dest_body,
        out_shape=jax.ShapeDtypeStruct((t, V7X_LANES), jnp.int32),
        grid=(t // tm,),
        in_specs=[spec, spec, pl.BlockSpec((1, V7X_LANES), lambda i: (0, 0))],
        out_specs=spec,
        compiler_params=_cparams(("arbitrary",)),
        name="moe_dest",
    )(top_e, rank, pad_start)


def _ffn_body(blk_e_ref, nused_ref,
              idx_hbm, h_hbm, w1_ref, b1_ref, w2_ref, b2_ref,
              out_hbm,
              idx_smem, xbuf, ybuf, w1bf, w2bf, sem_idx, sem_g, sem_s):
    bm = MOE_BLOCK_ROWS
    i = pl.program_id(0)
    n_used = nused_ref[0]

    def idx_copy(blk, slot):
        return pltpu.make_async_copy(idx_hbm.at[blk], idx_smem.at[slot], sem_idx.at[slot])

    def gather_copy(tok, slot, r):
        return pltpu.make_async_copy(h_hbm.at[pl.ds(tok, 1)], xbuf.at[slot, pl.ds(r, 1)], sem_g.at[slot])

    def scatter_copy(dst, slot, r):
        return pltpu.make_async_copy(ybuf.at[slot, pl.ds(r, 1)], out_hbm.at[pl.ds(dst, 1)], sem_s.at[slot])

    def start_gather(islot, slot):
        for r in range(bm):
            gather_copy(idx_smem[islot, r], slot, r).start()

    def wait_gather(slot):
        def body(r, c):
            gather_copy(0, slot, 0).wait()
            return c
        lax.fori_loop(0, bm, body, 0, unroll=MOE_DMA_UNROLL)

    def start_scatter(islot, slot):
        for r in range(bm):
            scatter_copy(idx_smem[islot, bm + r], slot, r).start()

    def wait_scatter(slot):
        def body(r, c):
            scatter_copy(0, slot, 0).wait()
            return c
        lax.fori_loop(0, bm, body, 0, unroll=MOE_DMA_UNROLL)

    @pl.when(i < n_used)
    def _():
        slot = i % 2
        islot = i % 3

        @pl.when(i == 0)
        def _():
            ybuf[...] = jnp.zeros_like(ybuf)
            for sl in range(2):
                tail = pltpu.make_async_copy(ybuf.at[sl], out_hbm.at[pl.ds(out_hbm.shape[0] - (2 - sl) * bm, bm)],
                                             sem_s.at[sl])
                tail.start()
                tail.wait()
            idx_copy(0, 0).start()
            idx_copy(0, 0).wait()
            start_gather(0, 0)

            @pl.when(n_used > 1)
            def _():
                idx_copy(1, 1).start()

        @pl.when(i + 2 < n_used)
        def _():
            idx_copy(i + 2, (i + 2) % 3).start()

        @pl.when(i + 1 < n_used)
        def _():
            idx_copy(i + 1, (i + 1) % 3).wait()
            start_gather((i + 1) % 3, 1 - slot)

        @pl.when(jnp.logical_or(i == 0, blk_e_ref[i] != blk_e_ref[jnp.maximum(i - 1, 0)]))
        def _():
            w1bf[...] = w1_ref[0].astype(BF16)
            w2bf[...] = w2_ref[0].astype(BF16)

        wait_gather(slot)

        @pl.when(i >= 2)
        def _():
            wait_scatter(slot)

        x = xbuf[slot].astype(BF16)
        u = jnp.dot(x, w1bf[...], preferred_element_type=F32) + b1_ref[0]
        gl = jnp.minimum(u[:, :D_FF], SWIGLU_LIMIT)
        lin = jnp.clip(u[:, D_FF:], -SWIGLU_LIMIT, SWIGLU_LIMIT)
        act = gl * jax.nn.sigmoid(SWIGLU_ALPHA * gl) * (lin + 1.0)
        ybuf[slot] = jnp.dot(act.astype(BF16), w2bf[...], preferred_element_type=F32) + b2_ref[0]
        start_scatter(islot, slot)

        @pl.when(i == n_used - 1)
        def _():
            @pl.when(i >= 1)
            def _():
                wait_scatter(1 - slot)
            wait_scatter(slot)


def moe_ffn(h_all, blk_e, n_used, idx, w1, b1, w2, b2):
    t, d = h_all.shape
    bm = MOE_BLOCK_ROWS
    n_blocks = idx.shape[0]
    grid_spec = pltpu.PrefetchScalarGridSpec(
        num_scalar_prefetch=2,
        grid=(n_blocks,),
        in_specs=[pl.BlockSpec(memory_space=pl.ANY),
                  pl.BlockSpec(memory_space=pl.ANY),
                  pl.BlockSpec((1, d, 2 * D_FF), lambda i, be, nu: (be[i], 0, 0)),
                  pl.BlockSpec((1, 1, 2 * D_FF), lambda i, be, nu: (be[i], 0, 0)),
                  pl.BlockSpec((1, D_FF, d), lambda i, be, nu: (be[i], 0, 0)),
                  pl.BlockSpec((1, 1, d), lambda i, be, nu: (be[i], 0, 0))],
        out_specs=pl.BlockSpec(memory_space=pl.ANY),
        scratch_shapes=[pltpu.SMEM((3, 2 * bm), jnp.int32),
                        pltpu.VMEM((2, bm, d), F32),
                        pltpu.VMEM((2, bm, d), F32),
                        pltpu.VMEM((d, 2 * D_FF), BF16),
                        pltpu.VMEM((D_FF, d), BF16),
                        pltpu.SemaphoreType.DMA((3,)),
                        pltpu.SemaphoreType.DMA((2,)),
                        pltpu.SemaphoreType.DMA((2,))],
    )
    return pl.pallas_call(
        _ffn_body,
        out_shape=jax.ShapeDtypeStruct((t * TOP_K + 2 * bm, d), F32),
        grid_spec=grid_spec,
        compiler_params=_cparams(("arbitrary",), V7X_VMEM_LIMIT_BYTES),
        name="moe_ffn",
    )(blk_e, n_used, idx, h_all, w1, b1.reshape(N_EXPERTS, 1, -1), w2, b2.reshape(N_EXPERTS, 1, -1))


def _moe_combine_body(x_ref, g_ref, rg_ref, y0_ref, y1_ref, y2_ref, y3_ref, o_ref):
    rg = rg_ref[...]
    acc = ((rg[:, 0:1] * y0_ref[...] + rg[:, 1:2] * y1_ref[...])
           + (rg[:, 2:3] * y2_ref[...] + rg[:, 3:4] * y3_ref[...]))
    o_ref[...] = x_ref[...] + g_ref[0] * acc


def moe_combine(x, gate, router_gate, y4, t_all, row_off, tm):
    t, d = x.shape
    m, r, _ = gate.shape
    rows_per_mod = t // m
    y_spec = lambda k: pl.BlockSpec((tm, d), lambda i: ((k * t_all + row_off) // tm + i, 0))
    return pl.pallas_call(
        _moe_combine_body,
        out_shape=jax.ShapeDtypeStruct((t, d), F32),
        grid=(t // tm,),
        in_specs=[pl.BlockSpec((tm, d), lambda i: (i, 0)),
                  pl.BlockSpec((1, r, d), lambda i: ((i * tm) // rows_per_mod, 0, 0)),
                  pl.BlockSpec((tm, V7X_LANES), lambda i: (i, 0)),
                  y_spec(0), y_spec(1), y_spec(2), y_spec(3)],
        out_specs=pl.BlockSpec((tm, d), lambda i: (i, 0)),
        compiler_params=_cparams(("arbitrary",)),
        name="moe_combine",
    )(x, gate, router_gate, y4, y4, y4, y4)


def _moe_block_layout(counts, tk):
    bm = MOE_BLOCK_ROWS
    padded = (counts + bm - 1) // bm * bm
    pad_end = jnp.cumsum(padded)
    pad_start = pad_end - padded
    n_blocks = -(-tk // bm) + N_EXPERTS
    blk_start = jnp.arange(n_blocks, dtype=jnp.int32) * bm
    blk_e = jnp.minimum(jnp.sum((pad_end[None, :] <= blk_start[:, None]).astype(jnp.int32), axis=1),
                        N_EXPERTS - 1)
    n_used = (pad_end[-1] // bm).astype(jnp.int32).reshape(1)
    return padded, pad_start, blk_e, n_used, n_blocks


def _moe_row_tables(dest, counts, padded, pad_start, n_blocks, t):
    bm = MOE_BLOCK_ROWS
    tk = t * TOP_K
    n_rows = n_blocks * bm
    big = jnp.int32(2 ** 30)
    p = jnp.arange(bm, dtype=jnp.int32)[None, :]
    e = jnp.arange(N_EXPERTS, dtype=jnp.int32)[:, None]
    pad_keys = jnp.where(p < (padded - counts)[:, None], (pad_start + counts)[:, None] + p, big + e * bm + p)
    n_fill = n_rows - tk - N_EXPERTS * bm
    keys = jnp.concatenate([dest.reshape(tk), pad_keys.reshape(-1), big + N_EXPERTS * bm + jnp.arange(n_fill, dtype=jnp.int32)])
    vals = jnp.concatenate([jnp.arange(tk, dtype=jnp.int32), jnp.full((n_rows - tk,), -1, jnp.int32)])
    _, v = lax.sort((keys, vals), num_keys=1)
    valid = v >= 0
    row = jnp.arange(n_rows, dtype=jnp.int32)
    src_tok = jnp.where(valid, v // TOP_K, 0)
    pad_row = tk + ((row // bm) % 2) * bm + row % bm
    dst_row = jnp.where(valid, (v % TOP_K) * t + v // TOP_K, pad_row)
    return jnp.concatenate([src_tok.reshape(n_blocks, bm), dst_row.reshape(n_blocks, bm)], axis=1)


def _rmsnorm(x, w):
    xf = x.astype(F32)
    y = xf * lax.rsqrt(jnp.mean(xf * xf, axis=-1, keepdims=True) + NORM_EPS)
    return y * w.astype(F32)


def _causal_conv(x, prev, w, b):
    width = w.shape[0]
    L = x.shape[1]
    xp = jnp.concatenate([prev.astype(x.dtype), x], axis=1)
    y = b
    for k in range(width):
        y = y + xp[:, k:k + L] * w[k]
    return y, xp[:, xp.shape[1] - (width - 1):]


def _ssd_scan(x, dt, A, Bm, Cm, h0):
    b, L, H, P = x.shape
    G, N = Bm.shape[2], Bm.shape[3]
    K = H // G
    q = math.gcd(L, SSD_CHUNK)
    nc = L // q
    a = (dt * A).reshape(b, nc, q, G, K)
    xdt = (x * dt[..., None]).reshape(b, nc, q, G, K, P)
    Bc = Bm.reshape(b, nc, q, G, N)
    Cc = Cm.reshape(b, nc, q, G, N)
    a_cs = jnp.cumsum(a, axis=2)
    causal = jnp.tril(jnp.ones((q, q), dtype=bool))[None, None, :, :, None, None]
    seg = a_cs[:, :, :, None] - a_cs[:, :, None, :]
    decay = jnp.exp(jnp.where(causal, seg, -jnp.inf))
    cb = jnp.einsum('bctgn,bcsgn->bctsg', Cc, Bc)
    y_diag = jnp.einsum('bctsgk,bcsgkp->bctgkp', cb[..., None] * decay, xdt)
    to_end = jnp.exp(a_cs[:, :, -1:] - a_cs)
    states = jnp.einsum('bcsgn,bcsgkp->bcgkpn', Bc, xdt * to_end[..., None])
    chunk_decay = jnp.exp(a_cs[:, :, -1])

    def step(h, inp):
        st, dec = inp
        return h * dec[..., None, None] + st, h

    h_last, h_prev = lax.scan(step, h0.reshape(b, G, K, P, N),
                              (jnp.moveaxis(states, 1, 0), jnp.moveaxis(chunk_decay, 1, 0)))
    h_prev = jnp.moveaxis(h_prev, 0, 1)
    y_off = jnp.einsum('bctgn,bcgkpn->bctgkp', Cc, h_prev) * jnp.exp(a_cs)[..., None]
    return (y_diag + y_off).reshape(b, L, H, P), h_last.reshape(b, H, P, N)


def _rg_lru(x, h0, wa, ba, wi, bi, lam):
    b, L, _ = x.shape
    xb = x.reshape(b, L, RG_BLOCKS, RG_BLOCK_DIM)
    r = jax.nn.sigmoid(jnp.einsum('blnd,nde->blne', xb, wa).reshape(b, L, RG_WIDTH) + ba)
    i = jax.nn.sigmoid(jnp.einsum('blnd,nde->blne', xb, wi).reshape(b, L, RG_WIDTH) + bi)
    log_a = -RG_C * r * jax.nn.softplus(-lam.astype(F32))
    a = jnp.exp(log_a)
    u = jnp.sqrt(-jnp.expm1(2.0 * log_a)) * (i * x)
    u = u.at[:, 0].add(a[:, 0] * h0)

    def combine(e1, e2):
        return e1[0] * e2[0], e2[0] * e1[1] + e2[1]

    _, h = lax.associative_scan(combine, (a, u), axis=1)
    return h, h[:, -1]


def _rec_core(z, xbc, dt, gate, xr, conv_ssd0, ssm0, conv_rg0, rg0, conv_w, conv_b, dt_bias, a_log,
              d_skip, norm_w, rg_conv_w, rg_conv_b, wa, ba, wi, bi, lam):
    b, L, _ = z.shape
    xbc, conv_ssd1 = _causal_conv(xbc, conv_ssd0, conv_w, conv_b)
    xbc = jax.nn.silu(xbc)
    xs, Bm, Cm = jnp.split(xbc, [SSD_WIDTH, SSD_WIDTH + SSD_GROUPS * SSD_STATE], axis=-1)
    dt = jax.nn.softplus(dt + dt_bias)
    A = -jnp.exp(a_log)
    xh = xs.reshape(b, L, SSD_HEADS, SSD_HEAD_DIM)
    y, ssm1 = _ssd_scan(xh, dt, A, Bm.reshape(b, L, SSD_GROUPS, SSD_STATE),
                        Cm.reshape(b, L, SSD_GROUPS, SSD_STATE), ssm0)
    y = y + d_skip[:, None] * xh
    y = y.reshape(b, L, SSD_WIDTH) * jax.nn.silu(z)
    y = _rmsnorm(y.reshape(b, L, SSD_GROUPS, -1), norm_w.reshape(SSD_GROUPS, -1)).reshape(b, L, SSD_WIDTH)
    xr, conv_rg1 = _causal_conv(xr, conv_rg0, rg_conv_w, rg_conv_b)
    r_out, rg1 = _rg_lru(xr, rg0, wa, ba, wi, bi, lam)
    r_out = r_out * jax.nn.gelu(gate)
    return y, r_out, conv_ssd1, ssm1, conv_rg1, rg1


CONV_TAIL = 8
RG_TILE = 256


def _split3_bf16(x):
    h = x.astype(BF16)
    r = x - h.astype(F32)
    m = r.astype(BF16)
    return h, m, (r - m.astype(F32)).astype(BF16)


def _dot3(parts, w, dims=None):
    if dims is None:
        outs = [jnp.dot(p, w, preferred_element_type=F32) for p in parts]
    else:
        outs = [lax.dot_general(w, p, dims, preferred_element_type=F32) for p in parts]
    return (outs[0] + outs[1]) + outs[2]


def _softplus(x):
    return jnp.maximum(x, 0.0) + jnp.log(1.0 + jnp.exp(-jnp.abs(x)))


def _silu(x):
    return x * jax.nn.sigmoid(x)


def _group_rmsnorm(y, w, n_groups):
    width = y.shape[1] // n_groups
    outs = []
    for g in range(n_groups):
        yg = y[:, g * width:(g + 1) * width]
        outs.append(yg * lax.rsqrt(jnp.mean(yg * yg, axis=-1, keepdims=True) + NORM_EPS))
    return jnp.concatenate(outs, axis=1) * w


def _conv_tile(xbuf, cw_ref, cb_ref, rows):
    y = cb_ref[...]
    for k in range(SSD_CONV):
        y = y + cw_ref[k:k + 1, :] * xbuf[pl.ds(CONV_TAIL - (SSD_CONV - 1) + k, rows), :]
    return y


def _ssd_prompt_body(z_ref, xs_ref, bc_ref, dt_ref, cw_ref, cb_ref, dtb_ref, a_ref, dexp_ref, nw_ref, tri_ref,
                     y_ref, conv_ref, state_ref, xbuf, h_scr):
    q = SSD_CHUNK
    c = pl.program_id(1)
    last = pl.num_programs(1) - 1
    P, N = SSD_HEAD_DIM, SSD_STATE

    @pl.when(c == 0)
    def _():
        xbuf[0:CONV_TAIL, :] = jnp.zeros((CONV_TAIL, SSD_XBC), F32)
        h_scr[...] = jnp.zeros_like(h_scr)

    xbuf[CONV_TAIL:CONV_TAIL + q, 0:SSD_WIDTH] = xs_ref[...]
    xbuf[CONV_TAIL:CONV_TAIL + q, SSD_WIDTH:SSD_XBC] = bc_ref[...]
    xc = _silu(_conv_tile(xbuf, cw_ref, cb_ref, q))

    @pl.when(c == last)
    def _():
        conv_ref[0] = xbuf[CONV_TAIL + q - (SSD_CONV - 1):CONV_TAIL + q, :]

    xbuf[0:CONV_TAIL, :] = xbuf[q:q + CONV_TAIL, :]

    xs = xc[:, 0:SSD_WIDTH]
    bm = xc[:, SSD_WIDTH:SSD_WIDTH + SSD_GROUPS * N].astype(BF16)
    cm = xc[:, SSD_WIDTH + SSD_GROUPS * N:SSD_XBC].astype(BF16)
    dt = _softplus(dt_ref[...] + dtb_ref[...])
    a = dt * a_ref[...]
    a_cs = _dot3(_split3_bf16(a), tri_ref[...], dims=(((1,), (0,)), ((), ())))
    a_cs_t = a_cs.T
    dt_t = dt.T
    a_end_t = a_cs_t[:, q - 1:q]
    w_t = dt_t * jnp.exp(a_end_t - a_cs_t)
    ea = jnp.exp(a_cs)
    xs_t = xs.T
    row = lax.broadcasted_iota(jnp.int32, (q, q), 0)
    col = lax.broadcasted_iota(jnp.int32, (q, q), 1)
    causal = col <= row
    heads_per_group = SSD_HEADS // SSD_GROUPS
    ys = []
    for g in range(SSD_GROUPS):
        bg = bm[:, g * N:(g + 1) * N]
        cg = cm[:, g * N:(g + 1) * N]
        cb = lax.dot_general(cg, bg, _NT_DIMS_SSD, preferred_element_type=F32)
        for k in range(heads_per_group):
            h = g * heads_per_group + k
            seg = a_cs[:, h:h + 1] - a_cs_t[h:h + 1, :]
            decay = jnp.exp(jnp.where(causal, seg, MASK_NEG))
            xh = xs[:, h * P:(h + 1) * P]
            xdt = (xh * dt[:, h:h + 1]).astype(BF16)
            y_diag = jnp.dot((cb * decay).astype(BF16), xdt, preferred_element_type=F32)
            h_prev = h_scr[h]
            y_off = lax.dot_general(cg, h_prev.astype(BF16), _NT_DIMS_SSD,
                                    preferred_element_type=F32) * ea[:, h:h + 1]
            st = jnp.dot((xs_t[h * P:(h + 1) * P, :] * w_t[h:h + 1, :]).astype(BF16), bg,
                         preferred_element_type=F32)
            h_scr[h] = h_prev * jnp.exp(a_end_t[h:h + 1, :]) + st
            ys.append(y_diag + y_off)
    y = jnp.concatenate(ys, axis=1) + dexp_ref[...] * xs
    y = y * _silu(z_ref[...])
    y_ref[...] = _group_rmsnorm(y, nw_ref[...], SSD_GROUPS)

    @pl.when(c == last)
    def _():
        state_ref[0] = h_scr[...]


_NT_DIMS_SSD = (((1,), (1,)), ((), ()))
MASK_NEG = -1e30


def _pad_lanes(v, n=V7X_LANES):
    return jnp.pad(v, (0, n - v.shape[0]))[None, :]


def ssd_prompt(proj, b, seq_len, conv_w, conv_b, dt_bias, a_log, d_skip, norm_w):
    q = SSD_CHUNK
    nc = seq_len // q
    t = b * seq_len
    tri = jnp.asarray(np.tril(np.ones((q, q), np.float32)), BF16)
    a_neg = _pad_lanes(-jnp.exp(a_log))
    dtb = _pad_lanes(dt_bias)
    dexp = jnp.repeat(d_skip, SSD_HEAD_DIM)[None, :]
    nw = norm_w[None, :]
    cb = conv_b[None, :]
    colblk = lambda j, w=SSD_WIDTH: pl.BlockSpec((q, w), lambda bi, c: (bi * nc + c, j))
    full = lambda a: pl.BlockSpec(a.shape, lambda bi, c: (0,) * a.ndim)
    return pl.pallas_call(
        _ssd_prompt_body,
        out_shape=(jax.ShapeDtypeStruct((t, SSD_WIDTH), F32),
                   jax.ShapeDtypeStruct((b, SSD_CONV - 1, SSD_XBC), F32),
                   jax.ShapeDtypeStruct((b, SSD_HEADS, SSD_HEAD_DIM, SSD_STATE), F32)),
        grid=(b, nc),
        in_specs=[colblk(0), colblk(1), colblk(2),
                  pl.BlockSpec((q, V7X_LANES), lambda bi, c: (bi * nc + c, 5 * SSD_WIDTH // V7X_LANES)),
                  full(conv_w), full(cb), full(dtb), full(a_neg), full(dexp), full(nw), full(tri)],
        out_specs=(pl.BlockSpec((q, SSD_WIDTH), lambda bi, c: (bi * nc + c, 0)),
                   pl.BlockSpec((1, SSD_CONV - 1, SSD_XBC), lambda bi, c: (bi, 0, 0)),
                   pl.BlockSpec((1, SSD_HEADS, SSD_HEAD_DIM, SSD_STATE), lambda bi, c: (bi, 0, 0, 0))),
        scratch_shapes=[pltpu.VMEM((CONV_TAIL + q, SSD_XBC), F32),
                        pltpu.VMEM((SSD_HEADS, SSD_HEAD_DIM, SSD_STATE), F32)],
        compiler_params=_cparams(("arbitrary", "arbitrary"), V7X_VMEM_LIMIT_BYTES),
        name="ssd_prompt",
    )(proj, proj, proj, proj, conv_w, cb, dtb, a_neg, dexp, nw, tri)


def _ssd_sample_pre_body(xs_ref, bc_ref, dt_ref, s0_ref, s1_ref, s2_ref, cw_ref, cb_ref, dtb_ref, a_ref, dexp_ref,
                         hexp_ref, yd_ref, xdt_ref, b_ref, c_ref, ea_ref, eaexp_ref):
    N = SSD_STATE
    x_new = jnp.concatenate([xs_ref[...], bc_ref[...]], axis=1)
    y = (cb_ref[...] + cw_ref[0:1, :] * s0_ref[...] + cw_ref[1:2, :] * s1_ref[...]
         + cw_ref[2:3, :] * s2_ref[...] + cw_ref[3:4, :] * x_new)
    xc = _silu(y)
    xs = xc[:, 0:SSD_WIDTH]
    bm = xc[:, SSD_WIDTH:SSD_WIDTH + SSD_GROUPS * N]
    cm = xc[:, SSD_WIDTH + SSD_GROUPS * N:SSD_XBC]
    dt = _softplus(dt_ref[...] + dtb_ref[...])
    ea = jnp.exp(dt * a_ref[...])
    hexp = hexp_ref[...]
    dt_exp = _dot3(_split3_bf16(dt), hexp)
    xdt = (xs * dt_exp).astype(BF16)
    bb = bm.astype(BF16)
    cc = cm.astype(BF16)
    prod = bb.astype(F32) * cc.astype(F32)
    hw = SSD_WIDTH // SSD_GROUPS
    cb = jnp.concatenate(
        [jnp.broadcast_to(jnp.sum(prod[:, g * N:(g + 1) * N], axis=-1, keepdims=True), (xs.shape[0], hw))
         for g in range(SSD_GROUPS)], axis=1)
    yd_ref[...] = cb.astype(BF16).astype(F32) * xdt.astype(F32) + dexp_ref[...] * xs
    xdt_ref[...] = xdt.astype(F32)
    b_ref[...] = bb.astype(F32)
    c_ref[...] = cc.astype(F32)
    ea_ref[...] = ea
    eaexp_ref[...] = _dot3(_split3_bf16(ea), hexp)


def _ssd_sample_state_body(ea_smem, xdt_ref, b_ref, c_ref, yd_ref, eaexp_ref, z_ref, nw_ref, h0_ref,
                           y_ref, h1_ref):
    i = pl.program_id(0)
    N = SSD_STATE
    gw = SSD_WIDTH // SSD_GROUPS
    heads_per_group = SSD_HEADS // SSD_GROUPS
    row0 = lax.broadcasted_iota(jnp.int32, (8, 1), 0) == 0
    y_off = []
    for g in range(SSD_GROUPS):
        x8 = jnp.broadcast_to(xdt_ref[0, :, g * gw:(g + 1) * gw], (8, gw)).astype(BF16)
        b8 = jnp.where(row0, jnp.broadcast_to(b_ref[0, :, g * N:(g + 1) * N], (8, N)), 0.0).astype(BF16)
        c8 = jnp.broadcast_to(c_ref[0, :, g * N:(g + 1) * N], (8, N)).astype(BF16)
        h0g = h0_ref[0, g * gw:(g + 1) * gw, :]
        st = lax.dot_general(x8, b8, (((0,), (0,)), ((), ())), preferred_element_type=F32)
        yo = lax.dot_general(c8, h0g.astype(BF16), _NT_DIMS_SSD, preferred_element_type=F32)
        y_off.append(yo[0:1, :])
        for k in range(heads_per_group):
            h = g * heads_per_group + k
            r = slice(k * SSD_HEAD_DIM, (k + 1) * SSD_HEAD_DIM)
            h1_ref[0, g * gw + k * SSD_HEAD_DIM:g * gw + (k + 1) * SSD_HEAD_DIM, :] = (
                h0g[r, :] * ea_smem[i, h] + st[r, :])
    y = jnp.concatenate(y_off, axis=1) * eaexp_ref[0] + yd_ref[0]
    y = y * _silu(z_ref[0])
    y_ref[0] = _group_rmsnorm(y, nw_ref[...], SSD_GROUPS)


def ssd_sample(proj, conv_state, ssm_state, conv_w, conv_b, dt_bias, a_log, d_skip, norm_w):
    bsz = proj.shape[0]
    H, P, N = SSD_HEADS, SSD_HEAD_DIM, SSD_STATE
    a_neg = _pad_lanes(-jnp.exp(a_log))
    dtb = _pad_lanes(dt_bias)
    dexp = jnp.repeat(d_skip, P)[None, :]
    hexp = jnp.asarray((np.arange(V7X_LANES)[:, None] == (np.arange(H * P)[None, :] // P)).astype(np.float32), BF16)
    cb = conv_b[None, :]
    s0, s1, s2 = conv_state[:, 0], conv_state[:, 1], conv_state[:, 2]
    blk = lambda j, w: pl.BlockSpec((bsz, w), lambda i: (0, j))
    full = lambda a: pl.BlockSpec(a.shape, lambda i: (0,) * a.ndim)
    o = lambda w, dt_: jax.ShapeDtypeStruct((bsz, w), dt_)
    yd, xdt, bb, cc, ea, eaexp = pl.pallas_call(
        _ssd_sample_pre_body,
        out_shape=(o(SSD_WIDTH, F32), o(SSD_WIDTH, F32), o(SSD_GROUPS * N, F32), o(SSD_GROUPS * N, F32),
                   o(V7X_LANES, F32), o(SSD_WIDTH, F32)),
        grid=(1,),
        in_specs=[blk(1, SSD_WIDTH), blk(2, SSD_WIDTH), blk(5 * SSD_WIDTH // V7X_LANES, V7X_LANES),
                  full(s0), full(s1), full(s2), full(conv_w), full(cb), full(dtb), full(a_neg), full(dexp), full(hexp)],
        out_specs=(full(o(SSD_WIDTH, F32)), full(o(SSD_WIDTH, F32)), full(o(SSD_GROUPS * N, F32)),
                   full(o(SSD_GROUPS * N, F32)), full(o(V7X_LANES, F32)), full(o(SSD_WIDTH, F32))),
        compiler_params=_cparams(("arbitrary",)),
        name="ssd_sample_pre",
    )(proj, proj, proj, s0, s1, s2, conv_w, cb, dtb, a_neg, dexp, hexp)
    x_new = jnp.concatenate([proj[:, SSD_WIDTH:2 * SSD_WIDTH], proj[:, 2 * SSD_WIDTH:3 * SSD_WIDTH]], axis=1)
    conv_new = jnp.stack([s1, s2, x_new], axis=1)
    z3 = proj[:, 0:SSD_WIDTH].reshape(bsz, 1, SSD_WIDTH)
    row = lambda w: pl.BlockSpec((1, 1, w), lambda i, ea_: (i, 0, 0))
    nw = norm_w[None, :]
    grid_spec = pltpu.PrefetchScalarGridSpec(
        num_scalar_prefetch=1,
        grid=(bsz,),
        in_specs=[row(SSD_WIDTH), row(SSD_GROUPS * N), row(SSD_GROUPS * N), row(SSD_WIDTH), row(SSD_WIDTH),
                  row(SSD_WIDTH), pl.BlockSpec(nw.shape, lambda i, ea_: (0, 0)),
                  pl.BlockSpec((1, H * P, N), lambda i, ea_: (i, 0, 0))],
        out_specs=(row(SSD_WIDTH), pl.BlockSpec((1, H * P, N), lambda i, ea_: (i, 0, 0))),
    )
    r3 = lambda a: a.reshape(bsz, 1, a.shape[1])
    y, h1 = pl.pallas_call(
        _ssd_sample_state_body,
        out_shape=(jax.ShapeDtypeStruct((bsz, 1, SSD_WIDTH), F32), jax.ShapeDtypeStruct((bsz, H * P, N), F32)),
        grid_spec=grid_spec,
        compiler_params=_cparams(("arbitrary",)),
        name="ssd_sample_state",
    )(ea[:, :H], r3(xdt), r3(bb), r3(cc), r3(yd), r3(eaexp), z3, nw, ssm_state.reshape(bsz, H * P, N))
    return y.reshape(bsz, SSD_WIDTH), conv_new, h1.reshape(bsz, H, P, N)


def _rg_gates(xc, wa_ref, ba_ref, wi_ref, bi_ref, sp_ref):
    xb = xc.astype(BF16)
    r = jax.nn.sigmoid(jnp.dot(xb, wa_ref[...], preferred_element_type=F32) + ba_ref[...])
    ig = jax.nn.sigmoid(jnp.dot(xb, wi_ref[...], preferred_element_type=F32) + bi_ref[...])
    log_a = -RG_C * r * sp_ref[...]
    a = jnp.exp(log_a)
    u = jnp.sqrt(1.0 - jnp.exp(2.0 * log_a)) * (ig * xc)
    return a, u


def _rg_prompt_body(gate_ref, xr_ref, cw_ref, cb_ref, wa_ref, ba_ref, wi_ref, bi_ref, sp_ref,
                    y_ref, conv_ref, state_ref, xbuf, h_scr):
    rows = RG_TILE
    c = pl.program_id(1)
    last = pl.num_programs(1) - 1

    @pl.when(c == 0)
    def _():
        xbuf[0:CONV_TAIL, :] = jnp.zeros((CONV_TAIL, RG_WIDTH), F32)
        h_scr[...] = jnp.zeros_like(h_scr)

    xbuf[CONV_TAIL:CONV_TAIL + rows, :] = xr_ref[...]
    xc = _conv_tile(xbuf, cw_ref, cb_ref, rows)

    @pl.when(c == last)
    def _():
        conv_ref[0] = xbuf[CONV_TAIL + rows - (SSD_CONV - 1):CONV_TAIL + rows, :]

    xbuf[0:CONV_TAIL, :] = xbuf[rows:rows + CONV_TAIL, :]
    a, u = _rg_gates(xc, wa_ref, ba_ref, wi_ref, bi_ref, sp_ref)
    t_idx = lax.broadcasted_iota(jnp.int32, (rows, 1), 0)
    d = 1
    while d < rows:
        keep = t_idx >= d
        a_sh = jnp.where(keep, pltpu.roll(a, d, 0), 1.0)
        u_sh = jnp.where(keep, pltpu.roll(u, d, 0), 0.0)
        u = u + a * u_sh
        a = a * a_sh
        d *= 2
    h = u + a * h_scr[0:1, :]
    h_scr[0:1, :] = h[rows - 1:rows, :]
    y_ref[...] = h * _gelu_tanh(gate_ref[...])

    @pl.when(c == last)
    def _():
        state_ref[0] = h[rows - 1:rows, :]


def _rg_weights(wa, ba, wi, bi, lam):
    eye = jnp.eye(RG_BLOCKS, dtype=F32)
    bd = lambda w: jnp.einsum('nde,nm->ndme', w, eye).reshape(RG_WIDTH, RG_WIDTH).astype(BF16)
    return bd(wa), ba[None, :], bd(wi), bi[None, :], jax.nn.softplus(-lam)[None, :]


def rg_prompt(proj, b, seq_len, conv_w, conv_b, wa, ba, wi, bi, lam):
    rows = RG_TILE
    nt = seq_len // rows
    t = b * seq_len
    wts = _rg_weights(wa, ba, wi, bi, lam)
    cb = conv_b[None, :]
    full = lambda a: pl.BlockSpec(a.shape, lambda bi_, c: (0,) * a.ndim)
    return pl.pallas_call(
        _rg_prompt_body,
        out_shape=(jax.ShapeDtypeStruct((t, RG_WIDTH), F32),
                   jax.ShapeDtypeStruct((b, SSD_CONV - 1, RG_WIDTH), F32),
                   jax.ShapeDtypeStruct((b, 1, RG_WIDTH), F32)),
        grid=(b, nt),
        in_specs=[pl.BlockSpec((rows, RG_WIDTH), lambda bi_, c: (bi_ * nt + c, 3)),
                  pl.BlockSpec((rows, RG_WIDTH), lambda bi_, c: (bi_ * nt + c, 4)),
                  full(conv_w), full(cb)] + [full(w) for w in wts],
        out_specs=(pl.BlockSpec((rows, RG_WIDTH), lambda bi_, c: (bi_ * nt + c, 0)),
                   pl.BlockSpec((1, SSD_CONV - 1, RG_WIDTH), lambda bi_, c: (bi_, 0, 0)),
                   pl.BlockSpec((1, 1, RG_WIDTH), lambda bi_, c: (bi_, 0, 0))),
        scratch_shapes=[pltpu.VMEM((CONV_TAIL + rows, RG_WIDTH), F32), pltpu.VMEM((8, RG_WIDTH), F32)],
        compiler_params=_cparams(("arbitrary", "arbitrary"), V7X_VMEM_LIMIT_BYTES),
        name="rg_prompt",
    )(proj, proj, conv_w, cb, *wts)


def _rg_sample_body(gate_ref, xr_ref, s0_ref, s1_ref, s2_ref, h0_ref, cw_ref, cb_ref, wa_ref, ba_ref, wi_ref, bi_ref,
                    sp_ref, y_ref, h1_ref):
    xc = (cb_ref[...] + cw_ref[0:1, :] * s0_ref[...] + cw_ref[1:2, :] * s1_ref[...]
          + cw_ref[2:3, :] * s2_ref[...] + cw_ref[3:4, :] * xr_ref[...])
    a, u = _rg_gates(xc, wa_ref, ba_ref, wi_ref, bi_ref, sp_ref)
    h = a * h0_ref[...] + u
    h1_ref[...] = h
    y_ref[...] = h * _gelu_tanh(gate_ref[...])


def rg_sample(proj, conv_state, h0, conv_w, conv_b, wa, ba, wi, bi, lam):
    bsz = proj.shape[0]
    wts = _rg_weights(wa, ba, wi, bi, lam)
    cb = conv_b[None, :]
    s0, s1, s2 = conv_state[:, 0], conv_state[:, 1], conv_state[:, 2]
    full = lambda a: pl.BlockSpec(a.shape, lambda i: (0,) * a.ndim)
    out = jax.ShapeDtypeStruct((bsz, RG_WIDTH), F32)
    y, h1 = pl.pallas_call(
        _rg_sample_body,
        out_shape=(out, out),
        grid=(1,),
        in_specs=[pl.BlockSpec((bsz, RG_WIDTH), lambda i: (0, 3)), pl.BlockSpec((bsz, RG_WIDTH), lambda i: (0, 4)),
                  full(s0), full(s1), full(s2), full(h0), full(conv_w), full(cb)] + [full(w) for w in wts],
        out_specs=(full(out), full(out)),
        compiler_params=_cparams(("arbitrary",)),
        name="rg_sample",
    )(proj, proj, s0, s1, s2, h0, conv_w, cb, *wts)
    conv_new = jnp.stack([s1, s2, proj[:, 4 * RG_WIDTH:5 * RG_WIDTH]], axis=1)
    return y, conv_new, h1


def _nsa_split(proj, b, L, q_norm, k_norm):
    q = proj[:, :NSA_Q_WIDTH]
    kv = proj[:, NSA_Q_WIDTH:NSA_Q_WIDTH + 6 * NSA_KV_WIDTH]
    g = proj[:, NSA_Q_WIDTH + 6 * NSA_KV_WIDTH:NSA_Q_WIDTH + 6 * NSA_KV_WIDTH + 3 * NSA_HEADS]
    q = _rmsnorm(q.reshape(b, L, NSA_KV_HEADS, NSA_GROUP, NSA_HEAD_DIM), q_norm) * (NSA_HEAD_DIM ** -0.5)
    kv = kv.reshape(b, L, 6, NSA_KV_HEADS, NSA_HEAD_DIM)
    k_slc = _rmsnorm(kv[:, :, 2], k_norm[1])
    k_win = _rmsnorm(kv[:, :, 4], k_norm[2])
    rows = jnp.stack([kv[:, :, 0], kv[:, :, 1], k_slc, kv[:, :, 3]], axis=2)
    win = jnp.stack([k_win, kv[:, :, 5]], axis=2)
    gates = jax.nn.sigmoid(g).reshape(b, L, NSA_KV_HEADS, NSA_GROUP, 3)
    return q, rows, win, gates


def _masked_softmax(s, mask):
    s = jnp.where(mask, s.astype(F32), -jnp.inf)
    m = jnp.max(s, axis=-1, keepdims=True)
    e = jnp.exp(s - jnp.where(jnp.isfinite(m), m, 0.0))
    d = jnp.sum(e, axis=-1, keepdims=True)
    return e / jnp.where(d > 0, d, 1.0)


def _compress(r, w1, pe, w2):
    b, T, G, dh = r.shape
    n_chunk = T // CMP_STRIDE
    nc = n_chunk - CMP_R + 1
    ch = r[:, :n_chunk * CMP_STRIDE].reshape(b, n_chunk, CMP_STRIDE, G, dh)
    proj = jnp.einsum('bcsgd,rsdh->bcrgh', ch, w1.reshape(CMP_R, CMP_STRIDE, dh, CMP_HID))
    hid = jnp.einsum('ld,ldh->h', pe, w1)
    for rr in range(CMP_R):
        hid = hid + proj[:, rr:rr + nc, rr]
    return jax.nn.gelu(hid) @ w2


def _nsa_context(rows, cmp_w1, cmp_pe, cmp_w2, k_norm_cmp):
    b, T = rows.shape[:2]
    kc = _rmsnorm(_compress(rows[:, :, 0], cmp_w1[0], cmp_pe[0], cmp_w2[0]), k_norm_cmp)
    vc = _compress(rows[:, :, 1], cmp_w1[1], cmp_pe[1], cmp_w2[1])
    ns = -(-T // SEL_BLOCK)
    sel = jnp.pad(rows[:, :, 2:4], ((0, 0), (0, ns * SEL_BLOCK - T), (0, 0), (0, 0), (0, 0)))
    sel = sel.reshape(b, ns, SEL_BLOCK, 2, NSA_KV_HEADS, NSA_HEAD_DIM).transpose(3, 0, 4, 1, 2, 5)
    return kc, vc, sel[0], sel[1]


def _overlap_matrix(nc, ns):
    i = np.arange(nc)[:, None]
    j = np.arange(ns)[None, :]
    ov = (i * CMP_STRIDE < (j + 1) * SEL_BLOCK) & (i * CMP_STRIDE + CMP_LEN > j * SEL_BLOCK)
    return ov.astype(np.float32)


def _nsa_attend(q, gates, t_pos, kc, vc, ks, vs, kw, vw, w_pos):
    b, Q, G, K, dh = q.shape
    nc = kc.shape[1]
    ns = ks.shape[2]
    tq = t_pos[None, :, None, None, None]
    c_end = jnp.arange(nc) * CMP_STRIDE + CMP_LEN - 1
    p_c = _masked_softmax(jnp.einsum('bqgkd,bngd->bqgkn', q, kc), c_end <= tq)
    o_c = jnp.einsum('bqgkn,bngd->bqgkd', p_c, vc)
    imp = jnp.einsum('bqgkn,ns->bqgs', p_c, jnp.asarray(_overlap_matrix(nc, ns)))
    jj = jnp.arange(ns)[None, :]
    jt = (t_pos // SEL_BLOCK)[:, None]
    valid = jj <= jt
    forced = valid & ((jj == 0) | (jj == jt) | (jj == jt - 1))
    imp = jnp.where(forced[None, :, None], jnp.inf, jnp.where(valid[None, :, None], imp, -jnp.inf))
    _, idx = lax.top_k(imp, min(SEL_TOPN, ns))
    n = idx.shape[-1]
    bi = jnp.arange(b)[:, None, None, None]
    gi = jnp.arange(G)[None, None, :, None]
    k_sel = ks[bi, gi, idx].reshape(b, Q, G, n * SEL_BLOCK, dh)
    v_sel = vs[bi, gi, idx].reshape(b, Q, G, n * SEL_BLOCK, dh)
    kpos = (idx[..., None] * SEL_BLOCK + jnp.arange(SEL_BLOCK)).reshape(b, Q, G, 1, n * SEL_BLOCK)
    p_s = _masked_softmax(jnp.einsum('bqgkd,bqgmd->bqgkm', q, k_sel), kpos <= tq)
    o_s = jnp.einsum('bqgkm,bqgmd->bqgkd', p_s, v_sel)
    m_w = (w_pos <= tq) & (w_pos > tq - WINDOW) & (w_pos >= 0)
    p_w = _masked_softmax(jnp.einsum('bqgkd,bwgd->bqgkw', q, kw), m_w)
    o_w = jnp.einsum('bqgkw,bwgd->bqgkd', p_w, vw)
    o = gates[..., 0:1] * o_c + gates[..., 1:2] * o_s + gates[..., 2:3] * o_w
    return o.reshape(b, Q, G * K * dh)


def _nsa_prompt_core(proj, b, L, q_norm, k_norm, cmp_w1, cmp_pe, cmp_w2):
    q, rows, win, gates = _nsa_split(proj, b, L, q_norm, k_norm)
    kc, vc, ks, vs = _nsa_context(rows, cmp_w1, cmp_pe, cmp_w2, k_norm[0])
    win_pad = jnp.pad(win, ((0, 0), (WINDOW, 0), (0, 0), (0, 0), (0, 0)))

    def block(i):
        s = i * NSA_Q_BLOCK
        qb = lax.dynamic_slice_in_dim(q, s, NSA_Q_BLOCK, axis=1)
        gb = lax.dynamic_slice_in_dim(gates, s, NSA_Q_BLOCK, axis=1)
        wb = lax.dynamic_slice_in_dim(win_pad, s, WINDOW + NSA_Q_BLOCK, axis=1)
        t_pos = s + jnp.arange(NSA_Q_BLOCK)
        w_pos = s - WINDOW + jnp.arange(WINDOW + NSA_Q_BLOCK)
        return _nsa_attend(qb, gb, t_pos, kc, vc, ks, vs, wb[:, :, 0], wb[:, :, 1], w_pos)

    o = lax.map(block, jnp.arange(L // NSA_Q_BLOCK))
    o = jnp.moveaxis(o, 0, 1).reshape(b, L, NSA_Q_WIDTH)
    return o, rows, win[:, L - min(WINDOW, L):]


def _nsa_sample_core(proj, b, L, cache, page_table, win_buf, q_norm, k_norm, cmp_w1, cmp_pe, cmp_w2):
    q, rows, win, gates = _nsa_split(proj, b, L, q_norm, k_norm)
    past = page_table.shape[1] * cache.shape[1]
    past_rows = cache[page_table].reshape(b, past, 4, NSA_KV_HEADS, NSA_HEAD_DIM)
    kc, vc, ks, vs = _nsa_context(jnp.concatenate([past_rows, rows], axis=1),
                                  cmp_w1, cmp_pe, cmp_w2, k_norm[0])
    wb_len = win_buf.shape[1]
    wk = jnp.concatenate([win_buf, win], axis=1)
    t_pos = past + jnp.arange(L)
    w_pos = past - wb_len + jnp.arange(wb_len + L)
    o = _nsa_attend(q, gates, t_pos, kc, vc, ks, vs, wk[:, :, 0], wk[:, :, 1], w_pos)
    return o, rows, wk[:, L:]


NSA_TQ = 128
NSA_TK_SLC = 1024
NSA_NS_PAD = 64
SEL_BIAS = -16384.0
MASK_VALUE = -1e30


def _split_bf16(x):
    hi = x.astype(BF16)
    lo = (x - hi.astype(F32)).astype(BF16)
    return hi, lo


def _seg_rms_scale(x, seg, seg_t):
    hi, lo = _split_bf16(x * x)
    ss = jnp.dot(hi, seg, preferred_element_type=F32) + jnp.dot(lo, seg, preferred_element_type=F32)
    r = lax.rsqrt(ss * (1.0 / NSA_HEAD_DIM) + NORM_EPS)
    rh, rl = _split_bf16(r)
    return jnp.dot(rh, seg_t, preferred_element_type=F32) + jnp.dot(rl, seg_t, preferred_element_type=F32)


def _nsa_prep_body(seq_len, p_ref, wq_ref, wks_ref, wkw_ref, segq_ref, segqt_ref, segk_ref, segkt_ref,
                   q_ref, rows_ref, win_ref, kaug_ref, vslc_ref, kwin_ref, vwin_ref, gate_ref):
    tm = p_ref.shape[0]
    dh = NSA_HEAD_DIM
    q = p_ref[:, 0:NSA_Q_WIDTH]
    qn = q * _seg_rms_scale(q, segq_ref[...], segqt_ref[...]) * wq_ref[...]
    kv = [p_ref[:, NSA_Q_WIDTH + NSA_KV_WIDTH * j:NSA_Q_WIDTH + NSA_KV_WIDTH * (j + 1)] for j in range(6)]
    ksl = kv[2] * _seg_rms_scale(kv[2], segk_ref[...], segkt_ref[...]) * wks_ref[...]
    kwn = kv[4] * _seg_rms_scale(kv[4], segk_ref[...], segkt_ref[...]) * wkw_ref[...]
    rows_ref[...] = jnp.concatenate([kv[0], kv[1], ksl, kv[3]], axis=1)
    win_ref[...] = jnp.concatenate([kwn, kv[5]], axis=1)
    gates = jax.nn.sigmoid(p_ref[:, NSA_Q_WIDTH + 6 * NSA_KV_WIDTH:NSA_Q_WIDTH + 6 * NSA_KV_WIDTH + V7X_LANES])
    t0 = (pl.program_id(0) * tm) % seq_len
    tpos = t0 + lax.broadcasted_iota(jnp.int32, (tm, NSA_NS_PAD), 0)
    blk = lax.broadcasted_iota(jnp.int32, (tm, NSA_NS_PAD), 1)
    onehot = jnp.where(blk == lax.shift_right_logical(tpos, 6), 1.0, 0.0).astype(BF16)
    for g in range(NSA_KV_HEADS):
        sl = slice(g * dh, (g + 1) * dh)
        kaug_ref[0, g] = jnp.concatenate([ksl[:, sl].astype(BF16), onehot], axis=1)
        vslc_ref[0, g] = kv[3][:, sl].astype(BF16)
        kwin_ref[0, g] = kwn[:, sl].astype(BF16)
        vwin_ref[0, g] = kv[5][:, sl].astype(BF16)
        gate_ref[0, g] = gates if g == 0 else pltpu.roll(gates, V7X_LANES - 3 * NSA_GROUP * g, 1)
        for k in range(NSA_GROUP):
            c0 = (g * NSA_GROUP + k) * dh
            q_ref[0, g, k] = qn[:, c0:c0 + dh].astype(BF16)


def _head_segments(width):
    lane = np.arange(width)[:, None] // NSA_HEAD_DIM
    seg = (lane == np.arange(V7X_LANES)[None, :]).astype(np.float32)
    return jnp.asarray(seg, BF16), jnp.asarray(seg.T, BF16)


def nsa_prep(proj, b, seq_len, q_norm, k_norm, tm):
    t = proj.shape[0]
    G, K, dh = NSA_KV_HEADS, NSA_GROUP, NSA_HEAD_DIM
    wq = (jnp.tile(q_norm, NSA_HEADS) * (dh ** -0.5))[None, :]
    wks = jnp.tile(k_norm[1], G)[None, :]
    wkw = jnp.tile(k_norm[2], G)[None, :]
    segq, segqt = _head_segments(NSA_Q_WIDTH)
    segk, segkt = _head_segments(NSA_KV_WIDTH)
    tiles_per_seq = seq_len // tm
    bi = lambda i: i // tiles_per_seq
    ti = lambda i: i % tiles_per_seq
    full = lambda a: pl.BlockSpec(a.shape, lambda i: (0,) * a.ndim)
    out_shape = (jax.ShapeDtypeStruct((b, G, K, seq_len, dh), BF16),
                 jax.ShapeDtypeStruct((t, 4 * NSA_KV_WIDTH), F32),
                 jax.ShapeDtypeStruct((t, 2 * NSA_KV_WIDTH), F32),
                 jax.ShapeDtypeStruct((b, G, seq_len, 2 * dh), BF16),
                 jax.ShapeDtypeStruct((b, G, seq_len, dh), BF16),
                 jax.ShapeDtypeStruct((b, G, seq_len, dh), BF16),
                 jax.ShapeDtypeStruct((b, G, seq_len, dh), BF16),
                 jax.ShapeDtypeStruct((b, G, seq_len, V7X_LANES), F32))
    per_g = lambda w: pl.BlockSpec((1, G, tm, w), lambda i: (bi(i), 0, ti(i), 0))
    out_specs = (pl.BlockSpec((1, G, K, tm, dh), lambda i: (bi(i), 0, 0, ti(i), 0)),
                 pl.BlockSpec((tm, 4 * NSA_KV_WIDTH), lambda i: (i, 0)),
                 pl.BlockSpec((tm, 2 * NSA_KV_WIDTH), lambda i: (i, 0)),
                 per_g(2 * dh), per_g(dh), per_g(dh), per_g(dh), per_g(V7X_LANES))
    return pl.pallas_call(
        functools.partial(_nsa_prep_body, seq_len),
        out_shape=out_shape,
        grid=(t // tm,),
        in_specs=[pl.BlockSpec((tm, proj.shape[1]), lambda i: (i, 0)),
                  full(wq), full(wks), full(wkw), full(segq), full(segqt), full(segk), full(segkt)],
        out_specs=out_specs,
        compiler_params=_cparams(("arbitrary",), V7X_VMEM_LIMIT_BYTES),
        name="nsa_prep",
    )(proj, wq, wks, wkw, segq, segqt, segk, segkt)


def _gelu_tanh(x):
    return 0.5 * x * (1.0 + jnp.tanh(math.sqrt(2.0 / math.pi) * (x + 0.044715 * (x * x * x))))


def _nsa_compress_body(n_chunk, x0_ref, x1_ref, x2_ref, x3_ref, wk_ref, wv_ref, pe_ref, w1f_ref, w2_ref, kn_ref,
                       kc_ref, vc_ref, pk_scr, pv_scr):
    s = pl.program_id(1)

    @pl.when(s == 0)
    def _():
        pk_scr[...] = jnp.zeros_like(pk_scr)
        pv_scr[...] = jnp.zeros_like(pv_scr)

    xs = [r[pl.ds(s, n_chunk, stride=CMP_STRIDE), :].astype(BF16) for r in (x0_ref, x1_ref, x2_ref, x3_ref)]
    pk_scr[...] += jnp.dot(jnp.concatenate(xs[0:2], axis=1), wk_ref[0], preferred_element_type=F32)
    pv_scr[...] += jnp.dot(jnp.concatenate(xs[2:4], axis=1), wv_ref[0], preferred_element_type=F32)

    @pl.when(s == CMP_STRIDE - 1)
    def _():
        for kv, p_scr, o_ref in ((0, pk_scr, kc_ref), (1, pv_scr, vc_ref)):
            p = p_scr[...]
            p_next = pltpu.roll(p, n_chunk - 1, 0)
            pe_h = jnp.dot(pe_ref[kv], w1f_ref[kv], preferred_element_type=F32)[0:1, :]
            for g in range(NSA_KV_HEADS):
                c0 = g * 2 * CMP_HID
                hid = pe_h + p[:, c0:c0 + CMP_HID] + p_next[:, c0 + CMP_HID:c0 + 2 * CMP_HID]
                y = jnp.dot(_gelu_tanh(hid).astype(BF16), w2_ref[kv], preferred_element_type=F32)
                if kv == 0:
                    y = y * lax.rsqrt(jnp.mean(y * y, axis=-1, keepdims=True) + NORM_EPS) * kn_ref[...]
                o_ref[0, g] = y.astype(BF16)


def nsa_compress(rows, b, seq_len, cmp_w1, cmp_pe, cmp_w2, k_norm_cmp):
    G, dh = NSA_KV_HEADS, NSA_HEAD_DIM
    n_chunk = seq_len // CMP_STRIDE
    w1 = cmp_w1.reshape(2, CMP_R, CMP_STRIDE, dh, CMP_HID)
    eye = jnp.eye(G, dtype=F32)
    wbd = jnp.einsum('vrsdh,gq->vsgdqrh', w1, eye).reshape(2, CMP_STRIDE, G * dh, G * CMP_R * CMP_HID).astype(BF16)
    pe = jnp.broadcast_to(cmp_pe.reshape(2, 1, CMP_LEN * dh), (2, 8, CMP_LEN * dh)).astype(BF16)
    w1f = cmp_w1.reshape(2, CMP_LEN * dh, CMP_HID).astype(BF16)
    w2 = cmp_w2.astype(BF16)
    kn = k_norm_cmp[None, :]
    full = lambda a: pl.BlockSpec(a.shape, lambda bi, s: (0,) * a.ndim)
    return pl.pallas_call(
        functools.partial(_nsa_compress_body, n_chunk),
        out_shape=(jax.ShapeDtypeStruct((b, G, n_chunk, dh), BF16),
                   jax.ShapeDtypeStruct((b, G, n_chunk, dh), BF16)),
        grid=(b, CMP_STRIDE),
        in_specs=[pl.BlockSpec((seq_len, V7X_LANES), lambda bi, s: (bi, 0)),
                  pl.BlockSpec((seq_len, V7X_LANES), lambda bi, s: (bi, 1)),
                  pl.BlockSpec((seq_len, V7X_LANES), lambda bi, s: (bi, 2)),
                  pl.BlockSpec((seq_len, V7X_LANES), lambda bi, s: (bi, 3)),
                  pl.BlockSpec((1, G * dh, G * CMP_R * CMP_HID), lambda bi, s: (s, 0, 0)),
                  pl.BlockSpec((1, G * dh, G * CMP_R * CMP_HID), lambda bi, s: (s, 0, 0)),
                  full(pe), full(w1f), full(w2), full(kn)],
        out_specs=(pl.BlockSpec((1, G, n_chunk, dh), lambda bi, s: (bi, 0, 0, 0)),
                   pl.BlockSpec((1, G, n_chunk, dh), lambda bi, s: (bi, 0, 0, 0))),
        scratch_shapes=[pltpu.VMEM((n_chunk, G * CMP_R * CMP_HID), F32),
                        pltpu.VMEM((n_chunk, G * CMP_R * CMP_HID), F32)],
        compiler_params=_cparams(("arbitrary", "arbitrary"), V7X_VMEM_LIMIT_BYTES),
        name="nsa_compress",
    )(rows, rows, rows, rows, wbd[0], wbd[1], pe, w1f, w2, kn)


_NT_DIMS = (((1,), (1,)), ((), ()))


def _flash_branch(q2, k_ref, v_ref, n_tiles, tk, last_mask_fn):
    rows = q2.shape[0]

    def step(j, carry, mask_fn):
        m, l, acc = carry
        k0 = pl.multiple_of(j * tk, tk)
        k = k_ref[0, 0, pl.ds(k0, tk), :]
        v = v_ref[0, 0, pl.ds(k0, tk), :]
        s = lax.dot_general(q2, k, _NT_DIMS, preferred_element_type=F32)
        if mask_fn is not None:
            s = jnp.where(mask_fn(k0), s, MASK_VALUE)
        m_new = jnp.maximum(m, jnp.max(s, axis=-1, keepdims=True))
        alpha = jnp.exp(m - m_new)
        p = jnp.exp(s - m_new)
        l = alpha * l + jnp.sum(p, axis=-1, keepdims=True)
        acc = alpha * acc + jnp.dot(p.astype(BF16), v, preferred_element_type=F32)
        return m_new, l, acc

    init = (jnp.full((rows, 1), MASK_VALUE, F32), jnp.zeros((rows, 1), F32),
            jnp.zeros((rows, NSA_HEAD_DIM), F32))
    carry = lax.fori_loop(0, n_tiles - 1, lambda j, c: step(j, c, None), init)
    _, l, acc = step(n_tiles - 1, carry, last_mask_fn)
    return acc / l


def _nsa_attn_body(n_cmp, q_ref, kc_ref, vc_ref, kaug_ref, vslc_ref, kwin_ref, vwin_ref, gate_ref, ovt_ref, o_ref):
    tq = NSA_TQ
    rows = NSA_GROUP * tq
    q0 = pl.program_id(2) * tq
    q2 = q_ref[0, 0].reshape(rows, NSA_HEAD_DIM)
    row_t = q0 + jnp.bitwise_and(lax.broadcasted_iota(jnp.int32, (rows, 1), 0), tq - 1)

    n_pad = kc_ref.shape[2]
    s = lax.dot_general(q2, kc_ref[0, 0], _NT_DIMS, preferred_element_type=F32)
    n_idx = lax.broadcasted_iota(jnp.int32, (1, n_pad), 1)
    cmask = jnp.logical_and(n_idx * CMP_STRIDE + (CMP_LEN - 1) <= row_t, n_idx < n_cmp)
    s = jnp.where(cmask, s, MASK_VALUE)
    m = jnp.max(s, axis=-1, keepdims=True)
    e = jnp.where(cmask, jnp.exp(s - m), 0.0)
    den = jnp.sum(e, axis=-1, keepdims=True)
    p_c = e / jnp.where(den > 0.0, den, 1.0)
    o_c = jnp.dot(p_c.astype(BF16), vc_ref[0, 0], preferred_element_type=F32)

    p_sum = (p_c[0:tq] + p_c[tq:2 * tq]) + (p_c[2 * tq:3 * tq] + p_c[3 * tq:4 * tq])
    ph, plo = _split_bf16(p_sum)
    ovt = ovt_ref[...]
    imp = (lax.dot_general(ovt, ph, _NT_DIMS, preferred_element_type=F32)
           + lax.dot_general(ovt, plo, _NT_DIMS, preferred_element_type=F32))
    blk = lax.broadcasted_iota(jnp.int32, (NSA_NS_PAD, tq), 0)
    jt = lax.shift_right_logical(q0 + lax.broadcasted_iota(jnp.int32, (NSA_NS_PAD, tq), 1), 6)
    valid = blk <= jt
    forced = jnp.logical_and(valid, jnp.logical_or(blk == 0, jnp.logical_or(blk == jt, blk == jt - 1)))
    eff = jnp.where(forced, jnp.inf, jnp.where(valid, imp, -jnp.inf))
    rank = jnp.zeros((NSA_NS_PAD, tq), jnp.int32)
    for j in range(NSA_NS_PAD):
        other = eff[j:j + 1, :]
        ahead = jnp.logical_or(other > eff, jnp.logical_and(other == eff, blk > j))
        rank = rank + ahead.astype(jnp.int32)
    sel = jnp.logical_and(valid, rank < SEL_TOPN)
    sel_bias = jnp.where(sel, 0.0, SEL_BIAS).T.astype(BF16)

    q_aug = jnp.concatenate([q2, jnp.concatenate([sel_bias] * NSA_GROUP, axis=0)], axis=1)
    hi = (q0 + tq - 1) // NSA_TK_SLC + 1

    def slc_mask(k0):
        kpos = k0 + lax.broadcasted_iota(jnp.int32, (1, NSA_TK_SLC), 1)
        return kpos <= row_t

    o_s = _flash_branch(q_aug, kaug_ref, vslc_ref, hi, NSA_TK_SLC, slc_mask)

    span = WINDOW + tq
    w0 = pl.multiple_of(jnp.maximum(q0 - WINDOW, 0), tq)
    kw = kwin_ref[0, 0, pl.ds(w0, span), :]
    vw = vwin_ref[0, 0, pl.ds(w0, span), :]
    s_w = lax.dot_general(q2, kw, _NT_DIMS, preferred_element_type=F32)
    kpos = w0 + lax.broadcasted_iota(jnp.int32, (1, span), 1)
    wmask = jnp.logical_and(kpos <= row_t, kpos > row_t - WINDOW)
    s_w = jnp.where(wmask, s_w, MASK_VALUE)
    p_w = jnp.exp(s_w - jnp.max(s_w, axis=-1, keepdims=True))
    o_w = (jnp.dot(p_w.astype(BF16), vw, preferred_element_type=F32)
           / jnp.sum(p_w, axis=-1, keepdims=True))

    gt = gate_ref[0, 0]
    outs = []
    for k in range(NSA_GROUP):
        r = slice(k * tq, (k + 1) * tq)
        outs.append(gt[:, 3 * k:3 * k + 1] * o_c[r] + gt[:, 3 * k + 1:3 * k + 2] * o_s[r]
                    + gt[:, 3 * k + 2:3 * k + 3] * o_w[r])
    o_ref[...] = jnp.concatenate(outs, axis=1)


def nsa_attention(q, kc, vc, kaug, vslc, kwin, vwin, gates, b, seq_len):
    G, K, dh = NSA_KV_HEADS, NSA_GROUP, NSA_HEAD_DIM
    tq = NSA_TQ
    nq = seq_len // tq
    n_chunk = kc.shape[2]
    n_cmp = n_chunk - CMP_R + 1
    ns = seq_len // SEL_BLOCK
    ovt = np.zeros((NSA_NS_PAD, n_chunk), np.float32)
    ovt[:ns, :n_cmp] = _overlap_matrix(n_cmp, ns).T
    ovt = jnp.asarray(ovt, BF16)
    seq_spec = lambda w: pl.BlockSpec((1, 1, seq_len, w), lambda bi, g, qi: (bi, g, 0, 0))
    return pl.pallas_call(
        functools.partial(_nsa_attn_body, n_cmp),
        out_shape=jax.ShapeDtypeStruct((b * seq_len, NSA_Q_WIDTH), F32),
        grid=(b, G, nq),
        in_specs=[pl.BlockSpec((1, 1, K, tq, dh), lambda bi, g, qi: (bi, g, 0, qi, 0)),
                  pl.BlockSpec((1, 1, n_chunk, dh), lambda bi, g, qi: (bi, g, 0, 0)),
                  pl.BlockSpec((1, 1, n_chunk, dh), lambda bi, g, qi: (bi, g, 0, 0)),
                  seq_spec(2 * dh), seq_spec(dh), seq_spec(dh), seq_spec(dh),
                  pl.BlockSpec((1, 1, tq, V7X_LANES), lambda bi, g, qi: (bi, g, qi, 0)),
                  pl.BlockSpec(ovt.shape, lambda bi, g, qi: (0, 0))],
        out_specs=pl.BlockSpec((tq, K * dh), lambda bi, g, qi: (bi * nq + qi, g)),
        compiler_params=_cparams(("arbitrary", "arbitrary", "arbitrary"), V7X_VMEM_LIMIT_BYTES),
        name="nsa_attention",
    )(q, kc, vc, kaug, vslc, kwin, vwin, gates, ovt)


def nsa_prompt_pallas(proj, b, seq_len, q_norm, k_norm, cmp_w1, cmp_pe, cmp_w2):
    q, rows, win, kaug, vslc, kwin, vwin, gates = nsa_prep(proj, b, seq_len, q_norm, k_norm, ROW_TILE)
    kc, vc = nsa_compress(rows, b, seq_len, cmp_w1, cmp_pe, cmp_w2, k_norm[0])
    o = nsa_attention(q, kc, vc, kaug, vslc, kwin, vwin, gates, b, seq_len)
    rows_out = rows.reshape(b, seq_len, 4, NSA_KV_HEADS, NSA_HEAD_DIM)
    wlen = min(WINDOW, seq_len)
    win_out = win.reshape(b, seq_len, 2, NSA_KV_HEADS, NSA_HEAD_DIM)[:, seq_len - wlen:]
    return o, rows_out, win_out


def _diag_heads(o_full):
    g_row = lax.shift_right_logical(lax.broadcasted_iota(jnp.int32, (NSA_HEADS, 1), 0), 2)
    out = jnp.zeros((NSA_HEADS, NSA_HEAD_DIM), F32)
    for g in range(NSA_KV_HEADS):
        out = out + jnp.where(g_row == g, o_full[:, g * NSA_HEAD_DIM:(g + 1) * NSA_HEAD_DIM], 0.0)
    return out


def _nsa_sample_body(n_pages, page_rows, pt_ref,
                     cache_hbm, qbd_ref, rown_ref, winn_ref, winbuf_ref, wc_ref, pe_ref, w1f_ref, w2_ref, kn_ref,
                     gsum_ref, ovs_ref, rep_ref, eblk_ref,
                     oc_ref, os_ref, ow_ref,
                     cmpt_buf, slct_buf, cmp_buf, sem):
    i = pl.program_id(0)
    nb = pl.num_programs(0)
    slot = i % 2
    past = n_pages * page_rows
    n_chunk = past // CMP_STRIDE
    n_cmp = n_chunk - CMP_R + 1
    t_pos = past
    kvw = NSA_KV_WIDTH

    def page_copies(bi, sl):
        copies = []
        for p in range(n_pages):
            pg = pt_ref[bi, p]
            copies.append(pltpu.make_async_copy(cache_hbm.at[pg, pl.ds(0, 2 * kvw), :],
                                                cmpt_buf.at[sl, p], sem.at[sl]))
            copies.append(pltpu.make_async_copy(cache_hbm.at[pg, pl.ds(2 * kvw, 2 * kvw), :],
                                                slct_buf.at[sl, :, pl.ds(p * page_rows, page_rows)], sem.at[sl]))
        return copies

    @pl.when(i == 0)
    def _():
        for c in page_copies(0, 0):
            c.start()

    @pl.when(i + 1 < nb)
    def _():
        for c in page_copies(i + 1, 1 - slot):
            c.start()

    for c in page_copies(i, slot):
        c.wait()

    for p in range(n_pages):
        for j in range(4):
            cmp_buf[j, p * page_rows:(p + 1) * page_rows, :] = (
                cmpt_buf[slot, p, j * V7X_LANES:(j + 1) * V7X_LANES, :].T)

    parts = []
    for j in range(4):
        acc = None
        for s in range(CMP_STRIDE):
            xs = cmp_buf[j, pl.ds(s, n_chunk, stride=CMP_STRIDE), :].astype(BF16)
            d = jnp.dot(xs, wc_ref[j // 2, s], preferred_element_type=F32)
            acc = d if acc is None else acc + d
        parts.append(acc)
    slabs = []
    for kv in range(2):
        p = jnp.concatenate(parts[2 * kv:2 * kv + 2], axis=1)
        p_next = pltpu.roll(p, n_chunk - 1, 0)
        pe_h = jnp.dot(pe_ref[kv], w1f_ref[kv], preferred_element_type=F32)[0:1, :]
        ys = []
        for g in range(NSA_KV_HEADS):
            c0 = g * 2 * CMP_HID
            hid = pe_h + p[:, c0:c0 + CMP_HID] + p_next[:, c0 + CMP_HID:c0 + 2 * CMP_HID]
            y = jnp.dot(_gelu_tanh(hid).astype(BF16), w2_ref[kv], preferred_element_type=F32)
            if kv == 0:
                y = y * lax.rsqrt(jnp.mean(y * y, axis=-1, keepdims=True) + NORM_EPS) * kn_ref[...]
            ys.append(y)
        slabs.append(jnp.concatenate(ys, axis=1).astype(BF16))
    kc, vc = slabs

    qbd = qbd_ref[0]
    qf = qbd.astype(F32)

    s_c = lax.dot_general(qbd, kc, _NT_DIMS, preferred_element_type=F32)
    n_idx = lax.broadcasted_iota(jnp.int32, (1, n_chunk), 1)
    cmask = jnp.logical_and(n_idx * CMP_STRIDE + (CMP_LEN - 1) <= t_pos, n_idx < n_cmp)
    s_c = jnp.where(cmask, s_c, MASK_VALUE)
    m = jnp.max(s_c, axis=-1, keepdims=True)
    e = jnp.where(cmask, jnp.exp(s_c - m), 0.0)
    den = jnp.sum(e, axis=-1, keepdims=True)
    p_c = e / jnp.where(den > 0.0, den, 1.0)
    oc_ref[0] = _diag_heads(jnp.dot(p_c.astype(BF16), vc, preferred_element_type=F32))

    gsum = gsum_ref[...]
    ph, plo = _split_bf16(p_c)
    p_sum = jnp.dot(gsum, ph, preferred_element_type=F32) + jnp.dot(gsum, plo, preferred_element_type=F32)
    sh, slo = _split_bf16(p_sum)
    ovs = ovs_ref[...]
    imp = jnp.dot(sh, ovs, preferred_element_type=F32) + jnp.dot(slo, ovs, preferred_element_type=F32)
    blk = lax.broadcasted_iota(jnp.int32, imp.shape, 1)
    jt = t_pos // SEL_BLOCK
    valid = blk <= jt
    forced = jnp.logical_and(valid, jnp.logical_or(blk == 0, jnp.logical_or(blk == jt, blk == jt - 1)))
    eff = jnp.where(forced, jnp.inf, jnp.where(valid, imp, -jnp.inf))
    rank = jnp.zeros(imp.shape, jnp.int32)
    for j in range(jt + 1):
        other = eff[:, j:j + 1]
        ahead = jnp.logical_or(other > eff, jnp.logical_and(other == eff, blk > j))
        rank = rank + ahead.astype(jnp.int32)
    sel = jnp.logical_and(valid, rank < SEL_TOPN)
    sel_bias = jnp.where(sel, 0.0, SEL_BIAS).astype(BF16)
    bias_h = jnp.dot(rep_ref[...], sel_bias, preferred_element_type=F32).astype(BF16)
    bias_keys = jnp.dot(bias_h, eblk_ref[...], preferred_element_type=F32)

    rn = rown_ref[0]
    ks_t = slct_buf[slot, 0:kvw, :].astype(BF16)
    vs_t = slct_buf[slot, kvw:2 * kvw, :].astype(BF16)
    s_s = jnp.dot(qbd, ks_t, preferred_element_type=F32) + bias_keys
    ks_new = rn[:, 2 * kvw:3 * kvw].astype(BF16).astype(F32)
    vs_new = rn[:, 3 * kvw:4 * kvw].astype(BF16).astype(F32)
    s_new = jnp.sum(qf * ks_new, axis=-1, keepdims=True)
    m = jnp.maximum(jnp.max(s_s, axis=-1, keepdims=True), s_new)
    p = jnp.exp(s_s - m)
    p_new = jnp.exp(s_new - m)
    den = jnp.sum(p, axis=-1, keepdims=True) + p_new
    o_full = (lax.dot_general(p.astype(BF16), vs_t, _NT_DIMS, preferred_element_type=F32)
              + p_new.astype(BF16).astype(F32) * vs_new)
    os_ref[0] = _diag_heads(o_full) / den

    wb = winbuf_ref[0]
    wn = winn_ref[0]
    wb_len = wb.shape[1]
    kw_t = wb[0:kvw, :].astype(BF16)
    vw_t = wb[kvw:2 * kvw, :].astype(BF16)
    s_w = jnp.dot(qbd, kw_t, preferred_element_type=F32)
    w_idx = lax.broadcasted_iota(jnp.int32, (1, wb_len), 1)
    w_pos = t_pos - wb_len + w_idx
    wmask = jnp.logical_and(w_pos > t_pos - WINDOW, w_pos >= 0)
    s_w = jnp.where(wmask, s_w, MASK_VALUE)
    kw_new = wn[:, 0:kvw].astype(BF16).astype(F32)
    vw_new = wn[:, kvw:2 * kvw].astype(BF16).astype(F32)
    s_new = jnp.sum(qf * kw_new, axis=-1, keepdims=True)
    m = jnp.maximum(jnp.max(s_w, axis=-1, keepdims=True), s_new)
    p = jnp.where(wmask, jnp.exp(s_w - m), 0.0)
    p_new = jnp.exp(s_new - m)
    den = jnp.sum(p, axis=-1, keepdims=True) + p_new
    o_full = (lax.dot_general(p.astype(BF16), vw_t, _NT_DIMS, preferred_element_type=F32)
              + p_new.astype(BF16).astype(F32) * vw_new)
    ow_ref[0] = _diag_heads(o_full) / den


def nsa_sample_attention(cache, page_table, win_buf, q, rows_new, win_new, cmp_w1, cmp_pe, cmp_w2, k_norm_cmp):
    G, K, dh = NSA_KV_HEADS, NSA_GROUP, NSA_HEAD_DIM
    bsz, n_pages = page_table.shape
    n_phys, page_rows = cache.shape[0], cache.shape[1]
    past = n_pages * page_rows
    n_chunk = past // CMP_STRIDE
    n_cmp = n_chunk - CMP_R + 1
    ns = -(-(past + 1) // SEL_BLOCK)
    cache3 = jnp.transpose(cache, (0, 2, 3, 4, 1)).reshape(n_phys, 4 * G * dh, page_rows)
    wb_len = win_buf.shape[1]
    win3 = jnp.transpose(win_buf, (0, 2, 3, 4, 1)).reshape(bsz, 2 * G * dh, wb_len)
    qh = jnp.transpose(q[0], (2, 0, 1, 3)).astype(F32)
    qbd = jnp.einsum('bgkd,gq->bgkqd', qh, jnp.eye(G, dtype=F32)).reshape(bsz, G * K, G * dh).astype(BF16)
    w1 = cmp_w1.reshape(2, CMP_R, CMP_STRIDE, dh, CMP_HID)
    wc = jnp.einsum('vrsdh,pq->vspdqrh', w1, jnp.eye(2, dtype=F32)).reshape(
        2, CMP_STRIDE, 2 * dh, 2 * CMP_R * CMP_HID).astype(BF16)
    pe = jnp.broadcast_to(cmp_pe.reshape(2, 1, CMP_LEN * dh), (2, 8, CMP_LEN * dh)).astype(BF16)
    w1f = cmp_w1.reshape(2, CMP_LEN * dh, CMP_HID).astype(BF16)
    w2 = cmp_w2.astype(BF16)
    kn = k_norm_cmp[None, :]
    gsum = np.zeros((8, G * K), np.float32)
    gsum[np.arange(G * K) // K, np.arange(G * K)] = 1.0
    ovs = np.zeros((n_chunk, NSA_NS_PAD), np.float32)
    ovs[:n_cmp, :ns] = _overlap_matrix(n_cmp, ns)
    eblk = (np.arange(NSA_NS_PAD)[:, None] == (np.arange(past)[None, :] // SEL_BLOCK)).astype(np.float32)
    gsum, ovs, eblk = jnp.asarray(gsum, BF16), jnp.asarray(ovs, BF16), jnp.asarray(eblk, BF16)
    rep = gsum.T
    full = lambda a: pl.BlockSpec(a.shape, lambda i, pt: (0,) * a.ndim)
    grid_spec = pltpu.PrefetchScalarGridSpec(
        num_scalar_prefetch=1,
        grid=(bsz,),
        in_specs=[pl.BlockSpec(memory_space=pl.ANY),
                  pl.BlockSpec((1, G * K, G * dh), lambda i, pt: (i, 0, 0)),
                  pl.BlockSpec((1, 1, 4 * G * dh), lambda i, pt: (i, 0, 0)),
                  pl.BlockSpec((1, 1, 2 * G * dh), lambda i, pt: (i, 0, 0)),
                  pl.BlockSpec((1, 2 * G * dh, wb_len), lambda i, pt: (i, 0, 0)),
                  full(wc), full(pe), full(w1f), full(w2), full(kn), full(gsum), full(ovs), full(rep), full(eblk)],
        out_specs=[pl.BlockSpec((1, G * K, dh), lambda i, pt: (i, 0, 0))] * 3,
        scratch_shapes=[pltpu.VMEM((2, n_pages, 2 * G * dh, page_rows), F32),
                        pltpu.VMEM((2, 2 * G * dh, past), F32),
                        pltpu.VMEM((4, past, V7X_LANES), F32),
                        pltpu.SemaphoreType.DMA((2,))],
    )
    out = jax.ShapeDtypeStruct((bsz, G * K, dh), F32)
    o_c, o_s, o_w = pl.pallas_call(
        functools.partial(_nsa_sample_body, n_pages, page_rows),
        out_shape=(out, out, out),
        grid_spec=grid_spec,
        compiler_params=_cparams(("arbitrary",), V7X_VMEM_LIMIT_BYTES),
        name="nsa_sample_attention",
    )(page_table, cache3, qbd, rows_new.reshape(bsz, 1, -1), win_new.reshape(bsz, 1, -1), win3,
      wc, pe, w1f, w2, kn, gsum, ovs, rep, eblk)
    return (o_c.reshape(bsz, -1), o_s.reshape(bsz, -1), o_w.reshape(bsz, -1))


def _mm_residual_gated_body(oc_ref, os_ref, ow_ref, gc_ref, gs_ref, gw_ref, w_ref, x_ref, g_ref, o_ref):
    a = gc_ref[...] * oc_ref[...] + gs_ref[...] * os_ref[...] + gw_ref[...] * ow_ref[...]
    acc = jnp.dot(a.astype(BF16), w_ref[...], preferred_element_type=F32)
    o_ref[...] = x_ref[...] + g_ref[0] * acc


def mm_residual_gated(branches, gates, w_bf, x, gate_mod):
    t, d = x.shape
    full2 = lambda a: pl.BlockSpec(a.shape, lambda i: (0,) * a.ndim)
    args = (*branches, *gates, w_bf, x, gate_mod)
    return pl.pallas_call(
        _mm_residual_gated_body,
        out_shape=jax.ShapeDtypeStruct((t, d), F32),
        grid=(1,),
        in_specs=[full2(a) for a in args],
        out_specs=pl.BlockSpec((t, d), lambda i: (0, 0)),
        compiler_params=_cparams(("arbitrary",)),
        name="mm_residual_gated",
    )(*args)


def nsa_sample_pallas(proj, bsz, cache, page_table, win_buf, q_norm, k_norm, cmp_w1, cmp_pe, cmp_w2):
    G, K, dh = NSA_KV_HEADS, NSA_GROUP, NSA_HEAD_DIM
    q, rows, win, _, _, _, _, gates = nsa_prep(proj, 1, bsz, q_norm, k_norm, bsz)
    branches = nsa_sample_attention(cache, page_table, win_buf, q, rows, win, cmp_w1, cmp_pe, cmp_w2, k_norm[0])
    g3 = jnp.transpose(gates[0, :, :, :3 * K], (1, 0, 2)).reshape(bsz, G, K, 3)
    gexp = [jnp.repeat(g3[..., br].reshape(bsz, G * K), dh, axis=1) for br in range(3)]
    rows_out = rows.reshape(bsz, 1, 4, G, dh)
    win_out = jnp.concatenate([win_buf[:, 1:], win.reshape(bsz, 1, 2, G, dh)], axis=1)
    return branches, gexp, rows_out, win_out


def _pad_cols(w, n):
    return jnp.pad(w, ((0, 0), (0, n - w.shape[1])))


def kernel(x_prompt, x_sample, cache_nsa_kv, state_nsa_win, state_ssd_conv, state_ssd, state_rg_conv, state_rg, page_table, c_prompt, c_sample, ada_w, ada_b, norm_mix, norm_ffn, rec_w_in, ssd_conv_w, ssd_conv_b, ssd_dt_bias, ssd_a_log, ssd_d, ssd_norm_w, rg_conv_w, rg_conv_b, rg_wa, rg_ba, rg_wi, rg_bi, rg_lambda, rec_w_out, nsa_w_in, nsa_q_norm, nsa_k_norm, cmp_w1, cmp_pe, cmp_w2, nsa_w_out, router_w, router_b, moe_w1, moe_b1, moe_w2, moe_b2):
    bp, L, d = x_prompt.shape
    bs = x_sample.shape[0]
    depth = ada_w.shape[0]
    tp = bp * L
    xp = x_prompt.reshape(tp, d)
    xs = x_sample.reshape(bs, d)

    n_c = bp + bs
    n_c_pad = -(-n_c // 8) * 8
    c_all = jnp.pad(jnp.concatenate([c_prompt, c_sample], axis=0), ((0, n_c_pad - n_c), (0, 0)))
    ada_w_cat = jnp.concatenate([ada_w[i] for i in range(depth)], axis=1).astype(BF16)
    ada_b_cat = jnp.concatenate([ada_b[i] for i in range(depth)], axis=0)[None, :]
    mod_all = adaln_mod(c_all, ada_w_cat, ada_b_cat)

    outs = {k: [] for k in ('rows_p', 'rows_s', 'win_p', 'win_s', 'sconv_p', 'sconv_s', 'ssm_p', 'ssm_s',
                            'rconv_p', 'rconv_s', 'rg_p', 'rg_s')}

    for i in range(depth):
        j = i // 2
        mod_i = mod_all[:, i * 6 * d:(i + 1) * 6 * d]
        mp = [mod_i[:bp, k * d:(k + 1) * d].reshape(bp, 1, d) for k in range(6)]
        ms = [mod_i[bp:bp + bs, k * d:(k + 1) * d].reshape(1, bs, d) for k in range(6)]
        g_mix = norm_mix[i][None, :]
        g_ffn = norm_ffn[i][None, :]

        if i % 2 == 0:
            w_in = rec_w_in[j]
            s0, s1, s2, s3 = 1024, 1024 + SSD_XBC, 1024 + SSD_XBC + SSD_HEADS, 1024 + SSD_XBC + SSD_HEADS + RG_WIDTH
            w_cat = jnp.concatenate([w_in[:, :s1], w_in[:, s2:], _pad_cols(w_in[:, s1:s2], 512)], axis=1).astype(BF16)
            proj_p = mod_matmul(xp, g_mix, mp[1], mp[0], w_cat, PROJ_ROW_TILE, 512)
            proj_s = mod_matmul(xs, g_mix, ms[1], ms[0], w_cat, bs, 512)

            ssd_w = (ssd_conv_w[j], ssd_conv_b[j], ssd_dt_bias[j], ssd_a_log[j], ssd_d[j], ssd_norm_w[j])
            rg_w = (rg_conv_w[j], rg_conv_b[j], rg_wa[j], rg_ba[j], rg_wi[j], rg_bi[j], rg_lambda[j])
            yp, a1, a2 = ssd_prompt(proj_p, bp, L, *ssd_w)
            rp, a3, a4 = rg_prompt(proj_p, bp, L, *rg_w)
            a4 = a4.reshape(bp, RG_WIDTH)
            ys, b1_, b2_ = ssd_sample(proj_s, state_ssd_conv[j], state_ssd[j], *ssd_w)
            rs, b3_, b4_ = rg_sample(proj_s, state_rg_conv[j], state_rg[j], *rg_w)
            outs['sconv_p'].append(a1); outs['ssm_p'].append(a2); outs['rconv_p'].append(a3); outs['rg_p'].append(a4)
            outs['sconv_s'].append(b1_); outs['ssm_s'].append(b2_); outs['rconv_s'].append(b3_); outs['rg_s'].append(b4_)
            w_out = rec_w_out[j].astype(BF16)
            w_parts = [w_out[:SSD_WIDTH], w_out[SSD_WIDTH:]]
            xp = mm_residual([yp.reshape(tp, -1), rp.reshape(tp, -1)], w_parts, xp, mp[2], ROW_TILE)
            xs = mm_residual([ys.reshape(bs, -1), rs.reshape(bs, -1)], w_parts, xs, ms[2], bs)
        else:
            w_in = nsa_w_in[j]
            w_cat = _pad_cols(w_in, 3072).astype(BF16)
            proj_p = mod_matmul(xp, g_mix, mp[1], mp[0], w_cat, PROJ_ROW_TILE, 512)
            proj_s = mod_matmul(xs, g_mix, ms[1], ms[0], w_cat, bs, 512)
            wts = (nsa_q_norm[j], nsa_k_norm[j], cmp_w1[j], cmp_pe[j], cmp_w2[j])
            op, rp, wp = nsa_prompt_pallas(proj_p, bp, L, *wts)
            br_s, gexp_s, rs, ws = nsa_sample_pallas(proj_s, bs, cache_nsa_kv[j], page_table, state_nsa_win[j], *wts)
            outs['rows_p'].append(rp); outs['win_p'].append(wp); outs['rows_s'].append(rs); outs['win_s'].append(ws)
            w_out = nsa_w_out[j].astype(BF16)
            xp = mm_residual([op.reshape(tp, -1)], [w_out], xp, mp[2], ROW_TILE)
            xs = mm_residual_gated(br_s, gexp_s, w_out, xs, ms[2])

        rw = _pad_cols(router_w[i], V7X_LANES)
        rwh = rw.astype(BF16)
        rwl = (rw - rwh.astype(F32)).astype(BF16)
        rb = jnp.concatenate([router_b[i], jnp.full((V7X_LANES - N_EXPERTS,), -1e30, F32)])[None, :]
        zero_base = jnp.zeros((1, V7X_LANES), F32)
        h_p, e_p, gt_p, rk_p, cnt_p = moe_router(xp, g_ffn, mp[4], mp[3], rwh, rwl, rb, zero_base, ROW_TILE)
        h_s, e_s, gt_s, rk_s, cnt_s = moe_router(xs, g_ffn, ms[4], ms[3], rwh, rwl, rb, cnt_p[-1], bs)
        h_all = jnp.concatenate([h_p, h_s], axis=0)
        t_all = tp + bs
        counts = cnt_s[-1, 0, :N_EXPERTS].astype(jnp.int32)
        padded, pad_start, blk_e, n_used, n_blocks = _moe_block_layout(counts, t_all * TOP_K)
        pstart = jnp.pad(pad_start.astype(F32), (0, V7X_LANES - N_EXPERTS))[None, :]
        dest = moe_dest(jnp.concatenate([e_p, e_s], axis=0), jnp.concatenate([rk_p, rk_s], axis=0), pstart, bs)
        idx = _moe_row_tables(dest[:, :TOP_K], counts, padded, pad_start, n_blocks, t_all)
        y4 = moe_ffn(h_all, blk_e, n_used, idx, moe_w1[i], moe_b1[i], moe_w2[i], moe_b2[i])
        xp = moe_combine(xp, mp[5], gt_p, y4, tp + bs, 0, bs)
        xs = moe_combine(xs, ms[5], gt_s, y4, tp + bs, tp, bs)

    st = lambda k: jnp.stack(outs[k])
    return (xp.reshape(bp, L, d), xs.reshape(bs, 1, d), st('rows_p'), st('rows_s'), st('win_p'), st('win_s'),
            st('sconv_p'), st('sconv_s'), st('ssm_p'), st('ssm_s'), st('rconv_p'), st('rconv_s'),
            st('rg_p'), st('rg_s'))
```

```python
import functools
import math

import jax
import jax.numpy as jnp
import numpy as np
from jax import lax
from jax.experimental import pallas as pl
from jax.experimental.pallas import tpu as pltpu

F32 = jnp.float32
BF16 = jnp.bfloat16

D_MODEL = 1024
NORM_EPS = 1e-6

SSD_WIDTH = 1024
SSD_HEAD_DIM = 64
SSD_HEADS = 16
SSD_GROUPS = 4
SSD_STATE = 128
SSD_CONV = 4
SSD_CHUNK = 128
SSD_XBC = SSD_WIDTH + 2 * SSD_GROUPS * SSD_STATE

RG_WIDTH = 1024
RG_BLOCKS = 16
RG_BLOCK_DIM = 64
RG_C = 8.0

NSA_HEADS = 16
NSA_KV_HEADS = 4
NSA_HEAD_DIM = 64
NSA_GROUP = 4
NSA_Q_WIDTH = 1024
NSA_KV_WIDTH = 256
CMP_LEN = 32
CMP_STRIDE = 16
CMP_R = 2
CMP_HID = 128
SEL_BLOCK = 64
SEL_TOPN = 16
WINDOW = 512
NSA_Q_BLOCK = 64

N_EXPERTS = 32
TOP_K = 4
D_FF = 1024
SWIGLU_LIMIT = 7.0
SWIGLU_ALPHA = 1.702

V7X_LANES = 128
V7X_VMEM_LIMIT_BYTES = 56 * 1024 * 1024

MOE_BLOCK_ROWS = 256
MOE_DMA_UNROLL = 16
ROW_TILE = 512
PROJ_ROW_TILE = 1024


def _cparams(sem, vmem=None):
    return pltpu.CompilerParams(dimension_semantics=sem, vmem_limit_bytes=vmem)


def _adaln_body(c_ref, w_ref, b_ref, o_ref):
    c = c_ref[...]
    s = c * jax.nn.sigmoid(c)
    o_ref[...] = jnp.dot(s.astype(BF16), w_ref[...], preferred_element_type=F32) + b_ref[...]


def adaln_mod(c, w_bf, b):
    r, d = c.shape
    n = w_bf.shape[1]
    tn = 1536
    return pl.pallas_call(
        _adaln_body,
        out_shape=jax.ShapeDtypeStruct((r, n), F32),
        grid=(n // tn,),
        in_specs=[pl.BlockSpec((r, d), lambda j: (0, 0)),
                  pl.BlockSpec((d, tn), lambda j: (0, j)),
                  pl.BlockSpec((1, tn), lambda j: (0, j))],
        out_specs=pl.BlockSpec((r, tn), lambda j: (0, j)),
        compiler_params=_cparams(("arbitrary",)),
        name="adaln_mod",
    )(c, w_bf, b)


def _modulated(x, g, scale, shift):
    ms = jnp.mean(x * x, axis=-1, keepdims=True)
    y = x * lax.rsqrt(ms + NORM_EPS) * g
    return y * (1.0 + scale) + shift


def _mod_matmul_body(x_ref, g_ref, sc_ref, sh_ref, w_ref, o_ref, h_scr):
    @pl.when(pl.program_id(1) == 0)
    def _():
        h_scr[...] = _modulated(x_ref[...], g_ref[...], sc_ref[0], sh_ref[0]).astype(BF16)

    o_ref[...] = jnp.dot(h_scr[...], w_ref[...], preferred_element_type=F32)


def mod_matmul(x, g, scale, shift, w_bf, tm, tn):
    t, d = x.shape
    n = w_bf.shape[1]
    m, r, _ = scale.shape
    rows_per_mod = t // m
    mod_spec = pl.BlockSpec((1, r, d), lambda i, j: ((i * tm) // rows_per_mod, 0, 0))
    return pl.pallas_call(
        _mod_matmul_body,
        out_shape=jax.ShapeDtypeStruct((t, n), F32),
        grid=(t // tm, n // tn),
        in_specs=[pl.BlockSpec((tm, d), lambda i, j: (i, 0)),
                  pl.BlockSpec((1, d), lambda i, j: (0, 0)),
                  mod_spec, mod_spec,
                  pl.BlockSpec((d, tn), lambda i, j: (0, j))],
        out_specs=pl.BlockSpec((tm, tn), lambda i, j: (i, j)),
        scratch_shapes=[pltpu.VMEM((tm, d), BF16)],
        compiler_params=_cparams(("arbitrary", "arbitrary")),
        name="mod_matmul",
    )(x, g, scale, shift, w_bf)


def _mm_residual_body(n_a, *refs):
    a_refs = refs[:n_a]
    w_refs = refs[n_a:2 * n_a]
    x_ref, g_ref, o_ref = refs[2 * n_a:]
    acc = None
    for a_ref, w_ref in zip(a_refs, w_refs):
        p = jnp.dot(a_ref[...].astype(BF16), w_ref[...], preferred_element_type=F32)
        acc = p if acc is None else acc + p
    o_ref[...] = x_ref[...] + g_ref[0] * acc


def mm_residual(a_list, w_list, x, gate, tm):
    t, d = x.shape
    m, r, _ = gate.shape
    rows_per_mod = t // m
    in_specs = [pl.BlockSpec((tm, a.shape[1]), lambda i: (i, 0)) for a in a_list]
    in_specs += [pl.BlockSpec(w.shape, lambda i: (0, 0)) for w in w_list]
    in_specs += [pl.BlockSpec((tm, d), lambda i: (i, 0)),
                 pl.BlockSpec((1, r, d), lambda i: ((i * tm) // rows_per_mod, 0, 0))]
    return pl.pallas_call(
        functools.partial(_mm_residual_body, len(a_list)),
        out_shape=jax.ShapeDtypeStruct((t, d), F32),
        grid=(t // tm,),
        in_specs=in_specs,
        out_specs=pl.BlockSpec((tm, d), lambda i: (i, 0)),
        compiler_params=_cparams(("arbitrary",)),
        name="mm_residual",
    )(*a_list, *w_list, x, gate)


def _router_body(x_ref, g_ref, sc_ref, sh_ref, wh_ref, wl_ref, rb_ref, base_ref, tri_ref,
                 h_ref, e_ref, gt_ref, rank_ref, cnt_ref, carry):
    h = _modulated(x_ref[...], g_ref[...], sc_ref[0], sh_ref[0])
    h_ref[...] = h
    hh = h.astype(BF16)
    hl = (h - hh.astype(F32)).astype(BF16)
    wh = wh_ref[...]
    wl = wl_ref[...]
    logits = (jnp.dot(hh, wh, preferred_element_type=F32)
              + (jnp.dot(hh, wl, preferred_element_type=F32)
                 + jnp.dot(hl, wh, preferred_element_type=F32))) + rb_ref[...]
    lane = lax.broadcasted_iota(jnp.int32, logits.shape, 1)
    neg = jnp.float32(-jnp.inf)
    vals, idxs = [], []
    cur = logits
    for _ in range(TOP_K):
        m = jnp.max(cur, axis=-1, keepdims=True)
        idx = jnp.min(jnp.where(cur == m, lane, V7X_LANES), axis=-1, keepdims=True)
        vals.append(m)
        idxs.append(idx)
        cur = jnp.where(lane == idx, neg, cur)
    exps = [jnp.exp(v - vals[0]) for v in vals]
    den = exps[0] + exps[1] + exps[2] + exps[3]
    e_out = jnp.zeros(logits.shape, jnp.int32)
    g_out = jnp.zeros(logits.shape, F32)
    hot = jnp.zeros(logits.shape, F32)
    for k in range(TOP_K):
        e_out = jnp.where(lane == k, idxs[k], e_out)
        g_out = jnp.where(lane == k, exps[k] / den, g_out)
        hot = hot + jnp.where(lane == idxs[k], 1.0, 0.0)
    e_ref[...] = e_out
    gt_ref[...] = g_out

    @pl.when(pl.program_id(0) == 0)
    def _():
        carry[...] = jnp.broadcast_to(base_ref[...], carry.shape)

    before = jnp.dot(tri_ref[...], hot.astype(BF16), preferred_element_type=F32) + carry[0:1, :]
    rank_out = jnp.zeros(logits.shape, F32)
    for k in range(TOP_K):
        r_k = jnp.sum(jnp.where(lane == idxs[k], before, 0.0), axis=-1, keepdims=True)
        rank_out = jnp.where(lane == k, r_k, rank_out)
    rank_ref[...] = rank_out
    total = carry[0:1, :] + jnp.sum(hot, axis=0, keepdims=True)
    carry[0:1, :] = total
    cnt_ref[0] = total


def moe_router(x, g, scale, shift, wh, wl, rb, base, tm):
    t, d = x.shape
    m, r, _ = scale.shape
    rows_per_mod = t // m
    mod_spec = pl.BlockSpec((1, r, d), lambda i: ((i * tm) // rows_per_mod, 0, 0))
    tri = jnp.asarray(np.tril(np.ones((tm, tm), np.float32), -1), BF16)
    lanes = lambda dt_: jax.ShapeDtypeStruct((t, V7X_LANES), dt_)
    lane_spec = pl.BlockSpec((tm, V7X_LANES), lambda i: (i, 0))
    return pl.pallas_call(
        _router_body,
        out_shape=(jax.ShapeDtypeStruct((t, d), F32), lanes(jnp.int32), lanes(F32), lanes(F32),
                   jax.ShapeDtypeStruct((t // tm, 1, V7X_LANES), F32)),
        grid=(t // tm,),
        in_specs=[pl.BlockSpec((tm, d), lambda i: (i, 0)),
                  pl.BlockSpec((1, d), lambda i: (0, 0)),
                  mod_spec, mod_spec,
                  pl.BlockSpec((d, V7X_LANES), lambda i: (0, 0)),
                  pl.BlockSpec((d, V7X_LANES), lambda i: (0, 0)),
                  pl.BlockSpec((1, V7X_LANES), lambda i: (0, 0)),
                  pl.BlockSpec((1, V7X_LANES), lambda i: (0, 0)),
                  pl.BlockSpec((tm, tm), lambda i: (0, 0))],
        out_specs=(pl.BlockSpec((tm, d), lambda i: (i, 0)), lane_spec, lane_spec, lane_spec,
                   pl.BlockSpec((1, 1, V7X_LANES), lambda i: (i, 0, 0))),
        scratch_shapes=[pltpu.VMEM((8, V7X_LANES), F32)],
        compiler_params=_cparams(("arbitrary",)),
        name="moe_router",
    )(x, g, scale, shift, wh, wl, rb, base, tri)


def _moe_dest_body(e_ref, rank_ref, pstart_ref, o_ref):
    e = e_ref[...]
    lane = lax.broadcasted_iota(jnp.int32, e.shape, 1)
    pstart = pstart_ref[...]
    out = jnp.zeros(e.shape, F32)
    for k in range(TOP_K):
        start_k = jnp.sum(jnp.where(lane == e[:, k:k + 1], pstart, 0.0), axis=-1, keepdims=True)
        out = jnp.where(lane == k, start_k, out)
    o_ref[...] = (out + rank_ref[...]).astype(jnp.int32)


def moe_dest(top_e, rank, pad_start, tm):
    t = top_e.shape[0]
    spec = pl.BlockSpec((tm, V7X_LANES), lambda i: (i, 0))
    return pl.pallas_call(
        _moe_dest_body,
        out_shape=jax.ShapeDtypeStruct((t, V7X_LANES), jnp.int32),
        grid=(t // tm,),
        in_specs=[spec, spec, pl.BlockSpec((1, V7X_LANES), lambda i: (0, 0))],
        out_specs=spec,
        compiler_params=_cparams(("arbitrary",)),
        name="moe_dest",
    )(top_e, rank, pad_start)


def _ffn_body(blk_e_ref, nused_ref,
              idx_hbm, h_hbm, w1_ref, b1_ref, w2_ref, b2_ref,
              out_hbm,
              idx_smem, xbuf, ybuf, w1bf, w2bf, sem_idx, sem_g, sem_s):
    bm = MOE_BLOCK_ROWS
    i = pl.program_id(0)
    n_used = nused_ref[0]

    def idx_copy(blk, slot):
        return pltpu.make_async_copy(idx_hbm.at[blk], idx_smem.at[slot], sem_idx.at[slot])

    def gather_copy(tok, slot, r):
        return pltpu.make_async_copy(h_hbm.at[pl.ds(tok, 1)], xbuf.at[slot, pl.ds(r, 1)], sem_g.at[slot])

    def scatter_copy(dst, slot, r):
        return pltpu.make_async_copy(ybuf.at[slot, pl.ds(r, 1)], out_hbm.at[pl.ds(dst, 1)], sem_s.at[slot])

    def start_gather(islot, slot):
        for r in range(bm):
            gather_copy(idx_smem[islot, r], slot, r).start()

    def wait_gather(slot):
        def body(r, c):
            gather_copy(0, slot, 0).wait()
            return c
        lax.fori_loop(0, bm, body, 0, unroll=MOE_DMA_UNROLL)

    def start_scatter(islot, slot):
        for r in range(bm):
            scatter_copy(idx_smem[islot, bm + r], slot, r).start()

    def wait_scatter(slot):
        def body(r, c):
            scatter_copy(0, slot, 0).wait()
            return c
        lax.fori_loop(0, bm, body, 0, unroll=MOE_DMA_UNROLL)

    @pl.when(i < n_used)
    def _():
        slot = i % 2
        islot = i % 3

        @pl.when(i == 0)
        def _():
            ybuf[...] = jnp.zeros_like(ybuf)
            for sl in range(2):
                tail = pltpu.make_async_copy(ybuf.at[sl], out_hbm.at[pl.ds(out_hbm.shape[0] - (2 - sl) * bm, bm)],
                                             sem_s.at[sl])
                tail.start()
                tail.wait()
            idx_copy(0, 0).start()
            idx_copy(0, 0).wait()
            start_gather(0, 0)

            @pl.when(n_used > 1)
            def _():
                idx_copy(1, 1).start()

        @pl.when(i + 2 < n_used)
        def _():
            idx_copy(i + 2, (i + 2) % 3).start()

        @pl.when(i + 1 < n_used)
        def _():
            idx_copy(i + 1, (i + 1) % 3).wait()
            start_gather((i + 1) % 3, 1 - slot)

        @pl.when(jnp.logical_or(i == 0, blk_e_ref[i] != blk_e_ref[jnp.maximum(i - 1, 0)]))
        def _():
            w1bf[...] = w1_ref[0].astype(BF16)
            w2bf[...] = w2_ref[0].astype(BF16)

        wait_gather(slot)

        @pl.when(i >= 2)
        def _():
            wait_scatter(slot)

        x = xbuf[slot].astype(BF16)
        u = jnp.dot(x, w1bf[...], preferred_element_type=F32) + b1_ref[0]
        gl = jnp.minimum(u[:, :D_FF], SWIGLU_LIMIT)
        lin = jnp.clip(u[:, D_FF:], -SWIGLU_LIMIT, SWIGLU_LIMIT)
        act = gl * jax.nn.sigmoid(SWIGLU_ALPHA * gl) * (lin + 1.0)
        ybuf[slot] = jnp.dot(act.astype(BF16), w2bf[...], preferred_element_type=F32) + b2_ref[0]
        start_scatter(islot, slot)

        @pl.when(i == n_used - 1)
        def _():
            @pl.when(i >= 1)
            def _():
                wait_scatter(1 - slot)
            wait_scatter(slot)


def moe_ffn(h_all, blk_e, n_used, idx, w1, b1, w2, b2):
    t, d = h_all.shape
    bm = MOE_BLOCK_ROWS
    n_blocks = idx.shape[0]
    grid_spec = pltpu.PrefetchScalarGridSpec(
        num_scalar_prefetch=2,
        grid=(n_blocks,),
        in_specs=[pl.BlockSpec(memory_space=pl.ANY),
                  pl.BlockSpec(memory_space=pl.ANY),
                  pl.BlockSpec((1, d, 2 * D_FF), lambda i, be, nu: (be[i], 0, 0)),
                  pl.BlockSpec((1, 1, 2 * D_FF), lambda i, be, nu: (be[i], 0, 0)),
                  pl.BlockSpec((1, D_FF, d), lambda i, be, nu: (be[i], 0, 0)),
                  pl.BlockSpec((1, 1, d), lambda i, be, nu: (be[i], 0, 0))],
        out_specs=pl.BlockSpec(memory_space=pl.ANY),
        scratch_shapes=[pltpu.SMEM((3, 2 * bm), jnp.int32),
                        pltpu.VMEM((2, bm, d), F32),
                        pltpu.VMEM((2, bm, d), F32),
                        pltpu.VMEM((d, 2 * D_FF), BF16),
                        pltpu.VMEM((D_FF, d), BF16),
                        pltpu.SemaphoreType.DMA((3,)),
                        pltpu.SemaphoreType.DMA((2,)),
                        pltpu.SemaphoreType.DMA((2,))],
    )
    return pl.pallas_call(
        _ffn_body,
        out_shape=jax.ShapeDtypeStruct((t * TOP_K + 2 * bm, d), F32),
        grid_spec=grid_spec,
        compiler_params=_cparams(("arbitrary",), V7X_VMEM_LIMIT_BYTES),
        name="moe_ffn",
    )(blk_e, n_used, idx, h_all, w1, b1.reshape(N_EXPERTS, 1, -1), w2, b2.reshape(N_EXPERTS, 1, -1))


def _moe_combine_body(x_ref, g_ref, rg_ref, y0_ref, y1_ref, y2_ref, y3_ref, o_ref):
    rg = rg_ref[...]
    acc = ((rg[:, 0:1] * y0_ref[...] + rg[:, 1:2] * y1_ref[...])
           + (rg[:, 2:3] * y2_ref[...] + rg[:, 3:4] * y3_ref[...]))
    o_ref[...] = x_ref[...] + g_ref[0] * acc


def moe_combine(x, gate, router_gate, y4, t_all, row_off, tm):
    t, d = x.shape
    m, r, _ = gate.shape
    rows_per_mod = t // m
    y_spec = lambda k: pl.BlockSpec((tm, d), lambda i: ((k * t_all + row_off) // tm + i, 0))
    return pl.pallas_call(
        _moe_combine_body,
        out_shape=jax.ShapeDtypeStruct((t, d), F32),
        grid=(t // tm,),
        in_specs=[pl.BlockSpec((tm, d), lambda i: (i, 0)),
                  pl.BlockSpec((1, r, d), lambda i: ((i * tm) // rows_per_mod, 0, 0)),
                  pl.BlockSpec((tm, V7X_LANES), lambda i: (i, 0)),
                  y_spec(0), y_spec(1), y_spec(2), y_spec(3)],
        out_specs=pl.BlockSpec((tm, d), lambda i: (i, 0)),
        compiler_params=_cparams(("arbitrary",)),
        name="moe_combine",
    )(x, gate, router_gate, y4, y4, y4, y4)


def _moe_block_layout(counts, tk):
    bm = MOE_BLOCK_ROWS
    padded = (counts + bm - 1) // bm * bm
    pad_end = jnp.cumsum(padded)
    pad_start = pad_end - padded
    n_blocks = -(-tk // bm) + N_EXPERTS
    blk_start = jnp.arange(n_blocks, dtype=jnp.int32) * bm
    blk_e = jnp.minimum(jnp.sum((pad_end[None, :] <= blk_start[:, None]).astype(jnp.int32), axis=1),
                        N_EXPERTS - 1)
    n_used = (pad_end[-1] // bm).astype(jnp.int32).reshape(1)
    return padded, pad_start, blk_e, n_used, n_blocks


def _moe_row_tables(dest, counts, padded, pad_start, n_blocks, t):
    bm = MOE_BLOCK_ROWS
    tk = t * TOP_K
    n_rows = n_blocks * bm
    big = jnp.int32(2 ** 30)
    p = jnp.arange(bm, dtype=jnp.int32)[None, :]
    e = jnp.arange(N_EXPERTS, dtype=jnp.int32)[:, None]
    pad_keys = jnp.where(p < (padded - counts)[:, None], (pad_start + counts)[:, None] + p, big + e * bm + p)
    n_fill = n_rows - tk - N_EXPERTS * bm
    keys = jnp.concatenate([dest.reshape(tk), pad_keys.reshape(-1), big + N_EXPERTS * bm + jnp.arange(n_fill, dtype=jnp.int32)])
    vals = jnp.concatenate([jnp.arange(tk, dtype=jnp.int32), jnp.full((n_rows - tk,), -1, jnp.int32)])
    _, v = lax.sort((keys, vals), num_keys=1)
    valid = v >= 0
    row = jnp.arange(n_rows, dtype=jnp.int32)
    src_tok = jnp.where(valid, v // TOP_K, 0)
    pad_row = tk + ((row // bm) % 2) * bm + row % bm
    dst_row = jnp.where(valid, (v % TOP_K) * t + v // TOP_K, pad_row)
    return jnp.concatenate([src_tok.reshape(n_blocks, bm), dst_row.reshape(n_blocks, bm)], axis=1)


CONV_TAIL = 8
RG_TILE = 256


def _split3_bf16(x):
    h = x.astype(BF16)
    r = x - h.astype(F32)
    m = r.astype(BF16)
    return h, m, (r - m.astype(F32)).astype(BF16)


def _dot3(parts, w, dims=None):
    if dims is None:
        outs = [jnp.dot(p, w, preferred_element_type=F32) for p in parts]
    else:
        outs = [lax.dot_general(w, p, dims, preferred_element_type=F32) for p in parts]
    return (outs[0] + outs[1]) + outs[2]


def _softplus(x):
    return jnp.maximum(x, 0.0) + jnp.log(1.0 + jnp.exp(-jnp.abs(x)))


def _silu(x):
    return x * jax.nn.sigmoid(x)


def _group_rmsnorm(y, w, n_groups):
    width = y.shape[1] // n_groups
    outs = []
    for g in range(n_groups):
        yg = y[:, g * width:(g + 1) * width]
        outs.append(yg * lax.rsqrt(jnp.mean(yg * yg, axis=-1, keepdims=True) + NORM_EPS))
    return jnp.concatenate(outs, axis=1) * w


def _conv_tile(xbuf, cw_ref, cb_ref, rows):
    y = cb_ref[...]
    for k in range(SSD_CONV):
        y = y + cw_ref[k:k + 1, :] * xbuf[pl.ds(CONV_TAIL - (SSD_CONV - 1) + k, rows), :]
    return y


def _ssd_prompt_body(z_ref, xs_ref, bc_ref, dt_ref, cw_ref, cb_ref, dtb_ref, a_ref, dexp_ref, nw_ref, tri_ref,
                     y_ref, conv_ref, state_ref, xbuf, h_scr):
    q = SSD_CHUNK
    c = pl.program_id(1)
    last = pl.num_programs(1) - 1
    P, N = SSD_HEAD_DIM, SSD_STATE

    @pl.when(c == 0)
    def _():
        xbuf[0:CONV_TAIL, :] = jnp.zeros((CONV_TAIL, SSD_XBC), F32)
        h_scr[...] = jnp.zeros_like(h_scr)

    xbuf[CONV_TAIL:CONV_TAIL + q, 0:SSD_WIDTH] = xs_ref[...]
    xbuf[CONV_TAIL:CONV_TAIL + q, SSD_WIDTH:SSD_XBC] = bc_ref[...]
    xc = _silu(_conv_tile(xbuf, cw_ref, cb_ref, q))

    @pl.when(c == last)
    def _():
        conv_ref[0] = xbuf[CONV_TAIL + q - (SSD_CONV - 1):CONV_TAIL + q, :]

    xbuf[0:CONV_TAIL, :] = xbuf[q:q + CONV_TAIL, :]

    xs = xc[:, 0:SSD_WIDTH]
    bm = xc[:, SSD_WIDTH:SSD_WIDTH + SSD_GROUPS * N].astype(BF16)
    cm = xc[:, SSD_WIDTH + SSD_GROUPS * N:SSD_XBC].astype(BF16)
    dt = _softplus(dt_ref[...] + dtb_ref[...])
    a = dt * a_ref[...]
    a_cs = _dot3(_split3_bf16(a), tri_ref[...], dims=(((1,), (0,)), ((), ())))
    a_cs_t = a_cs.T
    dt_t = dt.T
    a_end_t = a_cs_t[:, q - 1:q]
    w_t = dt_t * jnp.exp(a_end_t - a_cs_t)
    ea = jnp.exp(a_cs)
    xs_t = xs.T
    row = lax.broadcasted_iota(jnp.int32, (q, q), 0)
    col = lax.broadcasted_iota(jnp.int32, (q, q), 1)
    causal = col <= row
    heads_per_group = SSD_HEADS // SSD_GROUPS
    ys = []
    for g in range(SSD_GROUPS):
        bg = bm[:, g * N:(g + 1) * N]
        cg = cm[:, g * N:(g + 1) * N]
        cb = lax.dot_general(cg, bg, _NT_DIMS_SSD, preferred_element_type=F32)
        for k in range(heads_per_group):
            h = g * heads_per_group + k
            seg = a_cs[:, h:h + 1] - a_cs_t[h:h + 1, :]
            decay = jnp.exp(jnp.where(causal, seg, MASK_NEG))
            xh = xs[:, h * P:(h + 1) * P]
            xdt = (xh * dt[:, h:h + 1]).astype(BF16)
            y_diag = jnp.dot((cb * decay).astype(BF16), xdt, preferred_element_type=F32)
            h_prev = h_scr[h]
            y_off = lax.dot_general(cg, h_prev.astype(BF16), _NT_DIMS_SSD,
                                    preferred_element_type=F32) * ea[:, h:h + 1]
            st = jnp.dot((xs_t[h * P:(h + 1) * P, :] * w_t[h:h + 1, :]).astype(BF16), bg,
                         preferred_element_type=F32)
            h_scr[h] = h_prev * jnp.exp(a_end_t[h:h + 1, :]) + st
            ys.append(y_diag + y_off)
    y = jnp.concatenate(ys, axis=1) + dexp_ref[...] * xs
    y = y * _silu(z_ref[...])
    y_ref[...] = _group_rmsnorm(y, nw_ref[...], SSD_GROUPS)

    @pl.when(c == last)
    def _():
        state_ref[0] = h_scr[...]


_NT_DIMS_SSD = (((1,), (1,)), ((), ()))
MASK_NEG = -1e30


def _pad_lanes(v, n=V7X_LANES):
    return jnp.pad(v, (0, n - v.shape[0]))[None, :]


def ssd_prompt(proj, b, seq_len, conv_w, conv_b, dt_bias, a_log, d_skip, norm_w):
    q = SSD_CHUNK
    nc = seq_len // q
    t = b * seq_len
    tri = jnp.asarray(np.tril(np.ones((q, q), np.float32)), BF16)
    a_neg = _pad_lanes(-jnp.exp(a_log))
    dtb = _pad_lanes(dt_bias)
    dexp = jnp.repeat(d_skip, SSD_HEAD_DIM)[None, :]
    nw = norm_w[None, :]
    cb = conv_b[None, :]
    colblk = lambda j, w=SSD_WIDTH: pl.BlockSpec((q, w), lambda bi, c: (bi * nc + c, j))
    full = lambda a: pl.BlockSpec(a.shape, lambda bi, c: (0,) * a.ndim)
    return pl.pallas_call(
        _ssd_prompt_body,
        out_shape=(jax.ShapeDtypeStruct((t, SSD_WIDTH), F32),
                   jax.ShapeDtypeStruct((b, SSD_CONV - 1, SSD_XBC), F32),
                   jax.ShapeDtypeStruct((b, SSD_HEADS, SSD_HEAD_DIM, SSD_STATE), F32)),
        grid=(b, nc),
        in_specs=[colblk(0), colblk(1), colblk(2),
                  pl.BlockSpec((q, V7X_LANES), lambda bi, c: (bi * nc + c, 5 * SSD_WIDTH // V7X_LANES)),
                  full(conv_w), full(cb), full(dtb), full(a_neg), full(dexp), full(nw), full(tri)],
        out_specs=(pl.BlockSpec((q, SSD_WIDTH), lambda bi, c: (bi * nc + c, 0)),
                   pl.BlockSpec((1, SSD_CONV - 1, SSD_XBC), lambda bi, c: (bi, 0, 0)),
                   pl.BlockSpec((1, SSD_HEADS, SSD_HEAD_DIM, SSD_STATE), lambda bi, c: (bi, 0, 0, 0))),
        scratch_shapes=[pltpu.VMEM((CONV_TAIL + q, SSD_XBC), F32),
                        pltpu.VMEM((SSD_HEADS, SSD_HEAD_DIM, SSD_STATE), F32)],
        compiler_params=_cparams(("arbitrary", "arbitrary"), V7X_VMEM_LIMIT_BYTES),
        name="ssd_prompt",
    )(proj, proj, proj, proj, conv_w, cb, dtb, a_neg, dexp, nw, tri)


def _ssd_sample_pre_body(xs_ref, bc_ref, dt_ref, s0_ref, s1_ref, s2_ref, cw_ref, cb_ref, dtb_ref, a_ref, dexp_ref,
                         hexp_ref, yd_ref, xdt_ref, b_ref, c_ref, ea_ref, eaexp_ref):
    N = SSD_STATE
    x_new = jnp.concatenate([xs_ref[...], bc_ref[...]], axis=1)
    y = (cb_ref[...] + cw_ref[0:1, :] * s0_ref[...] + cw_ref[1:2, :] * s1_ref[...]
         + cw_ref[2:3, :] * s2_ref[...] + cw_ref[3:4, :] * x_new)
    xc = _silu(y)
    xs = xc[:, 0:SSD_WIDTH]
    bm = xc[:, SSD_WIDTH:SSD_WIDTH + SSD_GROUPS * N]
    cm = xc[:, SSD_WIDTH + SSD_GROUPS * N:SSD_XBC]
    dt = _softplus(dt_ref[...] + dtb_ref[...])
    ea = jnp.exp(dt * a_ref[...])
    hexp = hexp_ref[...]
    dt_exp = _dot3(_split3_bf16(dt), hexp)
    xdt = (xs * dt_exp).astype(BF16)
    bb = bm.astype(BF16)
    cc = cm.astype(BF16)
    prod = bb.astype(F32) * cc.astype(F32)
    hw = SSD_WIDTH // SSD_GROUPS
    cb = jnp.concatenate(
        [jnp.broadcast_to(jnp.sum(prod[:, g * N:(g + 1) * N], axis=-1, keepdims=True), (xs.shape[0], hw))
         for g in range(SSD_GROUPS)], axis=1)
    yd_ref[...] = cb.astype(BF16).astype(F32) * xdt.astype(F32) + dexp_ref[...] * xs
    xdt_ref[...] = xdt.astype(F32)
    b_ref[...] = bb.astype(F32)
    c_ref[...] = cc.astype(F32)
    ea_ref[...] = ea
    eaexp_ref[...] = _dot3(_split3_bf16(ea), hexp)


def _ssd_sample_state_body(ea_smem, xdt_ref, b_ref, c_ref, yd_ref, eaexp_ref, z_ref, nw_ref, h0_ref,
                           y_ref, h1_ref):
    i = pl.program_id(0)
    N = SSD_STATE
    gw = SSD_WIDTH // SSD_GROUPS
    heads_per_group = SSD_HEADS // SSD_GROUPS
    row0 = lax.broadcasted_iota(jnp.int32, (8, 1), 0) == 0
    y_off = []
    for g in range(SSD_GROUPS):
        x8 = jnp.broadcast_to(xdt_ref[0, :, g * gw:(g + 1) * gw], (8, gw)).astype(BF16)
        b8 = jnp.where(row0, jnp.broadcast_to(b_ref[0, :, g * N:(g + 1) * N], (8, N)), 0.0).astype(BF16)
        c8 = jnp.broadcast_to(c_ref[0, :, g * N:(g + 1) * N], (8, N)).astype(BF16)
        h0g = h0_ref[0, g * gw:(g + 1) * gw, :]
        st = lax.dot_general(x8, b8, (((0,), (0,)), ((), ())), preferred_element_type=F32)
        yo = lax.dot_general(c8, h0g.astype(BF16), _NT_DIMS_SSD, preferred_element_type=F32)
        y_off.append(yo[0:1, :])
        for k in range(heads_per_group):
            h = g * heads_per_group + k
            r = slice(k * SSD_HEAD_DIM, (k + 1) * SSD_HEAD_DIM)
            h1_ref[0, g * gw + k * SSD_HEAD_DIM:g * gw + (k + 1) * SSD_HEAD_DIM, :] = (
                h0g[r, :] * ea_smem[i, h] + st[r, :])
    y = jnp.concatenate(y_off, axis=1) * eaexp_ref[0] + yd_ref[0]
    y = y * _silu(z_ref[0])
    y_ref[0] = _group_rmsnorm(y, nw_ref[...], SSD_GROUPS)


def ssd_sample(proj, conv_state, ssm_state, conv_w, conv_b, dt_bias, a_log, d_skip, norm_w):
    bsz = proj.shape[0]
    H, P, N = SSD_HEADS, SSD_HEAD_DIM, SSD_STATE
    a_neg = _pad_lanes(-jnp.exp(a_log))
    dtb = _pad_lanes(dt_bias)
    dexp = jnp.repeat(d_skip, P)[None, :]
    hexp = jnp.asarray((np.arange(V7X_LANES)[:, None] == (np.arange(H * P)[None, :] // P)).astype(np.float32), BF16)
    cb = conv_b[None, :]
    s0, s1, s2 = conv_state[:, 0], conv_state[:, 1], conv_state[:, 2]
    blk = lambda j, w: pl.BlockSpec((bsz, w), lambda i: (0, j))
    full = lambda a: pl.BlockSpec(a.shape, lambda i: (0,) * a.ndim)
    o = lambda w, dt_: jax.ShapeDtypeStruct((bsz, w), dt_)
    yd, xdt, bb, cc, ea, eaexp = pl.pallas_call(
        _ssd_sample_pre_body,
        out_shape=(o(SSD_WIDTH, F32), o(SSD_WIDTH, F32), o(SSD_GROUPS * N, F32), o(SSD_GROUPS * N, F32),
                   o(V7X_LANES, F32), o(SSD_WIDTH, F32)),
        grid=(1,),
        in_specs=[blk(1, SSD_WIDTH), blk(2, SSD_WIDTH), blk(5 * SSD_WIDTH // V7X_LANES, V7X_LANES),
                  full(s0), full(s1), full(s2), full(conv_w), full(cb), full(dtb), full(a_neg), full(dexp), full(hexp)],
        out_specs=(full(o(SSD_WIDTH, F32)), full(o(SSD_WIDTH, F32)), full(o(SSD_GROUPS * N, F32)),
                   full(o(SSD_GROUPS * N, F32)), full(o(V7X_LANES, F32)), full(o(SSD_WIDTH, F32))),
        compiler_params=_cparams(("arbitrary",)),
        name="ssd_sample_pre",
    )(proj, proj, proj, s0, s1, s2, conv_w, cb, dtb, a_neg, dexp, hexp)
    x_new = jnp.concatenate([proj[:, SSD_WIDTH:2 * SSD_WIDTH], proj[:, 2 * SSD_WIDTH:3 * SSD_WIDTH]], axis=1)
    conv_new = jnp.stack([s1, s2, x_new], axis=1)
    z3 = proj[:, 0:SSD_WIDTH].reshape(bsz, 1, SSD_WIDTH)
    row = lambda w: pl.BlockSpec((1, 1, w), lambda i, ea_: (i, 0, 0))
    nw = norm_w[None, :]
    grid_spec = pltpu.PrefetchScalarGridSpec(
        num_scalar_prefetch=1,
        grid=(bsz,),
        in_specs=[row(SSD_WIDTH), row(SSD_GROUPS * N), row(SSD_GROUPS * N), row(SSD_WIDTH), row(SSD_WIDTH),
                  row(SSD_WIDTH), pl.BlockSpec(nw.shape, lambda i, ea_: (0, 0)),
                  pl.BlockSpec((1, H * P, N), lambda i, ea_: (i, 0, 0))],
        out_specs=(row(SSD_WIDTH), pl.BlockSpec((1, H * P, N), lambda i, ea_: (i, 0, 0))),
    )
    r3 = lambda a: a.reshape(bsz, 1, a.shape[1])
    y, h1 = pl.pallas_call(
        _ssd_sample_state_body,
        out_shape=(jax.ShapeDtypeStruct((bsz, 1, SSD_WIDTH), F32), jax.ShapeDtypeStruct((bsz, H * P, N), F32)),
        grid_spec=grid_spec,
        compiler_params=_cparams(("arbitrary",)),
        name="ssd_sample_state",
    )(ea[:, :H], r3(xdt), r3(bb), r3(cc), r3(yd), r3(eaexp), z3, nw, ssm_state.reshape(bsz, H * P, N))
    return y.reshape(bsz, SSD_WIDTH), conv_new, h1.reshape(bsz, H, P, N)


def _rg_gates(xc, wa_ref, ba_ref, wi_ref, bi_ref, sp_ref):
    xb = xc.astype(BF16)
    r = jax.nn.sigmoid(jnp.dot(xb, wa_ref[...], preferred_element_type=F32) + ba_ref[...])
    ig = jax.nn.sigmoid(jnp.dot(xb, wi_ref[...], preferred_element_type=F32) + bi_ref[...])
    log_a = -RG_C * r * sp_ref[...]
    a = jnp.exp(log_a)
    u = jnp.sqrt(1.0 - jnp.exp(2.0 * log_a)) * (ig * xc)
    return a, u


def _rg_prompt_body(gate_ref, xr_ref, cw_ref, cb_ref, wa_ref, ba_ref, wi_ref, bi_ref, sp_ref,
                    y_ref, conv_ref, state_ref, xbuf, h_scr):
    rows = RG_TILE
    c = pl.program_id(1)
    last = pl.num_programs(1) - 1

    @pl.when(c == 0)
    def _():
        xbuf[0:CONV_TAIL, :] = jnp.zeros((CONV_TAIL, RG_WIDTH), F32)
        h_scr[...] = jnp.zeros_like(h_scr)

    xbuf[CONV_TAIL:CONV_TAIL + rows, :] = xr_ref[...]
    xc = _conv_tile(xbuf, cw_ref, cb_ref, rows)

    @pl.when(c == last)
    def _():
        conv_ref[0] = xbuf[CONV_TAIL + rows - (SSD_CONV - 1):CONV_TAIL + rows, :]

    xbuf[0:CONV_TAIL, :] = xbuf[rows:rows + CONV_TAIL, :]
    a, u = _rg_gates(xc, wa_ref, ba_ref, wi_ref, bi_ref, sp_ref)
    t_idx = lax.broadcasted_iota(jnp.int32, (rows, 1), 0)
    d = 1
    while d < rows:
        keep = t_idx >= d
        a_sh = jnp.where(keep, pltpu.roll(a, d, 0), 1.0)
        u_sh = jnp.where(keep, pltpu.roll(u, d, 0), 0.0)
        u = u + a * u_sh
        a = a * a_sh
        d *= 2
    h = u + a * h_scr[0:1, :]
    h_scr[0:1, :] = h[rows - 1:rows, :]
    y_ref[...] = h * _gelu_tanh(gate_ref[...])

    @pl.when(c == last)
    def _():
        state_ref[0] = h[rows - 1:rows, :]


def _rg_weights(wa, ba, wi, bi, lam):
    eye = jnp.eye(RG_BLOCKS, dtype=F32)
    bd = lambda w: jnp.einsum('nde,nm->ndme', w, eye).reshape(RG_WIDTH, RG_WIDTH).astype(BF16)
    return bd(wa), ba[None, :], bd(wi), bi[None, :], jax.nn.softplus(-lam)[None, :]


def rg_prompt(proj, b, seq_len, conv_w, conv_b, wa, ba, wi, bi, lam):
    rows = RG_TILE
    nt = seq_len // rows
    t = b * seq_len
    wts = _rg_weights(wa, ba, wi, bi, lam)
    cb = conv_b[None, :]
    full = lambda a: pl.BlockSpec(a.shape, lambda bi_, c: (0,) * a.ndim)
    return pl.pallas_call(
        _rg_prompt_body,
        out_shape=(jax.ShapeDtypeStruct((t, RG_WIDTH), F32),
                   jax.ShapeDtypeStruct((b, SSD_CONV - 1, RG_WIDTH), F32),
                   jax.ShapeDtypeStruct((b, 1, RG_WIDTH), F32)),
        grid=(b, nt),
        in_specs=[pl.BlockSpec((rows, RG_WIDTH), lambda bi_, c: (bi_ * nt + c, 3)),
                  pl.BlockSpec((rows, RG_WIDTH), lambda bi_, c: (bi_ * nt + c, 4)),
                  full(conv_w), full(cb)] + [full(w) for w in wts],
        out_specs=(pl.BlockSpec((rows, RG_WIDTH), lambda bi_, c: (bi_ * nt + c, 0)),
                   pl.BlockSpec((1, SSD_CONV - 1, RG_WIDTH), lambda bi_, c: (bi_, 0, 0)),
                   pl.BlockSpec((1, 1, RG_WIDTH), lambda bi_, c: (bi_, 0, 0))),
        scratch_shapes=[pltpu.VMEM((CONV_TAIL + rows, RG_WIDTH), F32), pltpu.VMEM((8, RG_WIDTH), F32)],
        compiler_params=_cparams(("arbitrary", "arbitrary"), V7X_VMEM_LIMIT_BYTES),
        name="rg_prompt",
    )(proj, proj, conv_w, cb, *wts)


def _rg_sample_body(gate_ref, xr_ref, s0_ref, s1_ref, s2_ref, h0_ref, cw_ref, cb_ref, wa_ref, ba_ref, wi_ref, bi_ref,
                    sp_ref, y_ref, h1_ref):
    xc = (cb_ref[...] + cw_ref[0:1, :] * s0_ref[...] + cw_ref[1:2, :] * s1_ref[...]
          + cw_ref[2:3, :] * s2_ref[...] + cw_ref[3:4, :] * xr_ref[...])
    a, u = _rg_gates(xc, wa_ref, ba_ref, wi_ref, bi_ref, sp_ref)
    h = a * h0_ref[...] + u
    h1_ref[...] = h
    y_ref[...] = h * _gelu_tanh(gate_ref[...])


def rg_sample(proj, conv_state, h0, conv_w, conv_b, wa, ba, wi, bi, lam):
    bsz = proj.shape[0]
    wts = _rg_weights(wa, ba, wi, bi, lam)
    cb = conv_b[None, :]
    s0, s1, s2 = conv_state[:, 0], conv_state[:, 1], conv_state[:, 2]
    full = lambda a: pl.BlockSpec(a.shape, lambda i: (0,) * a.ndim)
    out = jax.ShapeDtypeStruct((bsz, RG_WIDTH), F32)
    y, h1 = pl.pallas_call(
        _rg_sample_body,
        out_shape=(out, out),
        grid=(1,),
        in_specs=[pl.BlockSpec((bsz, RG_WIDTH), lambda i: (0, 3)), pl.BlockSpec((bsz, RG_WIDTH), lambda i: (0, 4)),
                  full(s0), full(s1), full(s2), full(h0), full(conv_w), full(cb)] + [full(w) for w in wts],
        out_specs=(full(out), full(out)),
        compiler_params=_cparams(("arbitrary",)),
        name="rg_sample",
    )(proj, proj, s0, s1, s2, h0, conv_w, cb, *wts)
    conv_new = jnp.stack([s1, s2, proj[:, 4 * RG_WIDTH:5 * RG_WIDTH]], axis=1)
    return y, conv_new, h1


def _overlap_matrix(nc, ns):
    i = np.arange(nc)[:, None]
    j = np.arange(ns)[None, :]
    ov = (i * CMP_STRIDE < (j + 1) * SEL_BLOCK) & (i * CMP_STRIDE + CMP_LEN > j * SEL_BLOCK)
    return ov.astype(np.float32)


NSA_TQ = 128
NSA_TK_SLC = 1024
NSA_NS_PAD = 64
SEL_BIAS = -16384.0
MASK_VALUE = -1e30


def _split_bf16(x):
    hi = x.astype(BF16)
    lo = (x - hi.astype(F32)).astype(BF16)
    return hi, lo


def _seg_rms_scale(x, seg, seg_t):
    hi, lo = _split_bf16(x * x)
    ss = jnp.dot(hi, seg, preferred_element_type=F32) + jnp.dot(lo, seg, preferred_element_type=F32)
    r = lax.rsqrt(ss * (1.0 / NSA_HEAD_DIM) + NORM_EPS)
    rh, rl = _split_bf16(r)
    return jnp.dot(rh, seg_t, preferred_element_type=F32) + jnp.dot(rl, seg_t, preferred_element_type=F32)


def _nsa_prep_body(seq_len, p_ref, wq_ref, wks_ref, wkw_ref, segq_ref, segqt_ref, segk_ref, segkt_ref,
                   q_ref, rows_ref, win_ref, kaug_ref, vslc_ref, kwin_ref, vwin_ref, gate_ref, rows_t_ref, win_t_ref):
    tm = p_ref.shape[0]
    dh = NSA_HEAD_DIM
    q = p_ref[:, 0:NSA_Q_WIDTH]
    qn = q * _seg_rms_scale(q, segq_ref[...], segqt_ref[...]) * wq_ref[...]
    kv = [p_ref[:, NSA_Q_WIDTH + NSA_KV_WIDTH * j:NSA_Q_WIDTH + NSA_KV_WIDTH * (j + 1)] for j in range(6)]
    ksl = kv[2] * _seg_rms_scale(kv[2], segk_ref[...], segkt_ref[...]) * wks_ref[...]
    kwn = kv[4] * _seg_rms_scale(kv[4], segk_ref[...], segkt_ref[...]) * wkw_ref[...]
    rows = jnp.concatenate([kv[0], kv[1], ksl, kv[3]], axis=1)
    win = jnp.concatenate([kwn, kv[5]], axis=1)
    rows_ref[...] = rows
    win_ref[...] = win
    rows_t_ref[0] = rows.T
    win_t_ref[0] = win.T
    gates = jax.nn.sigmoid(p_ref[:, NSA_Q_WIDTH + 6 * NSA_KV_WIDTH:NSA_Q_WIDTH + 6 * NSA_KV_WIDTH + V7X_LANES])
    t0 = (pl.program_id(0) * tm) % seq_len
    tpos = t0 + lax.broadcasted_iota(jnp.int32, (tm, NSA_NS_PAD), 0)
    blk = lax.broadcasted_iota(jnp.int32, (tm, NSA_NS_PAD), 1)
    onehot = jnp.where(blk == lax.shift_right_logical(tpos, 6), 1.0, 0.0).astype(BF16)
    for g in range(NSA_KV_HEADS):
        sl = slice(g * dh, (g + 1) * dh)
        kaug_ref[0, g] = jnp.concatenate([ksl[:, sl].astype(BF16), onehot], axis=1)
        vslc_ref[0, g] = kv[3][:, sl].astype(BF16)
        kwin_ref[0, g] = kwn[:, sl].astype(BF16)
        vwin_ref[0, g] = kv[5][:, sl].astype(BF16)
        gate_ref[0, g] = gates if g == 0 else pltpu.roll(gates, V7X_LANES - 3 * NSA_GROUP * g, 1)
        for k in range(NSA_GROUP):
            c0 = (g * NSA_GROUP + k) * dh
            q_ref[0, g, k] = qn[:, c0:c0 + dh].astype(BF16)


def _head_segments(width):
    lane = np.arange(width)[:, None] // NSA_HEAD_DIM
    seg = (lane == np.arange(V7X_LANES)[None, :]).astype(np.float32)
    return jnp.asarray(seg, BF16), jnp.asarray(seg.T, BF16)


def nsa_prep(proj, b, seq_len, q_norm, k_norm, tm):
    t = proj.shape[0]
    G, K, dh = NSA_KV_HEADS, NSA_GROUP, NSA_HEAD_DIM
    wq = (jnp.tile(q_norm, NSA_HEADS) * (dh ** -0.5))[None, :]
    wks = jnp.tile(k_norm[1], G)[None, :]
    wkw = jnp.tile(k_norm[2], G)[None, :]
    segq, segqt = _head_segments(NSA_Q_WIDTH)
    segk, segkt = _head_segments(NSA_KV_WIDTH)
    tiles_per_seq = seq_len // tm
    bi = lambda i: i // tiles_per_seq
    ti = lambda i: i % tiles_per_seq
    full = lambda a: pl.BlockSpec(a.shape, lambda i: (0,) * a.ndim)
    out_shape = (jax.ShapeDtypeStruct((b, G, K, seq_len, dh), BF16),
                 jax.ShapeDtypeStruct((t, 4 * NSA_KV_WIDTH), F32),
                 jax.ShapeDtypeStruct((t, 2 * NSA_KV_WIDTH), F32),
                 jax.ShapeDtypeStruct((b, G, seq_len, 2 * dh), BF16),
                 jax.ShapeDtypeStruct((b, G, seq_len, dh), BF16),
                 jax.ShapeDtypeStruct((b, G, seq_len, dh), BF16),
                 jax.ShapeDtypeStruct((b, G, seq_len, dh), BF16),
                 jax.ShapeDtypeStruct((b, G, seq_len, V7X_LANES), F32),
                 jax.ShapeDtypeStruct((b, 4 * NSA_KV_WIDTH, seq_len), F32),
                 jax.ShapeDtypeStruct((b, 2 * NSA_KV_WIDTH, seq_len), F32))
    per_g = lambda w: pl.BlockSpec((1, G, tm, w), lambda i: (bi(i), 0, ti(i), 0))
    feat_major = lambda w: pl.BlockSpec((1, w, tm), lambda i: (bi(i), 0, ti(i)))
    out_specs = (pl.BlockSpec((1, G, K, tm, dh), lambda i: (bi(i), 0, 0, ti(i), 0)),
                 pl.BlockSpec((tm, 4 * NSA_KV_WIDTH), lambda i: (i, 0)),
                 pl.BlockSpec((tm, 2 * NSA_KV_WIDTH), lambda i: (i, 0)),
                 per_g(2 * dh), per_g(dh), per_g(dh), per_g(dh), per_g(V7X_LANES),
                 feat_major(4 * NSA_KV_WIDTH), feat_major(2 * NSA_KV_WIDTH))
    return pl.pallas_call(
        functools.partial(_nsa_prep_body, seq_len),
        out_shape=out_shape,
        grid=(t // tm,),
        in_specs=[pl.BlockSpec((tm, proj.shape[1]), lambda i: (i, 0)),
                  full(wq), full(wks), full(wkw), full(segq), full(segqt), full(segk), full(segkt)],
        out_specs=out_specs,
        compiler_params=_cparams(("arbitrary",), V7X_VMEM_LIMIT_BYTES),
        name="nsa_prep",
    )(proj, wq, wks, wkw, segq, segqt, segk, segkt)


def _gelu_tanh(x):
    return 0.5 * x * (1.0 + jnp.tanh(math.sqrt(2.0 / math.pi) * (x + 0.044715 * (x * x * x))))


def _nsa_compress_body(n_chunk, x0_ref, x1_ref, x2_ref, x3_ref, wk_ref, wv_ref, pe_ref, w1f_ref, w2_ref, kn_ref,
                       kc_ref, vc_ref, pk_scr, pv_scr):
    s = pl.program_id(1)

    @pl.when(s == 0)
    def _():
        pk_scr[...] = jnp.zeros_like(pk_scr)
        pv_scr[...] = jnp.zeros_like(pv_scr)

    xs = [r[pl.ds(s, n_chunk, stride=CMP_STRIDE), :].astype(BF16) for r in (x0_ref, x1_ref, x2_ref, x3_ref)]
    pk_scr[...] += jnp.dot(jnp.concatenate(xs[0:2], axis=1), wk_ref[0], preferred_element_type=F32)
    pv_scr[...] += jnp.dot(jnp.concatenate(xs[2:4], axis=1), wv_ref[0], preferred_element_type=F32)

    @pl.when(s == CMP_STRIDE - 1)
    def _():
        for kv, p_scr, o_ref in ((0, pk_scr, kc_ref), (1, pv_scr, vc_ref)):
            p = p_scr[...]
            p_next = pltpu.roll(p, n_chunk - 1, 0)
            pe_h = jnp.dot(pe_ref[kv], w1f_ref[kv], preferred_element_type=F32)[0:1, :]
            for g in range(NSA_KV_HEADS):
                c0 = g * 2 * CMP_HID
                hid = pe_h + p[:, c0:c0 + CMP_HID] + p_next[:, c0 + CMP_HID:c0 + 2 * CMP_HID]
                y = jnp.dot(_gelu_tanh(hid).astype(BF16), w2_ref[kv], preferred_element_type=F32)
                if kv == 0:
                    y = y * lax.rsqrt(jnp.mean(y * y, axis=-1, keepdims=True) + NORM_EPS) * kn_ref[...]
                o_ref[0, g] = y.astype(BF16)


def nsa_compress(rows, b, seq_len, cmp_w1, cmp_pe, cmp_w2, k_norm_cmp):
    G, dh = NSA_KV_HEADS, NSA_HEAD_DIM
    n_chunk = seq_len // CMP_STRIDE
    w1 = cmp_w1.reshape(2, CMP_R, CMP_STRIDE, dh, CMP_HID)
    eye = jnp.eye(G, dtype=F32)
    wbd = jnp.einsum('vrsdh,gq->vsgdqrh', w1, eye).reshape(2, CMP_STRIDE, G * dh, G * CMP_R * CMP_HID).astype(BF16)
    pe = jnp.broadcast_to(cmp_pe.reshape(2, 1, CMP_LEN * dh), (2, 8, CMP_LEN * dh)).astype(BF16)
    w1f = cmp_w1.reshape(2, CMP_LEN * dh, CMP_HID).astype(BF16)
    w2 = cmp_w2.astype(BF16)
    kn = k_norm_cmp[None, :]
    full = lambda a: pl.BlockSpec(a.shape, lambda bi, s: (0,) * a.ndim)
    return pl.pallas_call(
        functools.partial(_nsa_compress_body, n_chunk),
        out_shape=(jax.ShapeDtypeStruct((b, G, n_chunk, dh), BF16),
                   jax.ShapeDtypeStruct((b, G, n_chunk, dh), BF16)),
        grid=(b, CMP_STRIDE),
        in_specs=[pl.BlockSpec((seq_len, V7X_LANES), lambda bi, s: (bi, 0)),
                  pl.BlockSpec((seq_len, V7X_LANES), lambda bi, s: (bi, 1)),
                  pl.BlockSpec((seq_len, V7X_LANES), lambda bi, s: (bi, 2)),
                  pl.BlockSpec((seq_len, V7X_LANES), lambda bi, s: (bi, 3)),
                  pl.BlockSpec((1, G * dh, G * CMP_R * CMP_HID), lambda bi, s: (s, 0, 0)),
                  pl.BlockSpec((1, G * dh, G * CMP_R * CMP_HID), lambda bi, s: (s, 0, 0)),
                  full(pe), full(w1f), full(w2), full(kn)],
        out_specs=(pl.BlockSpec((1, G, n_chunk, dh), lambda bi, s: (bi, 0, 0, 0)),
                   pl.BlockSpec((1, G, n_chunk, dh), lambda bi, s: (bi, 0, 0, 0))),
        scratch_shapes=[pltpu.VMEM((n_chunk, G * CMP_R * CMP_HID), F32),
                        pltpu.VMEM((n_chunk, G * CMP_R * CMP_HID), F32)],
        compiler_params=_cparams(("arbitrary", "arbitrary"), V7X_VMEM_LIMIT_BYTES),
        name="nsa_compress",
    )(rows, rows, rows, rows, wbd[0], wbd[1], pe, w1f, w2, kn)


_NT_DIMS = (((1,), (1,)), ((), ()))


def _flash_branch(q2, k_ref, v_ref, n_tiles, tk, last_mask_fn):
    rows = q2.shape[0]

    def step(j, carry, mask_fn):
        m, l, acc = carry
        k0 = pl.multiple_of(j * tk, tk)
        k = k_ref[0, 0, pl.ds(k0, tk), :]
        v = v_ref[0, 0, pl.ds(k0, tk), :]
        s = lax.dot_general(q2, k, _NT_DIMS, preferred_element_type=F32)
        if mask_fn is not None:
            s = jnp.where(mask_fn(k0), s, MASK_VALUE)
        m_new = jnp.maximum(m, jnp.max(s, axis=-1, keepdims=True))
        alpha = jnp.exp(m - m_new)
        p = jnp.exp(s - m_new)
        l = alpha * l + jnp.sum(p, axis=-1, keepdims=True)
        acc = alpha * acc + jnp.dot(p.astype(BF16), v, preferred_element_type=F32)
        return m_new, l, acc

    init = (jnp.full((rows, 1), MASK_VALUE, F32), jnp.zeros((rows, 1), F32),
            jnp.zeros((rows, NSA_HEAD_DIM), F32))
    carry = lax.fori_loop(0, n_tiles - 1, lambda j, c: step(j, c, None), init)
    _, l, acc = step(n_tiles - 1, carry, last_mask_fn)
    return acc / l


def _nsa_attn_body(n_cmp, q_ref, kc_ref, vc_ref, kaug_ref, vslc_ref, kwin_ref, vwin_ref, gate_ref, ovt_ref, o_ref):
    tq = NSA_TQ
    rows = NSA_GROUP * tq
    q0 = pl.program_id(2) * tq
    q2 = q_ref[0, 0].reshape(rows, NSA_HEAD_DIM)
    row_t = q0 + jnp.bitwise_and(lax.broadcasted_iota(jnp.int32, (rows, 1), 0), tq - 1)

    n_pad = kc_ref.shape[2]
    s = lax.dot_general(q2, kc_ref[0, 0], _NT_DIMS, preferred_element_type=F32)
    n_idx = lax.broadcasted_iota(jnp.int32, (1, n_pad), 1)
    cmask = jnp.logical_and(n_idx * CMP_STRIDE + (CMP_LEN - 1) <= row_t, n_idx < n_cmp)
    s = jnp.where(cmask, s, MASK_VALUE)
    m = jnp.max(s, axis=-1, keepdims=True)
    e = jnp.where(cmask, jnp.exp(s - m), 0.0)
    den = jnp.sum(e, axis=-1, keepdims=True)
    p_c = e / jnp.where(den > 0.0, den, 1.0)
    o_c = jnp.dot(p_c.astype(BF16), vc_ref[0, 0], preferred_element_type=F32)

    p_sum = (p_c[0:tq] + p_c[tq:2 * tq]) + (p_c[2 * tq:3 * tq] + p_c[3 * tq:4 * tq])
    ph, plo = _split_bf16(p_sum)
    ovt = ovt_ref[...]
    imp = (lax.dot_general(ovt, ph, _NT_DIMS, preferred_element_type=F32)
           + lax.dot_general(ovt, plo, _NT_DIMS, preferred_element_type=F32))
    blk = lax.broadcasted_iota(jnp.int32, (NSA_NS_PAD, tq), 0)
    jt = lax.shift_right_logical(q0 + lax.broadcasted_iota(jnp.int32, (NSA_NS_PAD, tq), 1), 6)
    valid = blk <= jt
    forced = jnp.logical_and(valid, jnp.logical_or(blk == 0, jnp.logical_or(blk == jt, blk == jt - 1)))
    eff = jnp.where(forced, jnp.inf, jnp.where(valid, imp, -jnp.inf))
    rank = jnp.zeros((NSA_NS_PAD, tq), jnp.int32)
    for j in range(NSA_NS_PAD):
        other = eff[j:j + 1, :]
        ahead = jnp.logical_or(other > eff, jnp.logical_and(other == eff, blk > j))
        rank = rank + ahead.astype(jnp.int32)
    sel = jnp.logical_and(valid, rank < SEL_TOPN)
    sel_bias = jnp.where(sel, 0.0, SEL_BIAS).T.astype(BF16)

    q_aug = jnp.concatenate([q2, jnp.concatenate([sel_bias] * NSA_GROUP, axis=0)], axis=1)
    hi = (q0 + tq - 1) // NSA_TK_SLC + 1

    def slc_mask(k0):
        kpos = k0 + lax.broadcasted_iota(jnp.int32, (1, NSA_TK_SLC), 1)
        return kpos <= row_t

    o_s = _flash_branch(q_aug, kaug_ref, vslc_ref, hi, NSA_TK_SLC, slc_mask)

    span = WINDOW + tq
    w0 = pl.multiple_of(jnp.maximum(q0 - WINDOW, 0), tq)
    kw = kwin_ref[0, 0, pl.ds(w0, span), :]
    vw = vwin_ref[0, 0, pl.ds(w0, span), :]
    s_w = lax.dot_general(q2, kw, _NT_DIMS, preferred_element_type=F32)
    kpos = w0 + lax.broadcasted_iota(jnp.int32, (1, span), 1)
    wmask = jnp.logical_and(kpos <= row_t, kpos > row_t - WINDOW)
    s_w = jnp.where(wmask, s_w, MASK_VALUE)
    p_w = jnp.exp(s_w - jnp.max(s_w, axis=-1, keepdims=True))
    o_w = (jnp.dot(p_w.astype(BF16), vw, preferred_element_type=F32)
           / jnp.sum(p_w, axis=-1, keepdims=True))

    gt = gate_ref[0, 0]
    outs = []
    for k in range(NSA_GROUP):
        r = slice(k * tq, (k + 1) * tq)
        outs.append(gt[:, 3 * k:3 * k + 1] * o_c[r] + gt[:, 3 * k + 1:3 * k + 2] * o_s[r]
                    + gt[:, 3 * k + 2:3 * k + 3] * o_w[r])
    o_ref[...] = jnp.concatenate(outs, axis=1)


def nsa_attention(q, kc, vc, kaug, vslc, kwin, vwin, gates, b, seq_len):
    G, K, dh = NSA_KV_HEADS, NSA_GROUP, NSA_HEAD_DIM
    tq = NSA_TQ
    nq = seq_len // tq
    n_chunk = kc.shape[2]
    n_cmp = n_chunk - CMP_R + 1
    ns = seq_len // SEL_BLOCK
    ovt = np.zeros((NSA_NS_PAD, n_chunk), np.float32)
    ovt[:ns, :n_cmp] = _overlap_matrix(n_cmp, ns).T
    ovt = jnp.asarray(ovt, BF16)
    seq_spec = lambda w: pl.BlockSpec((1, 1, seq_len, w), lambda bi, g, qi: (bi, g, 0, 0))
    return pl.pallas_call(
        functools.partial(_nsa_attn_body, n_cmp),
        out_shape=jax.ShapeDtypeStruct((b * seq_len, NSA_Q_WIDTH), F32),
        grid=(b, G, nq),
        in_specs=[pl.BlockSpec((1, 1, K, tq, dh), lambda bi, g, qi: (bi, g, 0, qi, 0)),
                  pl.BlockSpec((1, 1, n_chunk, dh), lambda bi, g, qi: (bi, g, 0, 0)),
                  pl.BlockSpec((1, 1, n_chunk, dh), lambda bi, g, qi: (bi, g, 0, 0)),
                  seq_spec(2 * dh), seq_spec(dh), seq_spec(dh), seq_spec(dh),
                  pl.BlockSpec((1, 1, tq, V7X_LANES), lambda bi, g, qi: (bi, g, qi, 0)),
                  pl.BlockSpec(ovt.shape, lambda bi, g, qi: (0, 0))],
        out_specs=pl.BlockSpec((tq, K * dh), lambda bi, g, qi: (bi * nq + qi, g)),
        compiler_params=_cparams(("arbitrary", "arbitrary", "arbitrary"), V7X_VMEM_LIMIT_BYTES),
        name="nsa_attention",
    )(q, kc, vc, kaug, vslc, kwin, vwin, gates, ovt)


def nsa_prompt_pallas(proj, b, seq_len, q_norm, k_norm, cmp_w1, cmp_pe, cmp_w2):
    q, rows, _, kaug, vslc, kwin, vwin, gates, rows_t, win_t = nsa_prep(proj, b, seq_len, q_norm, k_norm, ROW_TILE)
    kc, vc = nsa_compress(rows, b, seq_len, cmp_w1, cmp_pe, cmp_w2, k_norm[0])
    o = nsa_attention(q, kc, vc, kaug, vslc, kwin, vwin, gates, b, seq_len)
    G, dh = NSA_KV_HEADS, NSA_HEAD_DIM
    rows_out = jnp.transpose(rows_t.reshape(b, 4, G, dh, seq_len), (0, 4, 1, 2, 3))
    wlen = min(WINDOW, seq_len)
    win_out = jnp.transpose(win_t[:, :, seq_len - wlen:].reshape(b, 2, G, dh, wlen), (0, 4, 1, 2, 3))
    return o, rows_out, win_out


def _diag_heads(o_full):
    g_row = lax.shift_right_logical(lax.broadcasted_iota(jnp.int32, (NSA_HEADS, 1), 0), 2)
    out = jnp.zeros((NSA_HEADS, NSA_HEAD_DIM), F32)
    for g in range(NSA_KV_HEADS):
        out = out + jnp.where(g_row == g, o_full[:, g * NSA_HEAD_DIM:(g + 1) * NSA_HEAD_DIM], 0.0)
    return out


def _nsa_sample_body(n_pages, page_rows, pt_ref,
                     cache_hbm, qbd_ref, rown_ref, winn_ref, winbuf_ref, wc_ref, pe_ref, w1f_ref, w2_ref, kn_ref,
                     gsum_ref, ovs_ref, rep_ref, eblk_ref,
                     oc_ref, os_ref, ow_ref,
                     cmpt_buf, slct_buf, cmp_buf, sem):
    i = pl.program_id(0)
    nb = pl.num_programs(0)
    slot = i % 2
    past = n_pages * page_rows
    n_chunk = past // CMP_STRIDE
    n_cmp = n_chunk - CMP_R + 1
    t_pos = past
    kvw = NSA_KV_WIDTH

    def page_copies(bi, sl):
        copies = []
        for p in range(n_pages):
            pg = pt_ref[bi, p]
            copies.append(pltpu.make_async_copy(cache_hbm.at[pg, pl.ds(0, 2 * kvw), :],
                                                cmpt_buf.at[sl, p], sem.at[sl]))
            copies.append(pltpu.make_async_copy(cache_hbm.at[pg, pl.ds(2 * kvw, 2 * kvw), :],
                                                slct_buf.at[sl, :, pl.ds(p * page_rows, page_rows)], sem.at[sl]))
        return copies

    @pl.when(i == 0)
    def _():
        for c in page_copies(0, 0):
            c.start()

    @pl.when(i + 1 < nb)
    def _():
        for c in page_copies(i + 1, 1 - slot):
            c.start()

    for c in page_copies(i, slot):
        c.wait()

    for p in range(n_pages):
        for j in range(4):
            cmp_buf[j, p * page_rows:(p + 1) * page_rows, :] = (
                cmpt_buf[slot, p, j * V7X_LANES:(j + 1) * V7X_LANES, :].T)

    parts = []
    for j in range(4):
        acc = None
        for s in range(CMP_STRIDE):
            xs = cmp_buf[j, pl.ds(s, n_chunk, stride=CMP_STRIDE), :].astype(BF16)
            d = jnp.dot(xs, wc_ref[j // 2, s], preferred_element_type=F32)
            acc = d if acc is None else acc + d
        parts.append(acc)
    slabs = []
    for kv in range(2):
        p = jnp.concatenate(parts[2 * kv:2 * kv + 2], axis=1)
        p_next = pltpu.roll(p, n_chunk - 1, 0)
        pe_h = jnp.dot(pe_ref[kv], w1f_ref[kv], preferred_element_type=F32)[0:1, :]
        ys = []
        for g in range(NSA_KV_HEADS):
            c0 = g * 2 * CMP_HID
            hid = pe_h + p[:, c0:c0 + CMP_HID] + p_next[:, c0 + CMP_HID:c0 + 2 * CMP_HID]
            y = jnp.dot(_gelu_tanh(hid).astype(BF16), w2_ref[kv], preferred_element_type=F32)
            if kv == 0:
                y = y * lax.rsqrt(jnp.mean(y * y, axis=-1, keepdims=True) + NORM_EPS) * kn_ref[...]
            ys.append(y)
        slabs.append(jnp.concatenate(ys, axis=1).astype(BF16))
    kc, vc = slabs

    qbd = qbd_ref[0]
    qf = qbd.astype(F32)

    s_c = lax.dot_general(qbd, kc, _NT_DIMS, preferred_element_type=F32)
    n_idx = lax.broadcasted_iota(jnp.int32, (1, n_chunk), 1)
    cmask = jnp.logical_and(n_idx * CMP_STRIDE + (CMP_LEN - 1) <= t_pos, n_idx < n_cmp)
    s_c = jnp.where(cmask, s_c, MASK_VALUE)
    m = jnp.max(s_c, axis=-1, keepdims=True)
    e = jnp.where(cmask, jnp.exp(s_c - m), 0.0)
    den = jnp.sum(e, axis=-1, keepdims=True)
    p_c = e / jnp.where(den > 0.0, den, 1.0)
    oc_ref[0] = _diag_heads(jnp.dot(p_c.astype(BF16), vc, preferred_element_type=F32))

    gsum = gsum_ref[...]
    ph, plo = _split_bf16(p_c)
    p_sum = jnp.dot(gsum, ph, preferred_element_type=F32) + jnp.dot(gsum, plo, preferred_element_type=F32)
    sh, slo = _split_bf16(p_sum)
    ovs = ovs_ref[...]
    imp = jnp.dot(sh, ovs, preferred_element_type=F32) + jnp.dot(slo, ovs, preferred_element_type=F32)
    blk = lax.broadcasted_iota(jnp.int32, imp.shape, 1)
    jt = t_pos // SEL_BLOCK
    valid = blk <= jt
    forced = jnp.logical_and(valid, jnp.logical_or(blk == 0, jnp.logical_or(blk == jt, blk == jt - 1)))
    eff = jnp.where(forced, jnp.inf, jnp.where(valid, imp, -jnp.inf))
    rank = jnp.zeros(imp.shape, jnp.int32)
    for j in range(jt + 1):
        other = eff[:, j:j + 1]
        ahead = jnp.logical_or(other > eff, jnp.logical_and(other == eff, blk > j))
        rank = rank + ahead.astype(jnp.int32)
    sel = jnp.logical_and(valid, rank < SEL_TOPN)
    sel_bias = jnp.where(sel, 0.0, SEL_BIAS).astype(BF16)
    bias_h = jnp.dot(rep_ref[...], sel_bias, preferred_element_type=F32).astype(BF16)
    bias_keys = jnp.dot(bias_h, eblk_ref[...], preferred_element_type=F32)

    rn = rown_ref[0]
    ks_t = slct_buf[slot, 0:kvw, :].astype(BF16)
    vs_t = slct_buf[slot, kvw:2 * kvw, :].astype(BF16)
    s_s = jnp.dot(qbd, ks_t, preferred_element_type=F32) + bias_keys
    ks_new = rn[:, 2 * kvw:3 * kvw].astype(BF16).astype(F32)
    vs_new = rn[:, 3 * kvw:4 * kvw].astype(BF16).astype(F32)
    s_new = jnp.sum(qf * ks_new, axis=-1, keepdims=True)
    m = jnp.maximum(jnp.max(s_s, axis=-1, keepdims=True), s_new)
    p = jnp.exp(s_s - m)
    p_new = jnp.exp(s_new - m)
    den = jnp.sum(p, axis=-1, keepdims=True) + p_new
    o_full = (lax.dot_general(p.astype(BF16), vs_t, _NT_DIMS, preferred_element_type=F32)
              + p_new.astype(BF16).astype(F32) * vs_new)
    os_ref[0] = _diag_heads(o_full) / den

    wb = winbuf_ref[0]
    wn = winn_ref[0]
    wb_len = wb.shape[1]
    kw_t = wb[0:kvw, :].astype(BF16)
    vw_t = wb[kvw:2 * kvw, :].astype(BF16)
    s_w = jnp.dot(qbd, kw_t, preferred_element_type=F32)
    w_idx = lax.broadcasted_iota(jnp.int32, (1, wb_len), 1)
    w_pos = t_pos - wb_len + w_idx
    wmask = jnp.logical_and(w_pos > t_pos - WINDOW, w_pos >= 0)
    s_w = jnp.where(wmask, s_w, MASK_VALUE)
    kw_new = wn[:, 0:kvw].astype(BF16).astype(F32)
    vw_new = wn[:, kvw:2 * kvw].astype(BF16).astype(F32)
    s_new = jnp.sum(qf * kw_new, axis=-1, keepdims=True)
    m = jnp.maximum(jnp.max(s_w, axis=-1, keepdims=True), s_new)
    p = jnp.where(wmask, jnp.exp(s_w - m), 0.0)
    p_new = jnp.exp(s_new - m)
    den = jnp.sum(p, axis=-1, keepdims=True) + p_new
    o_full = (lax.dot_general(p.astype(BF16), vw_t, _NT_DIMS, preferred_element_type=F32)
              + p_new.astype(BF16).astype(F32) * vw_new)
    ow_ref[0] = _diag_heads(o_full) / den


def nsa_sample_attention(cache, page_table, win_buf, q, rows_new, win_new, cmp_w1, cmp_pe, cmp_w2, k_norm_cmp):
    G, K, dh = NSA_KV_HEADS, NSA_GROUP, NSA_HEAD_DIM
    bsz, n_pages = page_table.shape
    n_phys, page_rows = cache.shape[0], cache.shape[1]
    past = n_pages * page_rows
    n_chunk = past // CMP_STRIDE
    n_cmp = n_chunk - CMP_R + 1
    ns = -(-(past + 1) // SEL_BLOCK)
    cache3 = jnp.transpose(cache, (0, 2, 3, 4, 1)).reshape(n_phys, 4 * G * dh, page_rows)
    wb_len = win_buf.shape[1]
    win3 = jnp.transpose(win_buf, (0, 2, 3, 4, 1)).reshape(bsz, 2 * G * dh, wb_len)
    qh = jnp.transpose(q[0], (2, 0, 1, 3)).astype(F32)
    qbd = jnp.einsum('bgkd,gq->bgkqd', qh, jnp.eye(G, dtype=F32)).reshape(bsz, G * K, G * dh).astype(BF16)
    w1 = cmp_w1.reshape(2, CMP_R, CMP_STRIDE, dh, CMP_HID)
    wc = jnp.einsum('vrsdh,pq->vspdqrh', w1, jnp.eye(2, dtype=F32)).reshape(
        2, CMP_STRIDE, 2 * dh, 2 * CMP_R * CMP_HID).astype(BF16)
    pe = jnp.broadcast_to(cmp_pe.reshape(2, 1, CMP_LEN * dh), (2, 8, CMP_LEN * dh)).astype(BF16)
    w1f = cmp_w1.reshape(2, CMP_LEN * dh, CMP_HID).astype(BF16)
    w2 = cmp_w2.astype(BF16)
    kn = k_norm_cmp[None, :]
    gsum = np.zeros((8, G * K), np.float32)
    gsum[np.arange(G * K) // K, np.arange(G * K)] = 1.0
    ovs = np.zeros((n_chunk, NSA_NS_PAD), np.float32)
    ovs[:n_cmp, :ns] = _overlap_matrix(n_cmp, ns)
    eblk = (np.arange(NSA_NS_PAD)[:, None] == (np.arange(past)[None, :] // SEL_BLOCK)).astype(np.float32)
    gsum, ovs, eblk = jnp.asarray(gsum, BF16), jnp.asarray(ovs, BF16), jnp.asarray(eblk, BF16)
    rep = gsum.T
    full = lambda a: pl.BlockSpec(a.shape, lambda i, pt: (0,) * a.ndim)
    grid_spec = pltpu.PrefetchScalarGridSpec(
        num_scalar_prefetch=1,
        grid=(bsz,),
        in_specs=[pl.BlockSpec(memory_space=pl.ANY),
                  pl.BlockSpec((1, G * K, G * dh), lambda i, pt: (i, 0, 0)),
                  pl.BlockSpec((1, 1, 4 * G * dh), lambda i, pt: (i, 0, 0)),
                  pl.BlockSpec((1, 1, 2 * G * dh), lambda i, pt: (i, 0, 0)),
                  pl.BlockSpec((1, 2 * G * dh, wb_len), lambda i, pt: (i, 0, 0)),
                  full(wc), full(pe), full(w1f), full(w2), full(kn), full(gsum), full(ovs), full(rep), full(eblk)],
        out_specs=[pl.BlockSpec((1, G * K, dh), lambda i, pt: (i, 0, 0))] * 3,
        scratch_shapes=[pltpu.VMEM((2, n_pages, 2 * G * dh, page_rows), F32),
                        pltpu.VMEM((2, 2 * G * dh, past), F32),
                        pltpu.VMEM((4, past, V7X_LANES), F32),
                        pltpu.SemaphoreType.DMA((2,))],
    )
    out = jax.ShapeDtypeStruct((bsz, G * K, dh), F32)
    o_c, o_s, o_w = pl.pallas_call(
        functools.partial(_nsa_sample_body, n_pages, page_rows),
        out_shape=(out, out, out),
        grid_spec=grid_spec,
        compiler_params=_cparams(("arbitrary",), V7X_VMEM_LIMIT_BYTES),
        name="nsa_sample_attention",
    )(page_table, cache3, qbd, rows_new.reshape(bsz, 1, -1), win_new.reshape(bsz, 1, -1), win3,
      wc, pe, w1f, w2, kn, gsum, ovs, rep, eblk)
    return (o_c.reshape(bsz, -1), o_s.reshape(bsz, -1), o_w.reshape(bsz, -1))


def _mm_residual_gated_body(oc_ref, os_ref, ow_ref, gc_ref, gs_ref, gw_ref, w_ref, x_ref, g_ref, o_ref):
    a = gc_ref[...] * oc_ref[...] + gs_ref[...] * os_ref[...] + gw_ref[...] * ow_ref[...]
    acc = jnp.dot(a.astype(BF16), w_ref[...], preferred_element_type=F32)
    o_ref[...] = x_ref[...] + g_ref[0] * acc


def mm_residual_gated(branches, gates, w_bf, x, gate_mod):
    t, d = x.shape
    full2 = lambda a: pl.BlockSpec(a.shape, lambda i: (0,) * a.ndim)
    args = (*branches, *gates, w_bf, x, gate_mod)
    return pl.pallas_call(
        _mm_residual_gated_body,
        out_shape=jax.ShapeDtypeStruct((t, d), F32),
        grid=(1,),
        in_specs=[full2(a) for a in args],
        out_specs=pl.BlockSpec((t, d), lambda i: (0, 0)),
        compiler_params=_cparams(("arbitrary",)),
        name="mm_residual_gated",
    )(*args)


def nsa_sample_pallas(proj, bsz, cache, page_table, win_buf, q_norm, k_norm, cmp_w1, cmp_pe, cmp_w2):
    G, K, dh = NSA_KV_HEADS, NSA_GROUP, NSA_HEAD_DIM
    q, rows, win, _, _, _, _, gates, _, _ = nsa_prep(proj, 1, bsz, q_norm, k_norm, bsz)
    branches = nsa_sample_attention(cache, page_table, win_buf, q, rows, win, cmp_w1, cmp_pe, cmp_w2, k_norm[0])
    g3 = jnp.transpose(gates[0, :, :, :3 * K], (1, 0, 2)).reshape(bsz, G, K, 3)
    gexp = [jnp.repeat(g3[..., br].reshape(bsz, G * K), dh, axis=1) for br in range(3)]
    rows_out = rows.reshape(bsz, 1, 4, G, dh)
    win_out = jnp.concatenate([win_buf[:, 1:], win.reshape(bsz, 1, 2, G, dh)], axis=1)
    return branches, gexp, rows_out, win_out


def _pad_cols(w, n):
    return jnp.pad(w, ((0, 0), (0, n - w.shape[1])))


def kernel(x_prompt, x_sample, cache_nsa_kv, state_nsa_win, state_ssd_conv, state_ssd, state_rg_conv, state_rg, page_table, c_prompt, c_sample, ada_w, ada_b, norm_mix, norm_ffn, rec_w_in, ssd_conv_w, ssd_conv_b, ssd_dt_bias, ssd_a_log, ssd_d, ssd_norm_w, rg_conv_w, rg_conv_b, rg_wa, rg_ba, rg_wi, rg_bi, rg_lambda, rec_w_out, nsa_w_in, nsa_q_norm, nsa_k_norm, cmp_w1, cmp_pe, cmp_w2, nsa_w_out, router_w, router_b, moe_w1, moe_b1, moe_w2, moe_b2):
    bp, L, d = x_prompt.shape
    bs = x_sample.shape[0]
    depth = ada_w.shape[0]
    tp = bp * L
    xp = x_prompt.reshape(tp, d)
    xs = x_sample.reshape(bs, d)

    n_c = bp + bs
    n_c_pad = -(-n_c // 8) * 8
    c_all = jnp.pad(jnp.concatenate([c_prompt, c_sample], axis=0), ((0, n_c_pad - n_c), (0, 0)))
    ada_w_cat = jnp.concatenate([ada_w[i] for i in range(depth)], axis=1).astype(BF16)
    ada_b_cat = jnp.concatenate([ada_b[i] for i in range(depth)], axis=0)[None, :]
    mod_all = adaln_mod(c_all, ada_w_cat, ada_b_cat)

    outs = {k: [] for k in ('rows_p', 'rows_s', 'win_p', 'win_s', 'sconv_p', 'sconv_s', 'ssm_p', 'ssm_s',
                            'rconv_p', 'rconv_s', 'rg_p', 'rg_s')}

    for i in range(depth):
        j = i // 2
        mod_i = mod_all[:, i * 6 * d:(i + 1) * 6 * d]
        mp = [mod_i[:bp, k * d:(k + 1) * d].reshape(bp, 1, d) for k in range(6)]
        ms = [mod_i[bp:bp + bs, k * d:(k + 1) * d].reshape(1, bs, d) for k in range(6)]
        g_mix = norm_mix[i][None, :]
        g_ffn = norm_ffn[i][None, :]

        if i % 2 == 0:
            w_in = rec_w_in[j]
            s0, s1, s2, s3 = 1024, 1024 + SSD_XBC, 1024 + SSD_XBC + SSD_HEADS, 1024 + SSD_XBC + SSD_HEADS + RG_WIDTH
            w_cat = jnp.concatenate([w_in[:, :s1], w_in[:, s2:], _pad_cols(w_in[:, s1:s2], 512)], axis=1).astype(BF16)
            proj_p = mod_matmul(xp, g_mix, mp[1], mp[0], w_cat, PROJ_ROW_TILE, 512)
            proj_s = mod_matmul(xs, g_mix, ms[1], ms[0], w_cat, bs, 512)

            ssd_w = (ssd_conv_w[j], ssd_conv_b[j], ssd_dt_bias[j], ssd_a_log[j], ssd_d[j], ssd_norm_w[j])
            rg_w = (rg_conv_w[j], rg_conv_b[j], rg_wa[j], rg_ba[j], rg_wi[j], rg_bi[j], rg_lambda[j])
            yp, a1, a2 = ssd_prompt(proj_p, bp, L, *ssd_w)
            rp, a3, a4 = rg_prompt(proj_p, bp, L, *rg_w)
            a4 = a4.reshape(bp, RG_WIDTH)
            ys, b1_, b2_ = ssd_sample(proj_s, state_ssd_conv[j], state_ssd[j], *ssd_w)
            rs, b3_, b4_ = rg_sample(proj_s, state_rg_conv[j], state_rg[j], *rg_w)
            outs['sconv_p'].append(a1); outs['ssm_p'].append(a2); outs['rconv_p'].append(a3); outs['rg_p'].append(a4)
            outs['sconv_s'].append(b1_); outs['ssm_s'].append(b2_); outs['rconv_s'].append(b3_); outs['rg_s'].append(b4_)
            w_out = rec_w_out[j].astype(BF16)
            w_parts = [w_out[:SSD_WIDTH], w_out[SSD_WIDTH:]]
            xp = mm_residual([yp.reshape(tp, -1), rp.reshape(tp, -1)], w_parts, xp, mp[2], ROW_TILE)
            xs = mm_residual([ys.reshape(bs, -1), rs.reshape(bs, -1)], w_parts, xs, ms[2], bs)
        else:
            w_in = nsa_w_in[j]
            w_cat = _pad_cols(w_in, 3072).astype(BF16)
            proj_p = mod_matmul(xp, g_mix, mp[1], mp[0], w_cat, PROJ_ROW_TILE, 512)
            proj_s = mod_matmul(xs, g_mix, ms[1], ms[0], w_cat, bs, 512)
            wts = (nsa_q_norm[j], nsa_k_norm[j], cmp_w1[j], cmp_pe[j], cmp_w2[j])
            op, rp, wp = nsa_prompt_pallas(proj_p, bp, L, *wts)
            br_s, gexp_s, rs, ws = nsa_sample_pallas(proj_s, bs, cache_nsa_kv[j], page_table, state_nsa_win[j], *wts)
            outs['rows_p'].append(rp); outs['win_p'].append(wp); outs['rows_s'].append(rs); outs['win_s'].append(ws)
            w_out = nsa_w_out[j].astype(BF16)
            xp = mm_residual([op.reshape(tp, -1)], [w_out], xp, mp[2], ROW_TILE)
            xs = mm_residual_gated(br_s, gexp_s, w_out, xs, ms[2])

        rw = _pad_cols(router_w[i], V7X_LANES)
        rwh = rw.astype(BF16)
        rwl = (rw - rwh.astype(F32)).astype(BF16)
        rb = jnp.concatenate([router_b[i], jnp.full((V7X_LANES - N_EXPERTS,), -1e30, F32)])[None, :]
        zero_base = jnp.zeros((1, V7X_LANES), F32)
        h_p, e_p, gt_p, rk_p, cnt_p = moe_router(xp, g_ffn, mp[4], mp[3], rwh, rwl, rb, zero_base, ROW_TILE)
        h_s, e_s, gt_s, rk_s, cnt_s = moe_router(xs, g_ffn, ms[4], ms[3], rwh, rwl, rb, cnt_p[-1], bs)
        h_all = jnp.concatenate([h_p, h_s], axis=0)
        t_all = tp + bs
        counts = cnt_s[-1, 0, :N_EXPERTS].astype(jnp.int32)
        padded, pad_start, blk_e, n_used, n_blocks = _moe_block_layout(counts, t_all * TOP_K)
        pstart = jnp.pad(pad_start.astype(F32), (0, V7X_LANES - N_EXPERTS))[None, :]
        dest_tile = t_all // 6 if t_all % 48 == 0 else bs
        dest = moe_dest(jnp.concatenate([e_p, e_s], axis=0), jnp.concatenate([rk_p, rk_s], axis=0), pstart,
                        dest_tile)
        idx = _moe_row_tables(dest[:, :TOP_K], counts, padded, pad_start, n_blocks, t_all)
        y4 = moe_ffn(h_all, blk_e, n_used, idx, moe_w1[i], moe_b1[i], moe_w2[i], moe_b2[i])
        xp = moe_combine(xp, mp[5], gt_p, y4, tp + bs, 0, bs)
        xs = moe_combine(xs, ms[5], gt_s, y4, tp + bs, tp, bs)

    st = lambda k: jnp.stack(outs[k])
    return (xp.reshape(bp, L, d), xs.reshape(bs, 1, d), st('rows_p'), st('rows_s'), st('win_p'), st('win_s'),
            st('sconv_p'), st('sconv_s'), st('ssm_p'), st('ssm_s'), st('rconv_p'), st('rconv_s'),
            st('rg_p'), st('rg_s'))
```

```python
import functools
import math

import jax
import jax.numpy as jnp
import numpy as np
from jax import lax
from jax.experimental import pallas as pl
from jax.experimental.pallas import tpu as pltpu

F32 = jnp.float32
BF16 = jnp.bfloat16

D_MODEL = 1024
NORM_EPS = 1e-6

SSD_WIDTH = 1024
SSD_HEAD_DIM = 64
SSD_HEADS = 16
SSD_GROUPS = 4
SSD_STATE = 128
SSD_CONV = 4
SSD_CHUNK = 128
SSD_XBC = SSD_WIDTH + 2 * SSD_GROUPS * SSD_STATE

RG_WIDTH = 1024
RG_BLOCKS = 16
RG_BLOCK_DIM = 64
RG_C = 8.0

NSA_HEADS = 16
NSA_KV_HEADS = 4
NSA_HEAD_DIM = 64
NSA_GROUP = 4
NSA_Q_WIDTH = 1024
NSA_KV_WIDTH = 256
CMP_LEN = 32
CMP_STRIDE = 16
CMP_R = 2
CMP_HID = 128
SEL_BLOCK = 64
SEL_TOPN = 16
WINDOW = 512
NSA_Q_BLOCK = 64

N_EXPERTS = 32
TOP_K = 4
D_FF = 1024
SWIGLU_LIMIT = 7.0
SWIGLU_ALPHA = 1.702

V7X_LANES = 128
V7X_VMEM_LIMIT_BYTES = 56 * 1024 * 1024

MOE_BLOCK_ROWS = 256
MOE_DMA_UNROLL = 16
ROW_TILE = 512
PROJ_ROW_TILE = 1024


def _cparams(sem, vmem=None):
    return pltpu.CompilerParams(dimension_semantics=sem, vmem_limit_bytes=vmem)


def _adaln_body(c_ref, w_ref, b_ref, o_ref):
    c = c_ref[...]
    s = c * jax.nn.sigmoid(c)
    o_ref[...] = jnp.dot(s.astype(BF16), w_ref[...], preferred_element_type=F32) + b_ref[...]


def adaln_mod(c, w_bf, b):
    r, d = c.shape
    n = w_bf.shape[1]
    tn = 1536
    return pl.pallas_call(
        _adaln_body,
        out_shape=jax.ShapeDtypeStruct((r, n), F32),
        grid=(n // tn,),
        in_specs=[pl.BlockSpec((r, d), lambda j: (0, 0)),
                  pl.BlockSpec((d, tn), lambda j: (0, j)),
                  pl.BlockSpec((1, tn), lambda j: (0, j))],
        out_specs=pl.BlockSpec((r, tn), lambda j: (0, j)),
        compiler_params=_cparams(("arbitrary",)),
        name="adaln_mod",
    )(c, w_bf, b)


def _modulated(x, g, scale, shift):
    ms = jnp.mean(x * x, axis=-1, keepdims=True)
    y = x * lax.rsqrt(ms + NORM_EPS) * g
    return y * (1.0 + scale) + shift


def _mod_matmul_body(x_ref, g_ref, sc_ref, sh_ref, w_ref, o_ref, h_scr):
    @pl.when(pl.program_id(1) == 0)
    def _():
        h_scr[...] = _modulated(x_ref[...], g_ref[...], sc_ref[0], sh_ref[0]).astype(BF16)

    o_ref[...] = jnp.dot(h_scr[...], w_ref[...], preferred_element_type=F32)


def mod_matmul(x, g, scale, shift, w_bf, tm, tn):
    t, d = x.shape
    n = w_bf.shape[1]
    m, r, _ = scale.shape
    rows_per_mod = t // m
    mod_spec = pl.BlockSpec((1, r, d), lambda i, j: ((i * tm) // rows_per_mod, 0, 0))
    return pl.pallas_call(
        _mod_matmul_body,
        out_shape=jax.ShapeDtypeStruct((t, n), F32),
        grid=(t // tm, n // tn),
        in_specs=[pl.BlockSpec((tm, d), lambda i, j: (i, 0)),
                  pl.BlockSpec((1, d), lambda i, j: (0, 0)),
                  mod_spec, mod_spec,
                  pl.BlockSpec((d, tn), lambda i, j: (0, j))],
        out_specs=pl.BlockSpec((tm, tn), lambda i, j: (i, j)),
        scratch_shapes=[pltpu.VMEM((tm, d), BF16)],
        compiler_params=_cparams(("arbitrary", "arbitrary")),
        name="mod_matmul",
    )(x, g, scale, shift, w_bf)


def _mm_residual_body(n_a, *refs):
    a_refs = refs[:n_a]
    w_refs = refs[n_a:2 * n_a]
    x_ref, g_ref, o_ref = refs[2 * n_a:]
    acc = None
    for a_ref, w_ref in zip(a_refs, w_refs):
        p = jnp.dot(a_ref[...].astype(BF16), w_ref[...], preferred_element_type=F32)
        acc = p if acc is None else acc + p
    o_ref[...] = x_ref[...] + g_ref[0] * acc


def mm_residual(a_list, w_list, x, gate, tm):
    t, d = x.shape
    m, r, _ = gate.shape
    rows_per_mod = t // m
    in_specs = [pl.BlockSpec((tm, a.shape[1]), lambda i: (i, 0)) for a in a_list]
    in_specs += [pl.BlockSpec(w.shape, lambda i: (0, 0)) for w in w_list]
    in_specs += [pl.BlockSpec((tm, d), lambda i: (i, 0)),
                 pl.BlockSpec((1, r, d), lambda i: ((i * tm) // rows_per_mod, 0, 0))]
    return pl.pallas_call(
        functools.partial(_mm_residual_body, len(a_list)),
        out_shape=jax.ShapeDtypeStruct((t, d), F32),
        grid=(t // tm,),
        in_specs=in_specs,
        out_specs=pl.BlockSpec((tm, d), lambda i: (i, 0)),
        compiler_params=_cparams(("arbitrary",)),
        name="mm_residual",
    )(*a_list, *w_list, x, gate)


def _router_body(x_ref, g_ref, sc_ref, sh_ref, wh_ref, wl_ref, rb_ref, base_ref, tri_ref,
                 h_ref, e_ref, gt_ref, rank_ref, cnt_ref, carry):
    h = _modulated(x_ref[...], g_ref[...], sc_ref[0], sh_ref[0])
    h_ref[...] = h
    hh = h.astype(BF16)
    hl = (h - hh.astype(F32)).astype(BF16)
    wh = wh_ref[...]
    wl = wl_ref[...]
    logits = (jnp.dot(hh, wh, preferred_element_type=F32)
              + (jnp.dot(hh, wl, preferred_element_type=F32)
                 + jnp.dot(hl, wh, preferred_element_type=F32))) + rb_ref[...]
    lane = lax.broadcasted_iota(jnp.int32, logits.shape, 1)
    neg = jnp.float32(-jnp.inf)
    vals, idxs = [], []
    cur = logits
    for _ in range(TOP_K):
        m = jnp.max(cur, axis=-1, keepdims=True)
        idx = jnp.min(jnp.where(cur == m, lane, V7X_LANES), axis=-1, keepdims=True)
        vals.append(m)
        idxs.append(idx)
        cur = jnp.where(lane == idx, neg, cur)
    exps = [jnp.exp(v - vals[0]) for v in vals]
    den = exps[0] + exps[1] + exps[2] + exps[3]
    e_out = jnp.zeros(logits.shape, jnp.int32)
    g_out = jnp.zeros(logits.shape, F32)
    hot = jnp.zeros(logits.shape, F32)
    for k in range(TOP_K):
        e_out = jnp.where(lane == k, idxs[k], e_out)
        g_out = jnp.where(lane == k, exps[k] / den, g_out)
        hot = hot + jnp.where(lane == idxs[k], 1.0, 0.0)
    e_ref[...] = e_out
    gt_ref[...] = g_out

    @pl.when(pl.program_id(0) == 0)
    def _():
        carry[...] = jnp.broadcast_to(base_ref[...], carry.shape)

    before = jnp.dot(tri_ref[...], hot.astype(BF16), preferred_element_type=F32) + carry[0:1, :]
    rank_out = jnp.zeros(logits.shape, F32)
    for k in range(TOP_K):
        r_k = jnp.sum(jnp.where(lane == idxs[k], before, 0.0), axis=-1, keepdims=True)
        rank_out = jnp.where(lane == k, r_k, rank_out)
    rank_ref[...] = rank_out
    total = carry[0:1, :] + jnp.sum(hot, axis=0, keepdims=True)
    carry[0:1, :] = total
    cnt_ref[0] = total


def moe_router(x, g, scale, shift, wh, wl, rb, base, tm):
    t, d = x.shape
    m, r, _ = scale.shape
    rows_per_mod = t // m
    mod_spec = pl.BlockSpec((1, r, d), lambda i: ((i * tm) // rows_per_mod, 0, 0))
    tri = jnp.asarray(np.tril(np.ones((tm, tm), np.float32), -1), BF16)
    lanes = lambda dt_: jax.ShapeDtypeStruct((t, V7X_LANES), dt_)
    lane_spec = pl.BlockSpec((tm, V7X_LANES), lambda i: (i, 0))
    return pl.pallas_call(
        _router_body,
        out_shape=(jax.ShapeDtypeStruct((t, d), F32), lanes(jnp.int32), lanes(F32), lanes(F32),
                   jax.ShapeDtypeStruct((t // tm, 1, V7X_LANES), F32)),
        grid=(t // tm,),
        in_specs=[pl.BlockSpec((tm, d), lambda i: (i, 0)),
                  pl.BlockSpec((1, d), lambda i: (0, 0)),
                  mod_spec, mod_spec,
                  pl.BlockSpec((d, V7X_LANES), lambda i: (0, 0)),
                  pl.BlockSpec((d, V7X_LANES), lambda i: (0, 0)),
                  pl.BlockSpec((1, V7X_LANES), lambda i: (0, 0)),
                  pl.BlockSpec((1, V7X_LANES), lambda i: (0, 0)),
                  pl.BlockSpec((tm, tm), lambda i: (0, 0))],
        out_specs=(pl.BlockSpec((tm, d), lambda i: (i, 0)), lane_spec, lane_spec, lane_spec,
                   pl.BlockSpec((1, 1, V7X_LANES), lambda i: (i, 0, 0))),
        scratch_shapes=[pltpu.VMEM((8, V7X_LANES), F32)],
        compiler_params=_cparams(("arbitrary",)),
        name="moe_router",
    )(x, g, scale, shift, wh, wl, rb, base, tri)


def _moe_dest_body(e_ref, rank_ref, pstart_ref, o_ref):
    e = e_ref[...]
    lane = lax.broadcasted_iota(jnp.int32, e.shape, 1)
    pstart = pstart_ref[...]
    out = jnp.zeros(e.shape, F32)
    for k in range(TOP_K):
        start_k = jnp.sum(jnp.where(lane == e[:, k:k + 1], pstart, 0.0), axis=-1, keepdims=True)
        out = jnp.where(lane == k, start_k, out)
    o_ref[...] = (out + rank_ref[...]).astype(jnp.int32)


def moe_dest(top_e, rank, pad_start, tm):
    t = top_e.shape[0]
    spec = pl.BlockSpec((tm, V7X_LANES), lambda i: (i, 0))
    return pl.pallas_call(
        _moe_dest_body,
        out_shape=jax.ShapeDtypeStruct((t, V7X_LANES), jnp.int32),
        grid=(t // tm,),
        in_specs=[spec, spec, pl.BlockSpec((1, V7X_LANES), lambda i: (0, 0))],
        out_specs=spec,
        compiler_params=_cparams(("arbitrary",)),
        name="moe_dest",
    )(top_e, rank, pad_start)


def _ffn_body(blk_e_ref, nused_ref,
              idx_hbm, h_hbm, w1_ref, b1_ref, w2_ref, b2_ref,
              out_hbm,
              idx_smem, xbuf, ybuf, w1bf, w2bf, sem_idx, sem_g, sem_s):
    bm = MOE_BLOCK_ROWS
    i = pl.program_id(0)
    n_used = nused_ref[0]

    def idx_copy(blk, slot):
        return pltpu.make_async_copy(idx_hbm.at[blk], idx_smem.at[slot], sem_idx.at[slot])

    def gather_copy(tok, slot, r):
        return pltpu.make_async_copy(h_hbm.at[pl.ds(tok, 1)], xbuf.at[slot, pl.ds(r, 1)], sem_g.at[slot])

    def scatter_copy(dst, slot, r):
        return pltpu.make_async_copy(ybuf.at[slot, pl.ds(r, 1)], out_hbm.at[pl.ds(dst, 1)], sem_s.at[slot])

    def start_gather(islot, slot):
        for r in range(bm):
            gather_copy(idx_smem[islot, r], slot, r).start()

    def wait_gather(slot):
        def body(r, c):
            gather_copy(0, slot, 0).wait()
            return c
        lax.fori_loop(0, bm, body, 0, unroll=MOE_DMA_UNROLL)

    def start_scatter(islot, slot):
        for r in range(bm):
            scatter_copy(idx_smem[islot, bm + r], slot, r).start()

    def wait_scatter(slot):
        def body(r, c):
            scatter_copy(0, slot, 0).wait()
            return c
        lax.fori_loop(0, bm, body, 0, unroll=MOE_DMA_UNROLL)

    @pl.when(i < n_used)
    def _():
        slot = i % 2
        islot = i % 3

        @pl.when(i == 0)
        def _():
            ybuf[...] = jnp.zeros_like(ybuf)
            for sl in range(2):
                tail = pltpu.make_async_copy(ybuf.at[sl], out_hbm.at[pl.ds(out_hbm.shape[0] - (2 - sl) * bm, bm)],
                                             sem_s.at[sl])
                tail.start()
                tail.wait()
            idx_copy(0, 0).start()
            idx_copy(0, 0).wait()
            start_gather(0, 0)

            @pl.when(n_used > 1)
            def _():
                idx_copy(1, 1).start()

        @pl.when(i + 2 < n_used)
        def _():
            idx_copy(i + 2, (i + 2) % 3).start()

        @pl.when(i + 1 < n_used)
        def _():
            idx_copy(i + 1, (i + 1) % 3).wait()
            start_gather((i + 1) % 3, 1 - slot)

        @pl.when(jnp.logical_or(i == 0, blk_e_ref[i] != blk_e_ref[jnp.maximum(i - 1, 0)]))
        def _():
            w1bf[...] = w1_ref[0].astype(BF16)
            w2bf[...] = w2_ref[0].astype(BF16)

        wait_gather(slot)

        @pl.when(i >= 2)
        def _():
            wait_scatter(slot)

        x = xbuf[slot].astype(BF16)
        u = jnp.dot(x, w1bf[...], preferred_element_type=F32) + b1_ref[0]
        gl = jnp.minimum(u[:, :D_FF], SWIGLU_LIMIT)
        lin = jnp.clip(u[:, D_FF:], -SWIGLU_LIMIT, SWIGLU_LIMIT)
        act = gl * jax.nn.sigmoid(SWIGLU_ALPHA * gl) * (lin + 1.0)
        ybuf[slot] = jnp.dot(act.astype(BF16), w2bf[...], preferred_element_type=F32) + b2_ref[0]
        start_scatter(islot, slot)

        @pl.when(i == n_used - 1)
        def _():
            @pl.when(i >= 1)
            def _():
                wait_scatter(1 - slot)
            wait_scatter(slot)


def moe_ffn(h_all, blk_e, n_used, idx, w1, b1, w2, b2):
    t, d = h_all.shape
    bm = MOE_BLOCK_ROWS
    n_blocks = idx.shape[0]
    grid_spec = pltpu.PrefetchScalarGridSpec(
        num_scalar_prefetch=2,
        grid=(n_blocks,),
        in_specs=[pl.BlockSpec(memory_space=pl.ANY),
                  pl.BlockSpec(memory_space=pl.ANY),
                  pl.BlockSpec((1, d, 2 * D_FF), lambda i, be, nu: (be[i], 0, 0)),
                  pl.BlockSpec((1, 1, 2 * D_FF), lambda i, be, nu: (be[i], 0, 0)),
                  pl.BlockSpec((1, D_FF, d), lambda i, be, nu: (be[i], 0, 0)),
                  pl.BlockSpec((1, 1, d), lambda i, be, nu: (be[i], 0, 0))],
        out_specs=pl.BlockSpec(memory_space=pl.ANY),
        scratch_shapes=[pltpu.SMEM((3, 2 * bm), jnp.int32),
                        pltpu.VMEM((2, bm, d), F32),
                        pltpu.VMEM((2, bm, d), F32),
                        pltpu.VMEM((d, 2 * D_FF), BF16),
                        pltpu.VMEM((D_FF, d), BF16),
                        pltpu.SemaphoreType.DMA((3,)),
                        pltpu.SemaphoreType.DMA((2,)),
                        pltpu.SemaphoreType.DMA((2,))],
    )
    return pl.pallas_call(
        _ffn_body,
        out_shape=jax.ShapeDtypeStruct((t * TOP_K + 2 * bm, d), F32),
        grid_spec=grid_spec,
        compiler_params=_cparams(("arbitrary",), V7X_VMEM_LIMIT_BYTES),
        name="moe_ffn",
    )(blk_e, n_used, idx, h_all, w1, b1.reshape(N_EXPERTS, 1, -1), w2, b2.reshape(N_EXPERTS, 1, -1))


def _moe_combine_body(x_ref, g_ref, rg_ref, y0_ref, y1_ref, y2_ref, y3_ref, o_ref):
    rg = rg_ref[...]
    acc = ((rg[:, 0:1] * y0_ref[...] + rg[:, 1:2] * y1_ref[...])
           + (rg[:, 2:3] * y2_ref[...] + rg[:, 3:4] * y3_ref[...]))
    o_ref[...] = x_ref[...] + g_ref[0] * acc


def moe_combine(x, gate, router_gate, y4, t_all, row_off, tm):
    t, d = x.shape
    m, r, _ = gate.shape
    rows_per_mod = t // m
    y_spec = lambda k: pl.BlockSpec((tm, d), lambda i: ((k * t_all + row_off) // tm + i, 0))
    return pl.pallas_call(
        _moe_combine_body,
        out_shape=jax.ShapeDtypeStruct((t, d), F32),
        grid=(t // tm,),
        in_specs=[pl.BlockSpec((tm, d), lambda i: (i, 0)),
                  pl.BlockSpec((1, r, d), lambda i: ((i * tm) // rows_per_mod, 0, 0)),
                  pl.BlockSpec((tm, V7X_LANES), lambda i: (i, 0)),
                  y_spec(0), y_spec(1), y_spec(2), y_spec(3)],
        out_specs=pl.BlockSpec((tm, d), lambda i: (i, 0)),
        compiler_params=_cparams(("arbitrary",)),
        name="moe_combine",
    )(x, gate, router_gate, y4, y4, y4, y4)


def _moe_block_layout(counts, tk):
    bm = MOE_BLOCK_ROWS
    padded = (counts + bm - 1) // bm * bm
    pad_end = jnp.cumsum(padded)
    pad_start = pad_end - padded
    n_blocks = -(-tk // bm) + N_EXPERTS
    blk_start = jnp.arange(n_blocks, dtype=jnp.int32) * bm
    blk_e = jnp.minimum(jnp.sum((pad_end[None, :] <= blk_start[:, None]).astype(jnp.int32), axis=1),
                        N_EXPERTS - 1)
    n_used = (pad_end[-1] // bm).astype(jnp.int32).reshape(1)
    return padded, pad_start, blk_e, n_used, n_blocks


def _moe_row_tables(dest, counts, padded, pad_start, n_blocks, t):
    bm = MOE_BLOCK_ROWS
    tk = t * TOP_K
    n_rows = n_blocks * bm
    big = jnp.int32(2 ** 30)
    p = jnp.arange(bm, dtype=jnp.int32)[None, :]
    e = jnp.arange(N_EXPERTS, dtype=jnp.int32)[:, None]
    pad_keys = jnp.where(p < (padded - counts)[:, None], (pad_start + counts)[:, None] + p, big + e * bm + p)
    n_fill = n_rows - tk - N_EXPERTS * bm
    keys = jnp.concatenate([dest.reshape(tk), pad_keys.reshape(-1), big + N_EXPERTS * bm + jnp.arange(n_fill, dtype=jnp.int32)])
    vals = jnp.concatenate([jnp.arange(tk, dtype=jnp.int32), jnp.full((n_rows - tk,), -1, jnp.int32)])
    _, v = lax.sort((keys, vals), num_keys=1)
    valid = v >= 0
    row = jnp.arange(n_rows, dtype=jnp.int32)
    src_tok = jnp.where(valid, v // TOP_K, 0)
    pad_row = tk + ((row // bm) % 2) * bm + row % bm
    dst_row = jnp.where(valid, (v % TOP_K) * t + v // TOP_K, pad_row)
    return jnp.concatenate([src_tok.reshape(n_blocks, bm), dst_row.reshape(n_blocks, bm)], axis=1)


CONV_TAIL = 8
RG_TILE = 256


def _split3_bf16(x):
    h = x.astype(BF16)
    r = x - h.astype(F32)
    m = r.astype(BF16)
    return h, m, (r - m.astype(F32)).astype(BF16)


def _dot3(parts, w, dims=None):
    if dims is None:
        outs = [jnp.dot(p, w, preferred_element_type=F32) for p in parts]
    else:
        outs = [lax.dot_general(w, p, dims, preferred_element_type=F32) for p in parts]
    return (outs[0] + outs[1]) + outs[2]


def _softplus(x):
    return jnp.maximum(x, 0.0) + jnp.log(1.0 + jnp.exp(-jnp.abs(x)))


def _silu(x):
    return x * jax.nn.sigmoid(x)


def _group_rmsnorm(y, w, n_groups):
    width = y.shape[1] // n_groups
    outs = []
    for g in range(n_groups):
        yg = y[:, g * width:(g + 1) * width]
        outs.append(yg * lax.rsqrt(jnp.mean(yg * yg, axis=-1, keepdims=True) + NORM_EPS))
    return jnp.concatenate(outs, axis=1) * w


def _conv_tile(xbuf, cw_ref, cb_ref, rows):
    y = cb_ref[...]
    for k in range(SSD_CONV):
        y = y + cw_ref[k:k + 1, :] * xbuf[pl.ds(CONV_TAIL - (SSD_CONV - 1) + k, rows), :]
    return y


def _ssd_prompt_body(z_ref, xs_ref, bc_ref, dt_ref, cw_ref, cb_ref, dtb_ref, a_ref, dexp_ref, nw_ref, tri_ref,
                     y_ref, conv_ref, state_ref, xbuf, h_scr):
    q = SSD_CHUNK
    c = pl.program_id(1)
    last = pl.num_programs(1) - 1
    P, N = SSD_HEAD_DIM, SSD_STATE

    @pl.when(c == 0)
    def _():
        xbuf[0:CONV_TAIL, :] = jnp.zeros((CONV_TAIL, SSD_XBC), F32)
        h_scr[...] = jnp.zeros_like(h_scr)

    xbuf[CONV_TAIL:CONV_TAIL + q, 0:SSD_WIDTH] = xs_ref[...]
    xbuf[CONV_TAIL:CONV_TAIL + q, SSD_WIDTH:SSD_XBC] = bc_ref[...]
    xc = _silu(_conv_tile(xbuf, cw_ref, cb_ref, q))

    @pl.when(c == last)
    def _():
        conv_ref[0] = xbuf[CONV_TAIL + q - (SSD_CONV - 1):CONV_TAIL + q, :]

    xbuf[0:CONV_TAIL, :] = xbuf[q:q + CONV_TAIL, :]

    xs = xc[:, 0:SSD_WIDTH]
    bm = xc[:, SSD_WIDTH:SSD_WIDTH + SSD_GROUPS * N].astype(BF16)
    cm = xc[:, SSD_WIDTH + SSD_GROUPS * N:SSD_XBC].astype(BF16)
    dt = _softplus(dt_ref[...] + dtb_ref[...])
    a = dt * a_ref[...]
    a_cs = _dot3(_split3_bf16(a), tri_ref[...], dims=(((1,), (0,)), ((), ())))
    a_cs_t = a_cs.T
    dt_t = dt.T
    a_end_t = a_cs_t[:, q - 1:q]
    w_t = dt_t * jnp.exp(a_end_t - a_cs_t)
    ea = jnp.exp(a_cs)
    xs_t = xs.T
    row = lax.broadcasted_iota(jnp.int32, (q, q), 0)
    col = lax.broadcasted_iota(jnp.int32, (q, q), 1)
    causal = col <= row
    heads_per_group = SSD_HEADS // SSD_GROUPS
    ys = []
    for g in range(SSD_GROUPS):
        bg = bm[:, g * N:(g + 1) * N]
        cg = cm[:, g * N:(g + 1) * N]
        cb = lax.dot_general(cg, bg, _NT_DIMS_SSD, preferred_element_type=F32)
        for k in range(heads_per_group):
            h = g * heads_per_group + k
            seg = a_cs[:, h:h + 1] - a_cs_t[h:h + 1, :]
            decay = jnp.exp(jnp.where(causal, seg, MASK_NEG))
            xh = xs[:, h * P:(h + 1) * P]
            xdt = (xh * dt[:, h:h + 1]).astype(BF16)
            y_diag = jnp.dot((cb * decay).astype(BF16), xdt, preferred_element_type=F32)
            h_prev = h_scr[h]
            y_off = lax.dot_general(cg, h_prev.astype(BF16), _NT_DIMS_SSD,
                                    preferred_element_type=F32) * ea[:, h:h + 1]
            st = jnp.dot((xs_t[h * P:(h + 1) * P, :] * w_t[h:h + 1, :]).astype(BF16), bg,
                         preferred_element_type=F32)
            h_scr[h] = h_prev * jnp.exp(a_end_t[h:h + 1, :]) + st
            ys.append(y_diag + y_off)
    y = jnp.concatenate(ys, axis=1) + dexp_ref[...] * xs
    y = y * _silu(z_ref[...])
    y_ref[...] = _group_rmsnorm(y, nw_ref[...], SSD_GROUPS)

    @pl.when(c == last)
    def _():
        state_ref[0] = h_scr[...]


_NT_DIMS_SSD = (((1,), (1,)), ((), ()))
MASK_NEG = -1e30


def _pad_lanes(v, n=V7X_LANES):
    return jnp.pad(v, (0, n - v.shape[0]))[None, :]


def ssd_prompt(proj, b, seq_len, conv_w, conv_b, dt_bias, a_log, d_skip, norm_w):
    q = SSD_CHUNK
    nc = seq_len // q
    t = b * seq_len
    tri = jnp.asarray(np.tril(np.ones((q, q), np.float32)), BF16)
    a_neg = _pad_lanes(-jnp.exp(a_log))
    dtb = _pad_lanes(dt_bias)
    dexp = jnp.repeat(d_skip, SSD_HEAD_DIM)[None, :]
    nw = norm_w[None, :]
    cb = conv_b[None, :]
    colblk = lambda j, w=SSD_WIDTH: pl.BlockSpec((q, w), lambda bi, c: (bi * nc + c, j))
    full = lambda a: pl.BlockSpec(a.shape, lambda bi, c: (0,) * a.ndim)
    return pl.pallas_call(
        _ssd_prompt_body,
        out_shape=(jax.ShapeDtypeStruct((t, SSD_WIDTH), F32),
                   jax.ShapeDtypeStruct((b, SSD_CONV - 1, SSD_XBC), F32),
                   jax.ShapeDtypeStruct((b, SSD_HEADS, SSD_HEAD_DIM, SSD_STATE), F32)),
        grid=(b, nc),
        in_specs=[colblk(0), colblk(1), colblk(2),
                  pl.BlockSpec((q, V7X_LANES), lambda bi, c: (bi * nc + c, 5 * SSD_WIDTH // V7X_LANES)),
                  full(conv_w), full(cb), full(dtb), full(a_neg), full(dexp), full(nw), full(tri)],
        out_specs=(pl.BlockSpec((q, SSD_WIDTH), lambda bi, c: (bi * nc + c, 0)),
                   pl.BlockSpec((1, SSD_CONV - 1, SSD_XBC), lambda bi, c: (bi, 0, 0)),
                   pl.BlockSpec((1, SSD_HEADS, SSD_HEAD_DIM, SSD_STATE), lambda bi, c: (bi, 0, 0, 0))),
        scratch_shapes=[pltpu.VMEM((CONV_TAIL + q, SSD_XBC), F32),
                        pltpu.VMEM((SSD_HEADS, SSD_HEAD_DIM, SSD_STATE), F32)],
        compiler_params=_cparams(("arbitrary", "arbitrary"), V7X_VMEM_LIMIT_BYTES),
        name="ssd_prompt",
    )(proj, proj, proj, proj, conv_w, cb, dtb, a_neg, dexp, nw, tri)


def _ssd_sample_pre_body(xs_ref, bc_ref, dt_ref, s0_ref, s1_ref, s2_ref, cw_ref, cb_ref, dtb_ref, a_ref, dexp_ref,
                         hexp_ref, yd_ref, xdt_ref, b_ref, c_ref, ea_ref, eaexp_ref):
    N = SSD_STATE
    x_new = jnp.concatenate([xs_ref[...], bc_ref[...]], axis=1)
    y = (cb_ref[...] + cw_ref[0:1, :] * s0_ref[...] + cw_ref[1:2, :] * s1_ref[...]
         + cw_ref[2:3, :] * s2_ref[...] + cw_ref[3:4, :] * x_new)
    xc = _silu(y)
    xs = xc[:, 0:SSD_WIDTH]
    bm = xc[:, SSD_WIDTH:SSD_WIDTH + SSD_GROUPS * N]
    cm = xc[:, SSD_WIDTH + SSD_GROUPS * N:SSD_XBC]
    dt = _softplus(dt_ref[...] + dtb_ref[...])
    ea = jnp.exp(dt * a_ref[...])
    hexp = hexp_ref[...]
    dt_exp = _dot3(_split3_bf16(dt), hexp)
    xdt = (xs * dt_exp).astype(BF16)
    bb = bm.astype(BF16)
    cc = cm.astype(BF16)
    prod = bb.astype(F32) * cc.astype(F32)
    hw = SSD_WIDTH // SSD_GROUPS
    cb = jnp.concatenate(
        [jnp.broadcast_to(jnp.sum(prod[:, g * N:(g + 1) * N], axis=-1, keepdims=True), (xs.shape[0], hw))
         for g in range(SSD_GROUPS)], axis=1)
    yd_ref[...] = cb.astype(BF16).astype(F32) * xdt.astype(F32) + dexp_ref[...] * xs
    xdt_ref[...] = xdt.astype(F32)
    b_ref[...] = bb.astype(F32)
    c_ref[...] = cc.astype(F32)
    ea_ref[...] = ea
    eaexp_ref[...] = _dot3(_split3_bf16(ea), hexp)


def _ssd_sample_state_body(ea_smem, xdt_ref, b_ref, c_ref, yd_ref, eaexp_ref, z_ref, nw_ref, h0_ref,
                           y_ref, h1_ref):
    i = pl.program_id(0)
    N = SSD_STATE
    gw = SSD_WIDTH // SSD_GROUPS
    heads_per_group = SSD_HEADS // SSD_GROUPS
    row0 = lax.broadcasted_iota(jnp.int32, (8, 1), 0) == 0
    y_off = []
    for g in range(SSD_GROUPS):
        x8 = jnp.broadcast_to(xdt_ref[0, :, g * gw:(g + 1) * gw], (8, gw)).astype(BF16)
        b8 = jnp.where(row0, jnp.broadcast_to(b_ref[0, :, g * N:(g + 1) * N], (8, N)), 0.0).astype(BF16)
        c8 = jnp.broadcast_to(c_ref[0, :, g * N:(g + 1) * N], (8, N)).astype(BF16)
        h0g = h0_ref[0, g * gw:(g + 1) * gw, :]
        st = lax.dot_general(x8, b8, (((0,), (0,)), ((), ())), preferred_element_type=F32)
        yo = lax.dot_general(c8, h0g.astype(BF16), _NT_DIMS_SSD, preferred_element_type=F32)
        y_off.append(yo[0:1, :])
        for k in range(heads_per_group):
            h = g * heads_per_group + k
            r = slice(k * SSD_HEAD_DIM, (k + 1) * SSD_HEAD_DIM)
            h1_ref[0, g * gw + k * SSD_HEAD_DIM:g * gw + (k + 1) * SSD_HEAD_DIM, :] = (
                h0g[r, :] * ea_smem[i, h] + st[r, :])
    y = jnp.concatenate(y_off, axis=1) * eaexp_ref[0] + yd_ref[0]
    y = y * _silu(z_ref[0])
    y_ref[0] = _group_rmsnorm(y, nw_ref[...], SSD_GROUPS)


def ssd_sample(proj, conv_state, ssm_state, conv_w, conv_b, dt_bias, a_log, d_skip, norm_w):
    bsz = proj.shape[0]
    H, P, N = SSD_HEADS, SSD_HEAD_DIM, SSD_STATE
    a_neg = _pad_lanes(-jnp.exp(a_log))
    dtb = _pad_lanes(dt_bias)
    dexp = jnp.repeat(d_skip, P)[None, :]
    hexp = jnp.asarray((np.arange(V7X_LANES)[:, None] == (np.arange(H * P)[None, :] // P)).astype(np.float32), BF16)
    cb = conv_b[None, :]
    s0, s1, s2 = conv_state[:, 0], conv_state[:, 1], conv_state[:, 2]
    blk = lambda j, w: pl.BlockSpec((bsz, w), lambda i: (0, j))
    full = lambda a: pl.BlockSpec(a.shape, lambda i: (0,) * a.ndim)
    o = lambda w, dt_: jax.ShapeDtypeStruct((bsz, w), dt_)
    yd, xdt, bb, cc, ea, eaexp = pl.pallas_call(
        _ssd_sample_pre_body,
        out_shape=(o(SSD_WIDTH, F32), o(SSD_WIDTH, F32), o(SSD_GROUPS * N, F32), o(SSD_GROUPS * N, F32),
                   o(V7X_LANES, F32), o(SSD_WIDTH, F32)),
        grid=(1,),
        in_specs=[blk(1, SSD_WIDTH), blk(2, SSD_WIDTH), blk(5 * SSD_WIDTH // V7X_LANES, V7X_LANES),
                  full(s0), full(s1), full(s2), full(conv_w), full(cb), full(dtb), full(a_neg), full(dexp), full(hexp)],
        out_specs=(full(o(SSD_WIDTH, F32)), full(o(SSD_WIDTH, F32)), full(o(SSD_GROUPS * N, F32)),
                   full(o(SSD_GROUPS * N, F32)), full(o(V7X_LANES, F32)), full(o(SSD_WIDTH, F32))),
        compiler_params=_cparams(("arbitrary",)),
        name="ssd_sample_pre",
    )(proj, proj, proj, s0, s1, s2, conv_w, cb, dtb, a_neg, dexp, hexp)
    x_new = jnp.concatenate([proj[:, SSD_WIDTH:2 * SSD_WIDTH], proj[:, 2 * SSD_WIDTH:3 * SSD_WIDTH]], axis=1)
    conv_new = jnp.stack([s1, s2, x_new], axis=1)
    z3 = proj[:, 0:SSD_WIDTH].reshape(bsz, 1, SSD_WIDTH)
    row = lambda w: pl.BlockSpec((1, 1, w), lambda i, ea_: (i, 0, 0))
    nw = norm_w[None, :]
    grid_spec = pltpu.PrefetchScalarGridSpec(
        num_scalar_prefetch=1,
        grid=(bsz,),
        in_specs=[row(SSD_WIDTH), row(SSD_GROUPS * N), row(SSD_GROUPS * N), row(SSD_WIDTH), row(SSD_WIDTH),
                  row(SSD_WIDTH), pl.BlockSpec(nw.shape, lambda i, ea_: (0, 0)),
                  pl.BlockSpec((1, H * P, N), lambda i, ea_: (i, 0, 0))],
        out_specs=(row(SSD_WIDTH), pl.BlockSpec((1, H * P, N), lambda i, ea_: (i, 0, 0))),
    )
    r3 = lambda a: a.reshape(bsz, 1, a.shape[1])
    y, h1 = pl.pallas_call(
        _ssd_sample_state_body,
        out_shape=(jax.ShapeDtypeStruct((bsz, 1, SSD_WIDTH), F32), jax.ShapeDtypeStruct((bsz, H * P, N), F32)),
        grid_spec=grid_spec,
        compiler_params=_cparams(("arbitrary",)),
        name="ssd_sample_state",
    )(ea[:, :H], r3(xdt), r3(bb), r3(cc), r3(yd), r3(eaexp), z3, nw, ssm_state.reshape(bsz, H * P, N))
    return y.reshape(bsz, SSD_WIDTH), conv_new, h1.reshape(bsz, H, P, N)


def _rg_gates(xc, wa_ref, ba_ref, wi_ref, bi_ref, sp_ref):
    xb = xc.astype(BF16)
    r = jax.nn.sigmoid(jnp.dot(xb, wa_ref[...], preferred_element_type=F32) + ba_ref[...])
    ig = jax.nn.sigmoid(jnp.dot(xb, wi_ref[...], preferred_element_type=F32) + bi_ref[...])
    log_a = -RG_C * r * sp_ref[...]
    a = jnp.exp(log_a)
    u = jnp.sqrt(1.0 - jnp.exp(2.0 * log_a)) * (ig * xc)
    return a, u


def _rg_prompt_body(gate_ref, xr_ref, cw_ref, cb_ref, wa_ref, ba_ref, wi_ref, bi_ref, sp_ref,
                    y_ref, conv_ref, state_ref, xbuf, h_scr):
    rows = RG_TILE
    c = pl.program_id(1)
    last = pl.num_programs(1) - 1

    @pl.when(c == 0)
    def _():
        xbuf[0:CONV_TAIL, :] = jnp.zeros((CONV_TAIL, RG_WIDTH), F32)
        h_scr[...] = jnp.zeros_like(h_scr)

    xbuf[CONV_TAIL:CONV_TAIL + rows, :] = xr_ref[...]
    xc = _conv_tile(xbuf, cw_ref, cb_ref, rows)

    @pl.when(c == last)
    def _():
        conv_ref[0] = xbuf[CONV_TAIL + rows - (SSD_CONV - 1):CONV_TAIL + rows, :]

    xbuf[0:CONV_TAIL, :] = xbuf[rows:rows + CONV_TAIL, :]
    a, u = _rg_gates(xc, wa_ref, ba_ref, wi_ref, bi_ref, sp_ref)
    t_idx = lax.broadcasted_iota(jnp.int32, (rows, 1), 0)
    d = 1
    while d < rows:
        keep = t_idx >= d
        a_sh = jnp.where(keep, pltpu.roll(a, d, 0), 1.0)
        u_sh = jnp.where(keep, pltpu.roll(u, d, 0), 0.0)
        u = u + a * u_sh
        a = a * a_sh
        d *= 2
    h = u + a * h_scr[0:1, :]
    h_scr[0:1, :] = h[rows - 1:rows, :]
    y_ref[...] = h * _gelu_tanh(gate_ref[...])

    @pl.when(c == last)
    def _():
        state_ref[0] = h[rows - 1:rows, :]


def _rg_weights(wa, ba, wi, bi, lam):
    eye = jnp.eye(RG_BLOCKS, dtype=F32)
    bd = lambda w: jnp.einsum('nde,nm->ndme', w, eye).reshape(RG_WIDTH, RG_WIDTH).astype(BF16)
    return bd(wa), ba[None, :], bd(wi), bi[None, :], jax.nn.softplus(-lam)[None, :]


def rg_prompt(proj, b, seq_len, conv_w, conv_b, wa, ba, wi, bi, lam):
    rows = RG_TILE
    nt = seq_len // rows
    t = b * seq_len
    wts = _rg_weights(wa, ba, wi, bi, lam)
    cb = conv_b[None, :]
    full = lambda a: pl.BlockSpec(a.shape, lambda bi_, c: (0,) * a.ndim)
    return pl.pallas_call(
        _rg_prompt_body,
        out_shape=(jax.ShapeDtypeStruct((t, RG_WIDTH), F32),
                   jax.ShapeDtypeStruct((b, SSD_CONV - 1, RG_WIDTH), F32),
                   jax.ShapeDtypeStruct((b, 1, RG_WIDTH), F32)),
        grid=(b, nt),
        in_specs=[pl.BlockSpec((rows, RG_WIDTH), lambda bi_, c: (bi_ * nt + c, 3)),
                  pl.BlockSpec((rows, RG_WIDTH), lambda bi_, c: (bi_ * nt + c, 4)),
                  full(conv_w), full(cb)] + [full(w) for w in wts],
        out_specs=(pl.BlockSpec((rows, RG_WIDTH), lambda bi_, c: (bi_ * nt + c, 0)),
                   pl.BlockSpec((1, SSD_CONV - 1, RG_WIDTH), lambda bi_, c: (bi_, 0, 0)),
                   pl.BlockSpec((1, 1, RG_WIDTH), lambda bi_, c: (bi_, 0, 0))),
        scratch_shapes=[pltpu.VMEM((CONV_TAIL + rows, RG_WIDTH), F32), pltpu.VMEM((8, RG_WIDTH), F32)],
        compiler_params=_cparams(("arbitrary", "arbitrary"), V7X_VMEM_LIMIT_BYTES),
        name="rg_prompt",
    )(proj, proj, conv_w, cb, *wts)


def _rg_sample_body(gate_ref, xr_ref, s0_ref, s1_ref, s2_ref, h0_ref, cw_ref, cb_ref, wa_ref, ba_ref, wi_ref, bi_ref,
                    sp_ref, y_ref, h1_ref):
    xc = (cb_ref[...] + cw_ref[0:1, :] * s0_ref[...] + cw_ref[1:2, :] * s1_ref[...]
          + cw_ref[2:3, :] * s2_ref[...] + cw_ref[3:4, :] * xr_ref[...])
    a, u = _rg_gates(xc, wa_ref, ba_ref, wi_ref, bi_ref, sp_ref)
    h = a * h0_ref[...] + u
    h1_ref[...] = h
    y_ref[...] = h * _gelu_tanh(gate_ref[...])


def rg_sample(proj, conv_state, h0, conv_w, conv_b, wa, ba, wi, bi, lam):
    bsz = proj.shape[0]
    wts = _rg_weights(wa, ba, wi, bi, lam)
    cb = conv_b[None, :]
    s0, s1, s2 = conv_state[:, 0], conv_state[:, 1], conv_state[:, 2]
    full = lambda a: pl.BlockSpec(a.shape, lambda i: (0,) * a.ndim)
    out = jax.ShapeDtypeStruct((bsz, RG_WIDTH), F32)
    y, h1 = pl.pallas_call(
        _rg_sample_body,
        out_shape=(out, out),
        grid=(1,),
        in_specs=[pl.BlockSpec((bsz, RG_WIDTH), lambda i: (0, 3)), pl.BlockSpec((bsz, RG_WIDTH), lambda i: (0, 4)),
                  full(s0), full(s1), full(s2), full(h0), full(conv_w), full(cb)] + [full(w) for w in wts],
        out_specs=(full(out), full(out)),
        compiler_params=_cparams(("arbitrary",)),
        name="rg_sample",
    )(proj, proj, s0, s1, s2, h0, conv_w, cb, *wts)
    conv_new = jnp.stack([s1, s2, proj[:, 4 * RG_WIDTH:5 * RG_WIDTH]], axis=1)
    return y, conv_new, h1


def _overlap_matrix(nc, ns):
    i = np.arange(nc)[:, None]
    j = np.arange(ns)[None, :]
    ov = (i * CMP_STRIDE < (j + 1) * SEL_BLOCK) & (i * CMP_STRIDE + CMP_LEN > j * SEL_BLOCK)
    return ov.astype(np.float32)


NSA_TQ = 128
NSA_TK_SLC = 1024
NSA_NS_PAD = 64
SEL_BIAS = -16384.0
MASK_VALUE = -1e30


def _split_bf16(x):
    hi = x.astype(BF16)
    lo = (x - hi.astype(F32)).astype(BF16)
    return hi, lo


def _seg_rms_scale(x, seg, seg_t):
    hi, lo = _split_bf16(x * x)
    ss = jnp.dot(hi, seg, preferred_element_type=F32) + jnp.dot(lo, seg, preferred_element_type=F32)
    r = lax.rsqrt(ss * (1.0 / NSA_HEAD_DIM) + NORM_EPS)
    rh, rl = _split_bf16(r)
    return jnp.dot(rh, seg_t, preferred_element_type=F32) + jnp.dot(rl, seg_t, preferred_element_type=F32)


def _nsa_prep_body(seq_len, p_ref, wq_ref, wks_ref, wkw_ref, segq_ref, segqt_ref, segk_ref, segkt_ref,
                   q_ref, rows_ref, win_ref, kaug_ref, vslc_ref, kwin_ref, vwin_ref, gate_ref, rows_t_ref, win_t_ref):
    tm = p_ref.shape[0]
    dh = NSA_HEAD_DIM
    q = p_ref[:, 0:NSA_Q_WIDTH]
    qn = q * _seg_rms_scale(q, segq_ref[...], segqt_ref[...]) * wq_ref[...]
    kv = [p_ref[:, NSA_Q_WIDTH + NSA_KV_WIDTH * j:NSA_Q_WIDTH + NSA_KV_WIDTH * (j + 1)] for j in range(6)]
    ksl = kv[2] * _seg_rms_scale(kv[2], segk_ref[...], segkt_ref[...]) * wks_ref[...]
    kwn = kv[4] * _seg_rms_scale(kv[4], segk_ref[...], segkt_ref[...]) * wkw_ref[...]
    rows = jnp.concatenate([kv[0], kv[1], ksl, kv[3]], axis=1)
    win = jnp.concatenate([kwn, kv[5]], axis=1)
    rows_ref[...] = rows
    win_ref[...] = win
    rows_t_ref[0] = rows.T
    win_t_ref[0] = win.T
    gates = jax.nn.sigmoid(p_ref[:, NSA_Q_WIDTH + 6 * NSA_KV_WIDTH:NSA_Q_WIDTH + 6 * NSA_KV_WIDTH + V7X_LANES])
    t0 = (pl.program_id(0) * tm) % seq_len
    tpos = t0 + lax.broadcasted_iota(jnp.int32, (tm, NSA_NS_PAD), 0)
    blk = lax.broadcasted_iota(jnp.int32, (tm, NSA_NS_PAD), 1)
    onehot = jnp.where(blk == lax.shift_right_logical(tpos, 6), 1.0, 0.0).astype(BF16)
    for g in range(NSA_KV_HEADS):
        sl = slice(g * dh, (g + 1) * dh)
        kaug_ref[0, g] = jnp.concatenate([ksl[:, sl].astype(BF16), onehot], axis=1)
        vslc_ref[0, g] = kv[3][:, sl].astype(BF16)
        kwin_ref[0, g] = kwn[:, sl].astype(BF16)
        vwin_ref[0, g] = kv[5][:, sl].astype(BF16)
        gate_ref[0, g] = gates if g == 0 else pltpu.roll(gates, V7X_LANES - 3 * NSA_GROUP * g, 1)
        for k in range(NSA_GROUP):
            c0 = (g * NSA_GROUP + k) * dh
            q_ref[0, g, k] = qn[:, c0:c0 + dh].astype(BF16)


def _head_segments(width):
    lane = np.arange(width)[:, None] // NSA_HEAD_DIM
    seg = (lane == np.arange(V7X_LANES)[None, :]).astype(np.float32)
    return jnp.asarray(seg, BF16), jnp.asarray(seg.T, BF16)


def nsa_prep(proj, b, seq_len, q_norm, k_norm, tm):
    t = proj.shape[0]
    G, K, dh = NSA_KV_HEADS, NSA_GROUP, NSA_HEAD_DIM
    wq = (jnp.tile(q_norm, NSA_HEADS) * (dh ** -0.5))[None, :]
    wks = jnp.tile(k_norm[1], G)[None, :]
    wkw = jnp.tile(k_norm[2], G)[None, :]
    segq, segqt = _head_segments(NSA_Q_WIDTH)
    segk, segkt = _head_segments(NSA_KV_WIDTH)
    tiles_per_seq = seq_len // tm
    bi = lambda i: i // tiles_per_seq
    ti = lambda i: i % tiles_per_seq
    full = lambda a: pl.BlockSpec(a.shape, lambda i: (0,) * a.ndim)
    out_shape = (jax.ShapeDtypeStruct((b, G, K, seq_len, dh), BF16),
                 jax.ShapeDtypeStruct((t, 4 * NSA_KV_WIDTH), F32),
                 jax.ShapeDtypeStruct((t, 2 * NSA_KV_WIDTH), F32),
                 jax.ShapeDtypeStruct((b, G, seq_len, 2 * dh), BF16),
                 jax.ShapeDtypeStruct((b, G, seq_len, dh), BF16),
                 jax.ShapeDtypeStruct((b, G, seq_len, dh), BF16),
                 jax.ShapeDtypeStruct((b, G, seq_len, dh), BF16),
                 jax.ShapeDtypeStruct((b, G, seq_len, V7X_LANES), F32),
                 jax.ShapeDtypeStruct((b, 4 * NSA_KV_WIDTH, seq_len), F32),
                 jax.ShapeDtypeStruct((b, 2 * NSA_KV_WIDTH, seq_len), F32))
    per_g = lambda w: pl.BlockSpec((1, G, tm, w), lambda i: (bi(i), 0, ti(i), 0))
    feat_major = lambda w: pl.BlockSpec((1, w, tm), lambda i: (bi(i), 0, ti(i)))
    out_specs = (pl.BlockSpec((1, G, K, tm, dh), lambda i: (bi(i), 0, 0, ti(i), 0)),
                 pl.BlockSpec((tm, 4 * NSA_KV_WIDTH), lambda i: (i, 0)),
                 pl.BlockSpec((tm, 2 * NSA_KV_WIDTH), lambda i: (i, 0)),
                 per_g(2 * dh), per_g(dh), per_g(dh), per_g(dh), per_g(V7X_LANES),
                 feat_major(4 * NSA_KV_WIDTH), feat_major(2 * NSA_KV_WIDTH))
    return pl.pallas_call(
        functools.partial(_nsa_prep_body, seq_len),
        out_shape=out_shape,
        grid=(t // tm,),
        in_specs=[pl.BlockSpec((tm, proj.shape[1]), lambda i: (i, 0)),
                  full(wq), full(wks), full(wkw), full(segq), full(segqt), full(segk), full(segkt)],
        out_specs=out_specs,
        compiler_params=_cparams(("arbitrary",), V7X_VMEM_LIMIT_BYTES),
        name="nsa_prep",
    )(proj, wq, wks, wkw, segq, segqt, segk, segkt)


def _gelu_tanh(x):
    return 0.5 * x * (1.0 + jnp.tanh(math.sqrt(2.0 / math.pi) * (x + 0.044715 * (x * x * x))))


def _nsa_compress_body(n_chunk, x0_ref, x1_ref, x2_ref, x3_ref, wk_ref, wv_ref, pe_ref, w1f_ref, w2_ref, kn_ref,
                       kc_ref, vc_ref, pk_scr, pv_scr):
    s = pl.program_id(1)

    @pl.when(s == 0)
    def _():
        pk_scr[...] = jnp.zeros_like(pk_scr)
        pv_scr[...] = jnp.zeros_like(pv_scr)

    xs = [r[pl.ds(s, n_chunk, stride=CMP_STRIDE), :].astype(BF16) for r in (x0_ref, x1_ref, x2_ref, x3_ref)]
    pk_scr[...] += jnp.dot(jnp.concatenate(xs[0:2], axis=1), wk_ref[0], preferred_element_type=F32)
    pv_scr[...] += jnp.dot(jnp.concatenate(xs[2:4], axis=1), wv_ref[0], preferred_element_type=F32)

    @pl.when(s == CMP_STRIDE - 1)
    def _():
        for kv, p_scr, o_ref in ((0, pk_scr, kc_ref), (1, pv_scr, vc_ref)):
            p = p_scr[...]
            p_next = pltpu.roll(p, n_chunk - 1, 0)
            pe_h = jnp.dot(pe_ref[kv], w1f_ref[kv], preferred_element_type=F32)[0:1, :]
            for g in range(NSA_KV_HEADS):
                c0 = g * 2 * CMP_HID
                hid = pe_h + p[:, c0:c0 + CMP_HID] + p_next[:, c0 + CMP_HID:c0 + 2 * CMP_HID]
                y = jnp.dot(_gelu_tanh(hid).astype(BF16), w2_ref[kv], preferred_element_type=F32)
                if kv == 0:
                    y = y * lax.rsqrt(jnp.mean(y * y, axis=-1, keepdims=True) + NORM_EPS) * kn_ref[...]
                o_ref[0, g] = y.astype(BF16)


def nsa_compress(rows, b, seq_len, cmp_w1, cmp_pe, cmp_w2, k_norm_cmp):
    G, dh = NSA_KV_HEADS, NSA_HEAD_DIM
    n_chunk = seq_len // CMP_STRIDE
    w1 = cmp_w1.reshape(2, CMP_R, CMP_STRIDE, dh, CMP_HID)
    eye = jnp.eye(G, dtype=F32)
    wbd = jnp.einsum('vrsdh,gq->vsgdqrh', w1, eye).reshape(2, CMP_STRIDE, G * dh, G * CMP_R * CMP_HID).astype(BF16)
    pe = jnp.broadcast_to(cmp_pe.reshape(2, 1, CMP_LEN * dh), (2, 8, CMP_LEN * dh)).astype(BF16)
    w1f = cmp_w1.reshape(2, CMP_LEN * dh, CMP_HID).astype(BF16)
    w2 = cmp_w2.astype(BF16)
    kn = k_norm_cmp[None, :]
    full = lambda a: pl.BlockSpec(a.shape, lambda bi, s: (0,) * a.ndim)
    return pl.pallas_call(
        functools.partial(_nsa_compress_body, n_chunk),
        out_shape=(jax.ShapeDtypeStruct((b, G, n_chunk, dh), BF16),
                   jax.ShapeDtypeStruct((b, G, n_chunk, dh), BF16)),
        grid=(b, CMP_STRIDE),
        in_specs=[pl.BlockSpec((seq_len, V7X_LANES), lambda bi, s: (bi, 0)),
                  pl.BlockSpec((seq_len, V7X_LANES), lambda bi, s: (bi, 1)),
                  pl.BlockSpec((seq_len, V7X_LANES), lambda bi, s: (bi, 2)),
                  pl.BlockSpec((seq_len, V7X_LANES), lambda bi, s: (bi, 3)),
                  pl.BlockSpec((1, G * dh, G * CMP_R * CMP_HID), lambda bi, s: (s, 0, 0)),
                  pl.BlockSpec((1, G * dh, G * CMP_R * CMP_HID), lambda bi, s: (s, 0, 0)),
                  full(pe), full(w1f), full(w2), full(kn)],
        out_specs=(pl.BlockSpec((1, G, n_chunk, dh), lambda bi, s: (bi, 0, 0, 0)),
                   pl.BlockSpec((1, G, n_chunk, dh), lambda bi, s: (bi, 0, 0, 0))),
        scratch_shapes=[pltpu.VMEM((n_chunk, G * CMP_R * CMP_HID), F32),
                        pltpu.VMEM((n_chunk, G * CMP_R * CMP_HID), F32)],
        compiler_params=_cparams(("arbitrary", "arbitrary"), V7X_VMEM_LIMIT_BYTES),
        name="nsa_compress",
    )(rows, rows, rows, rows, wbd[0], wbd[1], pe, w1f, w2, kn)


_NT_DIMS = (((1,), (1,)), ((), ()))


def _flash_branch(q2, k_ref, v_ref, n_tiles, tk, last_mask_fn):
    rows = q2.shape[0]

    def step(j, carry, mask_fn):
        m, l, acc = carry
        k0 = pl.multiple_of(j * tk, tk)
        k = k_ref[0, 0, pl.ds(k0, tk), :]
        v = v_ref[0, 0, pl.ds(k0, tk), :]
        s = lax.dot_general(q2, k, _NT_DIMS, preferred_element_type=F32)
        if mask_fn is not None:
            s = jnp.where(mask_fn(k0), s, MASK_VALUE)
        m_new = jnp.maximum(m, jnp.max(s, axis=-1, keepdims=True))
        alpha = jnp.exp(m - m_new)
        p = jnp.exp(s - m_new)
        l = alpha * l + jnp.sum(p, axis=-1, keepdims=True)
        acc = alpha * acc + jnp.dot(p.astype(BF16), v, preferred_element_type=F32)
        return m_new, l, acc

    init = (jnp.full((rows, 1), MASK_VALUE, F32), jnp.zeros((rows, 1), F32),
            jnp.zeros((rows, NSA_HEAD_DIM), F32))
    carry = lax.fori_loop(0, n_tiles - 1, lambda j, c: step(j, c, None), init)
    _, l, acc = step(n_tiles - 1, carry, last_mask_fn)
    return acc / l


def _nsa_attn_body(n_cmp, q_ref, kc_ref, vc_ref, kaug_ref, vslc_ref, kwin_ref, vwin_ref, gate_ref, ovt_ref, o_ref):
    tq = NSA_TQ
    rows = NSA_GROUP * tq
    q0 = pl.program_id(2) * tq
    q2 = q_ref[0, 0].reshape(rows, NSA_HEAD_DIM)
    row_t = q0 + jnp.bitwise_and(lax.broadcasted_iota(jnp.int32, (rows, 1), 0), tq - 1)

    n_pad = kc_ref.shape[2]
    s = lax.dot_general(q2, kc_ref[0, 0], _NT_DIMS, preferred_element_type=F32)
    n_idx = lax.broadcasted_iota(jnp.int32, (1, n_pad), 1)
    cmask = jnp.logical_and(n_idx * CMP_STRIDE + (CMP_LEN - 1) <= row_t, n_idx < n_cmp)
    s = jnp.where(cmask, s, MASK_VALUE)
    m = jnp.max(s, axis=-1, keepdims=True)
    e = jnp.where(cmask, jnp.exp(s - m), 0.0)
    den = jnp.sum(e, axis=-1, keepdims=True)
    p_c = e / jnp.where(den > 0.0, den, 1.0)
    o_c = jnp.dot(p_c.astype(BF16), vc_ref[0, 0], preferred_element_type=F32)

    p_sum = (p_c[0:tq] + p_c[tq:2 * tq]) + (p_c[2 * tq:3 * tq] + p_c[3 * tq:4 * tq])
    ph, plo = _split_bf16(p_sum)
    ovt = ovt_ref[...]
    imp = (lax.dot_general(ovt, ph, _NT_DIMS, preferred_element_type=F32)
           + lax.dot_general(ovt, plo, _NT_DIMS, preferred_element_type=F32))
    blk = lax.broadcasted_iota(jnp.int32, (NSA_NS_PAD, tq), 0)
    jt = lax.shift_right_logical(q0 + lax.broadcasted_iota(jnp.int32, (NSA_NS_PAD, tq), 1), 6)
    valid = blk <= jt
    forced = jnp.logical_and(valid, jnp.logical_or(blk == 0, jnp.logical_or(blk == jt, blk == jt - 1)))
    eff = jnp.where(forced, jnp.inf, jnp.where(valid, imp, -jnp.inf))
    rank = jnp.zeros((NSA_NS_PAD, tq), jnp.int32)
    for j in range(NSA_NS_PAD):
        other = eff[j:j + 1, :]
        ahead = jnp.logical_or(other > eff, jnp.logical_and(other == eff, blk > j))
        rank = rank + ahead.astype(jnp.int32)
    sel = jnp.logical_and(valid, rank < SEL_TOPN)
    sel_bias = jnp.where(sel, 0.0, SEL_BIAS).T.astype(BF16)

    q_aug = jnp.concatenate([q2, jnp.concatenate([sel_bias] * NSA_GROUP, axis=0)], axis=1)
    hi = (q0 + tq - 1) // NSA_TK_SLC + 1

    def slc_mask(k0):
        kpos = k0 + lax.broadcasted_iota(jnp.int32, (1, NSA_TK_SLC), 1)
        return kpos <= row_t

    o_s = _flash_branch(q_aug, kaug_ref, vslc_ref, hi, NSA_TK_SLC, slc_mask)

    span = WINDOW + tq
    w0 = pl.multiple_of(jnp.maximum(q0 - WINDOW, 0), tq)
    kw = kwin_ref[0, 0, pl.ds(w0, span), :]
    vw = vwin_ref[0, 0, pl.ds(w0, span), :]
    s_w = lax.dot_general(q2, kw, _NT_DIMS, preferred_element_type=F32)
    kpos = w0 + lax.broadcasted_iota(jnp.int32, (1, span), 1)
    wmask = jnp.logical_and(kpos <= row_t, kpos > row_t - WINDOW)
    s_w = jnp.where(wmask, s_w, MASK_VALUE)
    p_w = jnp.exp(s_w - jnp.max(s_w, axis=-1, keepdims=True))
    o_w = (jnp.dot(p_w.astype(BF16), vw, preferred_element_type=F32)
           / jnp.sum(p_w, axis=-1, keepdims=True))

    gt = gate_ref[0, 0]
    outs = []
    for k in range(NSA_GROUP):
        r = slice(k * tq, (k + 1) * tq)
        outs.append(gt[:, 3 * k:3 * k + 1] * o_c[r] + gt[:, 3 * k + 1:3 * k + 2] * o_s[r]
                    + gt[:, 3 * k + 2:3 * k + 3] * o_w[r])
    o_ref[...] = jnp.concatenate(outs, axis=1)


def nsa_attention(q, kc, vc, kaug, vslc, kwin, vwin, gates, b, seq_len):
    G, K, dh = NSA_KV_HEADS, NSA_GROUP, NSA_HEAD_DIM
    tq = NSA_TQ
    nq = seq_len // tq
    n_chunk = kc.shape[2]
    n_cmp = n_chunk - CMP_R + 1
    ns = seq_len // SEL_BLOCK
    ovt = np.zeros((NSA_NS_PAD, n_chunk), np.float32)
    ovt[:ns, :n_cmp] = _overlap_matrix(n_cmp, ns).T
    ovt = jnp.asarray(ovt, BF16)
    seq_spec = lambda w: pl.BlockSpec((1, 1, seq_len, w), lambda bi, g, qi: (bi, g, 0, 0))
    return pl.pallas_call(
        functools.partial(_nsa_attn_body, n_cmp),
        out_shape=jax.ShapeDtypeStruct((b * seq_len, NSA_Q_WIDTH), F32),
        grid=(b, G, nq),
        in_specs=[pl.BlockSpec((1, 1, K, tq, dh), lambda bi, g, qi: (bi, g, 0, qi, 0)),
                  pl.BlockSpec((1, 1, n_chunk, dh), lambda bi, g, qi: (bi, g, 0, 0)),
                  pl.BlockSpec((1, 1, n_chunk, dh), lambda bi, g, qi: (bi, g, 0, 0)),
                  seq_spec(2 * dh), seq_spec(dh), seq_spec(dh), seq_spec(dh),
                  pl.BlockSpec((1, 1, tq, V7X_LANES), lambda bi, g, qi: (bi, g, qi, 0)),
                  pl.BlockSpec(ovt.shape, lambda bi, g, qi: (0, 0))],
        out_specs=pl.BlockSpec((tq, K * dh), lambda bi, g, qi: (bi * nq + qi, g)),
        compiler_params=_cparams(("arbitrary", "arbitrary", "arbitrary"), V7X_VMEM_LIMIT_BYTES),
        name="nsa_attention",
    )(q, kc, vc, kaug, vslc, kwin, vwin, gates, ovt)


def nsa_prompt_pallas(proj, b, seq_len, q_norm, k_norm, cmp_w1, cmp_pe, cmp_w2):
    q, rows, _, kaug, vslc, kwin, vwin, gates, rows_t, win_t = nsa_prep(proj, b, seq_len, q_norm, k_norm, ROW_TILE)
    kc, vc = nsa_compress(rows, b, seq_len, cmp_w1, cmp_pe, cmp_w2, k_norm[0])
    o = nsa_attention(q, kc, vc, kaug, vslc, kwin, vwin, gates, b, seq_len)
    G, dh = NSA_KV_HEADS, NSA_HEAD_DIM
    rows_out = jnp.transpose(rows_t.reshape(b, 4, G, dh, seq_len), (0, 4, 1, 2, 3))
    wlen = min(WINDOW, seq_len)
    win_out = jnp.transpose(win_t[:, :, seq_len - wlen:].reshape(b, 2, G, dh, wlen), (0, 4, 1, 2, 3))
    return o, rows_out, win_out


def _diag_heads(o_full):
    g_row = lax.shift_right_logical(lax.broadcasted_iota(jnp.int32, (NSA_HEADS, 1), 0), 2)
    out = jnp.zeros((NSA_HEADS, NSA_HEAD_DIM), F32)
    for g in range(NSA_KV_HEADS):
        out = out + jnp.where(g_row == g, o_full[:, g * NSA_HEAD_DIM:(g + 1) * NSA_HEAD_DIM], 0.0)
    return out


def _nsa_sample_body(n_pages, page_rows, pt_ref,
                     cache_hbm, qbd_ref, rown_ref, winn_ref, winbuf_ref, wc_ref, pe_ref, w1f_ref, w2_ref, kn_ref,
                     gsum_ref, ovs_ref, rep_ref, eblk_ref,
                     oc_ref, os_ref, ow_ref, wout_ref,
                     cmpt_buf, slct_buf, cmp_buf, sem):
    i = pl.program_id(0)
    nb = pl.num_programs(0)
    slot = i % 2
    past = n_pages * page_rows
    n_chunk = past // CMP_STRIDE
    n_cmp = n_chunk - CMP_R + 1
    t_pos = past
    kvw = NSA_KV_WIDTH

    def page_copies(bi, sl):
        copies = []
        for p in range(n_pages):
            pg = pt_ref[bi, p]
            copies.append(pltpu.make_async_copy(cache_hbm.at[pg, pl.ds(0, 2 * kvw), :],
                                                cmpt_buf.at[sl, p], sem.at[sl]))
            copies.append(pltpu.make_async_copy(cache_hbm.at[pg, pl.ds(2 * kvw, 2 * kvw), :],
                                                slct_buf.at[sl, :, pl.ds(p * page_rows, page_rows)], sem.at[sl]))
        return copies

    @pl.when(i == 0)
    def _():
        for c in page_copies(0, 0):
            c.start()

    @pl.when(i + 1 < nb)
    def _():
        for c in page_copies(i + 1, 1 - slot):
            c.start()

    for c in page_copies(i, slot):
        c.wait()

    for p in range(n_pages):
        for j in range(4):
            cmp_buf[j, p * page_rows:(p + 1) * page_rows, :] = (
                cmpt_buf[slot, p, j * V7X_LANES:(j + 1) * V7X_LANES, :].T)

    parts = []
    for j in range(4):
        acc = None
        for s in range(CMP_STRIDE):
            xs = cmp_buf[j, pl.ds(s, n_chunk, stride=CMP_STRIDE), :].astype(BF16)
            d = jnp.dot(xs, wc_ref[j // 2, s], preferred_element_type=F32)
            acc = d if acc is None else acc + d
        parts.append(acc)
    slabs = []
    for kv in range(2):
        p = jnp.concatenate(parts[2 * kv:2 * kv + 2], axis=1)
        p_next = pltpu.roll(p, n_chunk - 1, 0)
        pe_h = jnp.dot(pe_ref[kv], w1f_ref[kv], preferred_element_type=F32)[0:1, :]
        ys = []
        for g in range(NSA_KV_HEADS):
            c0 = g * 2 * CMP_HID
            hid = pe_h + p[:, c0:c0 + CMP_HID] + p_next[:, c0 + CMP_HID:c0 + 2 * CMP_HID]
            y = jnp.dot(_gelu_tanh(hid).astype(BF16), w2_ref[kv], preferred_element_type=F32)
            if kv == 0:
                y = y * lax.rsqrt(jnp.mean(y * y, axis=-1, keepdims=True) + NORM_EPS) * kn_ref[...]
            ys.append(y)
        slabs.append(jnp.concatenate(ys, axis=1).astype(BF16))
    kc, vc = slabs

    qbd = qbd_ref[0]
    qf = qbd.astype(F32)

    s_c = lax.dot_general(qbd, kc, _NT_DIMS, preferred_element_type=F32)
    n_idx = lax.broadcasted_iota(jnp.int32, (1, n_chunk), 1)
    cmask = jnp.logical_and(n_idx * CMP_STRIDE + (CMP_LEN - 1) <= t_pos, n_idx < n_cmp)
    s_c = jnp.where(cmask, s_c, MASK_VALUE)
    m = jnp.max(s_c, axis=-1, keepdims=True)
    e = jnp.where(cmask, jnp.exp(s_c - m), 0.0)
    den = jnp.sum(e, axis=-1, keepdims=True)
    p_c = e / jnp.where(den > 0.0, den, 1.0)
    oc_ref[0] = _diag_heads(jnp.dot(p_c.astype(BF16), vc, preferred_element_type=F32))

    gsum = gsum_ref[...]
    ph, plo = _split_bf16(p_c)
    p_sum = jnp.dot(gsum, ph, preferred_element_type=F32) + jnp.dot(gsum, plo, preferred_element_type=F32)
    sh, slo = _split_bf16(p_sum)
    ovs = ovs_ref[...]
    imp = jnp.dot(sh, ovs, preferred_element_type=F32) + jnp.dot(slo, ovs, preferred_element_type=F32)
    blk = lax.broadcasted_iota(jnp.int32, imp.shape, 1)
    jt = t_pos // SEL_BLOCK
    valid = blk <= jt
    forced = jnp.logical_and(valid, jnp.logical_or(blk == 0, jnp.logical_or(blk == jt, blk == jt - 1)))
    eff = jnp.where(forced, jnp.inf, jnp.where(valid, imp, -jnp.inf))
    rank = jnp.zeros(imp.shape, jnp.int32)
    for j in range(jt + 1):
        other = eff[:, j:j + 1]
        ahead = jnp.logical_or(other > eff, jnp.logical_and(other == eff, blk > j))
        rank = rank + ahead.astype(jnp.int32)
    sel = jnp.logical_and(valid, rank < SEL_TOPN)
    sel_bias = jnp.where(sel, 0.0, SEL_BIAS).astype(BF16)
    bias_h = jnp.dot(rep_ref[...], sel_bias, preferred_element_type=F32).astype(BF16)
    bias_keys = jnp.dot(bias_h, eblk_ref[...], preferred_element_type=F32)

    rn = rown_ref[0]
    ks_t = slct_buf[slot, 0:kvw, :].astype(BF16)
    vs_t = slct_buf[slot, kvw:2 * kvw, :].astype(BF16)
    s_s = jnp.dot(qbd, ks_t, preferred_element_type=F32) + bias_keys
    ks_new = rn[:, 2 * kvw:3 * kvw].astype(BF16).astype(F32)
    vs_new = rn[:, 3 * kvw:4 * kvw].astype(BF16).astype(F32)
    s_new = jnp.sum(qf * ks_new, axis=-1, keepdims=True)
    m = jnp.maximum(jnp.max(s_s, axis=-1, keepdims=True), s_new)
    p = jnp.exp(s_s - m)
    p_new = jnp.exp(s_new - m)
    den = jnp.sum(p, axis=-1, keepdims=True) + p_new
    o_full = (lax.dot_general(p.astype(BF16), vs_t, _NT_DIMS, preferred_element_type=F32)
              + p_new.astype(BF16).astype(F32) * vs_new)
    os_ref[0] = _diag_heads(o_full) / den

    wb = winbuf_ref[0]
    wn = winn_ref[0]
    wb_len = wb.shape[1]
    kw_t = wb[0:kvw, :].astype(BF16)
    vw_t = wb[kvw:2 * kvw, :].astype(BF16)
    s_w = jnp.dot(qbd, kw_t, preferred_element_type=F32)
    w_idx = lax.broadcasted_iota(jnp.int32, (1, wb_len), 1)
    w_pos = t_pos - wb_len + w_idx
    wmask = jnp.logical_and(w_pos > t_pos - WINDOW, w_pos >= 0)
    s_w = jnp.where(wmask, s_w, MASK_VALUE)
    kw_new = wn[:, 0:kvw].astype(BF16).astype(F32)
    vw_new = wn[:, kvw:2 * kvw].astype(BF16).astype(F32)
    s_new = jnp.sum(qf * kw_new, axis=-1, keepdims=True)
    m = jnp.maximum(jnp.max(s_w, axis=-1, keepdims=True), s_new)
    p = jnp.where(wmask, jnp.exp(s_w - m), 0.0)
    p_new = jnp.exp(s_new - m)
    den = jnp.sum(p, axis=-1, keepdims=True) + p_new
    o_full = (lax.dot_general(p.astype(BF16), vw_t, _NT_DIMS, preferred_element_type=F32)
              + p_new.astype(BF16).astype(F32) * vw_new)
    ow_ref[0] = _diag_heads(o_full) / den

    new_col = jnp.broadcast_to(wn, (8, 2 * kvw)).T[:, 0:1]
    shifted = pltpu.roll(wb, wb_len - 1, 1)
    wout_ref[0] = jnp.where(w_idx == wb_len - 1, new_col, shifted)


def nsa_sample_attention(cache, page_table, win_buf, q, rows_new, win_new, cmp_w1, cmp_pe, cmp_w2, k_norm_cmp):
    G, K, dh = NSA_KV_HEADS, NSA_GROUP, NSA_HEAD_DIM
    bsz, n_pages = page_table.shape
    n_phys, page_rows = cache.shape[0], cache.shape[1]
    past = n_pages * page_rows
    n_chunk = past // CMP_STRIDE
    n_cmp = n_chunk - CMP_R + 1
    ns = -(-(past + 1) // SEL_BLOCK)
    cache3 = jnp.transpose(cache, (0, 2, 3, 4, 1)).reshape(n_phys, 4 * G * dh, page_rows)
    wb_len = win_buf.shape[1]
    win3 = jnp.transpose(win_buf, (0, 2, 3, 4, 1)).reshape(bsz, 2 * G * dh, wb_len)
    qh = jnp.transpose(q[0], (2, 0, 1, 3)).astype(F32)
    qbd = jnp.einsum('bgkd,gq->bgkqd', qh, jnp.eye(G, dtype=F32)).reshape(bsz, G * K, G * dh).astype(BF16)
    w1 = cmp_w1.reshape(2, CMP_R, CMP_STRIDE, dh, CMP_HID)
    wc = jnp.einsum('vrsdh,pq->vspdqrh', w1, jnp.eye(2, dtype=F32)).reshape(
        2, CMP_STRIDE, 2 * dh, 2 * CMP_R * CMP_HID).astype(BF16)
    pe = jnp.broadcast_to(cmp_pe.reshape(2, 1, CMP_LEN * dh), (2, 8, CMP_LEN * dh)).astype(BF16)
    w1f = cmp_w1.reshape(2, CMP_LEN * dh, CMP_HID).astype(BF16)
    w2 = cmp_w2.astype(BF16)
    kn = k_norm_cmp[None, :]
    gsum = np.zeros((8, G * K), np.float32)
    gsum[np.arange(G * K) // K, np.arange(G * K)] = 1.0
    ovs = np.zeros((n_chunk, NSA_NS_PAD), np.float32)
    ovs[:n_cmp, :ns] = _overlap_matrix(n_cmp, ns)
    eblk = (np.arange(NSA_NS_PAD)[:, None] == (np.arange(past)[None, :] // SEL_BLOCK)).astype(np.float32)
    gsum, ovs, eblk = jnp.asarray(gsum, BF16), jnp.asarray(ovs, BF16), jnp.asarray(eblk, BF16)
    rep = gsum.T
    full = lambda a: pl.BlockSpec(a.shape, lambda i, pt: (0,) * a.ndim)
    grid_spec = pltpu.PrefetchScalarGridSpec(
        num_scalar_prefetch=1,
        grid=(bsz,),
        in_specs=[pl.BlockSpec(memory_space=pl.ANY),
                  pl.BlockSpec((1, G * K, G * dh), lambda i, pt: (i, 0, 0)),
                  pl.BlockSpec((1, 1, 4 * G * dh), lambda i, pt: (i, 0, 0)),
                  pl.BlockSpec((1, 1, 2 * G * dh), lambda i, pt: (i, 0, 0)),
                  pl.BlockSpec((1, 2 * G * dh, wb_len), lambda i, pt: (i, 0, 0)),
                  full(wc), full(pe), full(w1f), full(w2), full(kn), full(gsum), full(ovs), full(rep), full(eblk)],
        out_specs=[pl.BlockSpec((1, G * K, dh), lambda i, pt: (i, 0, 0))] * 3
        + [pl.BlockSpec((1, 2 * G * dh, wb_len), lambda i, pt: (i, 0, 0))],
        scratch_shapes=[pltpu.VMEM((2, n_pages, 2 * G * dh, page_rows), F32),
                        pltpu.VMEM((2, 2 * G * dh, past), F32),
                        pltpu.VMEM((4, past, V7X_LANES), F32),
                        pltpu.SemaphoreType.DMA((2,))],
    )
    out = jax.ShapeDtypeStruct((bsz, G * K, dh), F32)
    o_c, o_s, o_w, win_next = pl.pallas_call(
        functools.partial(_nsa_sample_body, n_pages, page_rows),
        out_shape=(out, out, out, jax.ShapeDtypeStruct((bsz, 2 * G * dh, wb_len), F32)),
        grid_spec=grid_spec,
        compiler_params=_cparams(("arbitrary",), V7X_VMEM_LIMIT_BYTES),
        name="nsa_sample_attention",
    )(page_table, cache3, qbd, rows_new.reshape(bsz, 1, -1), win_new.reshape(bsz, 1, -1), win3,
      wc, pe, w1f, w2, kn, gsum, ovs, rep, eblk)
    win_next = jnp.transpose(win_next.reshape(bsz, 2, G, dh, wb_len), (0, 4, 1, 2, 3))
    return (o_c.reshape(bsz, -1), o_s.reshape(bsz, -1), o_w.reshape(bsz, -1)), win_next


def _mm_residual_gated_body(oc_ref, os_ref, ow_ref, gc_ref, gs_ref, gw_ref, w_ref, x_ref, g_ref, o_ref):
    a = gc_ref[...] * oc_ref[...] + gs_ref[...] * os_ref[...] + gw_ref[...] * ow_ref[...]
    acc = jnp.dot(a.astype(BF16), w_ref[...], preferred_element_type=F32)
    o_ref[...] = x_ref[...] + g_ref[0] * acc


def mm_residual_gated(branches, gates, w_bf, x, gate_mod):
    t, d = x.shape
    full2 = lambda a: pl.BlockSpec(a.shape, lambda i: (0,) * a.ndim)
    args = (*branches, *gates, w_bf, x, gate_mod)
    return pl.pallas_call(
        _mm_residual_gated_body,
        out_shape=jax.ShapeDtypeStruct((t, d), F32),
        grid=(1,),
        in_specs=[full2(a) for a in args],
        out_specs=pl.BlockSpec((t, d), lambda i: (0, 0)),
        compiler_params=_cparams(("arbitrary",)),
        name="mm_residual_gated",
    )(*args)


def nsa_sample_pallas(proj, bsz, cache, page_table, win_buf, q_norm, k_norm, cmp_w1, cmp_pe, cmp_w2):
    G, K, dh = NSA_KV_HEADS, NSA_GROUP, NSA_HEAD_DIM
    q, rows, win, _, _, _, _, gates, _, _ = nsa_prep(proj, 1, bsz, q_norm, k_norm, bsz)
    branches, win_out = nsa_sample_attention(cache, page_table, win_buf, q, rows, win, cmp_w1, cmp_pe, cmp_w2,
                                             k_norm[0])
    g3 = jnp.transpose(gates[0, :, :, :3 * K], (1, 0, 2)).reshape(bsz, G, K, 3)
    gexp = [jnp.repeat(g3[..., br].reshape(bsz, G * K), dh, axis=1) for br in range(3)]
    rows_out = rows.reshape(bsz, 1, 4, G, dh)
    return branches, gexp, rows_out, win_out


def _pad_cols(w, n):
    return jnp.pad(w, ((0, 0), (0, n - w.shape[1])))


def kernel(x_prompt, x_sample, cache_nsa_kv, state_nsa_win, state_ssd_conv, state_ssd, state_rg_conv, state_rg, page_table, c_prompt, c_sample, ada_w, ada_b, norm_mix, norm_ffn, rec_w_in, ssd_conv_w, ssd_conv_b, ssd_dt_bias, ssd_a_log, ssd_d, ssd_norm_w, rg_conv_w, rg_conv_b, rg_wa, rg_ba, rg_wi, rg_bi, rg_lambda, rec_w_out, nsa_w_in, nsa_q_norm, nsa_k_norm, cmp_w1, cmp_pe, cmp_w2, nsa_w_out, router_w, router_b, moe_w1, moe_b1, moe_w2, moe_b2):
    bp, L, d = x_prompt.shape
    bs = x_sample.shape[0]
    depth = ada_w.shape[0]
    tp = bp * L
    xp = x_prompt.reshape(tp, d)
    xs = x_sample.reshape(bs, d)

    n_c = bp + bs
    n_c_pad = -(-n_c // 8) * 8
    c_all = jnp.pad(jnp.concatenate([c_prompt, c_sample], axis=0), ((0, n_c_pad - n_c), (0, 0)))
    ada_w_cat = jnp.concatenate([ada_w[i] for i in range(depth)], axis=1).astype(BF16)
    ada_b_cat = jnp.concatenate([ada_b[i] for i in range(depth)], axis=0)[None, :]
    mod_all = adaln_mod(c_all, ada_w_cat, ada_b_cat)

    outs = {k: [] for k in ('rows_p', 'rows_s', 'win_p', 'win_s', 'sconv_p', 'sconv_s', 'ssm_p', 'ssm_s',
                            'rconv_p', 'rconv_s', 'rg_p', 'rg_s')}

    for i in range(depth):
        j = i // 2
        mod_i = mod_all[:, i * 6 * d:(i + 1) * 6 * d]
        mp = [mod_i[:bp, k * d:(k + 1) * d].reshape(bp, 1, d) for k in range(6)]
        ms = [mod_i[bp:bp + bs, k * d:(k + 1) * d].reshape(1, bs, d) for k in range(6)]
        g_mix = norm_mix[i][None, :]
        g_ffn = norm_ffn[i][None, :]

        if i % 2 == 0:
            w_in = rec_w_in[j]
            s0, s1, s2, s3 = 1024, 1024 + SSD_XBC, 1024 + SSD_XBC + SSD_HEADS, 1024 + SSD_XBC + SSD_HEADS + RG_WIDTH
            w_cat = jnp.concatenate([w_in[:, :s1], w_in[:, s2:], _pad_cols(w_in[:, s1:s2], 512)], axis=1).astype(BF16)
            proj_p = mod_matmul(xp, g_mix, mp[1], mp[0], w_cat, PROJ_ROW_TILE, 512)
            proj_s = mod_matmul(xs, g_mix, ms[1], ms[0], w_cat, bs, 512)

            ssd_w = (ssd_conv_w[j], ssd_conv_b[j], ssd_dt_bias[j], ssd_a_log[j], ssd_d[j], ssd_norm_w[j])
            rg_w = (rg_conv_w[j], rg_conv_b[j], rg_wa[j], rg_ba[j], rg_wi[j], rg_bi[j], rg_lambda[j])
            yp, a1, a2 = ssd_prompt(proj_p, bp, L, *ssd_w)
            rp, a3, a4 = rg_prompt(proj_p, bp, L, *rg_w)
            a4 = a4.reshape(bp, RG_WIDTH)
            ys, b1_, b2_ = ssd_sample(proj_s, state_ssd_conv[j], state_ssd[j], *ssd_w)
            rs, b3_, b4_ = rg_sample(proj_s, state_rg_conv[j], state_rg[j], *rg_w)
            outs['sconv_p'].append(a1); outs['ssm_p'].append(a2); outs['rconv_p'].append(a3); outs['rg_p'].append(a4)
            outs['sconv_s'].append(b1_); outs['ssm_s'].append(b2_); outs['rconv_s'].append(b3_); outs['rg_s'].append(b4_)
            w_out = rec_w_out[j].astype(BF16)
            w_parts = [w_out[:SSD_WIDTH], w_out[SSD_WIDTH:]]
            xp = mm_residual([yp.reshape(tp, -1), rp.reshape(tp, -1)], w_parts, xp, mp[2], ROW_TILE)
            xs = mm_residual([ys.reshape(bs, -1), rs.reshape(bs, -1)], w_parts, xs, ms[2], bs)
        else:
            w_in = nsa_w_in[j]
            w_cat = _pad_cols(w_in, 3072).astype(BF16)
            proj_p = mod_matmul(xp, g_mix, mp[1], mp[0], w_cat, PROJ_ROW_TILE, 512)
            proj_s = mod_matmul(xs, g_mix, ms[1], ms[0], w_cat, bs, 512)
            wts = (nsa_q_norm[j], nsa_k_norm[j], cmp_w1[j], cmp_pe[j], cmp_w2[j])
            op, rp, wp = nsa_prompt_pallas(proj_p, bp, L, *wts)
            br_s, gexp_s, rs, ws = nsa_sample_pallas(proj_s, bs, cache_nsa_kv[j], page_table, state_nsa_win[j], *wts)
            outs['rows_p'].append(rp); outs['win_p'].append(wp); outs['rows_s'].append(rs); outs['win_s'].append(ws)
            w_out = nsa_w_out[j].astype(BF16)
            xp = mm_residual([op.reshape(tp, -1)], [w_out], xp, mp[2], ROW_TILE)
            xs = mm_residual_gated(br_s, gexp_s, w_out, xs, ms[2])

        rw = _pad_cols(router_w[i], V7X_LANES)
        rwh = rw.astype(BF16)
        rwl = (rw - rwh.astype(F32)).astype(BF16)
        rb = jnp.concatenate([router_b[i], jnp.full((V7X_LANES - N_EXPERTS,), -1e30, F32)])[None, :]
        zero_base = jnp.zeros((1, V7X_LANES), F32)
        h_p, e_p, gt_p, rk_p, cnt_p = moe_router(xp, g_ffn, mp[4], mp[3], rwh, rwl, rb, zero_base, ROW_TILE)
        h_s, e_s, gt_s, rk_s, cnt_s = moe_router(xs, g_ffn, ms[4], ms[3], rwh, rwl, rb, cnt_p[-1], bs)
        h_all = jnp.concatenate([h_p, h_s], axis=0)
        t_all = tp + bs
        counts = cnt_s[-1, 0, :N_EXPERTS].astype(jnp.int32)
        padded, pad_start, blk_e, n_used, n_blocks = _moe_block_layout(counts, t_all * TOP_K)
        pstart = jnp.pad(pad_start.astype(F32), (0, V7X_LANES - N_EXPERTS))[None, :]
        dest_tile = t_all // 6 if t_all % 48 == 0 else bs
        dest = moe_dest(jnp.concatenate([e_p, e_s], axis=0), jnp.concatenate([rk_p, rk_s], axis=0), pstart,
                        dest_tile)
        idx = _moe_row_tables(dest[:, :TOP_K], counts, padded, pad_start, n_blocks, t_all)
        y4 = moe_ffn(h_all, blk_e, n_used, idx, moe_w1[i], moe_b1[i], moe_w2[i], moe_b2[i])
        xp = moe_combine(xp, mp[5], gt_p, y4, tp + bs, 0, bs)
        xs = moe_combine(xs, ms[5], gt_s, y4, tp + bs, tp, bs)

    st = lambda k: jnp.stack(outs[k])
    return (xp.reshape(bp, L, d), xs.reshape(bs, 1, d), st('rows_p'), st('rows_s'), st('win_p'), st('win_s'),
            st('sconv_p'), st('sconv_s'), st('ssm_p'), st('ssm_s'), st('rconv_p'), st('rconv_s'),
            st('rg_p'), st('rg_s'))
```

```python
import functools
import math

import jax
import jax.numpy as jnp
import numpy as np
from jax import lax
from jax.experimental import pallas as pl
from jax.experimental.pallas import tpu as pltpu

F32 = jnp.float32
BF16 = jnp.bfloat16

D_MODEL = 1024
NORM_EPS = 1e-6

SSD_WIDTH = 1024
SSD_HEAD_DIM = 64
SSD_HEADS = 16
SSD_GROUPS = 4
SSD_STATE = 128
SSD_CONV = 4
SSD_CHUNK = 128
SSD_XBC = SSD_WIDTH + 2 * SSD_GROUPS * SSD_STATE

RG_WIDTH = 1024
RG_BLOCKS = 16
RG_BLOCK_DIM = 64
RG_C = 8.0

NSA_HEADS = 16
NSA_KV_HEADS = 4
NSA_HEAD_DIM = 64
NSA_GROUP = 4
NSA_Q_WIDTH = 1024
NSA_KV_WIDTH = 256
CMP_LEN = 32
CMP_STRIDE = 16
CMP_R = 2
CMP_HID = 128
SEL_BLOCK = 64
SEL_TOPN = 16
WINDOW = 512
NSA_Q_BLOCK = 64

N_EXPERTS = 32
TOP_K = 4
D_FF = 1024
SWIGLU_LIMIT = 7.0
SWIGLU_ALPHA = 1.702

V7X_LANES = 128
V7X_VMEM_LIMIT_BYTES = 56 * 1024 * 1024

MOE_BLOCK_ROWS = 256
MOE_DMA_UNROLL = 16
ROW_TILE = 512
PROJ_ROW_TILE = 1024


def _cparams(sem, vmem=None):
    return pltpu.CompilerParams(dimension_semantics=sem, vmem_limit_bytes=vmem)


def _adaln_body(c_ref, w_ref, b_ref, o_ref):
    c = c_ref[...]
    s = c * jax.nn.sigmoid(c)
    o_ref[...] = jnp.dot(s.astype(BF16), w_ref[...], preferred_element_type=F32) + b_ref[...]


def adaln_mod(c, w_bf, b):
    r, d = c.shape
    n = w_bf.shape[1]
    tn = 1536
    return pl.pallas_call(
        _adaln_body,
        out_shape=jax.ShapeDtypeStruct((r, n), F32),
        grid=(n // tn,),
        in_specs=[pl.BlockSpec((r, d), lambda j: (0, 0)),
                  pl.BlockSpec((d, tn), lambda j: (0, j)),
                  pl.BlockSpec((1, tn), lambda j: (0, j))],
        out_specs=pl.BlockSpec((r, tn), lambda j: (0, j)),
        compiler_params=_cparams(("arbitrary",)),
        name="adaln_mod",
    )(c, w_bf, b)


def _modulated(x, g, scale, shift):
    ms = jnp.mean(x * x, axis=-1, keepdims=True)
    y = x * lax.rsqrt(ms + NORM_EPS) * g
    return y * (1.0 + scale) + shift


def _mod_matmul_body(x_ref, g_ref, sc_ref, sh_ref, w_ref, o_ref, h_scr):
    @pl.when(pl.program_id(1) == 0)
    def _():
        h_scr[...] = _modulated(x_ref[...], g_ref[...], sc_ref[0], sh_ref[0]).astype(BF16)

    o_ref[...] = jnp.dot(h_scr[...], w_ref[...], preferred_element_type=F32)


def mod_matmul(x, g, scale, shift, w_bf, tm, tn):
    t, d = x.shape
    n = w_bf.shape[1]
    m, r, _ = scale.shape
    rows_per_mod = t // m
    mod_spec = pl.BlockSpec((1, r, d), lambda i, j: ((i * tm) // rows_per_mod, 0, 0))
    return pl.pallas_call(
        _mod_matmul_body,
        out_shape=jax.ShapeDtypeStruct((t, n), F32),
        grid=(t // tm, n // tn),
        in_specs=[pl.BlockSpec((tm, d), lambda i, j: (i, 0)),
                  pl.BlockSpec((1, d), lambda i, j: (0, 0)),
                  mod_spec, mod_spec,
                  pl.BlockSpec((d, tn), lambda i, j: (0, j))],
        out_specs=pl.BlockSpec((tm, tn), lambda i, j: (i, j)),
        scratch_shapes=[pltpu.VMEM((tm, d), BF16)],
        compiler_params=_cparams(("arbitrary", "arbitrary")),
        name="mod_matmul",
    )(x, g, scale, shift, w_bf)


def _mm_residual_body(n_a, *refs):
    a_refs = refs[:n_a]
    w_refs = refs[n_a:2 * n_a]
    x_ref, g_ref, o_ref = refs[2 * n_a:]
    acc = None
    for a_ref, w_ref in zip(a_refs, w_refs):
        p = jnp.dot(a_ref[...].astype(BF16), w_ref[...], preferred_element_type=F32)
        acc = p if acc is None else acc + p
    o_ref[...] = x_ref[...] + g_ref[0] * acc


def mm_residual(a_list, w_list, x, gate, tm):
    t, d = x.shape
    m, r, _ = gate.shape
    rows_per_mod = t // m
    in_specs = [pl.BlockSpec((tm, a.shape[1]), lambda i: (i, 0)) for a in a_list]
    in_specs += [pl.BlockSpec(w.shape, lambda i: (0, 0)) for w in w_list]
    in_specs += [pl.BlockSpec((tm, d), lambda i: (i, 0)),
                 pl.BlockSpec((1, r, d), lambda i: ((i * tm) // rows_per_mod, 0, 0))]
    return pl.pallas_call(
        functools.partial(_mm_residual_body, len(a_list)),
        out_shape=jax.ShapeDtypeStruct((t, d), F32),
        grid=(t // tm,),
        in_specs=in_specs,
        out_specs=pl.BlockSpec((tm, d), lambda i: (i, 0)),
        compiler_params=_cparams(("arbitrary",)),
        name="mm_residual",
    )(*a_list, *w_list, x, gate)


def _router_body(x_ref, g_ref, sc_ref, sh_ref, wh_ref, wl_ref, rb_ref, base_ref, tri_ref,
                 h_ref, e_ref, gt_ref, rank_ref, cnt_ref, carry):
    h = _modulated(x_ref[...], g_ref[...], sc_ref[0], sh_ref[0])
    h_ref[...] = h
    hh = h.astype(BF16)
    hl = (h - hh.astype(F32)).astype(BF16)
    wh = wh_ref[...]
    wl = wl_ref[...]
    logits = (jnp.dot(hh, wh, preferred_element_type=F32)
              + (jnp.dot(hh, wl, preferred_element_type=F32)
                 + jnp.dot(hl, wh, preferred_element_type=F32))) + rb_ref[...]
    lane = lax.broadcasted_iota(jnp.int32, logits.shape, 1)
    neg = jnp.float32(-jnp.inf)
    vals, idxs = [], []
    cur = logits
    for _ in range(TOP_K):
        m = jnp.max(cur, axis=-1, keepdims=True)
        idx = jnp.min(jnp.where(cur == m, lane, V7X_LANES), axis=-1, keepdims=True)
        vals.append(m)
        idxs.append(idx)
        cur = jnp.where(lane == idx, neg, cur)
    exps = [jnp.exp(v - vals[0]) for v in vals]
    den = exps[0] + exps[1] + exps[2] + exps[3]
    e_out = jnp.zeros(logits.shape, jnp.int32)
    g_out = jnp.zeros(logits.shape, F32)
    hot = jnp.zeros(logits.shape, F32)
    for k in range(TOP_K):
        e_out = jnp.where(lane == k, idxs[k], e_out)
        g_out = jnp.where(lane == k, exps[k] / den, g_out)
        hot = hot + jnp.where(lane == idxs[k], 1.0, 0.0)
    e_ref[...] = e_out
    gt_ref[...] = g_out

    @pl.when(pl.program_id(0) == 0)
    def _():
        carry[...] = jnp.broadcast_to(base_ref[...], carry.shape)

    before = jnp.dot(tri_ref[...], hot.astype(BF16), preferred_element_type=F32) + carry[0:1, :]
    rank_out = jnp.zeros(logits.shape, F32)
    for k in range(TOP_K):
        r_k = jnp.sum(jnp.where(lane == idxs[k], before, 0.0), axis=-1, keepdims=True)
        rank_out = jnp.where(lane == k, r_k, rank_out)
    rank_ref[...] = rank_out
    total = carry[0:1, :] + jnp.sum(hot, axis=0, keepdims=True)
    carry[0:1, :] = total
    cnt_ref[0] = total


def moe_router(x, g, scale, shift, wh, wl, rb, base, tm):
    t, d = x.shape
    m, r, _ = scale.shape
    rows_per_mod = t // m
    mod_spec = pl.BlockSpec((1, r, d), lambda i: ((i * tm) // rows_per_mod, 0, 0))
    tri = jnp.asarray(np.tril(np.ones((tm, tm), np.float32), -1), BF16)
    lanes = lambda dt_: jax.ShapeDtypeStruct((t, V7X_LANES), dt_)
    lane_spec = pl.BlockSpec((tm, V7X_LANES), lambda i: (i, 0))
    return pl.pallas_call(
        _router_body,
        out_shape=(jax.ShapeDtypeStruct((t, d), F32), lanes(jnp.int32), lanes(F32), lanes(F32),
                   jax.ShapeDtypeStruct((t // tm, 1, V7X_LANES), F32)),
        grid=(t // tm,),
        in_specs=[pl.BlockSpec((tm, d), lambda i: (i, 0)),
                  pl.BlockSpec((1, d), lambda i: (0, 0)),
                  mod_spec, mod_spec,
                  pl.BlockSpec((d, V7X_LANES), lambda i: (0, 0)),
                  pl.BlockSpec((d, V7X_LANES), lambda i: (0, 0)),
                  pl.BlockSpec((1, V7X_LANES), lambda i: (0, 0)),
                  pl.BlockSpec((1, V7X_LANES), lambda i: (0, 0)),
                  pl.BlockSpec((tm, tm), lambda i: (0, 0))],
        out_specs=(pl.BlockSpec((tm, d), lambda i: (i, 0)), lane_spec, lane_spec, lane_spec,
                   pl.BlockSpec((1, 1, V7X_LANES), lambda i: (i, 0, 0))),
        scratch_shapes=[pltpu.VMEM((8, V7X_LANES), F32)],
        compiler_params=_cparams(("arbitrary",)),
        name="moe_router",
    )(x, g, scale, shift, wh, wl, rb, base, tri)


def _moe_dest_body(e_ref, rank_ref, pstart_ref, o_ref):
    e = e_ref[...]
    lane = lax.broadcasted_iota(jnp.int32, e.shape, 1)
    pstart = pstart_ref[...]
    out = jnp.zeros(e.shape, F32)
    for k in range(TOP_K):
        start_k = jnp.sum(jnp.where(lane == e[:, k:k + 1], pstart, 0.0), axis=-1, keepdims=True)
        out = jnp.where(lane == k, start_k, out)
    o_ref[...] = (out + rank_ref[...]).astype(jnp.int32)


def moe_dest(top_e, rank, pad_start, tm):
    t = top_e.shape[0]
    spec = pl.BlockSpec((tm, V7X_LANES), lambda i: (i, 0))
    return pl.pallas_call(
        _moe_dest_body,
        out_shape=jax.ShapeDtypeStruct((t, V7X_LANES), jnp.int32),
        grid=(t // tm,),
        in_specs=[spec, spec, pl.BlockSpec((1, V7X_LANES), lambda i: (0, 0))],
        out_specs=spec,
        compiler_params=_cparams(("arbitrary",)),
        name="moe_dest",
    )(top_e, rank, pad_start)


def _ffn_body(blk_e_ref, nused_ref,
              idx_hbm, h_hbm, w1_ref, b1_ref, w2_ref, b2_ref,
              out_hbm,
              idx_smem, xbuf, ybuf, w1bf, w2bf, sem_idx, sem_g, sem_s):
    bm = MOE_BLOCK_ROWS
    i = pl.program_id(0)
    n_used = nused_ref[0]

    def idx_copy(blk, slot):
        return pltpu.make_async_copy(idx_hbm.at[blk], idx_smem.at[slot], sem_idx.at[slot])

    def gather_copy(tok, slot, r):
        return pltpu.make_async_copy(h_hbm.at[pl.ds(tok, 1)], xbuf.at[slot, pl.ds(r, 1)], sem_g.at[slot])

    def scatter_copy(dst, slot, r):
        return pltpu.make_async_copy(ybuf.at[slot, pl.ds(r, 1)], out_hbm.at[pl.ds(dst, 1)], sem_s.at[slot])

    def start_gather(islot, slot):
        for r in range(bm):
            gather_copy(idx_smem[islot, r], slot, r).start()

    def wait_gather(slot):
        def body(r, c):
            gather_copy(0, slot, 0).wait()
            return c
        lax.fori_loop(0, bm, body, 0, unroll=MOE_DMA_UNROLL)

    def start_scatter(islot, slot):
        for r in range(bm):
            scatter_copy(idx_smem[islot, bm + r], slot, r).start()

    def wait_scatter(slot):
        def body(r, c):
            scatter_copy(0, slot, 0).wait()
            return c
        lax.fori_loop(0, bm, body, 0, unroll=MOE_DMA_UNROLL)

    @pl.when(i < n_used)
    def _():
        slot = i % 2
        islot = i % 3

        @pl.when(i == 0)
        def _():
            ybuf[...] = jnp.zeros_like(ybuf)
            for sl in range(2):
                tail = pltpu.make_async_copy(ybuf.at[sl], out_hbm.at[pl.ds(out_hbm.shape[0] - (2 - sl) * bm, bm)],
                                             sem_s.at[sl])
                tail.start()
                tail.wait()
            idx_copy(0, 0).start()
            idx_copy(0, 0).wait()
            start_gather(0, 0)

            @pl.when(n_used > 1)
            def _():
                idx_copy(1, 1).start()

        @pl.when(i + 2 < n_used)
        def _():
            idx_copy(i + 2, (i + 2) % 3).start()

        @pl.when(i + 1 < n_used)
        def _():
            idx_copy(i + 1, (i + 1) % 3).wait()
            start_gather((i + 1) % 3, 1 - slot)

        @pl.when(jnp.logical_or(i == 0, blk_e_ref[i] != blk_e_ref[jnp.maximum(i - 1, 0)]))
        def _():
            w1bf[...] = w1_ref[0, 0].astype(BF16)
            w2bf[...] = w2_ref[0, 0].astype(BF16)

        wait_gather(slot)

        @pl.when(i >= 2)
        def _():
            wait_scatter(slot)

        x = xbuf[slot].astype(BF16)
        u = jnp.dot(x, w1bf[...], preferred_element_type=F32) + b1_ref[0, 0]
        gl = jnp.minimum(u[:, :D_FF], SWIGLU_LIMIT)
        lin = jnp.clip(u[:, D_FF:], -SWIGLU_LIMIT, SWIGLU_LIMIT)
        act = gl * jax.nn.sigmoid(SWIGLU_ALPHA * gl) * (lin + 1.0)
        ybuf[slot] = jnp.dot(act.astype(BF16), w2bf[...], preferred_element_type=F32) + b2_ref[0, 0]
        start_scatter(islot, slot)

        @pl.when(i == n_used - 1)
        def _():
            @pl.when(i >= 1)
            def _():
                wait_scatter(1 - slot)
            wait_scatter(slot)


def moe_ffn(h_all, blk_e, n_used, idx, w1, b1, w2, b2, layer):
    t, d = h_all.shape
    bm = MOE_BLOCK_ROWS
    n_blocks = idx.shape[0]
    grid_spec = pltpu.PrefetchScalarGridSpec(
        num_scalar_prefetch=2,
        grid=(n_blocks,),
        in_specs=[pl.BlockSpec(memory_space=pl.ANY),
                  pl.BlockSpec(memory_space=pl.ANY),
                  pl.BlockSpec((1, 1, d, 2 * D_FF), lambda i, be, nu: (layer, be[i], 0, 0)),
                  pl.BlockSpec((1, 1, 1, 2 * D_FF), lambda i, be, nu: (layer, be[i], 0, 0)),
                  pl.BlockSpec((1, 1, D_FF, d), lambda i, be, nu: (layer, be[i], 0, 0)),
                  pl.BlockSpec((1, 1, 1, d), lambda i, be, nu: (layer, be[i], 0, 0))],
        out_specs=pl.BlockSpec(memory_space=pl.ANY),
        scratch_shapes=[pltpu.SMEM((3, 2 * bm), jnp.int32),
                        pltpu.VMEM((2, bm, d), F32),
                        pltpu.VMEM((2, bm, d), F32),
                        pltpu.VMEM((d, 2 * D_FF), BF16),
                        pltpu.VMEM((D_FF, d), BF16),
                        pltpu.SemaphoreType.DMA((3,)),
                        pltpu.SemaphoreType.DMA((2,)),
                        pltpu.SemaphoreType.DMA((2,))],
    )
    return pl.pallas_call(
        _ffn_body,
        out_shape=jax.ShapeDtypeStruct((t * TOP_K + 2 * bm, d), F32),
        grid_spec=grid_spec,
        compiler_params=_cparams(("arbitrary",), V7X_VMEM_LIMIT_BYTES),
        name="moe_ffn",
    )(blk_e, n_used, idx, h_all, w1, b1.reshape(b1.shape[0], N_EXPERTS, 1, -1), w2,
      b2.reshape(b2.shape[0], N_EXPERTS, 1, -1))


def _moe_combine_body(x_ref, g_ref, rg_ref, y0_ref, y1_ref, y2_ref, y3_ref, o_ref):
    rg = rg_ref[...]
    acc = ((rg[:, 0:1] * y0_ref[...] + rg[:, 1:2] * y1_ref[...])
           + (rg[:, 2:3] * y2_ref[...] + rg[:, 3:4] * y3_ref[...]))
    o_ref[...] = x_ref[...] + g_ref[0] * acc


def moe_combine(x, gate, router_gate, y4, t_all, row_off, tm):
    t, d = x.shape
    m, r, _ = gate.shape
    rows_per_mod = t // m
    y_spec = lambda k: pl.BlockSpec((tm, d), lambda i: ((k * t_all + row_off) // tm + i, 0))
    return pl.pallas_call(
        _moe_combine_body,
        out_shape=jax.ShapeDtypeStruct((t, d), F32),
        grid=(t // tm,),
        in_specs=[pl.BlockSpec((tm, d), lambda i: (i, 0)),
                  pl.BlockSpec((1, r, d), lambda i: ((i * tm) // rows_per_mod, 0, 0)),
                  pl.BlockSpec((tm, V7X_LANES), lambda i: (i, 0)),
                  y_spec(0), y_spec(1), y_spec(2), y_spec(3)],
        out_specs=pl.BlockSpec((tm, d), lambda i: (i, 0)),
        compiler_params=_cparams(("arbitrary",)),
        name="moe_combine",
    )(x, gate, router_gate, y4, y4, y4, y4)


def _moe_block_layout(counts, tk):
    bm = MOE_BLOCK_ROWS
    padded = (counts + bm - 1) // bm * bm
    pad_end = jnp.cumsum(padded)
    pad_start = pad_end - padded
    n_blocks = -(-tk // bm) + N_EXPERTS
    blk_start = jnp.arange(n_blocks, dtype=jnp.int32) * bm
    blk_e = jnp.minimum(jnp.sum((pad_end[None, :] <= blk_start[:, None]).astype(jnp.int32), axis=1),
                        N_EXPERTS - 1)
    n_used = (pad_end[-1] // bm).astype(jnp.int32).reshape(1)
    return padded, pad_start, blk_e, n_used, n_blocks


def _moe_row_tables(dest, counts, padded, pad_start, n_blocks, t):
    bm = MOE_BLOCK_ROWS
    tk = t * TOP_K
    n_rows = n_blocks * bm
    big = jnp.int32(2 ** 30)
    p = jnp.arange(bm, dtype=jnp.int32)[None, :]
    e = jnp.arange(N_EXPERTS, dtype=jnp.int32)[:, None]
    pad_keys = jnp.where(p < (padded - counts)[:, None], (pad_start + counts)[:, None] + p, big + e * bm + p)
    n_fill = n_rows - tk - N_EXPERTS * bm
    keys = jnp.concatenate([dest.reshape(tk), pad_keys.reshape(-1), big + N_EXPERTS * bm + jnp.arange(n_fill, dtype=jnp.int32)])
    vals = jnp.concatenate([jnp.arange(tk, dtype=jnp.int32), jnp.full((n_rows - tk,), -1, jnp.int32)])
    _, v = lax.sort((keys, vals), num_keys=1)
    valid = v >= 0
    row = jnp.arange(n_rows, dtype=jnp.int32)
    src_tok = jnp.where(valid, v // TOP_K, 0)
    pad_row = tk + ((row // bm) % 2) * bm + row % bm
    dst_row = jnp.where(valid, (v % TOP_K) * t + v // TOP_K, pad_row)
    return jnp.concatenate([src_tok.reshape(n_blocks, bm), dst_row.reshape(n_blocks, bm)], axis=1)


CONV_TAIL = 8
RG_TILE = 256


def _split3_bf16(x):
    h = x.astype(BF16)
    r = x - h.astype(F32)
    m = r.astype(BF16)
    return h, m, (r - m.astype(F32)).astype(BF16)


def _dot3(parts, w, dims=None):
    if dims is None:
        outs = [jnp.dot(p, w, preferred_element_type=F32) for p in parts]
    else:
        outs = [lax.dot_general(w, p, dims, preferred_element_type=F32) for p in parts]
    return (outs[0] + outs[1]) + outs[2]


def _softplus(x):
    return jnp.maximum(x, 0.0) + jnp.log(1.0 + jnp.exp(-jnp.abs(x)))


def _silu(x):
    return x * jax.nn.sigmoid(x)


def _group_rmsnorm(y, w, n_groups):
    width = y.shape[1] // n_groups
    outs = []
    for g in range(n_groups):
        yg = y[:, g * width:(g + 1) * width]
        outs.append(yg * lax.rsqrt(jnp.mean(yg * yg, axis=-1, keepdims=True) + NORM_EPS))
    return jnp.concatenate(outs, axis=1) * w


def _conv_tile(xbuf, cw_ref, cb_ref, rows):
    y = cb_ref[...]
    for k in range(SSD_CONV):
        y = y + cw_ref[k:k + 1, :] * xbuf[pl.ds(CONV_TAIL - (SSD_CONV - 1) + k, rows), :]
    return y


def _ssd_prompt_body(z_ref, xs_ref, bc_ref, dt_ref, cw_ref, cb_ref, dtb_ref, a_ref, dexp_ref, nw_ref, tri_ref,
                     y_ref, conv_ref, state_ref, xbuf, h_scr):
    q = SSD_CHUNK
    c = pl.program_id(1)
    last = pl.num_programs(1) - 1
    P, N = SSD_HEAD_DIM, SSD_STATE

    @pl.when(c == 0)
    def _():
        xbuf[0:CONV_TAIL, :] = jnp.zeros((CONV_TAIL, SSD_XBC), F32)
        h_scr[...] = jnp.zeros_like(h_scr)

    xbuf[CONV_TAIL:CONV_TAIL + q, 0:SSD_WIDTH] = xs_ref[...]
    xbuf[CONV_TAIL:CONV_TAIL + q, SSD_WIDTH:SSD_XBC] = bc_ref[...]
    xc = _silu(_conv_tile(xbuf, cw_ref, cb_ref, q))

    @pl.when(c == last)
    def _():
        conv_ref[0] = xbuf[CONV_TAIL + q - (SSD_CONV - 1):CONV_TAIL + q, :]

    xbuf[0:CONV_TAIL, :] = xbuf[q:q + CONV_TAIL, :]

    xs = xc[:, 0:SSD_WIDTH]
    bm = xc[:, SSD_WIDTH:SSD_WIDTH + SSD_GROUPS * N].astype(BF16)
    cm = xc[:, SSD_WIDTH + SSD_GROUPS * N:SSD_XBC].astype(BF16)
    dt = _softplus(dt_ref[...] + dtb_ref[...])
    a = dt * a_ref[...]
    a_cs = _dot3(_split3_bf16(a), tri_ref[...], dims=(((1,), (0,)), ((), ())))
    a_cs_t = a_cs.T
    dt_t = dt.T
    a_end_t = a_cs_t[:, q - 1:q]
    w_t = dt_t * jnp.exp(a_end_t - a_cs_t)
    ea = jnp.exp(a_cs)
    xs_t = xs.T
    row = lax.broadcasted_iota(jnp.int32, (q, q), 0)
    col = lax.broadcasted_iota(jnp.int32, (q, q), 1)
    causal = col <= row
    heads_per_group = SSD_HEADS // SSD_GROUPS
    ys = []
    for g in range(SSD_GROUPS):
        bg = bm[:, g * N:(g + 1) * N]
        cg = cm[:, g * N:(g + 1) * N]
        cb = lax.dot_general(cg, bg, _NT_DIMS_SSD, preferred_element_type=F32)
        for k in range(heads_per_group):
            h = g * heads_per_group + k
            seg = a_cs[:, h:h + 1] - a_cs_t[h:h + 1, :]
            decay = jnp.exp(jnp.where(causal, seg, MASK_NEG))
            xh = xs[:, h * P:(h + 1) * P]
            xdt = (xh * dt[:, h:h + 1]).astype(BF16)
            y_diag = jnp.dot((cb * decay).astype(BF16), xdt, preferred_element_type=F32)
            h_prev = h_scr[h]
            y_off = lax.dot_general(cg, h_prev.astype(BF16), _NT_DIMS_SSD,
                                    preferred_element_type=F32) * ea[:, h:h + 1]
            st = jnp.dot((xs_t[h * P:(h + 1) * P, :] * w_t[h:h + 1, :]).astype(BF16), bg,
                         preferred_element_type=F32)
            h_scr[h] = h_prev * jnp.exp(a_end_t[h:h + 1, :]) + st
            ys.append(y_diag + y_off)
    y = jnp.concatenate(ys, axis=1) + dexp_ref[...] * xs
    y = y * _silu(z_ref[...])
    y_ref[...] = _group_rmsnorm(y, nw_ref[...], SSD_GROUPS)

    @pl.when(c == last)
    def _():
        state_ref[0] = h_scr[...]


_NT_DIMS_SSD = (((1,), (1,)), ((), ()))
MASK_NEG = -1e30


def _pad_lanes(v, n=V7X_LANES):
    return jnp.pad(v, (0, n - v.shape[0]))[None, :]


def ssd_prompt(proj, b, seq_len, conv_w, conv_b, dt_bias, a_log, d_skip, norm_w):
    q = SSD_CHUNK
    nc = seq_len // q
    t = b * seq_len
    tri = jnp.asarray(np.tril(np.ones((q, q), np.float32)), BF16)
    a_neg = _pad_lanes(-jnp.exp(a_log))
    dtb = _pad_lanes(dt_bias)
    dexp = jnp.repeat(d_skip, SSD_HEAD_DIM)[None, :]
    nw = norm_w[None, :]
    cb = conv_b[None, :]
    colblk = lambda j, w=SSD_WIDTH: pl.BlockSpec((q, w), lambda bi, c: (bi * nc + c, j))
    full = lambda a: pl.BlockSpec(a.shape, lambda bi, c: (0,) * a.ndim)
    return pl.pallas_call(
        _ssd_prompt_body,
        out_shape=(jax.ShapeDtypeStruct((t, SSD_WIDTH), F32),
                   jax.ShapeDtypeStruct((b, SSD_CONV - 1, SSD_XBC), F32),
                   jax.ShapeDtypeStruct((b, SSD_HEADS, SSD_HEAD_DIM, SSD_STATE), F32)),
        grid=(b, nc),
        in_specs=[colblk(0), colblk(1), colblk(2),
                  pl.BlockSpec((q, V7X_LANES), lambda bi, c: (bi * nc + c, 5 * SSD_WIDTH // V7X_LANES)),
                  full(conv_w), full(cb), full(dtb), full(a_neg), full(dexp), full(nw), full(tri)],
        out_specs=(pl.BlockSpec((q, SSD_WIDTH), lambda bi, c: (bi * nc + c, 0)),
                   pl.BlockSpec((1, SSD_CONV - 1, SSD_XBC), lambda bi, c: (bi, 0, 0)),
                   pl.BlockSpec((1, SSD_HEADS, SSD_HEAD_DIM, SSD_STATE), lambda bi, c: (bi, 0, 0, 0))),
        scratch_shapes=[pltpu.VMEM((CONV_TAIL + q, SSD_XBC), F32),
                        pltpu.VMEM((SSD_HEADS, SSD_HEAD_DIM, SSD_STATE), F32)],
        compiler_params=_cparams(("arbitrary", "arbitrary"), V7X_VMEM_LIMIT_BYTES),
        name="ssd_prompt",
    )(proj, proj, proj, proj, conv_w, cb, dtb, a_neg, dexp, nw, tri)


def _ssd_sample_pre_body(xs_ref, bc_ref, dt_ref, s0_ref, s1_ref, s2_ref, cw_ref, cb_ref, dtb_ref, a_ref, dexp_ref,
                         hexp_ref, yd_ref, xdt_ref, b_ref, c_ref, ea_ref, eaexp_ref):
    N = SSD_STATE
    x_new = jnp.concatenate([xs_ref[...], bc_ref[...]], axis=1)
    y = (cb_ref[...] + cw_ref[0:1, :] * s0_ref[...] + cw_ref[1:2, :] * s1_ref[...]
         + cw_ref[2:3, :] * s2_ref[...] + cw_ref[3:4, :] * x_new)
    xc = _silu(y)
    xs = xc[:, 0:SSD_WIDTH]
    bm = xc[:, SSD_WIDTH:SSD_WIDTH + SSD_GROUPS * N]
    cm = xc[:, SSD_WIDTH + SSD_GROUPS * N:SSD_XBC]
    dt = _softplus(dt_ref[...] + dtb_ref[...])
    ea = jnp.exp(dt * a_ref[...])
    hexp = hexp_ref[...]
    dt_exp = _dot3(_split3_bf16(dt), hexp)
    xdt = (xs * dt_exp).astype(BF16)
    bb = bm.astype(BF16)
    cc = cm.astype(BF16)
    prod = bb.astype(F32) * cc.astype(F32)
    hw = SSD_WIDTH // SSD_GROUPS
    cb = jnp.concatenate(
        [jnp.broadcast_to(jnp.sum(prod[:, g * N:(g + 1) * N], axis=-1, keepdims=True), (xs.shape[0], hw))
         for g in range(SSD_GROUPS)], axis=1)
    yd_ref[...] = cb.astype(BF16).astype(F32) * xdt.astype(F32) + dexp_ref[...] * xs
    xdt_ref[...] = xdt.astype(F32)
    b_ref[...] = bb.astype(F32)
    c_ref[...] = cc.astype(F32)
    ea_ref[...] = ea
    eaexp_ref[...] = _dot3(_split3_bf16(ea), hexp)


def _ssd_sample_state_body(ea_smem, xdt_ref, b_ref, c_ref, yd_ref, eaexp_ref, z_ref, nw_ref, h0_ref,
                           y_ref, h1_ref):
    i = pl.program_id(0)
    N = SSD_STATE
    gw = SSD_WIDTH // SSD_GROUPS
    heads_per_group = SSD_HEADS // SSD_GROUPS
    row0 = lax.broadcasted_iota(jnp.int32, (8, 1), 0) == 0
    y_off = []
    for g in range(SSD_GROUPS):
        x8 = jnp.broadcast_to(xdt_ref[0, :, g * gw:(g + 1) * gw], (8, gw)).astype(BF16)
        b8 = jnp.where(row0, jnp.broadcast_to(b_ref[0, :, g * N:(g + 1) * N], (8, N)), 0.0).astype(BF16)
        c8 = jnp.broadcast_to(c_ref[0, :, g * N:(g + 1) * N], (8, N)).astype(BF16)
        h0g = h0_ref[0, g * gw:(g + 1) * gw, :]
        st = lax.dot_general(x8, b8, (((0,), (0,)), ((), ())), preferred_element_type=F32)
        yo = lax.dot_general(c8, h0g.astype(BF16), _NT_DIMS_SSD, preferred_element_type=F32)
        y_off.append(yo[0:1, :])
        for k in range(heads_per_group):
            h = g * heads_per_group + k
            r = slice(k * SSD_HEAD_DIM, (k + 1) * SSD_HEAD_DIM)
            h1_ref[0, g * gw + k * SSD_HEAD_DIM:g * gw + (k + 1) * SSD_HEAD_DIM, :] = (
                h0g[r, :] * ea_smem[i, h] + st[r, :])
    y = jnp.concatenate(y_off, axis=1) * eaexp_ref[0] + yd_ref[0]
    y = y * _silu(z_ref[0])
    y_ref[0] = _group_rmsnorm(y, nw_ref[...], SSD_GROUPS)


def ssd_sample(proj, conv_state, ssm_state, conv_w, conv_b, dt_bias, a_log, d_skip, norm_w):
    bsz = proj.shape[0]
    H, P, N = SSD_HEADS, SSD_HEAD_DIM, SSD_STATE
    a_neg = _pad_lanes(-jnp.exp(a_log))
    dtb = _pad_lanes(dt_bias)
    dexp = jnp.repeat(d_skip, P)[None, :]
    hexp = jnp.asarray((np.arange(V7X_LANES)[:, None] == (np.arange(H * P)[None, :] // P)).astype(np.float32), BF16)
    cb = conv_b[None, :]
    s0, s1, s2 = conv_state[:, 0], conv_state[:, 1], conv_state[:, 2]
    blk = lambda j, w: pl.BlockSpec((bsz, w), lambda i: (0, j))
    full = lambda a: pl.BlockSpec(a.shape, lambda i: (0,) * a.ndim)
    o = lambda w, dt_: jax.ShapeDtypeStruct((bsz, w), dt_)
    yd, xdt, bb, cc, ea, eaexp = pl.pallas_call(
        _ssd_sample_pre_body,
        out_shape=(o(SSD_WIDTH, F32), o(SSD_WIDTH, F32), o(SSD_GROUPS * N, F32), o(SSD_GROUPS * N, F32),
                   o(V7X_LANES, F32), o(SSD_WIDTH, F32)),
        grid=(1,),
        in_specs=[blk(1, SSD_WIDTH), blk(2, SSD_WIDTH), blk(5 * SSD_WIDTH // V7X_LANES, V7X_LANES),
                  full(s0), full(s1), full(s2), full(conv_w), full(cb), full(dtb), full(a_neg), full(dexp), full(hexp)],
        out_specs=(full(o(SSD_WIDTH, F32)), full(o(SSD_WIDTH, F32)), full(o(SSD_GROUPS * N, F32)),
                   full(o(SSD_GROUPS * N, F32)), full(o(V7X_LANES, F32)), full(o(SSD_WIDTH, F32))),
        compiler_params=_cparams(("arbitrary",)),
        name="ssd_sample_pre",
    )(proj, proj, proj, s0, s1, s2, conv_w, cb, dtb, a_neg, dexp, hexp)
    x_new = jnp.concatenate([proj[:, SSD_WIDTH:2 * SSD_WIDTH], proj[:, 2 * SSD_WIDTH:3 * SSD_WIDTH]], axis=1)
    conv_new = jnp.stack([s1, s2, x_new], axis=1)
    z3 = proj[:, 0:SSD_WIDTH].reshape(bsz, 1, SSD_WIDTH)
    row = lambda w: pl.BlockSpec((1, 1, w), lambda i, ea_: (i, 0, 0))
    nw = norm_w[None, :]
    grid_spec = pltpu.PrefetchScalarGridSpec(
        num_scalar_prefetch=1,
        grid=(bsz,),
        in_specs=[row(SSD_WIDTH), row(SSD_GROUPS * N), row(SSD_GROUPS * N), row(SSD_WIDTH), row(SSD_WIDTH),
                  row(SSD_WIDTH), pl.BlockSpec(nw.shape, lambda i, ea_: (0, 0)),
                  pl.BlockSpec((1, H * P, N), lambda i, ea_: (i, 0, 0))],
        out_specs=(row(SSD_WIDTH), pl.BlockSpec((1, H * P, N), lambda i, ea_: (i, 0, 0))),
    )
    r3 = lambda a: a.reshape(bsz, 1, a.shape[1])
    y, h1 = pl.pallas_call(
        _ssd_sample_state_body,
        out_shape=(jax.ShapeDtypeStruct((bsz, 1, SSD_WIDTH), F32), jax.ShapeDtypeStruct((bsz, H * P, N), F32)),
        grid_spec=grid_spec,
        compiler_params=_cparams(("arbitrary",)),
        name="ssd_sample_state",
    )(ea[:, :H], r3(xdt), r3(bb), r3(cc), r3(yd), r3(eaexp), z3, nw, ssm_state.reshape(bsz, H * P, N))
    return y.reshape(bsz, SSD_WIDTH), conv_new, h1.reshape(bsz, H, P, N)


def _rg_gates(xc, wa_ref, ba_ref, wi_ref, bi_ref, sp_ref):
    xb = xc.astype(BF16)
    r = jax.nn.sigmoid(jnp.dot(xb, wa_ref[...], preferred_element_type=F32) + ba_ref[...])
    ig = jax.nn.sigmoid(jnp.dot(xb, wi_ref[...], preferred_element_type=F32) + bi_ref[...])
    log_a = -RG_C * r * sp_ref[...]
    a = jnp.exp(log_a)
    u = jnp.sqrt(1.0 - jnp.exp(2.0 * log_a)) * (ig * xc)
    return a, u


def _rg_prompt_body(gate_ref, xr_ref, cw_ref, cb_ref, wa_ref, ba_ref, wi_ref, bi_ref, sp_ref,
                    y_ref, conv_ref, state_ref, xbuf, h_scr):
    rows = RG_TILE
    c = pl.program_id(1)
    last = pl.num_programs(1) - 1

    @pl.when(c == 0)
    def _():
        xbuf[0:CONV_TAIL, :] = jnp.zeros((CONV_TAIL, RG_WIDTH), F32)
        h_scr[...] = jnp.zeros_like(h_scr)

    xbuf[CONV_TAIL:CONV_TAIL + rows, :] = xr_ref[...]
    xc = _conv_tile(xbuf, cw_ref, cb_ref, rows)

    @pl.when(c == last)
    def _():
        conv_ref[0] = xbuf[CONV_TAIL + rows - (SSD_CONV - 1):CONV_TAIL + rows, :]

    xbuf[0:CONV_TAIL, :] = xbuf[rows:rows + CONV_TAIL, :]
    a, u = _rg_gates(xc, wa_ref, ba_ref, wi_ref, bi_ref, sp_ref)
    t_idx = lax.broadcasted_iota(jnp.int32, (rows, 1), 0)
    d = 1
    while d < rows:
        keep = t_idx >= d
        a_sh = jnp.where(keep, pltpu.roll(a, d, 0), 1.0)
        u_sh = jnp.where(keep, pltpu.roll(u, d, 0), 0.0)
        u = u + a * u_sh
        a = a * a_sh
        d *= 2
    h = u + a * h_scr[0:1, :]
    h_scr[0:1, :] = h[rows - 1:rows, :]
    y_ref[...] = h * _gelu_tanh(gate_ref[...])

    @pl.when(c == last)
    def _():
        state_ref[0] = h[rows - 1:rows, :]


def _rg_weights(wa, ba, wi, bi, lam):
    eye = jnp.eye(RG_BLOCKS, dtype=F32)
    bd = lambda w: jnp.einsum('nde,nm->ndme', w, eye).reshape(RG_WIDTH, RG_WIDTH).astype(BF16)
    return bd(wa), ba[None, :], bd(wi), bi[None, :], jax.nn.softplus(-lam)[None, :]


def rg_prompt(proj, b, seq_len, conv_w, conv_b, wa, ba, wi, bi, lam):
    rows = RG_TILE
    nt = seq_len // rows
    t = b * seq_len
    wts = _rg_weights(wa, ba, wi, bi, lam)
    cb = conv_b[None, :]
    full = lambda a: pl.BlockSpec(a.shape, lambda bi_, c: (0,) * a.ndim)
    return pl.pallas_call(
        _rg_prompt_body,
        out_shape=(jax.ShapeDtypeStruct((t, RG_WIDTH), F32),
                   jax.ShapeDtypeStruct((b, SSD_CONV - 1, RG_WIDTH), F32),
                   jax.ShapeDtypeStruct((b, 1, RG_WIDTH), F32)),
        grid=(b, nt),
        in_specs=[pl.BlockSpec((rows, RG_WIDTH), lambda bi_, c: (bi_ * nt + c, 3)),
                  pl.BlockSpec((rows, RG_WIDTH), lambda bi_, c: (bi_ * nt + c, 4)),
                  full(conv_w), full(cb)] + [full(w) for w in wts],
        out_specs=(pl.BlockSpec((rows, RG_WIDTH), lambda bi_, c: (bi_ * nt + c, 0)),
                   pl.BlockSpec((1, SSD_CONV - 1, RG_WIDTH), lambda bi_, c: (bi_, 0, 0)),
                   pl.BlockSpec((1, 1, RG_WIDTH), lambda bi_, c: (bi_, 0, 0))),
        scratch_shapes=[pltpu.VMEM((CONV_TAIL + rows, RG_WIDTH), F32), pltpu.VMEM((8, RG_WIDTH), F32)],
        compiler_params=_cparams(("arbitrary", "arbitrary"), V7X_VMEM_LIMIT_BYTES),
        name="rg_prompt",
    )(proj, proj, conv_w, cb, *wts)


def _rg_sample_body(gate_ref, xr_ref, s0_ref, s1_ref, s2_ref, h0_ref, cw_ref, cb_ref, wa_ref, ba_ref, wi_ref, bi_ref,
                    sp_ref, y_ref, h1_ref):
    xc = (cb_ref[...] + cw_ref[0:1, :] * s0_ref[...] + cw_ref[1:2, :] * s1_ref[...]
          + cw_ref[2:3, :] * s2_ref[...] + cw_ref[3:4, :] * xr_ref[...])
    a, u = _rg_gates(xc, wa_ref, ba_ref, wi_ref, bi_ref, sp_ref)
    h = a * h0_ref[...] + u
    h1_ref[...] = h
    y_ref[...] = h * _gelu_tanh(gate_ref[...])


def rg_sample(proj, conv_state, h0, conv_w, conv_b, wa, ba, wi, bi, lam):
    bsz = proj.shape[0]
    wts = _rg_weights(wa, ba, wi, bi, lam)
    cb = conv_b[None, :]
    s0, s1, s2 = conv_state[:, 0], conv_state[:, 1], conv_state[:, 2]
    full = lambda a: pl.BlockSpec(a.shape, lambda i: (0,) * a.ndim)
    out = jax.ShapeDtypeStruct((bsz, RG_WIDTH), F32)
    y, h1 = pl.pallas_call(
        _rg_sample_body,
        out_shape=(out, out),
        grid=(1,),
        in_specs=[pl.BlockSpec((bsz, RG_WIDTH), lambda i: (0, 3)), pl.BlockSpec((bsz, RG_WIDTH), lambda i: (0, 4)),
                  full(s0), full(s1), full(s2), full(h0), full(conv_w), full(cb)] + [full(w) for w in wts],
        out_specs=(full(out), full(out)),
        compiler_params=_cparams(("arbitrary",)),
        name="rg_sample",
    )(proj, proj, s0, s1, s2, h0, conv_w, cb, *wts)
    conv_new = jnp.stack([s1, s2, proj[:, 4 * RG_WIDTH:5 * RG_WIDTH]], axis=1)
    return y, conv_new, h1


def _overlap_matrix(nc, ns):
    i = np.arange(nc)[:, None]
    j = np.arange(ns)[None, :]
    ov = (i * CMP_STRIDE < (j + 1) * SEL_BLOCK) & (i * CMP_STRIDE + CMP_LEN > j * SEL_BLOCK)
    return ov.astype(np.float32)


NSA_TQ = 128
NSA_TK_SLC = 1024
NSA_NS_PAD = 64
SEL_BIAS = -16384.0
MASK_VALUE = -1e30


def _split_bf16(x):
    hi = x.astype(BF16)
    lo = (x - hi.astype(F32)).astype(BF16)
    return hi, lo


def _seg_rms_scale(x, seg, seg_t):
    hi, lo = _split_bf16(x * x)
    ss = jnp.dot(hi, seg, preferred_element_type=F32) + jnp.dot(lo, seg, preferred_element_type=F32)
    r = lax.rsqrt(ss * (1.0 / NSA_HEAD_DIM) + NORM_EPS)
    rh, rl = _split_bf16(r)
    return jnp.dot(rh, seg_t, preferred_element_type=F32) + jnp.dot(rl, seg_t, preferred_element_type=F32)


def _nsa_prep_body(seq_len, p_ref, wq_ref, wks_ref, wkw_ref, segq_ref, segqt_ref, segk_ref, segkt_ref,
                   q_ref, rows_ref, win_ref, kaug_ref, vslc_ref, kwin_ref, vwin_ref, gate_ref, rows_t_ref, win_t_ref):
    tm = p_ref.shape[0]
    dh = NSA_HEAD_DIM
    q = p_ref[:, 0:NSA_Q_WIDTH]
    qn = q * _seg_rms_scale(q, segq_ref[...], segqt_ref[...]) * wq_ref[...]
    kv = [p_ref[:, NSA_Q_WIDTH + NSA_KV_WIDTH * j:NSA_Q_WIDTH + NSA_KV_WIDTH * (j + 1)] for j in range(6)]
    ksl = kv[2] * _seg_rms_scale(kv[2], segk_ref[...], segkt_ref[...]) * wks_ref[...]
    kwn = kv[4] * _seg_rms_scale(kv[4], segk_ref[...], segkt_ref[...]) * wkw_ref[...]
    rows = jnp.concatenate([kv[0], kv[1], ksl, kv[3]], axis=1)
    win = jnp.concatenate([kwn, kv[5]], axis=1)
    rows_ref[...] = rows
    win_ref[...] = win
    rows_t_ref[0] = rows.T
    win_t_ref[0] = win.T
    gates = jax.nn.sigmoid(p_ref[:, NSA_Q_WIDTH + 6 * NSA_KV_WIDTH:NSA_Q_WIDTH + 6 * NSA_KV_WIDTH + V7X_LANES])
    t0 = (pl.program_id(0) * tm) % seq_len
    tpos = t0 + lax.broadcasted_iota(jnp.int32, (tm, NSA_NS_PAD), 0)
    blk = lax.broadcasted_iota(jnp.int32, (tm, NSA_NS_PAD), 1)
    onehot = jnp.where(blk == lax.shift_right_logical(tpos, 6), 1.0, 0.0).astype(BF16)
    for g in range(NSA_KV_HEADS):
        sl = slice(g * dh, (g + 1) * dh)
        kaug_ref[0, g] = jnp.concatenate([ksl[:, sl].astype(BF16), onehot], axis=1)
        vslc_ref[0, g] = kv[3][:, sl].astype(BF16)
        kwin_ref[0, g] = kwn[:, sl].astype(BF16)
        vwin_ref[0, g] = kv[5][:, sl].astype(BF16)
        gate_ref[0, g] = gates if g == 0 else pltpu.roll(gates, V7X_LANES - 3 * NSA_GROUP * g, 1)
        for k in range(NSA_GROUP):
            c0 = (g * NSA_GROUP + k) * dh
            q_ref[0, g, k] = qn[:, c0:c0 + dh].astype(BF16)


def _head_segments(width):
    lane = np.arange(width)[:, None] // NSA_HEAD_DIM
    seg = (lane == np.arange(V7X_LANES)[None, :]).astype(np.float32)
    return jnp.asarray(seg, BF16), jnp.asarray(seg.T, BF16)


def nsa_prep(proj, b, seq_len, q_norm, k_norm, tm):
    t = proj.shape[0]
    G, K, dh = NSA_KV_HEADS, NSA_GROUP, NSA_HEAD_DIM
    wq = (jnp.tile(q_norm, NSA_HEADS) * (dh ** -0.5))[None, :]
    wks = jnp.tile(k_norm[1], G)[None, :]
    wkw = jnp.tile(k_norm[2], G)[None, :]
    segq, segqt = _head_segments(NSA_Q_WIDTH)
    segk, segkt = _head_segments(NSA_KV_WIDTH)
    tiles_per_seq = seq_len // tm
    bi = lambda i: i // tiles_per_seq
    ti = lambda i: i % tiles_per_seq
    full = lambda a: pl.BlockSpec(a.shape, lambda i: (0,) * a.ndim)
    out_shape = (jax.ShapeDtypeStruct((b, G, K, seq_len, dh), BF16),
                 jax.ShapeDtypeStruct((t, 4 * NSA_KV_WIDTH), F32),
                 jax.ShapeDtypeStruct((t, 2 * NSA_KV_WIDTH), F32),
                 jax.ShapeDtypeStruct((b, G, seq_len, 2 * dh), BF16),
                 jax.ShapeDtypeStruct((b, G, seq_len, dh), BF16),
                 jax.ShapeDtypeStruct((b, G, seq_len, dh), BF16),
                 jax.ShapeDtypeStruct((b, G, seq_len, dh), BF16),
                 jax.ShapeDtypeStruct((b, G, seq_len, V7X_LANES), F32),
                 jax.ShapeDtypeStruct((b, 4 * NSA_KV_WIDTH, seq_len), F32),
                 jax.ShapeDtypeStruct((b, 2 * NSA_KV_WIDTH, seq_len), F32))
    per_g = lambda w: pl.BlockSpec((1, G, tm, w), lambda i: (bi(i), 0, ti(i), 0))
    feat_major = lambda w: pl.BlockSpec((1, w, tm), lambda i: (bi(i), 0, ti(i)))
    out_specs = (pl.BlockSpec((1, G, K, tm, dh), lambda i: (bi(i), 0, 0, ti(i), 0)),
                 pl.BlockSpec((tm, 4 * NSA_KV_WIDTH), lambda i: (i, 0)),
                 pl.BlockSpec((tm, 2 * NSA_KV_WIDTH), lambda i: (i, 0)),
                 per_g(2 * dh), per_g(dh), per_g(dh), per_g(dh), per_g(V7X_LANES),
                 feat_major(4 * NSA_KV_WIDTH), feat_major(2 * NSA_KV_WIDTH))
    return pl.pallas_call(
        functools.partial(_nsa_prep_body, seq_len),
        out_shape=out_shape,
        grid=(t // tm,),
        in_specs=[pl.BlockSpec((tm, proj.shape[1]), lambda i: (i, 0)),
                  full(wq), full(wks), full(wkw), full(segq), full(segqt), full(segk), full(segkt)],
        out_specs=out_specs,
        compiler_params=_cparams(("arbitrary",), V7X_VMEM_LIMIT_BYTES),
        name="nsa_prep",
    )(proj, wq, wks, wkw, segq, segqt, segk, segkt)


def _gelu_tanh(x):
    return 0.5 * x * (1.0 + jnp.tanh(math.sqrt(2.0 / math.pi) * (x + 0.044715 * (x * x * x))))


def _nsa_compress_body(n_chunk, x0_ref, x1_ref, x2_ref, x3_ref, wk_ref, wv_ref, pe_ref, w1f_ref, w2_ref, kn_ref,
                       kc_ref, vc_ref, pk_scr, pv_scr):
    s = pl.program_id(1)

    @pl.when(s == 0)
    def _():
        pk_scr[...] = jnp.zeros_like(pk_scr)
        pv_scr[...] = jnp.zeros_like(pv_scr)

    xs = [r[pl.ds(s, n_chunk, stride=CMP_STRIDE), :].astype(BF16) for r in (x0_ref, x1_ref, x2_ref, x3_ref)]
    pk_scr[...] += jnp.dot(jnp.concatenate(xs[0:2], axis=1), wk_ref[0], preferred_element_type=F32)
    pv_scr[...] += jnp.dot(jnp.concatenate(xs[2:4], axis=1), wv_ref[0], preferred_element_type=F32)

    @pl.when(s == CMP_STRIDE - 1)
    def _():
        for kv, p_scr, o_ref in ((0, pk_scr, kc_ref), (1, pv_scr, vc_ref)):
            p = p_scr[...]
            p_next = pltpu.roll(p, n_chunk - 1, 0)
            pe_h = jnp.dot(pe_ref[kv], w1f_ref[kv], preferred_element_type=F32)[0:1, :]
            for g in range(NSA_KV_HEADS):
                c0 = g * 2 * CMP_HID
                hid = pe_h + p[:, c0:c0 + CMP_HID] + p_next[:, c0 + CMP_HID:c0 + 2 * CMP_HID]
                y = jnp.dot(_gelu_tanh(hid).astype(BF16), w2_ref[kv], preferred_element_type=F32)
                if kv == 0:
                    y = y * lax.rsqrt(jnp.mean(y * y, axis=-1, keepdims=True) + NORM_EPS) * kn_ref[...]
                o_ref[0, g] = y.astype(BF16)


def nsa_compress(rows, b, seq_len, cmp_w1, cmp_pe, cmp_w2, k_norm_cmp):
    G, dh = NSA_KV_HEADS, NSA_HEAD_DIM
    n_chunk = seq_len // CMP_STRIDE
    w1 = cmp_w1.reshape(2, CMP_R, CMP_STRIDE, dh, CMP_HID)
    eye = jnp.eye(G, dtype=F32)
    wbd = jnp.einsum('vrsdh,gq->vsgdqrh', w1, eye).reshape(2, CMP_STRIDE, G * dh, G * CMP_R * CMP_HID).astype(BF16)
    pe = jnp.broadcast_to(cmp_pe.reshape(2, 1, CMP_LEN * dh), (2, 8, CMP_LEN * dh)).astype(BF16)
    w1f = cmp_w1.reshape(2, CMP_LEN * dh, CMP_HID).astype(BF16)
    w2 = cmp_w2.astype(BF16)
    kn = k_norm_cmp[None, :]
    full = lambda a: pl.BlockSpec(a.shape, lambda bi, s: (0,) * a.ndim)
    return pl.pallas_call(
        functools.partial(_nsa_compress_body, n_chunk),
        out_shape=(jax.ShapeDtypeStruct((b, G, n_chunk, dh), BF16),
                   jax.ShapeDtypeStruct((b, G, n_chunk, dh), BF16)),
        grid=(b, CMP_STRIDE),
        in_specs=[pl.BlockSpec((seq_len, V7X_LANES), lambda bi, s: (bi, 0)),
                  pl.BlockSpec((seq_len, V7X_LANES), lambda bi, s: (bi, 1)),
                  pl.BlockSpec((seq_len, V7X_LANES), lambda bi, s: (bi, 2)),
                  pl.BlockSpec((seq_len, V7X_LANES), lambda bi, s: (bi, 3)),
                  pl.BlockSpec((1, G * dh, G * CMP_R * CMP_HID), lambda bi, s: (s, 0, 0)),
                  pl.BlockSpec((1, G * dh, G * CMP_R * CMP_HID), lambda bi, s: (s, 0, 0)),
                  full(pe), full(w1f), full(w2), full(kn)],
        out_specs=(pl.BlockSpec((1, G, n_chunk, dh), lambda bi, s: (bi, 0, 0, 0)),
                   pl.BlockSpec((1, G, n_chunk, dh), lambda bi, s: (bi, 0, 0, 0))),
        scratch_shapes=[pltpu.VMEM((n_chunk, G * CMP_R * CMP_HID), F32),
                        pltpu.VMEM((n_chunk, G * CMP_R * CMP_HID), F32)],
        compiler_params=_cparams(("arbitrary", "arbitrary"), V7X_VMEM_LIMIT_BYTES),
        name="nsa_compress",
    )(rows, rows, rows, rows, wbd[0], wbd[1], pe, w1f, w2, kn)


_NT_DIMS = (((1,), (1,)), ((), ()))


def _flash_branch(q2, k_ref, v_ref, n_tiles, tk, last_mask_fn):
    rows = q2.shape[0]

    def step(j, carry, mask_fn):
        m, l, acc = carry
        k0 = pl.multiple_of(j * tk, tk)
        k = k_ref[0, 0, pl.ds(k0, tk), :]
        v = v_ref[0, 0, pl.ds(k0, tk), :]
        s = lax.dot_general(q2, k, _NT_DIMS, preferred_element_type=F32)
        if mask_fn is not None:
            s = jnp.where(mask_fn(k0), s, MASK_VALUE)
        m_new = jnp.maximum(m, jnp.max(s, axis=-1, keepdims=True))
        alpha = jnp.exp(m - m_new)
        p = jnp.exp(s - m_new)
        l = alpha * l + jnp.sum(p, axis=-1, keepdims=True)
        acc = alpha * acc + jnp.dot(p.astype(BF16), v, preferred_element_type=F32)
        return m_new, l, acc

    init = (jnp.full((rows, 1), MASK_VALUE, F32), jnp.zeros((rows, 1), F32),
            jnp.zeros((rows, NSA_HEAD_DIM), F32))
    carry = lax.fori_loop(0, n_tiles - 1, lambda j, c: step(j, c, None), init)
    _, l, acc = step(n_tiles - 1, carry, last_mask_fn)
    return acc / l


def _nsa_attn_body(n_cmp, q_ref, kc_ref, vc_ref, kaug_ref, vslc_ref, kwin_ref, vwin_ref, gate_ref, ovt_ref, o_ref):
    tq = NSA_TQ
    rows = NSA_GROUP * tq
    q0 = pl.program_id(2) * tq
    q2 = q_ref[0, 0].reshape(rows, NSA_HEAD_DIM)
    row_t = q0 + jnp.bitwise_and(lax.broadcasted_iota(jnp.int32, (rows, 1), 0), tq - 1)

    n_pad = kc_ref.shape[2]
    s = lax.dot_general(q2, kc_ref[0, 0], _NT_DIMS, preferred_element_type=F32)
    n_idx = lax.broadcasted_iota(jnp.int32, (1, n_pad), 1)
    cmask = jnp.logical_and(n_idx * CMP_STRIDE + (CMP_LEN - 1) <= row_t, n_idx < n_cmp)
    s = jnp.where(cmask, s, MASK_VALUE)
    m = jnp.max(s, axis=-1, keepdims=True)
    e = jnp.where(cmask, jnp.exp(s - m), 0.0)
    den = jnp.sum(e, axis=-1, keepdims=True)
    p_c = e / jnp.where(den > 0.0, den, 1.0)
    o_c = jnp.dot(p_c.astype(BF16), vc_ref[0, 0], preferred_element_type=F32)

    p_sum = (p_c[0:tq] + p_c[tq:2 * tq]) + (p_c[2 * tq:3 * tq] + p_c[3 * tq:4 * tq])
    ph, plo = _split_bf16(p_sum)
    ovt = ovt_ref[...]
    imp = (lax.dot_general(ovt, ph, _NT_DIMS, preferred_element_type=F32)
           + lax.dot_general(ovt, plo, _NT_DIMS, preferred_element_type=F32))
    blk = lax.broadcasted_iota(jnp.int32, (NSA_NS_PAD, tq), 0)
    jt = lax.shift_right_logical(q0 + lax.broadcasted_iota(jnp.int32, (NSA_NS_PAD, tq), 1), 6)
    valid = blk <= jt
    forced = jnp.logical_and(valid, jnp.logical_or(blk == 0, jnp.logical_or(blk == jt, blk == jt - 1)))
    eff = jnp.where(forced, jnp.inf, jnp.where(valid, imp, -jnp.inf))
    rank = jnp.zeros((NSA_NS_PAD, tq), jnp.int32)
    for j in range(NSA_NS_PAD):
        other = eff[j:j + 1, :]
        ahead = jnp.logical_or(other > eff, jnp.logical_and(other == eff, blk > j))
        rank = rank + ahead.astype(jnp.int32)
    sel = jnp.logical_and(valid, rank < SEL_TOPN)
    sel_bias = jnp.where(sel, 0.0, SEL_BIAS).T.astype(BF16)

    q_aug = jnp.concatenate([q2, jnp.concatenate([sel_bias] * NSA_GROUP, axis=0)], axis=1)
    hi = (q0 + tq - 1) // NSA_TK_SLC + 1

    def slc_mask(k0):
        kpos = k0 + lax.broadcasted_iota(jnp.int32, (1, NSA_TK_SLC), 1)
        return kpos <= row_t

    o_s = _flash_branch(q_aug, kaug_ref, vslc_ref, hi, NSA_TK_SLC, slc_mask)

    span = WINDOW + tq
    w0 = pl.multiple_of(jnp.maximum(q0 - WINDOW, 0), tq)
    kw = kwin_ref[0, 0, pl.ds(w0, span), :]
    vw = vwin_ref[0, 0, pl.ds(w0, span), :]
    s_w = lax.dot_general(q2, kw, _NT_DIMS, preferred_element_type=F32)
    kpos = w0 + lax.broadcasted_iota(jnp.int32, (1, span), 1)
    wmask = jnp.logical_and(kpos <= row_t, kpos > row_t - WINDOW)
    s_w = jnp.where(wmask, s_w, MASK_VALUE)
    p_w = jnp.exp(s_w - jnp.max(s_w, axis=-1, keepdims=True))
    o_w = (jnp.dot(p_w.astype(BF16), vw, preferred_element_type=F32)
           / jnp.sum(p_w, axis=-1, keepdims=True))

    gt = gate_ref[0, 0]
    outs = []
    for k in range(NSA_GROUP):
        r = slice(k * tq, (k + 1) * tq)
        outs.append(gt[:, 3 * k:3 * k + 1] * o_c[r] + gt[:, 3 * k + 1:3 * k + 2] * o_s[r]
                    + gt[:, 3 * k + 2:3 * k + 3] * o_w[r])
    o_ref[...] = jnp.concatenate(outs, axis=1)


def nsa_attention(q, kc, vc, kaug, vslc, kwin, vwin, gates, b, seq_len):
    G, K, dh = NSA_KV_HEADS, NSA_GROUP, NSA_HEAD_DIM
    tq = NSA_TQ
    nq = seq_len // tq
    n_chunk = kc.shape[2]
    n_cmp = n_chunk - CMP_R + 1
    ns = seq_len // SEL_BLOCK
    ovt = np.zeros((NSA_NS_PAD, n_chunk), np.float32)
    ovt[:ns, :n_cmp] = _overlap_matrix(n_cmp, ns).T
    ovt = jnp.asarray(ovt, BF16)
    seq_spec = lambda w: pl.BlockSpec((1, 1, seq_len, w), lambda bi, g, qi: (bi, g, 0, 0))
    return pl.pallas_call(
        functools.partial(_nsa_attn_body, n_cmp),
        out_shape=jax.ShapeDtypeStruct((b * seq_len, NSA_Q_WIDTH), F32),
        grid=(b, G, nq),
        in_specs=[pl.BlockSpec((1, 1, K, tq, dh), lambda bi, g, qi: (bi, g, 0, qi, 0)),
                  pl.BlockSpec((1, 1, n_chunk, dh), lambda bi, g, qi: (bi, g, 0, 0)),
                  pl.BlockSpec((1, 1, n_chunk, dh), lambda bi, g, qi: (bi, g, 0, 0)),
                  seq_spec(2 * dh), seq_spec(dh), seq_spec(dh), seq_spec(dh),
                  pl.BlockSpec((1, 1, tq, V7X_LANES), lambda bi, g, qi: (bi, g, qi, 0)),
                  pl.BlockSpec(ovt.shape, lambda bi, g, qi: (0, 0))],
        out_specs=pl.BlockSpec((tq, K * dh), lambda bi, g, qi: (bi * nq + qi, g)),
        compiler_params=_cparams(("arbitrary", "arbitrary", "arbitrary"), V7X_VMEM_LIMIT_BYTES),
        name="nsa_attention",
    )(q, kc, vc, kaug, vslc, kwin, vwin, gates, ovt)


def nsa_prompt_pallas(proj, b, seq_len, q_norm, k_norm, cmp_w1, cmp_pe, cmp_w2):
    q, rows, _, kaug, vslc, kwin, vwin, gates, rows_t, win_t = nsa_prep(proj, b, seq_len, q_norm, k_norm, ROW_TILE)
    kc, vc = nsa_compress(rows, b, seq_len, cmp_w1, cmp_pe, cmp_w2, k_norm[0])
    o = nsa_attention(q, kc, vc, kaug, vslc, kwin, vwin, gates, b, seq_len)
    G, dh = NSA_KV_HEADS, NSA_HEAD_DIM
    rows_out = jnp.transpose(rows_t.reshape(b, 4, G, dh, seq_len), (0, 4, 1, 2, 3))
    wlen = min(WINDOW, seq_len)
    win_out = jnp.transpose(win_t[:, :, seq_len - wlen:].reshape(b, 2, G, dh, wlen), (0, 4, 1, 2, 3))
    return o, rows_out, win_out


def _diag_heads(o_full):
    g_row = lax.shift_right_logical(lax.broadcasted_iota(jnp.int32, (NSA_HEADS, 1), 0), 2)
    out = jnp.zeros((NSA_HEADS, NSA_HEAD_DIM), F32)
    for g in range(NSA_KV_HEADS):
        out = out + jnp.where(g_row == g, o_full[:, g * NSA_HEAD_DIM:(g + 1) * NSA_HEAD_DIM], 0.0)
    return out


def _nsa_sample_body(n_pages, page_rows, pt_ref,
                     cache_hbm, qbd_ref, rown_ref, winn_ref, winbuf_ref, wc_ref, pe_ref, w1f_ref, w2_ref, kn_ref,
                     gsum_ref, ovs_ref, rep_ref, eblk_ref,
                     oc_ref, os_ref, ow_ref, wout_ref,
                     cmpt_buf, slct_buf, cmp_buf, sem):
    i = pl.program_id(0)
    nb = pl.num_programs(0)
    slot = i % 2
    past = n_pages * page_rows
    n_chunk = past // CMP_STRIDE
    n_cmp = n_chunk - CMP_R + 1
    t_pos = past
    kvw = NSA_KV_WIDTH

    def page_copies(bi, sl):
        copies = []
        for p in range(n_pages):
            pg = pt_ref[bi, p]
            copies.append(pltpu.make_async_copy(cache_hbm.at[pg, pl.ds(0, 2 * kvw), :],
                                                cmpt_buf.at[sl, p], sem.at[sl]))
            copies.append(pltpu.make_async_copy(cache_hbm.at[pg, pl.ds(2 * kvw, 2 * kvw), :],
                                                slct_buf.at[sl, :, pl.ds(p * page_rows, page_rows)], sem.at[sl]))
        return copies

    @pl.when(i == 0)
    def _():
        for c in page_copies(0, 0):
            c.start()

    @pl.when(i + 1 < nb)
    def _():
        for c in page_copies(i + 1, 1 - slot):
            c.start()

    for c in page_copies(i, slot):
        c.wait()

    for p in range(n_pages):
        for j in range(4):
            cmp_buf[j, p * page_rows:(p + 1) * page_rows, :] = (
                cmpt_buf[slot, p, j * V7X_LANES:(j + 1) * V7X_LANES, :].T)

    parts = []
    for j in range(4):
        acc = None
        for s in range(CMP_STRIDE):
            xs = cmp_buf[j, pl.ds(s, n_chunk, stride=CMP_STRIDE), :].astype(BF16)
            d = jnp.dot(xs, wc_ref[j // 2, s], preferred_element_type=F32)
            acc = d if acc is None else acc + d
        parts.append(acc)
    slabs = []
    for kv in range(2):
        p = jnp.concatenate(parts[2 * kv:2 * kv + 2], axis=1)
        p_next = pltpu.roll(p, n_chunk - 1, 0)
        pe_h = jnp.dot(pe_ref[kv], w1f_ref[kv], preferred_element_type=F32)[0:1, :]
        ys = []
        for g in range(NSA_KV_HEADS):
            c0 = g * 2 * CMP_HID
            hid = pe_h + p[:, c0:c0 + CMP_HID] + p_next[:, c0 + CMP_HID:c0 + 2 * CMP_HID]
            y = jnp.dot(_gelu_tanh(hid).astype(BF16), w2_ref[kv], preferred_element_type=F32)
            if kv == 0:
                y = y * lax.rsqrt(jnp.mean(y * y, axis=-1, keepdims=True) + NORM_EPS) * kn_ref[...]
            ys.append(y)
        slabs.append(jnp.concatenate(ys, axis=1).astype(BF16))
    kc, vc = slabs

    qbd = qbd_ref[0]
    qf = qbd.astype(F32)

    s_c = lax.dot_general(qbd, kc, _NT_DIMS, preferred_element_type=F32)
    n_idx = lax.broadcasted_iota(jnp.int32, (1, n_chunk), 1)
    cmask = jnp.logical_and(n_idx * CMP_STRIDE + (CMP_LEN - 1) <= t_pos, n_idx < n_cmp)
    s_c = jnp.where(cmask, s_c, MASK_VALUE)
    m = jnp.max(s_c, axis=-1, keepdims=True)
    e = jnp.where(cmask, jnp.exp(s_c - m), 0.0)
    den = jnp.sum(e, axis=-1, keepdims=True)
    p_c = e / jnp.where(den > 0.0, den, 1.0)
    oc_ref[0] = _diag_heads(jnp.dot(p_c.astype(BF16), vc, preferred_element_type=F32))

    gsum = gsum_ref[...]
    ph, plo = _split_bf16(p_c)
    p_sum = jnp.dot(gsum, ph, preferred_element_type=F32) + jnp.dot(gsum, plo, preferred_element_type=F32)
    sh, slo = _split_bf16(p_sum)
    ovs = ovs_ref[...]
    imp = jnp.dot(sh, ovs, preferred_element_type=F32) + jnp.dot(slo, ovs, preferred_element_type=F32)
    blk = lax.broadcasted_iota(jnp.int32, imp.shape, 1)
    jt = t_pos // SEL_BLOCK
    valid = blk <= jt
    forced = jnp.logical_and(valid, jnp.logical_or(blk == 0, jnp.logical_or(blk == jt, blk == jt - 1)))
    eff = jnp.where(forced, jnp.inf, jnp.where(valid, imp, -jnp.inf))
    rank = jnp.zeros(imp.shape, jnp.int32)
    for j in range(jt + 1):
        other = eff[:, j:j + 1]
        ahead = jnp.logical_or(other > eff, jnp.logical_and(other == eff, blk > j))
        rank = rank + ahead.astype(jnp.int32)
    sel = jnp.logical_and(valid, rank < SEL_TOPN)
    sel_bias = jnp.where(sel, 0.0, SEL_BIAS).astype(BF16)
    bias_h = jnp.dot(rep_ref[...], sel_bias, preferred_element_type=F32).astype(BF16)
    bias_keys = jnp.dot(bias_h, eblk_ref[...], preferred_element_type=F32)

    rn = rown_ref[0]
    ks_t = slct_buf[slot, 0:kvw, :].astype(BF16)
    vs_t = slct_buf[slot, kvw:2 * kvw, :].astype(BF16)
    s_s = jnp.dot(qbd, ks_t, preferred_element_type=F32) + bias_keys
    ks_new = rn[:, 2 * kvw:3 * kvw].astype(BF16).astype(F32)
    vs_new = rn[:, 3 * kvw:4 * kvw].astype(BF16).astype(F32)
    s_new = jnp.sum(qf * ks_new, axis=-1, keepdims=True)
    m = jnp.maximum(jnp.max(s_s, axis=-1, keepdims=True), s_new)
    p = jnp.exp(s_s - m)
    p_new = jnp.exp(s_new - m)
    den = jnp.sum(p, axis=-1, keepdims=True) + p_new
    o_full = (lax.dot_general(p.astype(BF16), vs_t, _NT_DIMS, preferred_element_type=F32)
              + p_new.astype(BF16).astype(F32) * vs_new)
    os_ref[0] = _diag_heads(o_full) / den

    wb = winbuf_ref[0]
    wn = winn_ref[0]
    wb_len = wb.shape[1]
    kw_t = wb[0:kvw, :].astype(BF16)
    vw_t = wb[kvw:2 * kvw, :].astype(BF16)
    s_w = jnp.dot(qbd, kw_t, preferred_element_type=F32)
    w_idx = lax.broadcasted_iota(jnp.int32, (1, wb_len), 1)
    w_pos = t_pos - wb_len + w_idx
    wmask = jnp.logical_and(w_pos > t_pos - WINDOW, w_pos >= 0)
    s_w = jnp.where(wmask, s_w, MASK_VALUE)
    kw_new = wn[:, 0:kvw].astype(BF16).astype(F32)
    vw_new = wn[:, kvw:2 * kvw].astype(BF16).astype(F32)
    s_new = jnp.sum(qf * kw_new, axis=-1, keepdims=True)
    m = jnp.maximum(jnp.max(s_w, axis=-1, keepdims=True), s_new)
    p = jnp.where(wmask, jnp.exp(s_w - m), 0.0)
    p_new = jnp.exp(s_new - m)
    den = jnp.sum(p, axis=-1, keepdims=True) + p_new
    o_full = (lax.dot_general(p.astype(BF16), vw_t, _NT_DIMS, preferred_element_type=F32)
              + p_new.astype(BF16).astype(F32) * vw_new)
    ow_ref[0] = _diag_heads(o_full) / den

    new_col = jnp.broadcast_to(wn, (8, 2 * kvw)).T[:, 0:1]
    shifted = pltpu.roll(wb, wb_len - 1, 1)
    wout_ref[0] = jnp.where(w_idx == wb_len - 1, new_col, shifted)


def nsa_sample_attention(cache, page_table, win_buf, q, rows_new, win_new, cmp_w1, cmp_pe, cmp_w2, k_norm_cmp):
    G, K, dh = NSA_KV_HEADS, NSA_GROUP, NSA_HEAD_DIM
    bsz, n_pages = page_table.shape
    n_phys, page_rows = cache.shape[0], cache.shape[1]
    past = n_pages * page_rows
    n_chunk = past // CMP_STRIDE
    n_cmp = n_chunk - CMP_R + 1
    ns = -(-(past + 1) // SEL_BLOCK)
    cache3 = jnp.transpose(cache, (0, 2, 3, 4, 1)).reshape(n_phys, 4 * G * dh, page_rows)
    wb_len = win_buf.shape[1]
    win3 = jnp.transpose(win_buf, (0, 2, 3, 4, 1)).reshape(bsz, 2 * G * dh, wb_len)
    qh = jnp.transpose(q[0], (2, 0, 1, 3)).astype(F32)
    qbd = jnp.einsum('bgkd,gq->bgkqd', qh, jnp.eye(G, dtype=F32)).reshape(bsz, G * K, G * dh).astype(BF16)
    w1 = cmp_w1.reshape(2, CMP_R, CMP_STRIDE, dh, CMP_HID)
    wc = jnp.einsum('vrsdh,pq->vspdqrh', w1, jnp.eye(2, dtype=F32)).reshape(
        2, CMP_STRIDE, 2 * dh, 2 * CMP_R * CMP_HID).astype(BF16)
    pe = jnp.broadcast_to(cmp_pe.reshape(2, 1, CMP_LEN * dh), (2, 8, CMP_LEN * dh)).astype(BF16)
    w1f = cmp_w1.reshape(2, CMP_LEN * dh, CMP_HID).astype(BF16)
    w2 = cmp_w2.astype(BF16)
    kn = k_norm_cmp[None, :]
    gsum = np.zeros((8, G * K), np.float32)
    gsum[np.arange(G * K) // K, np.arange(G * K)] = 1.0
    ovs = np.zeros((n_chunk, NSA_NS_PAD), np.float32)
    ovs[:n_cmp, :ns] = _overlap_matrix(n_cmp, ns)
    eblk = (np.arange(NSA_NS_PAD)[:, None] == (np.arange(past)[None, :] // SEL_BLOCK)).astype(np.float32)
    gsum, ovs, eblk = jnp.asarray(gsum, BF16), jnp.asarray(ovs, BF16), jnp.asarray(eblk, BF16)
    rep = gsum.T
    full = lambda a: pl.BlockSpec(a.shape, lambda i, pt: (0,) * a.ndim)
    grid_spec = pltpu.PrefetchScalarGridSpec(
        num_scalar_prefetch=1,
        grid=(bsz,),
        in_specs=[pl.BlockSpec(memory_space=pl.ANY),
                  pl.BlockSpec((1, G * K, G * dh), lambda i, pt: (i, 0, 0)),
                  pl.BlockSpec((1, 1, 4 * G * dh), lambda i, pt: (i, 0, 0)),
                  pl.BlockSpec((1, 1, 2 * G * dh), lambda i, pt: (i, 0, 0)),
                  pl.BlockSpec((1, 2 * G * dh, wb_len), lambda i, pt: (i, 0, 0)),
                  full(wc), full(pe), full(w1f), full(w2), full(kn), full(gsum), full(ovs), full(rep), full(eblk)],
        out_specs=[pl.BlockSpec((1, G * K, dh), lambda i, pt: (i, 0, 0))] * 3
        + [pl.BlockSpec((1, 2 * G * dh, wb_len), lambda i, pt: (i, 0, 0))],
        scratch_shapes=[pltpu.VMEM((2, n_pages, 2 * G * dh, page_rows), F32),
                        pltpu.VMEM((2, 2 * G * dh, past), F32),
                        pltpu.VMEM((4, past, V7X_LANES), F32),
                        pltpu.SemaphoreType.DMA((2,))],
    )
    out = jax.ShapeDtypeStruct((bsz, G * K, dh), F32)
    o_c, o_s, o_w, win_next = pl.pallas_call(
        functools.partial(_nsa_sample_body, n_pages, page_rows),
        out_shape=(out, out, out, jax.ShapeDtypeStruct((bsz, 2 * G * dh, wb_len), F32)),
        grid_spec=grid_spec,
        compiler_params=_cparams(("arbitrary",), V7X_VMEM_LIMIT_BYTES),
        name="nsa_sample_attention",
    )(page_table, cache3, qbd, rows_new.reshape(bsz, 1, -1), win_new.reshape(bsz, 1, -1), win3,
      wc, pe, w1f, w2, kn, gsum, ovs, rep, eblk)
    win_next = jnp.transpose(win_next.reshape(bsz, 2, G, dh, wb_len), (0, 4, 1, 2, 3))
    return (o_c.reshape(bsz, -1), o_s.reshape(bsz, -1), o_w.reshape(bsz, -1)), win_next


def _mm_residual_gated_body(oc_ref, os_ref, ow_ref, gc_ref, gs_ref, gw_ref, w_ref, x_ref, g_ref, o_ref):
    a = gc_ref[...] * oc_ref[...] + gs_ref[...] * os_ref[...] + gw_ref[...] * ow_ref[...]
    acc = jnp.dot(a.astype(BF16), w_ref[...], preferred_element_type=F32)
    o_ref[...] = x_ref[...] + g_ref[0] * acc


def mm_residual_gated(branches, gates, w_bf, x, gate_mod):
    t, d = x.shape
    full2 = lambda a: pl.BlockSpec(a.shape, lambda i: (0,) * a.ndim)
    args = (*branches, *gates, w_bf, x, gate_mod)
    return pl.pallas_call(
        _mm_residual_gated_body,
        out_shape=jax.ShapeDtypeStruct((t, d), F32),
        grid=(1,),
        in_specs=[full2(a) for a in args],
        out_specs=pl.BlockSpec((t, d), lambda i: (0, 0)),
        compiler_params=_cparams(("arbitrary",)),
        name="mm_residual_gated",
    )(*args)


def nsa_sample_pallas(proj, bsz, cache, page_table, win_buf, q_norm, k_norm, cmp_w1, cmp_pe, cmp_w2):
    G, K, dh = NSA_KV_HEADS, NSA_GROUP, NSA_HEAD_DIM
    q, rows, win, _, _, _, _, gates, _, _ = nsa_prep(proj, 1, bsz, q_norm, k_norm, bsz)
    branches, win_out = nsa_sample_attention(cache, page_table, win_buf, q, rows, win, cmp_w1, cmp_pe, cmp_w2,
                                             k_norm[0])
    g3 = jnp.transpose(gates[0, :, :, :3 * K], (1, 0, 2)).reshape(bsz, G, K, 3)
    gexp = [jnp.repeat(g3[..., br].reshape(bsz, G * K), dh, axis=1) for br in range(3)]
    rows_out = rows.reshape(bsz, 1, 4, G, dh)
    return branches, gexp, rows_out, win_out


def _pad_cols(w, n):
    return jnp.pad(w, ((0, 0), (0, n - w.shape[1])))


def kernel(x_prompt, x_sample, cache_nsa_kv, state_nsa_win, state_ssd_conv, state_ssd, state_rg_conv, state_rg, page_table, c_prompt, c_sample, ada_w, ada_b, norm_mix, norm_ffn, rec_w_in, ssd_conv_w, ssd_conv_b, ssd_dt_bias, ssd_a_log, ssd_d, ssd_norm_w, rg_conv_w, rg_conv_b, rg_wa, rg_ba, rg_wi, rg_bi, rg_lambda, rec_w_out, nsa_w_in, nsa_q_norm, nsa_k_norm, cmp_w1, cmp_pe, cmp_w2, nsa_w_out, router_w, router_b, moe_w1, moe_b1, moe_w2, moe_b2):
    bp, L, d = x_prompt.shape
    bs = x_sample.shape[0]
    depth = ada_w.shape[0]
    tp = bp * L
    xp = x_prompt.reshape(tp, d)
    xs = x_sample.reshape(bs, d)

    n_c = bp + bs
    n_c_pad = -(-n_c // 8) * 8
    c_all = jnp.pad(jnp.concatenate([c_prompt, c_sample], axis=0), ((0, n_c_pad - n_c), (0, 0)))
    ada_w_cat = jnp.concatenate([ada_w[i] for i in range(depth)], axis=1).astype(BF16)
    ada_b_cat = jnp.concatenate([ada_b[i] for i in range(depth)], axis=0)[None, :]
    mod_all = adaln_mod(c_all, ada_w_cat, ada_b_cat)

    outs = {k: [] for k in ('rows_p', 'rows_s', 'win_p', 'win_s', 'sconv_p', 'sconv_s', 'ssm_p', 'ssm_s',
                            'rconv_p', 'rconv_s', 'rg_p', 'rg_s')}

    for i in range(depth):
        j = i // 2
        mod_i = mod_all[:, i * 6 * d:(i + 1) * 6 * d]
        mp = [mod_i[:bp, k * d:(k + 1) * d].reshape(bp, 1, d) for k in range(6)]
        ms = [mod_i[bp:bp + bs, k * d:(k + 1) * d].reshape(1, bs, d) for k in range(6)]
        g_mix = norm_mix[i][None, :]
        g_ffn = norm_ffn[i][None, :]

        if i % 2 == 0:
            w_in = rec_w_in[j]
            s0, s1, s2, s3 = 1024, 1024 + SSD_XBC, 1024 + SSD_XBC + SSD_HEADS, 1024 + SSD_XBC + SSD_HEADS + RG_WIDTH
            w_cat = jnp.concatenate([w_in[:, :s1], w_in[:, s2:], _pad_cols(w_in[:, s1:s2], 512)], axis=1).astype(BF16)
            proj_p = mod_matmul(xp, g_mix, mp[1], mp[0], w_cat, PROJ_ROW_TILE, 512)
            proj_s = mod_matmul(xs, g_mix, ms[1], ms[0], w_cat, bs, 512)

            ssd_w = (ssd_conv_w[j], ssd_conv_b[j], ssd_dt_bias[j], ssd_a_log[j], ssd_d[j], ssd_norm_w[j])
            rg_w = (rg_conv_w[j], rg_conv_b[j], rg_wa[j], rg_ba[j], rg_wi[j], rg_bi[j], rg_lambda[j])
            yp, a1, a2 = ssd_prompt(proj_p, bp, L, *ssd_w)
            rp, a3, a4 = rg_prompt(proj_p, bp, L, *rg_w)
            a4 = a4.reshape(bp, RG_WIDTH)
            ys, b1_, b2_ = ssd_sample(proj_s, state_ssd_conv[j], state_ssd[j], *ssd_w)
            rs, b3_, b4_ = rg_sample(proj_s, state_rg_conv[j], state_rg[j], *rg_w)
            outs['sconv_p'].append(a1); outs['ssm_p'].append(a2); outs['rconv_p'].append(a3); outs['rg_p'].append(a4)
            outs['sconv_s'].append(b1_); outs['ssm_s'].append(b2_); outs['rconv_s'].append(b3_); outs['rg_s'].append(b4_)
            w_out = rec_w_out[j].astype(BF16)
            w_parts = [w_out[:SSD_WIDTH], w_out[SSD_WIDTH:]]
            xp = mm_residual([yp.reshape(tp, -1), rp.reshape(tp, -1)], w_parts, xp, mp[2], ROW_TILE)
            xs = mm_residual([ys.reshape(bs, -1), rs.reshape(bs, -1)], w_parts, xs, ms[2], bs)
        else:
            w_in = nsa_w_in[j]
            w_cat = _pad_cols(w_in, 3072).astype(BF16)
            proj_p = mod_matmul(xp, g_mix, mp[1], mp[0], w_cat, PROJ_ROW_TILE, 512)
            proj_s = mod_matmul(xs, g_mix, ms[1], ms[0], w_cat, bs, 512)
            wts = (nsa_q_norm[j], nsa_k_norm[j], cmp_w1[j], cmp_pe[j], cmp_w2[j])
            op, rp, wp = nsa_prompt_pallas(proj_p, bp, L, *wts)
            br_s, gexp_s, rs, ws = nsa_sample_pallas(proj_s, bs, cache_nsa_kv[j], page_table, state_nsa_win[j], *wts)
            outs['rows_p'].append(rp); outs['win_p'].append(wp); outs['rows_s'].append(rs); outs['win_s'].append(ws)
            w_out = nsa_w_out[j].astype(BF16)
            xp = mm_residual([op.reshape(tp, -1)], [w_out], xp, mp[2], ROW_TILE)
            xs = mm_residual_gated(br_s, gexp_s, w_out, xs, ms[2])

        rw = _pad_cols(router_w[i], V7X_LANES)
        rwh = rw.astype(BF16)
        rwl = (rw - rwh.astype(F32)).astype(BF16)
        rb = jnp.concatenate([router_b[i], jnp.full((V7X_LANES - N_EXPERTS,), -1e30, F32)])[None, :]
        zero_base = jnp.zeros((1, V7X_LANES), F32)
        h_p, e_p, gt_p, rk_p, cnt_p = moe_router(xp, g_ffn, mp[4], mp[3], rwh, rwl, rb, zero_base, ROW_TILE)
        h_s, e_s, gt_s, rk_s, cnt_s = moe_router(xs, g_ffn, ms[4], ms[3], rwh, rwl, rb, cnt_p[-1], bs)
        h_all = jnp.concatenate([h_p, h_s], axis=0)
        t_all = tp + bs
        counts = cnt_s[-1, 0, :N_EXPERTS].astype(jnp.int32)
        padded, pad_start, blk_e, n_used, n_blocks = _moe_block_layout(counts, t_all * TOP_K)
        pstart = jnp.pad(pad_start.astype(F32), (0, V7X_LANES - N_EXPERTS))[None, :]
        dest_tile = t_all // 6 if t_all % 48 == 0 else bs
        dest = moe_dest(jnp.concatenate([e_p, e_s], axis=0), jnp.concatenate([rk_p, rk_s], axis=0), pstart,
                        dest_tile)
        idx = _moe_row_tables(dest[:, :TOP_K], counts, padded, pad_start, n_blocks, t_all)
        y4 = moe_ffn(h_all, blk_e, n_used, idx, moe_w1, moe_b1, moe_w2, moe_b2, i)
        xp = moe_combine(xp, mp[5], gt_p, y4, tp + bs, 0, bs)
        xs = moe_combine(xs, ms[5], gt_s, y4, tp + bs, tp, bs)

    st = lambda k: jnp.stack(outs[k])
    return (xp.reshape(bp, L, d), xs.reshape(bs, 1, d), st('rows_p'), st('rows_s'), st('win_p'), st('win_s'),
            st('sconv_p'), st('sconv_s'), st('ssm_p'), st('ssm_s'), st('rconv_p'), st('rconv_s'),
            st('rg_p'), st('rg_s'))
```
